```python
import math
import jax
import jax.numpy as jnp
from jax import lax
import numpy as np

D_MODEL = 1024
BATCH = 4
SEQ = 4096
DEPTH = 4

GRID_W = 64
CTX_LEN = 256
CHUNK = 64
EPS = 1e-6
H_A = 4
DH_A = 128
W_A = H_A * DH_A
H_B = 4
DH_B = 128
W_B = H_B * DH_B
CONV_K = 5
H_C = 8
DK_C = 128
DV_C = 128
C_K = H_C * DK_C
C_V = H_C * DV_C
N_EXPERTS = 16
N_GROUPS = 4
EXPERTS_PER_GROUP = N_EXPERTS // N_GROUPS
GROUP_SCORE_K = 2
TOP_K = 2
D_FF_EXPERT = 512
N_EVEN = (DEPTH + 1) // 2
N_ODD = DEPTH // 2
AB_SPLITS = (W_A, W_A, W_A, W_A, W_B, W_B, W_B, W_B, 2 * H_A, 2 * H_A, 2 * H_B, 2 * H_B)
P_AB = sum(AB_SPLITS)
C_SPLITS = (C_K, C_K, C_K, C_V, C_V)
P_C = sum(C_SPLITS)

kernel_name = 'hybrid_mlstm_gdn_hgrn2_moe_prefix_dit'


def rms_norm(x, gain):
    xf = x.astype(jnp.float32)
    y = xf * lax.rsqrt(jnp.mean(jnp.square(xf), axis=-1, keepdims=True) + EPS)
    return (y * gain.astype(jnp.float32)).astype(x.dtype)


def head_rms_norm(h, gain):
    b, l, nh, dh = h.shape
    y = h * lax.rsqrt(jnp.mean(jnp.square(h), axis=-1, keepdims=True) + EPS)
    return y.reshape(b, l, nh * dh) * gain.astype(jnp.float32)


def l2_normalize(t):
    return t * lax.rsqrt(jnp.sum(jnp.square(t), axis=-1, keepdims=True) + EPS)


def split_cols(p, sizes):
    return jnp.split(p, np.cumsum(sizes)[:-1].tolist(), axis=-1)


def seg_flip(t, n_ctx):
    return jnp.concatenate([jnp.flip(t[:, :n_ctx], 1), jnp.flip(t[:, n_ctx:], 1)], axis=1)


def stack_dirs(t_fwd, t_bwd, n_ctx):
    return jnp.concatenate([t_fwd, seg_flip(t_bwd, n_ctx)], axis=0)


def merge_dirs(y, n_ctx):
    nb = y.shape[0] // 2
    return y[:nb] + seg_flip(y[nb:], n_ctx)


def grid_transpose(t, r, w):
    b, _, ch = t.shape
    return t.reshape(b, r, w, ch).transpose(0, 2, 1, 3).reshape(b, r * w, ch)


def centred_dwconv(x, w):
    k, ch = w.shape
    pad = (k - 1) // 2
    return lax.conv_general_dilated(x, w.astype(x.dtype)[:, None, :], window_strides=(1,),
                                    padding=[(pad, pad)], dimension_numbers=('NWC', 'WIO', 'NWC'),
                                    feature_group_count=ch)


def to_chunks(t):
    n, l, nh = t.shape[:3]
    t = t.reshape((n, l // CHUNK, CHUNK, nh) + t.shape[3:])
    return jnp.moveaxis(t, (1, 3), (0, 2))


def from_chunks(t):
    t = jnp.moveaxis(t, (0, 2), (1, 3))
    return t.reshape((t.shape[0], t.shape[1] * t.shape[2]) + t.shape[3:])


def mlstm_chunkwise(q, k, v, log_i, log_f):
    n, l, nh, dk = q.shape
    dv = v.shape[-1]
    incl = jnp.tril(jnp.ones((CHUNK, CHUNK), dtype=bool))
    qc, kc, vc = to_chunks(q), to_chunks(k * dk ** -0.5), to_chunks(v)
    li = to_chunks(log_i)
    bcum = jnp.cumsum(to_chunks(log_f), axis=-1)
    d_log = jnp.where(incl, bcum[..., :, None] - bcum[..., None, :] + li[..., None, :], -jnp.inf)
    m_intra = jnp.max(d_log, axis=-1)
    qk = jnp.einsum('znhtd,znhsd->znhts', qc, kc)

    def step(carry, xs):
        c_st, n_st, m_st = carry
        q_, k_, v_, li_, b_, dl_, mi_, qk_ = xs
        m_inter = b_ + m_st[..., None]
        m_t = jnp.maximum(m_inter, mi_)
        w_inter = jnp.exp(m_inter - m_t)
        w_intra = jnp.exp(dl_ - m_t[..., None]) * qk_
        num = w_inter[..., None] * jnp.einsum('nhtd,nhde->nhte', q_, c_st) + jnp.einsum('nhts,nhse->nhte', w_intra, v_)
        den = w_inter * jnp.einsum('nhtd,nhd->nht', q_, n_st) + jnp.sum(w_intra, axis=-1)
        h = num / jnp.maximum(jnp.abs(den), jnp.exp(-m_t))[..., None]
        m_new = m_t[..., -1]
        a = jnp.exp(b_[..., -1:] - b_ + li_ - m_new[..., None])
        ka = k_ * a[..., None]
        carry_decay = jnp.exp(b_[..., -1] + m_st - m_new)
        c_new = carry_decay[..., None, None] * c_st + jnp.einsum('nhsd,nhse->nhde', ka, v_)
        n_new = carry_decay[..., None] * n_st + jnp.sum(ka, axis=-2)
        return (c_new, n_new, m_new), h

    init = (jnp.zeros((n, nh, dk, dv), q.dtype), jnp.zeros((n, nh, dk), q.dtype), jnp.zeros((n, nh), q.dtype))
    _, h = lax.scan(step, init, (qc, kc, vc, li, bcum, d_log, m_intra, qk))
    return from_chunks(h)


def gated_delta_chunkwise(q, k, v, beta, g):
    n, l, nh, dk = q.shape
    dv = v.shape[-1]
    incl = jnp.tril(jnp.ones((CHUNK, CHUNK), dtype=bool))
    strict = jnp.tril(jnp.ones((CHUNK, CHUNK), dtype=bool), -1)
    qc, kc, vc = to_chunks(q * dk ** -0.5), to_chunks(k), to_chunks(v)
    bc = to_chunks(beta)
    gc = jnp.cumsum(to_chunks(g), axis=-1)
    decay = jnp.exp(jnp.where(incl, gc[..., :, None] - gc[..., None, :], -jnp.inf))
    kb = kc * bc[..., None]
    a_strict = jnp.where(strict, jnp.einsum('znhtd,znhsd->znhts', kb, kc) * decay, 0.0)
    tmat = jnp.eye(CHUNK, dtype=q.dtype) + a_strict
    u = lax.linalg.triangular_solve(tmat, vc * bc[..., None], left_side=True, lower=True, unit_diagonal=True)
    w = lax.linalg.triangular_solve(tmat, kb * jnp.exp(gc)[..., None], left_side=True, lower=True, unit_diagonal=True)
    att = jnp.einsum('znhtd,znhsd->znhts', qc, kc) * decay
    q_dec = qc * jnp.exp(gc)[..., None]
    k_dec = kc * jnp.exp(gc[..., -1:] - gc)[..., None]
    g_last = jnp.exp(gc[..., -1])

    def step(s, xs):
        u_, w_, att_, qd_, kd_, gl_ = xs
        v_new = u_ - jnp.einsum('nhtd,nhde->nhte', w_, s)
        o = jnp.einsum('nhtd,nhde->nhte', qd_, s) + jnp.einsum('nhts,nhse->nhte', att_, v_new)
        s_new = gl_[..., None, None] * s + jnp.einsum('nhsd,nhse->nhde', kd_, v_new)
        return s_new, o

    _, o = lax.scan(step, jnp.zeros((n, nh, dk, dv), q.dtype), (u, w, att, q_dec, k_dec, g_last))
    return from_chunks(o)


def gla_chunkwise(q, k, v, g):
    n, l, nh, dk = q.shape
    dv = v.shape[-1]
    incl = jnp.tril(jnp.ones((CHUNK, CHUNK), dtype=bool))
    qc, kc, vc = to_chunks(q), to_chunks(k), to_chunks(v)
    gc = jnp.cumsum(to_chunks(g), axis=-2)
    q_dec = qc * jnp.exp(gc)
    k_dec = kc * jnp.exp(gc[..., -1:, :] - gc)
    g_last = jnp.exp(gc[..., -1, :])

    def step(s, xs):
        q_, k_, v_, gc_, qd_, kd_, gl_ = xs
        rel = jnp.exp(jnp.where(incl[..., None], gc_[..., :, None, :] - gc_[..., None, :, :], -jnp.inf))
        att = jnp.einsum('nhtd,nhsd,nhtsd->nhts', q_, k_, rel)
        o = jnp.einsum('nhtd,nhde->nhte', qd_, s) + jnp.einsum('nhts,nhse->nhte', att, v_)
        s_new = gl_[..., :, None] * s + jnp.einsum('nhsd,nhse->nhde', kd_, v_)
        return s_new, o

    _, o = lax.scan(step, jnp.zeros((n, nh, dk, dv), q.dtype), (qc, kc, vc, gc, q_dec, k_dec, g_last))
    return from_chunks(o)


def mixer_ab(u, n_ctx, w_in, i_bias, f_bias, conv_w, a_log, dt_bias, norm_a, norm_b):
    b, l, _ = u.shape
    p = (u @ w_in).astype(jnp.float32)
    qa, ka, va, oa, qb, kb, vb, zb, ia, fa, bb, ab = split_cols(p, AB_SPLITS)
    heads_a = lambda t: t.reshape(b, l, H_A, DH_A)
    log_i = (ia + i_bias.astype(jnp.float32).reshape(-1)).reshape(b, l, 2, H_A)
    log_f = jax.nn.log_sigmoid(fa + f_bias.astype(jnp.float32).reshape(-1)).reshape(b, l, 2, H_A)
    qa, ka, va = heads_a(qa), heads_a(ka), heads_a(va)
    h_a = merge_dirs(mlstm_chunkwise(stack_dirs(qa, qa, n_ctx), stack_dirs(ka, ka, n_ctx), stack_dirs(va, va, n_ctx),
                                     stack_dirs(log_i[:, :, 0], log_i[:, :, 1], n_ctx),
                                     stack_dirs(log_f[:, :, 0], log_f[:, :, 1], n_ctx)), n_ctx)
    y_a = head_rms_norm(h_a, norm_a) * jax.nn.sigmoid(oa)
    qkv = jnp.concatenate([qb, kb, vb], axis=-1)
    qkv = jax.nn.silu(jnp.concatenate([centred_dwconv(qkv[:, :n_ctx], conv_w),
                                       centred_dwconv(qkv[:, n_ctx:], conv_w)], axis=1))
    qb, kb, vb = jnp.split(qkv, 3, axis=-1)
    heads_b = lambda t: t.reshape(b, l, H_B, DH_B)
    qb, kb, vb = l2_normalize(heads_b(qb)), l2_normalize(heads_b(kb)), heads_b(vb)
    beta = jax.nn.sigmoid(bb).reshape(b, l, 2, H_B)
    g = (-jnp.exp(a_log.astype(jnp.float32).reshape(-1))
         * jax.nn.softplus(ab + dt_bias.astype(jnp.float32).reshape(-1))).reshape(b, l, 2, H_B)
    o_b = merge_dirs(gated_delta_chunkwise(stack_dirs(qb, qb, n_ctx), stack_dirs(kb, kb, n_ctx), stack_dirs(vb, vb, n_ctx),
                                           stack_dirs(beta[:, :, 0], beta[:, :, 1], n_ctx),
                                           stack_dirs(g[:, :, 0], g[:, :, 1], n_ctx)), n_ctx)
    y_b = head_rms_norm(o_b, norm_b) * jax.nn.silu(zb)
    return jnp.concatenate([y_a, y_b], axis=-1)


def mixer_c(u, n_ctx, rows, column_major, w_in, f_bias, lb, norm_c):
    b, l, _ = u.shape
    p = (u @ w_in).astype(jnp.float32)
    if column_major:
        p = jnp.concatenate([p[:, :n_ctx], grid_transpose(p[:, n_ctx:], rows, GRID_W)], axis=1)
    q, zf, zb, vi, og = split_cols(p, C_SPLITS)
    heads_k = lambda t: t.reshape(b, l, H_C, DK_C)
    heads_v = lambda t: t.reshape(b, l, H_C, DV_C)
    fb = f_bias.astype(jnp.float32)

    def gates(z, lb_d):
        f = lb_d + (1.0 - lb_d) * jax.nn.sigmoid(z)
        key = (1.0 - lb_d) * jax.nn.sigmoid(-z)
        return heads_k(jnp.log(f)), heads_k(key)

    lf_f, k_f = gates(zf + fb[0], lb[0])
    lf_b, k_b = gates(zb + fb[1], lb[1])
    q = heads_k(jax.nn.silu(q))
    vi = heads_v(vi)
    o = merge_dirs(gla_chunkwise(stack_dirs(q, q, n_ctx), stack_dirs(k_f, k_b, n_ctx),
                                 stack_dirs(vi, vi, n_ctx), stack_dirs(lf_f, lf_b, n_ctx)), n_ctx)
    y = head_rms_norm(o, norm_c) * jax.nn.silu(og)
    if column_major:
        y = jnp.concatenate([y[:, :n_ctx], grid_transpose(y[:, n_ctx:], GRID_W, rows)], axis=1)
    return y


def moe_ffn(h, w_router, router_bias, w1, w3, w2):
    t = h.shape[0]
    scores = jax.nn.sigmoid(jnp.dot(h.astype(jnp.float32), w_router.astype(jnp.float32)))
    sel = scores + router_bias.astype(jnp.float32)
    group_score = jnp.sum(lax.top_k(sel.reshape(t, N_GROUPS, EXPERTS_PER_GROUP), GROUP_SCORE_K)[0], axis=-1)
    best_group = jnp.argmax(group_score, axis=-1)
    expert_group = jnp.arange(N_EXPERTS) // EXPERTS_PER_GROUP
    sel = jnp.where(expert_group[None, :] == best_group[:, None], sel, -jnp.inf)
    _, idx = lax.top_k(sel, TOP_K)
    wts = jnp.take_along_axis(scores, idx, axis=-1)
    wts = wts / jnp.sum(wts, axis=-1, keepdims=True)
    gate = jnp.sum(jax.nn.one_hot(idx, N_EXPERTS, dtype=jnp.float32) * wts[..., None], axis=1)
    out = jnp.zeros(h.shape, jnp.float32)
    for e in range(N_EXPERTS):
        a = jax.nn.silu(h @ w1[e]) * (h @ w3[e])
        out = out + gate[:, e:e + 1] * (a @ w2[e]).astype(jnp.float32)
    return out.astype(h.dtype)


def setup_inputs(seed: int = 0) -> dict:
    key = jax.random.key(seed)
    ks = iter(jax.random.split(key, 32))
    f32 = jnp.float32
    nrm = lambda shape, scale: jax.random.normal(next(ks), shape, f32) * scale
    x = nrm((BATCH, SEQ, D_MODEL), 1.0)
    c = nrm((BATCH, D_MODEL), 1.0)
    ctx = nrm((BATCH, CTX_LEN, D_MODEL), 1.0)
    c_ctx = nrm((D_MODEL,), 1.0)
    w_mod = nrm((DEPTH, D_MODEL, 6 * D_MODEL), 0.5 * D_MODEL ** -0.5)
    b_mod = nrm((DEPTH, 6 * D_MODEL), 0.01)
    norm_mix = 1.0 + nrm((DEPTH, D_MODEL), 0.05)
    norm_ffn = 1.0 + nrm((DEPTH, D_MODEL), 0.05)
    norm_final = 1.0 + nrm((D_MODEL,), 0.05)
    ab_w_in = nrm((N_EVEN, D_MODEL, P_AB), D_MODEL ** -0.5)
    ab_i_bias = nrm((N_EVEN, 2, H_A), 0.1)
    ab_f_bias = jnp.linspace(3.0, 6.0, H_A, dtype=f32) + nrm((N_EVEN, 2, H_A), 0.1)
    ab_conv = nrm((N_EVEN, CONV_K, 3 * W_B), CONV_K ** -0.5)
    ab_a_log = jnp.log(jax.random.uniform(next(ks), (N_EVEN, 2, H_B), f32, 1.0, 16.0))
    dt = jnp.exp(jax.random.uniform(next(ks), (N_EVEN, 2, H_B), f32, math.log(1e-3), math.log(1e-1)))
    ab_dt_bias = dt + jnp.log(-jnp.expm1(-dt))
    ab_norm_a = 1.0 + nrm((N_EVEN, W_A), 0.05)
    ab_norm_b = 1.0 + nrm((N_EVEN, W_B), 0.05)
    ab_w_out = nrm((N_EVEN, W_A + W_B, D_MODEL), (W_A + W_B) ** -0.5)
    c_w_in = nrm((N_ODD, D_MODEL, P_C), D_MODEL ** -0.5)
    c_f_bias = nrm((N_ODD, 2, C_K), 0.1)
    c_lb_raw = nrm((N_ODD, 2, C_K), 0.5)
    c_norm = 1.0 + nrm((N_ODD, C_V), 0.05)
    c_w_out = nrm((N_ODD, C_V, D_MODEL), C_V ** -0.5)
    w_router = nrm((D_MODEL, N_EXPERTS), D_MODEL ** -0.5)
    router_bias = nrm((N_EXPERTS,), 0.01)
    w1 = nrm((DEPTH, N_EXPERTS, D_MODEL, D_FF_EXPERT), D_MODEL ** -0.5)
    w3 = nrm((DEPTH, N_EXPERTS, D_MODEL, D_FF_EXPERT), D_MODEL ** -0.5)
    w2 = nrm((DEPTH, N_EXPERTS, D_FF_EXPERT, D_MODEL), D_FF_EXPERT ** -0.5)
    return {'x': x, 'c': c, 'ctx': ctx, 'c_ctx': c_ctx, 'w_mod': w_mod, 'b_mod': b_mod,
            'norm_mix': norm_mix, 'norm_ffn': norm_ffn, 'norm_final': norm_final,
            'ab_w_in': ab_w_in, 'ab_i_bias': ab_i_bias, 'ab_f_bias': ab_f_bias, 'ab_conv': ab_conv,
            'ab_a_log': ab_a_log, 'ab_dt_bias': ab_dt_bias, 'ab_norm_a': ab_norm_a, 'ab_norm_b': ab_norm_b,
            'ab_w_out': ab_w_out, 'c_w_in': c_w_in, 'c_f_bias': c_f_bias, 'c_lb_raw': c_lb_raw,
            'c_norm': c_norm, 'c_w_out': c_w_out, 'w_router': w_router, 'router_bias': router_bias,
            'w1': w1, 'w3': w3, 'w2': w2}


def reference(x, c, ctx, c_ctx, w_mod, b_mod, norm_mix, norm_ffn, norm_final,
              ab_w_in, ab_i_bias, ab_f_bias, ab_conv, ab_a_log, ab_dt_bias, ab_norm_a, ab_norm_b, ab_w_out,
              c_w_in, c_f_bias, c_lb_raw, c_norm, c_w_out, w_router, router_bias, w1, w3, w2):
    bsz, seq, d = x.shape
    n_ctx = ctx.shape[1]
    rows = seq // GRID_W
    lb_p = jax.nn.softmax(c_lb_raw.astype(jnp.float32), axis=0)
    lb_all = jnp.cumsum(lb_p, axis=0) - lb_p[0:1]
    silu_c = jax.nn.silu(c)
    silu_cc = jax.nn.silu(c_ctx)
    lat, cx = x, ctx
    for layer in range(DEPTH):
        last = layer == DEPTH - 1
        sh1, sc1, g1, sh2, sc2, g2 = jnp.split((silu_c @ w_mod[layer] + b_mod[layer])[:, None, :], 6, axis=-1)
        sh1c, sc1c, g1c, sh2c, sc2c, g2c = jnp.split(silu_cc @ w_mod[layer] + b_mod[layer], 6, axis=-1)
        u = jnp.concatenate([rms_norm(cx, norm_mix[layer]) * (1 + sc1c) + sh1c,
                             rms_norm(lat, norm_mix[layer]) * (1 + sc1) + sh1], axis=1)
        j = layer // 2
        if layer % 2 == 0:
            feats = mixer_ab(u, n_ctx, ab_w_in[j], ab_i_bias[j], ab_f_bias[j], ab_conv[j], ab_a_log[j],
                             ab_dt_bias[j], ab_norm_a[j], ab_norm_b[j])
            w_out = ab_w_out[j]
        else:
            feats = mixer_c(u, n_ctx, rows, j % 2 == 1, c_w_in[j], c_f_bias[j], lb_all[j], c_norm[j])
            w_out = c_w_out[j]
        feats = feats.astype(x.dtype)
        lat = lat + g1 * (feats[:, n_ctx:] @ w_out)
        hl = rms_norm(lat, norm_ffn[layer]) * (1 + sc2) + sh2
        if last:
            lat = lat + g2 * moe_ffn(hl.reshape(-1, d), w_router, router_bias, w1[layer], w3[layer],
                                     w2[layer]).reshape(bsz, seq, d)
        else:
            cx = cx + g1c * (feats[:, :n_ctx] @ w_out)
            hc = rms_norm(cx, norm_ffn[layer]) * (1 + sc2c) + sh2c
            y = moe_ffn(jnp.concatenate([hc, hl], axis=1).reshape(-1, d), w_router, router_bias,
                        w1[layer], w3[layer], w2[layer]).reshape(bsz, n_ctx + seq, d)
            cx = cx + g2c * y[:, :n_ctx]
            lat = lat + g2 * y[:, n_ctx:]
    return rms_norm(lat, norm_final)
```

```python
import functools

import numpy as np
import jax
import jax.numpy as jnp
from jax import lax
from jax.experimental import pallas as pl
from jax.experimental.pallas import tpu as pltpu

F32 = jnp.float32
BF16 = jnp.bfloat16

EPS = 1e-6
GRID_W = 64
HEAD = 128
H_AB = 4
H_C = 8
CONV_K = 5
N_EXPERTS = 16
N_GROUPS = 4
LANES = 128
SUBLANES = 8
VMEM_LIMIT = 56 * 1024 * 1024

TOK_TILE = 512
MLSTM_CHUNK = 128
GDN_CHUNK = 64
GLA_CHUNK = 64
GDN_INV_BLOCK = 16


def _cparams(sem):
    return pltpu.CompilerParams(dimension_semantics=sem, vmem_limit_bytes=VMEM_LIMIT)


def _dot(a, b):
    return jnp.dot(a, b, preferred_element_type=F32)


def _dot_nt(a, b):
    return lax.dot_general(a, b, (((1,), (1,)), ((), ())), preferred_element_type=F32)


def _dot_tn(a, b):
    return lax.dot_general(a, b, (((0,), (0,)), ((), ())), preferred_element_type=F32)


def _split2(x):
    hi = x.astype(BF16)
    lo = (x - hi.astype(F32)).astype(BF16)
    return hi, lo


def _split3(x):
    hi = x.astype(BF16)
    r = x - hi.astype(F32)
    mid = r.astype(BF16)
    lo = (r - mid.astype(F32)).astype(BF16)
    return hi, mid, lo


def _dot_exact_lhs(m_bf16, x):
    hi, mid, lo = _split3(x)
    return _dot(m_bf16, hi) + _dot(m_bf16, mid) + _dot(m_bf16, lo)


def _dot3(a, b):
    ah, al = _split2(a)
    bh, bl = _split2(b)
    return _dot(ah, bh) + _dot(ah, bl) + _dot(al, bh)


def _dot_nt3(a, b):
    ah, al = _split2(a)
    bh, bl = _split2(b)
    return _dot_nt(ah, bh) + _dot_nt(ah, bl) + _dot_nt(al, bh)


def _sigmoid_pair(z):
    e = jnp.exp(-jnp.abs(z))
    r = 1.0 / (1.0 + e)
    er = e * r
    pos = z >= 0
    return jnp.where(pos, r, er), jnp.where(pos, er, r)


def _silu(z):
    return z * _sigmoid_pair(z)[0]


def _mod_kernel(c_ref, w_ref, b_ref, o_ref):
    s = _silu(c_ref[...]).astype(BF16)
    o_ref[0] = _dot(s, w_ref[0].astype(BF16)) + b_ref[0]


def _modulation(cpad, w_mod, b_mod):
    depth, d, n = w_mod.shape
    tn = 1536
    return pl.pallas_call(
        _mod_kernel,
        grid=(depth, n // tn),
        in_specs=[pl.BlockSpec((SUBLANES, d), lambda l, j: (0, 0)),
                  pl.BlockSpec((1, d, tn), lambda l, j: (l, 0, j)),
                  pl.BlockSpec((1, 1, tn), lambda l, j: (l, 0, j))],
        out_specs=pl.BlockSpec((1, SUBLANES, tn), lambda l, j: (l, 0, j)),
        out_shape=jax.ShapeDtypeStruct((depth, SUBLANES, n), F32),
        compiler_params=_cparams(("parallel", "parallel")),
        name="modulation",
    )(cpad, w_mod, b_mod.reshape(depth, 1, n))


def _gate_activations(raw, par):
    y = raw + par[0:1, :]
    e = jnp.exp(-jnp.abs(y))
    sp = jnp.log1p(e)
    lsig = jnp.minimum(y, 0.0) - sp
    r = 1.0 / (1.0 + e)
    sig = jnp.where(y >= 0, r, e * r)
    gdec = -jnp.exp(par[1:2, :]) * (jnp.maximum(y, 0.0) + sp)
    lane = lax.broadcasted_iota(jnp.int32, raw.shape, 1)
    return jnp.where(lane < 4, y, jnp.where(lane < 8, lsig, jnp.where(lane < 12, sig, gdec)))


def _inproj_kernel(x_ref, gain_ref, sc_ref, sh_ref, w_ref, *rest, with_gates):
    if with_gates:
        wg_ref, gpar_ref, p_ref, g_ref, u_ref = rest
    else:
        p_ref, u_ref = rest

    @pl.when(pl.program_id(1) == 0)
    def _():
        x = x_ref[...]
        y = x * lax.rsqrt(jnp.mean(x * x, axis=-1, keepdims=True) + EPS) * gain_ref[...]
        u = (y * (1.0 + sc_ref[0]) + sh_ref[0]).astype(BF16)
        u_ref[...] = u
        if with_gates:
            for d in range(2):
                g_ref[d] = _gate_activations(_dot(u, wg_ref[d]), gpar_ref[d])

    p_ref[...] = _dot(u_ref[...], w_ref[...])


def _mod_row(i, tm, seq, nb):
    return jnp.minimum((i * tm) // seq, nb)


def _inproj(xs, gain, mod, layer, w, seq, nb, wg=None, gpar=None):
    t, d = xs.shape
    p = w.shape[1]
    tm, tn = TOK_TILE, 1024
    with_gates = wg is not None
    base = layer * SUBLANES * 6

    def mod_spec(which):
        return pl.BlockSpec((1, 1, d), lambda i, j: (base + _mod_row(i, tm, seq, nb) * 6 + which, 0, 0))

    in_specs = [pl.BlockSpec((tm, d), lambda i, j: (i, 0)),
                pl.BlockSpec((1, d), lambda i, j: (0, 0)),
                mod_spec(1), mod_spec(0),
                pl.BlockSpec((d, tn), lambda i, j: (0, j))]
    args = [xs, gain.reshape(1, d), mod, mod, w]
    out_specs = [pl.BlockSpec((tm, tn), lambda i, j: (i, j))]
    out_shape = [jax.ShapeDtypeStruct((t, p), F32)]
    if with_gates:
        in_specs += [pl.BlockSpec((2, d, LANES), lambda i, j: (0, 0, 0)),
                     pl.BlockSpec((2, SUBLANES, LANES), lambda i, j: (0, 0, 0))]
        args += [wg, gpar]
        out_specs.append(pl.BlockSpec((2, tm, LANES), lambda i, j: (0, i, 0)))
        out_shape.append(jax.ShapeDtypeStruct((2, t, LANES), F32))
    return pl.pallas_call(
        functools.partial(_inproj_kernel, with_gates=with_gates),
        grid=(t // tm, p // tn),
        in_specs=in_specs, out_specs=out_specs, out_shape=out_shape,
        scratch_shapes=[pltpu.VMEM((tm, d), BF16)],
        compiler_params=_cparams(("parallel", "arbitrary")),
        name="inproj",
    )(*args)


def _gdn_prep_kernel(prev_ref, x_ref, next_ref, w_ref, o_ref, *, blocks_per_seq, n_lat_blocks, scale):
    i = pl.program_id(0)
    tm = x_ref.shape[0]
    in_lat = i < n_lat_blocks
    first = jnp.logical_or(jnp.logical_not(in_lat), i % blocks_per_seq == 0)
    last = jnp.logical_or(jnp.logical_not(in_lat), i % blocks_per_seq == blocks_per_seq - 1)
    prev = jnp.where(first, 0.0, prev_ref[...])
    nxt = jnp.where(last, 0.0, next_ref[...])
    xe = jnp.concatenate([prev, x_ref[...], nxt], axis=0)
    n = xe.shape[0]
    pad = (CONV_K - 1) // 2
    acc = None
    for j in range(CONV_K):
        shift = (pad - j) % n
        xr = xe if shift == 0 else pltpu.roll(xe, shift, 0)
        term = xr[SUBLANES:SUBLANES + tm] * w_ref[j:j + 1, :]
        acc = term if acc is None else acc + term
    y = _silu(acc)
    part = pl.program_id(1)
    for h in range(H_AB):
        sl = slice(h * HEAD, (h + 1) * HEAD)
        t = y[:, sl]
        inv_norm = lax.rsqrt(jnp.sum(t * t, axis=-1, keepdims=True) + EPS)
        factor = jnp.where(part == 0, inv_norm * scale, jnp.where(part == 1, inv_norm, 1.0))
        o_ref[:, sl] = t * factor


def _gdn_prep(p, conv_w, seq, n_ctx, nb):
    t = p.shape[0]
    tm = 256
    w = H_AB * HEAD
    first_col = 4
    wpad = jnp.zeros((SUBLANES, 3 * w), F32).at[:CONV_K].set(conv_w)
    hb = tm // SUBLANES
    nblk8 = t // SUBLANES
    return pl.pallas_call(
        functools.partial(_gdn_prep_kernel, blocks_per_seq=seq // tm, n_lat_blocks=nb * seq // tm,
                          scale=HEAD ** -0.5),
        grid=(t // tm, 3),
        in_specs=[pl.BlockSpec((SUBLANES, w), lambda i, c: (jnp.maximum(i * hb - 1, 0), first_col + c)),
                  pl.BlockSpec((tm, w), lambda i, c: (i, first_col + c)),
                  pl.BlockSpec((SUBLANES, w), lambda i, c: (jnp.minimum((i + 1) * hb, nblk8 - 1), first_col + c)),
                  pl.BlockSpec((SUBLANES, w), lambda i, c: (0, c))],
        out_specs=pl.BlockSpec((tm, w), lambda i, c: (i, c)),
        out_shape=jax.ShapeDtypeStruct((t, 3 * w), F32),
        compiler_params=_cparams(("parallel", "parallel")),
        name="gdn_prep",
    )(p, p, p, wpad)


def _tri_tables(c):
    idx = np.arange(c)
    fwd = (idx[None, :] <= idx[:, None]).astype(np.float32)
    return jnp.asarray(np.stack([fwd, fwd.T]))


def _row_block(seg, chunk, seq, n_ctx, nb):
    if seg == "lat":
        n = seq // chunk
        return n, (lambda b, d, i: b * n + i + d * (n - 1 - 2 * i))
    n = n_ctx // chunk
    off = nb * seq // chunk
    return n, (lambda b, d, i: off + b * n + i + d * (n - 1 - 2 * i))


def _mlstm_kernel(q_ref, k_ref, v_ref, g_ref, tri_ref, s0_ref, m0_ref, _h_in, h_ref, sn_ref, mn_ref,
                  s_scr, m_scr, *, scale):
    i = pl.program_id(2)

    @pl.when(i == 0)
    def _():
        s_scr[...] = s0_ref[0, 0]
        m_scr[...] = m0_ref[0, 0]

    tri = tri_ref[0]
    mask = tri > 0.5
    g = g_ref[0]
    b = _dot_exact_lhs(tri.astype(BF16), g)
    total = jnp.sum(g, axis=0, keepdims=True)
    g_t = g.T
    b_t = b.T
    c = g.shape[0]
    ones_col = (lax.broadcasted_iota(jnp.int32, (c, HEAD), 1) == 0).astype(BF16)
    for h in range(H_AB):
        sl = slice(h * HEAD, (h + 1) * HEAD)
        q = q_ref[:, sl].astype(BF16)
        k = k_ref[:, sl] * scale
        v_aug = jnp.concatenate([v_ref[:, sl].astype(BF16), ones_col], axis=1)
        li_c, li_r = g[:, h:h + 1], g_t[h:h + 1, :]
        b_c, b_r = b[:, 4 + h:5 + h], b_t[4 + h:5 + h, :]
        tot = total[:, 4 + h:5 + h]
        m_st = m_scr[h][0:1, 0:1]
        d_log = jnp.where(mask, b_c - b_r + li_r, -jnp.inf)
        m_inter = b_c + m_st
        m_t = jnp.maximum(m_inter, jnp.max(d_log, axis=-1, keepdims=True))
        w_inter = jnp.exp(m_inter - m_t)
        w_intra = jnp.exp(d_log - m_t) * _dot_nt(q, k.astype(BF16))
        s = s_scr[h]
        num = w_inter * _dot(q, s.astype(BF16)) + _dot(w_intra.astype(BF16), v_aug)
        den = num[:, HEAD:HEAD + 1]
        h_ref[0, :, sl] = num[:, :HEAD] / jnp.maximum(jnp.abs(den), jnp.exp(-m_t))
        colv = tot - b_c + li_c
        m_new = jnp.maximum(tot + m_st, jnp.max(colv, axis=0, keepdims=True))
        ka = (k * jnp.exp(colv - m_new)).astype(BF16)
        s_scr[h] = jnp.exp(tot + m_st - m_new) * s + _dot_tn(ka, v_aug)
        m_scr[h] = jnp.broadcast_to(m_new, (SUBLANES, LANES))

    @pl.when(i == pl.num_programs(2) - 1)
    def _():
        sn_ref[0, 0] = s_scr[...]
        mn_ref[0, 0] = m_scr[...]


def _mlstm_scan(p, gates, seq, n_ctx, nb):
    t = p.shape[0]
    c = MLSTM_CHUNK
    w = H_AB * HEAD
    tri = _tri_tables(c)
    s = jnp.zeros((nb, 2, H_AB, HEAD, 2 * HEAD), F32)
    m = jnp.zeros((nb, 2, H_AB, SUBLANES, LANES), F32)
    h = None
    for seg in ("ctx", "lat"):
        n, rb = _row_block(seg, c, seq, n_ctx, nb)
        st_spec_s = pl.BlockSpec((1, 1, H_AB, HEAD, 2 * HEAD), lambda b, d, i: (b, d, 0, 0, 0))
        st_spec_m = pl.BlockSpec((1, 1, H_AB, SUBLANES, LANES), lambda b, d, i: (b, d, 0, 0, 0))
        in_specs = [pl.BlockSpec((c, w), lambda b, d, i, col=col: (rb(b, d, i), col)) for col in range(3)]
        in_specs += [pl.BlockSpec((1, c, LANES), lambda b, d, i: (d, rb(b, d, i), 0)),
                     pl.BlockSpec((1, c, c), lambda b, d, i: (d, 0, 0)),
                     st_spec_s, st_spec_m]
        args = [p, p, p, gates, tri, s, m]
        aliases = {}
        if h is None:
            in_specs.append(pl.BlockSpec((SUBLANES, LANES), lambda b, d, i: (0, 0)))
            args.append(jnp.zeros((SUBLANES, LANES), F32))
        else:
            in_specs.append(pl.BlockSpec(memory_space=pl.ANY))
            args.append(h)
            aliases = {7: 0}
        h, s, m = pl.pallas_call(
            functools.partial(_mlstm_kernel, scale=HEAD ** -0.5),
            grid=(nb, 2, n),
            in_specs=in_specs,
            out_specs=[pl.BlockSpec((1, c, w), lambda b, d, i: (d, rb(b, d, i), 0)), st_spec_s, st_spec_m],
            out_shape=[jax.ShapeDtypeStruct((2, t, w), F32),
                       jax.ShapeDtypeStruct(s.shape, F32), jax.ShapeDtypeStruct(m.shape, F32)],
            scratch_shapes=[pltpu.VMEM((H_AB, HEAD, 2 * HEAD), F32), pltpu.VMEM((H_AB, SUBLANES, LANES), F32)],
            input_output_aliases=aliases,
            compiler_params=_cparams(("parallel", "parallel", "arbitrary")),
            name="mlstm_" + seg,
        )(*args)
    return h


def _gdn_inv_masks(c):
    idx = np.arange(c)
    blk = lambda n: (idx[:, None] // n) == (idx[None, :] // n)
    levels = [blk(GDN_INV_BLOCK)]
    n = GDN_INV_BLOCK
    while n < c:
        levels.append(np.logical_and(blk(2 * n), np.logical_not(blk(n))))
        n *= 2
    return jnp.asarray(np.stack(levels).astype(np.float32))


def _unit_tri_inverse(a, inv_masks):
    c = a.shape[0]
    eye = (lax.broadcasted_iota(jnp.int32, (c, c), 0) == lax.broadcasted_iota(jnp.int32, (c, c), 1)).astype(F32)
    ad = a * inv_masks[0]
    inv = eye - ad
    pw = ad
    n = 2
    while n < GDN_INV_BLOCK:
        pw = _dot3(pw, pw)
        inv = inv + _dot3(inv, pw)
        n *= 2
    for lvl in range(1, inv_masks.shape[0]):
        e = a * inv_masks[lvl]
        inv = inv - _dot3(_dot3(inv, e), inv)
    return inv


def _gdn_kernel(q_ref, k_ref, v_ref, g_ref, tri_ref, im_ref, s0_ref, _o_in, o_ref, sn_ref, s_scr):
    i = pl.program_id(2)

    @pl.when(i == 0)
    def _():
        s_scr[...] = s0_ref[0, 0]

    tri = tri_ref[0]
    c = tri.shape[0]
    mask = tri > 0.5
    eye = lax.broadcasted_iota(jnp.int32, (c, c), 0) == lax.broadcasted_iota(jnp.int32, (c, c), 1)
    strict = jnp.logical_and(mask, jnp.logical_not(eye))
    inv_masks = im_ref[...]
    g = g_ref[0]
    gc = _dot_exact_lhs(tri.astype(BF16), g)
    total = jnp.sum(g, axis=0, keepdims=True)
    gc_t = gc.T
    for h in range(H_AB):
        sl = slice(h * HEAD, (h + 1) * HEAD)
        q, k, v = q_ref[:, sl], k_ref[:, sl], v_ref[:, sl]
        beta = g[:, 8 + h:9 + h]
        gc_c, gc_r = gc[:, 12 + h:13 + h], gc_t[12 + h:13 + h, :]
        tot = total[:, 12 + h:13 + h]
        decay = jnp.exp(jnp.where(mask, gc_c - gc_r, -jnp.inf))
        kb = k * beta
        a = jnp.where(strict, _dot_nt3(kb, k) * decay, 0.0)
        inv = _unit_tri_inverse(a, inv_masks)
        eg = jnp.exp(gc_c)
        rhs = jnp.concatenate([v * beta, kb * eg], axis=1).astype(BF16)
        ih, il = _split2(inv)
        uw = _dot(ih, rhs) + _dot(il, rhs)
        u, w = uw[:, :HEAD], uw[:, HEAD:]
        kbf = k.astype(BF16)
        att = _dot_nt(q.astype(BF16), kbf) * decay
        s = s_scr[h]
        sb = s.astype(BF16)
        v_new = (u - _dot(w.astype(BF16), sb)).astype(BF16)
        o_ref[0, :, sl] = _dot((q * eg).astype(BF16), sb) + _dot(att.astype(BF16), v_new)
        k_dec = (k * jnp.exp(tot - gc_c)).astype(BF16)
        s_scr[h] = jnp.exp(tot) * s + _dot_tn(k_dec, v_new)

    @pl.when(i == pl.num_programs(2) - 1)
    def _():
        sn_ref[0, 0] = s_scr[...]


def _gdn_scan(qkv, gates, seq, n_ctx, nb):
    t = qkv.shape[0]
    c = GDN_CHUNK
    w = H_AB * HEAD
    tri = _tri_tables(c)
    im = _gdn_inv_masks(c)
    s = jnp.zeros((nb, 2, H_AB, HEAD, HEAD), F32)
    o = None
    for seg in ("ctx", "lat"):
        n, rb = _row_block(seg, c, seq, n_ctx, nb)
        st_spec = pl.BlockSpec((1, 1, H_AB, HEAD, HEAD), lambda b, d, i: (b, d, 0, 0, 0))
        in_specs = [pl.BlockSpec((c, w), lambda b, d, i, col=col: (rb(b, d, i), col)) for col in range(3)]
        in_specs += [pl.BlockSpec((1, c, LANES), lambda b, d, i: (d, rb(b, d, i), 0)),
                     pl.BlockSpec((1, c, c), lambda b, d, i: (d, 0, 0)),
                     pl.BlockSpec(im.shape, lambda b, d, i: (0, 0, 0)),
                     st_spec]
        args = [qkv, qkv, qkv, gates, tri, im, s]
        aliases = {}
        if o is None:
            in_specs.append(pl.BlockSpec((SUBLANES, LANES), lambda b, d, i: (0, 0)))
            args.append(jnp.zeros((SUBLANES, LANES), F32))
        else:
            in_specs.append(pl.BlockSpec(memory_space=pl.ANY))
            args.append(o)
            aliases = {7: 0}
        o, s = pl.pallas_call(
            _gdn_kernel,
            grid=(nb, 2, n),
            in_specs=in_specs,
            out_specs=[pl.BlockSpec((1, c, w), lambda b, d, i: (d, rb(b, d, i), 0)), st_spec],
            out_shape=[jax.ShapeDtypeStruct((2, t, w), F32), jax.ShapeDtypeStruct(s.shape, F32)],
            scratch_shapes=[pltpu.VMEM((H_AB, HEAD, HEAD), F32)],
            input_output_aliases=aliases,
            compiler_params=_cparams(("parallel", "parallel", "arbitrary")),
            name="gdn_" + seg,
        )(*args)
    return o


def _gla_tables(c):
    idx = np.arange(c)
    sel, sgn, msk = [], [], []
    for d in range(2):
        sel_d, sgn_d, msk_d = [], [], []
        bs = c // 2
        while bs >= 1:
            pair = idx // (2 * bs)
            second = (idx // bs) % 2 == 1
            ref_row = pair * 2 * bs + (bs - 1 if d == 0 else bs)
            sel_d.append((idx[None, :] == ref_row[:, None]).astype(np.float32))
            qside = second if d == 0 else np.logical_not(second)
            sgn_d.append(np.where(qside, 1.0, -1.0)[:, None] * np.ones((1, HEAD)))
            same_pair = pair[:, None] == pair[None, :]
            msk_d.append(np.logical_and(same_pair, np.logical_and(qside[:, None], np.logical_not(qside)[None, :])))
            bs //= 2
        msk_d.append(np.eye(c, dtype=bool))
        sel.append(np.stack(sel_d)); sgn.append(np.stack(sgn_d)); msk.append(np.stack(msk_d).astype(np.float32))
    return (jnp.asarray(np.stack(sel), dtype=BF16), jnp.asarray(np.stack(sgn), dtype=F32),
            jnp.asarray(np.stack(msk), dtype=F32))


def _gla_kernel(q_ref, z_ref, v_ref, fb_ref, lb_ref, tri_ref, sel_ref, sgn_ref, msk_ref, s0_ref, _o_in,
                o_ref, sn_ref, s_scr):
    i = pl.program_id(2)

    @pl.when(i == 0)
    def _():
        s_scr[...] = s0_ref[0, 0]

    tri = tri_ref[0].astype(BF16)
    n_lvl = sel_ref.shape[1]
    lb = lb_ref[0]
    z = z_ref[...] + fb_ref[0]
    sig_pos, sig_neg = _sigmoid_pair(z)
    g_all = jnp.log(lb + (1.0 - lb) * sig_pos)
    key_all = (1.0 - lb) * sig_neg
    q_all = _silu(q_ref[...])
    gc_all = _dot_exact_lhs(tri, g_all)
    total_all = jnp.sum(g_all, axis=0, keepdims=True)
    gc_b = gc_all.astype(BF16)
    refs = [_dot(sel_ref[0, l], gc_b) for l in range(n_lvl)]
    for h in range(H_C):
        sl = slice(h * HEAD, (h + 1) * HEAD)
        q, k, v = q_all[:, sl], key_all[:, sl], v_ref[:, sl].astype(BF16)
        gc, tot = gc_all[:, sl], total_all[:, sl]
        att = msk_ref[0, n_lvl] * _dot_nt(q.astype(BF16), k.astype(BF16))
        for l in range(n_lvl):
            e = jnp.exp(sgn_ref[0, l] * (gc - refs[l][:, sl]))
            att = att + msk_ref[0, l] * _dot_nt((q * e).astype(BF16), (k * e).astype(BF16))
        st = s_scr[h]
        o_ref[0, :, sl] = _dot_nt((q * jnp.exp(gc)).astype(BF16), st.astype(BF16)) + _dot(att.astype(BF16), v)
        k_dec = (k * jnp.exp(tot - gc)).astype(BF16)
        s_scr[h] = jnp.exp(tot) * st + _dot_tn(v, k_dec)

    @pl.when(i == pl.num_programs(2) - 1)
    def _():
        sn_ref[0, 0] = s_scr[...]


def _gla_scan(p, f_bias, lb, seq, n_ctx, nb, column_major):
    t, pw = p.shape
    c = GLA_CHUNK
    w = H_C * HEAD
    tri = _tri_tables(c)
    sel, sgn, msk = _gla_tables(c)
    s = jnp.zeros((nb, 2, H_C, HEAD, HEAD), F32)
    o = None
    parts = pw // w
    fb3, lb3 = f_bias.reshape(2, 1, w), lb.reshape(2, 1, w)
    for seg in ("ctx", "lat"):
        n, rb = _row_block(seg, c, seq, n_ctx, nb)
        if seg == "lat" and column_major:
            assert seq == GRID_W * c, "column-major scan expects one chunk per grid column"
            src = p.reshape(t // c, c * pw)
            cm = lambda b, d, i: i + d * (n - 1 - 2 * i)
            dspec = lambda col: pl.BlockSpec((c, w), lambda b, d, i: (b, cm(b, d, i) * parts + col(d)))
            o_spec = pl.BlockSpec((1, c, w), lambda b, d, i: (d, b, cm(b, d, i)))
            o_shape = (2, t // c, c * w)
        else:
            src = p
            dspec = lambda col: pl.BlockSpec((c, w), lambda b, d, i: (rb(b, d, i), col(d)))
            o_spec = pl.BlockSpec((1, c, w), lambda b, d, i: (d, rb(b, d, i), 0))
            o_shape = (2, t, w)
        st_spec = pl.BlockSpec((1, 1, H_C, HEAD, HEAD), lambda b, d, i: (b, d, 0, 0, 0))
        dir_spec = lambda shp: pl.BlockSpec((1,) + shp, lambda b, d, i: (d,) + (0,) * len(shp))
        in_specs = [dspec(lambda d: 0), dspec(lambda d: 1 + d), dspec(lambda d: 3),
                    dir_spec((1, w)), dir_spec((1, w)), dir_spec((c, c)),
                    dir_spec(sel.shape[1:]), dir_spec(sgn.shape[1:]), dir_spec(msk.shape[1:]), st_spec]
        args = [src, src, src, fb3, lb3, tri, sel, sgn, msk, s]
        aliases = {}
        if o is None:
            in_specs.append(pl.BlockSpec((SUBLANES, LANES), lambda b, d, i: (0, 0)))
            args.append(jnp.zeros((SUBLANES, LANES), F32))
        else:
            in_specs.append(pl.BlockSpec(memory_space=pl.ANY))
            args.append(o.reshape(o_shape))
            aliases = {10: 0}
        o, s = pl.pallas_call(
            _gla_kernel,
            grid=(nb, 2, n),
            in_specs=in_specs,
            out_specs=[o_spec, st_spec],
            out_shape=[jax.ShapeDtypeStruct(o_shape, F32), jax.ShapeDtypeStruct(s.shape, F32)],
            scratch_shapes=[pltpu.VMEM((H_C, HEAD, HEAD), F32)],
            input_output_aliases=aliases,
            compiler_params=_cparams(("parallel", "parallel", "arbitrary")),
            name="gla_" + seg,
        )(*args)
    return o.reshape(2, t, w)


def _route(scores_t, bias):
    s = [scores_t[e:e + 1, :] for e in range(N_EXPERTS)]
    sel = [s[e] + bias[e:e + 1, :] for e in range(N_EXPERTS)]
    per = N_EXPERTS // N_GROUPS
    gscore = []
    for grp in range(N_GROUPS):
        v = sel[grp * per:(grp + 1) * per]
        best = None
        for a in range(per):
            for b in range(a + 1, per):
                pair = v[a] + v[b]
                best = pair if best is None else jnp.maximum(best, pair)
        gscore.append(best)
    best_g = jnp.zeros_like(gscore[0], dtype=jnp.int32)
    best_v = gscore[0]
    for grp in range(1, N_GROUPS):
        better = gscore[grp] > best_v
        best_g = jnp.where(better, grp, best_g)
        best_v = jnp.where(better, gscore[grp], best_v)
    masked = [jnp.where(best_g == e // per, sel[e], -jnp.inf) for e in range(N_EXPERTS)]
    idx1 = jnp.zeros_like(best_g)
    v1 = masked[0]
    for e in range(1, N_EXPERTS):
        better = masked[e] > v1
        idx1 = jnp.where(better, e, idx1)
        v1 = jnp.where(better, masked[e], v1)
    masked2 = [jnp.where(idx1 == e, -jnp.inf, masked[e]) for e in range(N_EXPERTS)]
    idx2 = jnp.zeros_like(best_g)
    v2 = masked2[0]
    for e in range(1, N_EXPERTS):
        better = masked2[e] > v2
        idx2 = jnp.where(better, e, idx2)
        v2 = jnp.where(better, masked2[e], v2)
    w1 = sum(jnp.where(idx1 == e, s[e], 0.0) for e in range(N_EXPERTS))
    w2 = sum(jnp.where(idx2 == e, s[e], 0.0) for e in range(N_EXPERTS))
    tot = w1 + w2
    w1, w2 = w1 / tot, w2 / tot
    return [jnp.where(idx1 == e, w1, 0.0) + jnp.where(idx2 == e, w2, 0.0) for e in range(N_EXPERTS)]


def _outproj_kernel(x_ref, of0, ob0, gt0, of1, ob1, gt1, hn_ref, w_ref, g1_ref, sc_ref, sh_ref, nf_ref,
                    wr_ref, rb_ref, xo_ref, h_ref, gate_ref, *, first_half_sigmoid):
    feats = []
    for half, (of, ob, gt) in enumerate(((of0, ob0, gt0), (of1, ob1, gt1))):
        o = of[0] + ob[0]
        gate = gt[...]
        for h in range(o.shape[1] // HEAD):
            sl = slice(h * HEAD, (h + 1) * HEAD)
            t = o[:, sl]
            y = t * lax.rsqrt(jnp.mean(t * t, axis=-1, keepdims=True) + EPS)
            y = y * hn_ref[:, half * o.shape[1] + h * HEAD: half * o.shape[1] + (h + 1) * HEAD]
            gz = gate[:, sl]
            act = _sigmoid_pair(gz)[0] if (half == 0 and first_half_sigmoid) else _silu(gz)
            feats.append((y * act).astype(BF16))
    feats = jnp.concatenate(feats, axis=1)
    xn = x_ref[...] + g1_ref[0] * _dot(feats, w_ref[...])
    xo_ref[...] = xn
    y = xn * lax.rsqrt(jnp.mean(xn * xn, axis=-1, keepdims=True) + EPS) * nf_ref[...]
    hl = y * (1.0 + sc_ref[0]) + sh_ref[0]
    h_ref[...] = hl.astype(BF16)
    logits = _dot3(hl, wr_ref[...])
    scores_t = _sigmoid_pair(logits)[0].T
    rows = _route(scores_t, rb_ref[...])
    pad = jnp.zeros((LANES - N_EXPERTS, scores_t.shape[1]), F32)
    gate_ref[...] = jnp.concatenate(rows + [pad], axis=0).T


def _outproj(xs, o_halves, gate_src, gate_cols, hn_gain, w_out, mod, layer, nf_gain, wr_pad, rb_col,
             seq, nb, n_rows, first_half_sigmoid):
    t, d = xs.shape
    tm = TOK_TILE
    half = d // 2
    base = layer * SUBLANES * 6

    def mod_spec(which):
        return pl.BlockSpec((1, 1, d), lambda i: (base + _mod_row(i, tm, seq, nb) * 6 + which, 0, 0))

    in_specs = [pl.BlockSpec((tm, d), lambda i: (i, 0))]
    args = [xs]
    for (arr, col), gcol in zip(o_halves, gate_cols):
        in_specs += [pl.BlockSpec((1, tm, half), lambda i, col=col: (0, i, col)),
                     pl.BlockSpec((1, tm, half), lambda i, col=col: (1, i, col)),
                     pl.BlockSpec((tm, half), lambda i, gcol=gcol: (i, gcol))]
        args += [arr, arr, gate_src]
    in_specs += [pl.BlockSpec((1, d), lambda i: (0, 0)),
                 pl.BlockSpec((d, d), lambda i: (0, 0)),
                 mod_spec(2), mod_spec(4), mod_spec(3),
                 pl.BlockSpec((1, d), lambda i: (0, 0)),
                 pl.BlockSpec((d, LANES), lambda i: (0, 0)),
                 pl.BlockSpec((N_EXPERTS, 1), lambda i: (0, 0))]
    args += [hn_gain.reshape(1, d), w_out, mod, mod, mod, nf_gain.reshape(1, d), wr_pad, rb_col]
    return pl.pallas_call(
        functools.partial(_outproj_kernel, first_half_sigmoid=first_half_sigmoid),
        grid=(n_rows // tm,),
        in_specs=in_specs,
        out_specs=[pl.BlockSpec((tm, d), lambda i: (i, 0)),
                   pl.BlockSpec((tm, d), lambda i: (i, 0)),
                   pl.BlockSpec((tm, LANES), lambda i: (i, 0))],
        out_shape=[jax.ShapeDtypeStruct((t, d), F32), jax.ShapeDtypeStruct((t, d), BF16),
                   jax.ShapeDtypeStruct((t, LANES), F32)],
        compiler_params=_cparams(("parallel",)),
        name="outproj",
    )(*args)


def _moe_kernel(x_ref, h_ref, gate_ref, w1_ref, w3_ref, w2_ref, g2_ref, nfin_ref, o_ref, acc_ref, *, final_norm):
    e = pl.program_id(1)

    @pl.when(e == 0)
    def _():
        acc_ref[...] = jnp.zeros_like(acc_ref)

    h = h_ref[...]
    lane = lax.broadcasted_iota(jnp.int32, gate_ref.shape, 1)
    gcol = jnp.sum(jnp.where(lane == e, gate_ref[...], 0.0), axis=-1, keepdims=True)
    a = _silu(_dot(h, w1_ref[0])) * _dot(h, w3_ref[0]) * gcol
    acc_ref[...] += _dot(a.astype(BF16), w2_ref[0])

    @pl.when(e == pl.num_programs(1) - 1)
    def _():
        y = x_ref[...] + g2_ref[0] * acc_ref[...]
        if final_norm:
            y = y * lax.rsqrt(jnp.mean(y * y, axis=-1, keepdims=True) + EPS) * nfin_ref[...]
        o_ref[...] = y


def _moe(xs, h, gate, w1, w3, w2, mod, layer, nfin, seq, nb, n_rows, final_norm):
    t, d = xs.shape
    tm = 1024
    e, _, f = w1.shape
    base = layer * SUBLANES * 6
    out_rows = n_rows if final_norm else t
    return pl.pallas_call(
        functools.partial(_moe_kernel, final_norm=final_norm),
        grid=(n_rows // tm, e),
        in_specs=[pl.BlockSpec((tm, d), lambda i, j: (i, 0)),
                  pl.BlockSpec((tm, d), lambda i, j: (i, 0)),
                  pl.BlockSpec((tm, LANES), lambda i, j: (i, 0)),
                  pl.BlockSpec((1, d, f), lambda i, j: (j, 0, 0)),
                  pl.BlockSpec((1, d, f), lambda i, j: (j, 0, 0)),
                  pl.BlockSpec((1, f, d), lambda i, j: (j, 0, 0)),
                  pl.BlockSpec((1, 1, d), lambda i, j: (base + _mod_row(i, tm, seq, nb) * 6 + 5, 0, 0)),
                  pl.BlockSpec((1, d), lambda i, j: (0, 0))],
        out_specs=pl.BlockSpec((tm, d), lambda i, j: (i, 0)),
        out_shape=jax.ShapeDtypeStruct((out_rows, d), F32),
        scratch_shapes=[pltpu.VMEM((tm, d), F32)],
        compiler_params=_cparams(("parallel", "arbitrary")),
        name="moe",
    )(xs, h, gate, w1, w3, w2, mod, nfin.reshape(1, d))


def kernel(x, c, ctx, c_ctx, w_mod, b_mod, norm_mix, norm_ffn, norm_final, ab_w_in, ab_i_bias, ab_f_bias,
           ab_conv, ab_a_log, ab_dt_bias, ab_norm_a, ab_norm_b, ab_w_out, c_w_in, c_f_bias, c_lb_raw, c_norm,
           c_w_out, w_router, router_bias, w1, w3, w2):
    nb, seq, d = x.shape
    n_ctx = ctx.shape[1]
    depth = w_mod.shape[0]
    n_lat = nb * seq
    assert nb + 1 <= SUBLANES and seq % 1024 == 0 and (nb * n_ctx) % 1024 == 0 and n_ctx % 256 == 0

    xs = jnp.concatenate([x.reshape(n_lat, d), ctx.reshape(nb * n_ctx, d)], axis=0)
    cpad = jnp.zeros((SUBLANES, d), F32).at[:nb].set(c).at[nb].set(c_ctx)
    mod = _modulation(cpad, w_mod, b_mod).reshape(depth * SUBLANES * 6, 1, d)

    lb_p = jax.nn.softmax(c_lb_raw.astype(F32), axis=0)
    lb_all = jnp.cumsum(lb_p, axis=0) - lb_p[0:1]

    wr_pad = jnp.zeros((d, LANES), F32).at[:, :N_EXPERTS].set(w_router)
    rb_col = router_bias.astype(F32).reshape(N_EXPERTS, 1)
    w_main = H_AB * HEAD * 8
    hw = H_AB * HEAD

    out = None
    for layer in range(depth):
        last = layer == depth - 1
        j = layer // 2
        if layer % 2 == 0:
            w_in = ab_w_in[j]
            gw = w_in[:, w_main:].reshape(d, 4, 2, H_AB)
            wg = jnp.zeros((2, d, LANES), F32).at[:, :, :4 * H_AB].set(
                jnp.transpose(gw, (2, 0, 1, 3)).reshape(2, d, 4 * H_AB)).astype(BF16)
            zeros = jnp.zeros((2, H_AB), F32)
            bias = jnp.concatenate([ab_i_bias[j], ab_f_bias[j], zeros, ab_dt_bias[j]], axis=1)
            alog = jnp.concatenate([zeros, zeros, zeros, ab_a_log[j]], axis=1)
            gpar = jnp.zeros((2, SUBLANES, LANES), F32).at[:, 0, :4 * H_AB].set(bias).at[:, 1, :4 * H_AB].set(alog)
            p, gates = _inproj(xs, norm_mix[layer], mod, layer, w_in[:, :w_main].astype(BF16), seq, nb, wg, gpar)
            h_a = _mlstm_scan(p, gates, seq, n_ctx, nb)
            qkv = _gdn_prep(p, ab_conv[j], seq, n_ctx, nb)
            o_b = _gdn_scan(qkv, gates, seq, n_ctx, nb)
            o_halves = ((h_a, 0), (o_b, 0))
            gate_cols = (3, 7)
            hn_gain = jnp.concatenate([ab_norm_a[j], ab_norm_b[j]])
            w_out = ab_w_out[j]
        else:
            (p,) = _inproj(xs, norm_mix[layer], mod, layer, c_w_in[j].astype(BF16), seq, nb)
            o_c = _gla_scan(p, c_f_bias[j], lb_all[j], seq, n_ctx, nb, column_major=(j % 2 == 1))
            o_halves = ((o_c, 0), (o_c, 1))
            gate_cols = (8, 9)
            hn_gain = c_norm[j]
            w_out = c_w_out[j]
        n_rows = n_lat if last else n_lat + nb * n_ctx
        xs, h, gate = _outproj(xs, o_halves, p, gate_cols, hn_gain, w_out.astype(BF16), mod, layer,
                               norm_ffn[layer], wr_pad, rb_col, seq, nb, n_rows, layer % 2 == 0)
        res = _moe(xs, h, gate, w1[layer].astype(BF16), w3[layer].astype(BF16), w2[layer].astype(BF16),
                   mod, layer, norm_final, seq, nb, n_rows, last)
        if last:
            out = res
        else:
            xs = res
    return out.reshape(nb, seq, d)
```

```python
import functools

import numpy as np
import jax
import jax.numpy as jnp
from jax import lax
from jax.experimental import pallas as pl
from jax.experimental.pallas import tpu as pltpu

F32 = jnp.float32
BF16 = jnp.bfloat16

EPS = 1e-6
GRID_W = 64
HEAD = 128
H_AB = 4
H_C = 8
CONV_K = 5
N_EXPERTS = 16
N_GROUPS = 4
LANES = 128
SUBLANES = 8
VMEM_LIMIT = 56 * 1024 * 1024

TOK_TILE = 512
MLSTM_CHUNK = 128
GDN_CHUNK = 64
GLA_CHUNK = 64
GDN_INV_BLOCK = 16


def _cparams(sem):
    return pltpu.CompilerParams(dimension_semantics=sem, vmem_limit_bytes=VMEM_LIMIT)


def _dot(a, b):
    return jnp.dot(a, b, preferred_element_type=F32)


def _dot_nt(a, b):
    return lax.dot_general(a, b, (((1,), (1,)), ((), ())), preferred_element_type=F32)


def _dot_tn(a, b):
    return lax.dot_general(a, b, (((0,), (0,)), ((), ())), preferred_element_type=F32)


def _split2(x):
    hi = x.astype(BF16)
    lo = (x - hi.astype(F32)).astype(BF16)
    return hi, lo


def _split3(x):
    hi = x.astype(BF16)
    r = x - hi.astype(F32)
    mid = r.astype(BF16)
    lo = (r - mid.astype(F32)).astype(BF16)
    return hi, mid, lo


def _dot_exact_lhs(m_bf16, x):
    hi, mid, lo = _split3(x)
    return _dot(m_bf16, hi) + _dot(m_bf16, mid) + _dot(m_bf16, lo)


def _dot3(a, b):
    ah, al = _split2(a)
    bh, bl = _split2(b)
    return _dot(ah, bh) + _dot(ah, bl) + _dot(al, bh)


def _dot_nt3(a, b):
    ah, al = _split2(a)
    bh, bl = _split2(b)
    return _dot_nt(ah, bh) + _dot_nt(ah, bl) + _dot_nt(al, bh)


def _sigmoid_pair(z):
    e = jnp.exp(-jnp.abs(z))
    r = 1.0 / (1.0 + e)
    er = e * r
    pos = z >= 0
    return jnp.where(pos, r, er), jnp.where(pos, er, r)


def _silu(z):
    return z * _sigmoid_pair(z)[0]


def _mod_kernel(c_ref, w_ref, b_ref, o_ref):
    s = _silu(c_ref[...]).astype(BF16)
    o_ref[0] = _dot(s, w_ref[0].astype(BF16)) + b_ref[0]


def _modulation(cpad, w_mod, b_mod):
    depth, d, n = w_mod.shape
    tn = 1536
    return pl.pallas_call(
        _mod_kernel,
        grid=(depth, n // tn),
        in_specs=[pl.BlockSpec((SUBLANES, d), lambda l, j: (0, 0)),
                  pl.BlockSpec((1, d, tn), lambda l, j: (l, 0, j)),
                  pl.BlockSpec((1, 1, tn), lambda l, j: (l, 0, j))],
        out_specs=pl.BlockSpec((1, SUBLANES, tn), lambda l, j: (l, 0, j)),
        out_shape=jax.ShapeDtypeStruct((depth, SUBLANES, n), F32),
        compiler_params=_cparams(("parallel", "parallel")),
        name="modulation",
    )(cpad, w_mod, b_mod.reshape(depth, 1, n))


def _gate_activations(raw, par):
    y = raw + par[0:1, :]
    e = jnp.exp(-jnp.abs(y))
    sp = jnp.log1p(e)
    lsig = jnp.minimum(y, 0.0) - sp
    r = 1.0 / (1.0 + e)
    sig = jnp.where(y >= 0, r, e * r)
    gdec = -jnp.exp(par[1:2, :]) * (jnp.maximum(y, 0.0) + sp)
    lane = lax.broadcasted_iota(jnp.int32, raw.shape, 1)
    return jnp.where(lane < 4, y, jnp.where(lane < 8, lsig, jnp.where(lane < 12, sig, gdec)))


def _inproj_kernel(x_ref, gain_ref, sc_ref, sh_ref, w_ref, *rest, with_gates):
    if with_gates:
        wg_ref, gpar_ref, p_ref, g_ref, u_ref = rest
    else:
        p_ref, u_ref = rest

    @pl.when(pl.program_id(1) == 0)
    def _():
        x = x_ref[...]
        y = x * lax.rsqrt(jnp.mean(x * x, axis=-1, keepdims=True) + EPS) * gain_ref[...]
        u = (y * (1.0 + sc_ref[0]) + sh_ref[0]).astype(BF16)
        u_ref[...] = u
        if with_gates:
            for d in range(2):
                g_ref[d] = _gate_activations(_dot(u, wg_ref[d]), gpar_ref[d])

    p_ref[...] = _dot(u_ref[...], w_ref[...])


def _mod_row(i, tm, seq, nb):
    return jnp.minimum((i * tm) // seq, nb)


def _inproj(xs, gain, mod, layer, w, seq, nb, wg=None, gpar=None):
    t, d = xs.shape
    p = w.shape[1]
    tm, tn = TOK_TILE, 1024
    with_gates = wg is not None
    base = layer * SUBLANES * 6

    def mod_spec(which):
        return pl.BlockSpec((1, 1, d), lambda i, j: (base + _mod_row(i, tm, seq, nb) * 6 + which, 0, 0))

    in_specs = [pl.BlockSpec((tm, d), lambda i, j: (i, 0)),
                pl.BlockSpec((1, d), lambda i, j: (0, 0)),
                mod_spec(1), mod_spec(0),
                pl.BlockSpec((d, tn), lambda i, j: (0, j))]
    args = [xs, gain.reshape(1, d), mod, mod, w]
    out_specs = [pl.BlockSpec((tm, tn), lambda i, j: (i, j))]
    out_shape = [jax.ShapeDtypeStruct((t, p), F32)]
    if with_gates:
        in_specs += [pl.BlockSpec((2, d, LANES), lambda i, j: (0, 0, 0)),
                     pl.BlockSpec((2, SUBLANES, LANES), lambda i, j: (0, 0, 0))]
        args += [wg, gpar]
        out_specs.append(pl.BlockSpec((2, tm, LANES), lambda i, j: (0, i, 0)))
        out_shape.append(jax.ShapeDtypeStruct((2, t, LANES), F32))
    return pl.pallas_call(
        functools.partial(_inproj_kernel, with_gates=with_gates),
        grid=(t // tm, p // tn),
        in_specs=in_specs, out_specs=out_specs, out_shape=out_shape,
        scratch_shapes=[pltpu.VMEM((tm, d), BF16)],
        compiler_params=_cparams(("parallel", "arbitrary")),
        name="inproj",
    )(*args)


def _gdn_prep_kernel(prev_ref, x_ref, next_ref, w_ref, o_ref, *, blocks_per_seq, n_lat_blocks, scale):
    i = pl.program_id(0)
    tm = x_ref.shape[0]
    in_lat = i < n_lat_blocks
    first = jnp.logical_or(jnp.logical_not(in_lat), i % blocks_per_seq == 0)
    last = jnp.logical_or(jnp.logical_not(in_lat), i % blocks_per_seq == blocks_per_seq - 1)
    prev = jnp.where(first, 0.0, prev_ref[...])
    nxt = jnp.where(last, 0.0, next_ref[...])
    xe = jnp.concatenate([prev, x_ref[...], nxt], axis=0)
    n = xe.shape[0]
    pad = (CONV_K - 1) // 2
    acc = None
    for j in range(CONV_K):
        shift = (pad - j) % n
        xr = xe if shift == 0 else pltpu.roll(xe, shift, 0)
        term = xr[SUBLANES:SUBLANES + tm] * w_ref[j:j + 1, :]
        acc = term if acc is None else acc + term
    y = _silu(acc)
    part = pl.program_id(1)
    for h in range(H_AB):
        sl = slice(h * HEAD, (h + 1) * HEAD)
        t = y[:, sl]
        inv_norm = lax.rsqrt(jnp.sum(t * t, axis=-1, keepdims=True) + EPS)
        factor = jnp.where(part == 0, inv_norm * scale, jnp.where(part == 1, inv_norm, 1.0))
        o_ref[:, sl] = t * factor


def _gdn_prep(p, conv_w, seq, n_ctx, nb):
    t = p.shape[0]
    tm = 256
    w = H_AB * HEAD
    first_col = 4
    wpad = jnp.zeros((SUBLANES, 3 * w), F32).at[:CONV_K].set(conv_w)
    hb = tm // SUBLANES
    nblk8 = t // SUBLANES
    return pl.pallas_call(
        functools.partial(_gdn_prep_kernel, blocks_per_seq=seq // tm, n_lat_blocks=nb * seq // tm,
                          scale=HEAD ** -0.5),
        grid=(t // tm, 3),
        in_specs=[pl.BlockSpec((SUBLANES, w), lambda i, c: (jnp.maximum(i * hb - 1, 0), first_col + c)),
                  pl.BlockSpec((tm, w), lambda i, c: (i, first_col + c)),
                  pl.BlockSpec((SUBLANES, w), lambda i, c: (jnp.minimum((i + 1) * hb, nblk8 - 1), first_col + c)),
                  pl.BlockSpec((SUBLANES, w), lambda i, c: (0, c))],
        out_specs=pl.BlockSpec((tm, w), lambda i, c: (i, c)),
        out_shape=jax.ShapeDtypeStruct((t, 3 * w), F32),
        compiler_params=_cparams(("parallel", "parallel")),
        name="gdn_prep",
    )(p, p, p, wpad)


def _tri_tables(c):
    idx = np.arange(c)
    fwd = (idx[None, :] <= idx[:, None]).astype(np.float32)
    return jnp.asarray(np.stack([fwd, fwd.T]))


def _row_block(seg, chunk, seq, n_ctx, nb):
    if seg == "lat":
        n = seq // chunk
        return n, (lambda b, d, i: b * n + i + d * (n - 1 - 2 * i))
    n = n_ctx // chunk
    off = nb * seq // chunk
    return n, (lambda b, d, i: off + b * n + i + d * (n - 1 - 2 * i))


def _mlstm_kernel(q_ref, k_ref, v_ref, g_ref, tri_ref, s0_ref, m0_ref, _h_in, h_ref, sn_ref, mn_ref,
                  s_scr, m_scr, *, scale):
    i = pl.program_id(2)

    @pl.when(i == 0)
    def _():
        s_scr[...] = s0_ref[0, 0]
        m_scr[...] = m0_ref[0, 0]

    tri = tri_ref[0]
    mask = tri > 0.5
    g = g_ref[0]
    b = _dot_exact_lhs(tri.astype(BF16), g)
    total = jnp.sum(g, axis=0, keepdims=True)
    g_t = g.T
    b_t = b.T
    c = g.shape[0]
    ones_col = (lax.broadcasted_iota(jnp.int32, (c, HEAD), 1) == 0).astype(BF16)
    for h in range(H_AB):
        sl = slice(h * HEAD, (h + 1) * HEAD)
        q = q_ref[:, sl].astype(BF16)
        k = k_ref[:, sl] * scale
        v_aug = jnp.concatenate([v_ref[:, sl].astype(BF16), ones_col], axis=1)
        li_c, li_r = g[:, h:h + 1], g_t[h:h + 1, :]
        b_c, b_r = b[:, 4 + h:5 + h], b_t[4 + h:5 + h, :]
        tot = total[:, 4 + h:5 + h]
        m_st = m_scr[h][0:1, 0:1]
        d_log = jnp.where(mask, b_c - b_r + li_r, -jnp.inf)
        m_inter = b_c + m_st
        m_t = jnp.maximum(m_inter, jnp.max(d_log, axis=-1, keepdims=True))
        w_inter = jnp.exp(m_inter - m_t)
        w_intra = jnp.exp(d_log - m_t) * _dot_nt(q, k.astype(BF16))
        s = s_scr[h]
        num = w_inter * _dot(q, s.astype(BF16)) + _dot(w_intra.astype(BF16), v_aug)
        den = num[:, HEAD:HEAD + 1]
        h_ref[0, :, sl] = num[:, :HEAD] / jnp.maximum(jnp.abs(den), jnp.exp(-m_t))
        colv = tot - b_c + li_c
        m_new = jnp.maximum(tot + m_st, jnp.max(colv, axis=0, keepdims=True))
        ka = (k * jnp.exp(colv - m_new)).astype(BF16)
        s_scr[h] = jnp.exp(tot + m_st - m_new) * s + _dot_tn(ka, v_aug)
        m_scr[h] = jnp.broadcast_to(m_new, (SUBLANES, LANES))

    @pl.when(i == pl.num_programs(2) - 1)
    def _():
        sn_ref[0, 0] = s_scr[...]
        mn_ref[0, 0] = m_scr[...]


def _mlstm_scan(p, gates, seq, n_ctx, nb):
    t = p.shape[0]
    c = MLSTM_CHUNK
    w = H_AB * HEAD
    tri = _tri_tables(c)
    s = jnp.zeros((nb, 2, H_AB, HEAD, 2 * HEAD), F32)
    m = jnp.zeros((nb, 2, H_AB, SUBLANES, LANES), F32)
    h = None
    for seg in ("ctx", "lat"):
        n, rb = _row_block(seg, c, seq, n_ctx, nb)
        st_spec_s = pl.BlockSpec((1, 1, H_AB, HEAD, 2 * HEAD), lambda b, d, i: (b, d, 0, 0, 0))
        st_spec_m = pl.BlockSpec((1, 1, H_AB, SUBLANES, LANES), lambda b, d, i: (b, d, 0, 0, 0))
        in_specs = [pl.BlockSpec((c, w), lambda b, d, i, col=col: (rb(b, d, i), col)) for col in range(3)]
        in_specs += [pl.BlockSpec((1, c, LANES), lambda b, d, i: (d, rb(b, d, i), 0)),
                     pl.BlockSpec((1, c, c), lambda b, d, i: (d, 0, 0)),
                     st_spec_s, st_spec_m]
        args = [p, p, p, gates, tri, s, m]
        aliases = {}
        if h is None:
            in_specs.append(pl.BlockSpec((SUBLANES, LANES), lambda b, d, i: (0, 0)))
            args.append(jnp.zeros((SUBLANES, LANES), F32))
        else:
            in_specs.append(pl.BlockSpec(memory_space=pl.ANY))
            args.append(h)
            aliases = {7: 0}
        h, s, m = pl.pallas_call(
            functools.partial(_mlstm_kernel, scale=HEAD ** -0.5),
            grid=(nb, 2, n),
            in_specs=in_specs,
            out_specs=[pl.BlockSpec((1, c, w), lambda b, d, i: (d, rb(b, d, i), 0)), st_spec_s, st_spec_m],
            out_shape=[jax.ShapeDtypeStruct((2, t, w), F32),
                       jax.ShapeDtypeStruct(s.shape, F32), jax.ShapeDtypeStruct(m.shape, F32)],
            scratch_shapes=[pltpu.VMEM((H_AB, HEAD, 2 * HEAD), F32), pltpu.VMEM((H_AB, SUBLANES, LANES), F32)],
            input_output_aliases=aliases,
            compiler_params=_cparams(("parallel", "parallel", "arbitrary")),
            name="mlstm_" + seg,
        )(*args)
    return h


def _gdn_inv_masks(c):
    idx = np.arange(c)
    blk = lambda n: (idx[:, None] // n) == (idx[None, :] // n)
    levels = [blk(GDN_INV_BLOCK)]
    n = GDN_INV_BLOCK
    while n < c:
        levels.append(np.logical_and(blk(2 * n), np.logical_not(blk(n))))
        n *= 2
    levels = np.stack(levels).astype(np.float32)
    return jnp.asarray(np.tile(levels, (1, 1, H_AB)))


def _block_diag(x, n_blocks):
    blk = lax.broadcasted_iota(jnp.int32, x.shape, 1) // (x.shape[1] // n_blocks)
    return jnp.concatenate([jnp.where(blk == j, x, jnp.zeros_like(x)) for j in range(n_blocks)], axis=0)


def _heads_to_lanes(x, first, width):
    return jnp.concatenate([jnp.broadcast_to(x[:, first + h:first + h + 1], (x.shape[0], width))
                            for h in range(H_AB)], axis=1)


def _unit_tri_inverse(a, eye, inv_masks):
    prod = lambda x, y: _dot(x.astype(BF16), _block_diag(y.astype(BF16), H_AB))
    ad = a * inv_masks[0]
    inv = eye - ad
    pw = ad
    n = 2
    while n < GDN_INV_BLOCK:
        pw = prod(pw, pw)
        inv = inv + prod(inv, pw)
        n *= 2
    for lvl in range(1, inv_masks.shape[0]):
        inv = inv - prod(prod(inv, a * inv_masks[lvl]), inv)
    return inv


def _gdn_chunk_kernel(q_ref, k_ref, v_ref, g_ref, tri_ref, im_ref, u_ref, w_ref, qd_ref, kd_ref, att_ref):
    c = GDN_CHUNK
    n_chunks = q_ref.shape[0] // c
    inv_masks = im_ref[...]
    row = lax.broadcasted_iota(jnp.int32, (c, H_AB * c), 0)
    col = lax.broadcasted_iota(jnp.int32, (c, H_AB * c), 1) % c
    eye_b = row == col
    eye = eye_b.astype(F32)
    masks = [col <= row, col >= row]
    stricts = [col < row, col > row]
    tris = [tri_ref[d].astype(BF16) for d in range(2)]
    ones = jnp.ones((c, c), BF16)
    for ci in range(n_chunks):
        rows = slice(ci * c, (ci + 1) * c)
        q, k, v = q_ref[rows, :], k_ref[rows, :], v_ref[rows, :]
        k_hi, k_lo = _split2(k)
        kbd_hi, kbd_lo = _block_diag(k_hi, H_AB), _block_diag(k_lo, H_AB)
        r_hi = _dot_nt(jnp.concatenate([k_hi, k_lo, q.astype(BF16)], axis=0), kbd_hi)
        kk = r_hi[:c] + r_hi[c:2 * c] + _dot_nt(k_hi, kbd_lo)
        qk = r_hi[2 * c:]
        for d in range(2):
            g = g_ref[d, rows, :]
            gc = _dot_exact_lhs(tris[d], g)
            total = jnp.sum(g, axis=0, keepdims=True)
            gc_c = _heads_to_lanes(gc, 12, c)
            gc_r = _dot_exact_lhs(ones, eye * gc_c)
            decay = jnp.exp(jnp.where(masks[d], gc_c - gc_r, -jnp.inf))
            a = jnp.where(stricts[d], _heads_to_lanes(g, 8, c) * kk * decay, 0.0)
            inv = _unit_tri_inverse(a, eye, inv_masks)
            gc_w = _heads_to_lanes(gc, 12, HEAD)
            eg = jnp.exp(gc_w)
            beta = _heads_to_lanes(g, 8, HEAD)
            inv_hl = jnp.concatenate(_split2(inv), axis=0)
            for ref, rhs in ((u_ref, v * beta), (w_ref, k * (beta * eg))):
                res = _dot(inv_hl, _block_diag(rhs.astype(BF16), H_AB))
                ref[d, rows, :] = (res[:c] + res[c:]).astype(BF16)
            qd_ref[d, rows, :] = (q * eg).astype(BF16)
            kd_ref[d, rows, :] = (k * jnp.exp(_heads_to_lanes(total, 12, HEAD) - gc_w)).astype(BF16)
            att_ref[d, rows, :] = (qk * decay).astype(BF16)


def _gdn_scan_kernel(*refs):
    ins, o_refs, s_scr = refs[:12], refs[12:14], refs[14]
    c = GDN_CHUNK

    @pl.when(pl.program_id(1) == 0)
    def _():
        s_scr[...] = jnp.zeros_like(s_scr)

    pw = 2 * HEAD
    rblk = lax.broadcasted_iota(jnp.int32, (pw, pw), 0) // HEAD
    cblk = lax.broadcasted_iota(jnp.int32, (pw, pw), 1) // HEAD
    on_diag = rblk == cblk
    for d in range(2):
        u_ref, w_ref, qd_ref, kd_ref, att_ref, g_ref = ins[d::2]
        total = jnp.sum(g_ref[0], axis=0, keepdims=True)
        for pr in range(H_AB // 2):
            sl = slice(pr * pw, (pr + 1) * pw)
            s = s_scr[d, pr]
            res = _dot(jnp.concatenate([w_ref[0, :, sl], qd_ref[0, :, sl]], axis=0), s.astype(BF16))
            v_new = (u_ref[0, :, sl].astype(F32) - res[:c]).astype(BF16)
            o_refs[d][0, :, sl] = res[c:] + _dot(att_ref[0, :, pr * 2 * c:(pr + 1) * 2 * c], _block_diag(v_new, 2))
            gl = jnp.concatenate([jnp.broadcast_to(jnp.exp(total[:, 12 + 2 * pr + j:13 + 2 * pr + j]), (1, HEAD))
                                  for j in range(2)], axis=1)
            s_scr[d, pr] = gl * s + jnp.where(on_diag, _dot_tn(kd_ref[0, :, sl], v_new), 0.0)


def _step_block(chunk, seq, n_ctx, nb):
    n_c, n_l = n_ctx // chunk, seq // chunk
    off = nb * n_l

    def fn(b, d, i):
        j = i - n_c
        ctx_blk = off + b * n_c + (i if d == 0 else n_c - 1 - i)
        lat_blk = b * n_l + (j if d == 0 else n_l - 1 - j)
        return jnp.where(i < n_c, ctx_blk, lat_blk)

    return n_c + n_l, fn


def _gdn_scan(qkv, gates, seq, n_ctx, nb):
    t = qkv.shape[0]
    c = GDN_CHUNK
    w = H_AB * HEAD
    tri = _tri_tables(c)
    im = _gdn_inv_masks(c)
    tm = TOK_TILE
    sds = lambda width: jax.ShapeDtypeStruct((2, t, width), BF16)
    u, wv, qd, kd, att = pl.pallas_call(
        _gdn_chunk_kernel,
        grid=(t // tm,),
        in_specs=[pl.BlockSpec((tm, w), lambda i, col=col: (i, col)) for col in range(3)] + [
            pl.BlockSpec((2, tm, LANES), lambda i: (0, i, 0)),
            pl.BlockSpec((2, c, c), lambda i: (0, 0, 0)),
            pl.BlockSpec(im.shape, lambda i: (0, 0, 0))],
        out_specs=[pl.BlockSpec((2, tm, w), lambda i: (0, i, 0))] * 4
        + [pl.BlockSpec((2, tm, H_AB * c), lambda i: (0, i, 0))],
        out_shape=[sds(w)] * 4 + [sds(H_AB * c)],
        compiler_params=_cparams(("parallel",)),
        name="gdn_chunk",
    )(qkv, qkv, qkv, gates, tri, im)

    n_steps, blk = _step_block(c, seq, n_ctx, nb)
    in_specs, args = [], []
    for arr, width in ((u, w), (wv, w), (qd, w), (kd, w), (att, H_AB * c), (gates, LANES)):
        for d in range(2):
            in_specs.append(pl.BlockSpec((1, c, width), lambda b, i, d=d: (d, blk(b, d, i), 0)))
            args.append(arr)
    o_f, o_b = pl.pallas_call(
        _gdn_scan_kernel,
        grid=(nb, n_steps),
        in_specs=in_specs,
        out_specs=[pl.BlockSpec((1, c, w), lambda b, i, d=d: (0, blk(b, d, i), 0)) for d in range(2)],
        out_shape=[jax.ShapeDtypeStruct((1, t, w), F32)] * 2,
        scratch_shapes=[pltpu.VMEM((2, H_AB // 2, 2 * HEAD, 2 * HEAD), F32)],
        compiler_params=_cparams(("parallel", "arbitrary")),
        name="gdn_scan",
    )(*args)
    return o_f, o_b


def _gla_tables(c):
    idx = np.arange(c)
    sel, sgn, msk = [], [], []
    for d in range(2):
        sel_d, sgn_d, msk_d = [], [], []
        bs = c // 2
        while bs >= 1:
            pair = idx // (2 * bs)
            second = (idx // bs) % 2 == 1
            ref_row = pair * 2 * bs + (bs - 1 if d == 0 else bs)
            sel_d.append((idx[None, :] == ref_row[:, None]).astype(np.float32))
            qside = second if d == 0 else np.logical_not(second)
            sgn_d.append(np.where(qside, 1.0, -1.0)[:, None] * np.ones((1, HEAD)))
            same_pair = pair[:, None] == pair[None, :]
            msk_d.append(np.logical_and(same_pair, np.logical_and(qside[:, None], np.logical_not(qside)[None, :])))
            bs //= 2
        msk_d.append(np.eye(c, dtype=bool))
        sel.append(np.stack(sel_d)); sgn.append(np.stack(sgn_d)); msk.append(np.stack(msk_d).astype(np.float32))
    return (jnp.asarray(np.stack(sel), dtype=BF16), jnp.asarray(np.stack(sgn), dtype=F32),
            jnp.asarray(np.stack(msk), dtype=F32))


def _gla_kernel(q_ref, z_ref, v_ref, fb_ref, lb_ref, tri_ref, sel_ref, sgn_ref, msk_ref, s0_ref, _o_in,
                o_ref, sn_ref, s_scr):
    i = pl.program_id(2)

    @pl.when(i == 0)
    def _():
        s_scr[...] = s0_ref[0, 0]

    tri = tri_ref[0].astype(BF16)
    n_lvl = sel_ref.shape[1]
    lb = lb_ref[0]
    z = z_ref[...] + fb_ref[0]
    sig_pos, sig_neg = _sigmoid_pair(z)
    g_all = jnp.log(lb + (1.0 - lb) * sig_pos)
    key_all = (1.0 - lb) * sig_neg
    q_all = _silu(q_ref[...])
    gc_all = _dot_exact_lhs(tri, g_all)
    total_all = jnp.sum(g_all, axis=0, keepdims=True)
    gc_b = gc_all.astype(BF16)
    refs = [_dot(sel_ref[0, l], gc_b) for l in range(n_lvl)]
    for h in range(H_C):
        sl = slice(h * HEAD, (h + 1) * HEAD)
        q, k, v = q_all[:, sl], key_all[:, sl], v_ref[:, sl].astype(BF16)
        gc, tot = gc_all[:, sl], total_all[:, sl]
        att = msk_ref[0, n_lvl] * _dot_nt(q.astype(BF16), k.astype(BF16))
        for l in range(n_lvl):
            e = jnp.exp(sgn_ref[0, l] * (gc - refs[l][:, sl]))
            att = att + msk_ref[0, l] * _dot_nt((q * e).astype(BF16), (k * e).astype(BF16))
        st = s_scr[h]
        o_ref[0, :, sl] = _dot_nt((q * jnp.exp(gc)).astype(BF16), st.astype(BF16)) + _dot(att.astype(BF16), v)
        k_dec = (k * jnp.exp(tot - gc)).astype(BF16)
        s_scr[h] = jnp.exp(tot) * st + _dot_tn(v, k_dec)

    @pl.when(i == pl.num_programs(2) - 1)
    def _():
        sn_ref[0, 0] = s_scr[...]


def _gla_scan(p, f_bias, lb, seq, n_ctx, nb, column_major):
    t, pw = p.shape
    c = GLA_CHUNK
    w = H_C * HEAD
    tri = _tri_tables(c)
    sel, sgn, msk = _gla_tables(c)
    s = jnp.zeros((nb, 2, H_C, HEAD, HEAD), F32)
    o = None
    parts = pw // w
    fb3, lb3 = f_bias.reshape(2, 1, w), lb.reshape(2, 1, w)
    for seg in ("ctx", "lat"):
        n, rb = _row_block(seg, c, seq, n_ctx, nb)
        if seg == "lat" and column_major:
            assert seq == GRID_W * c, "column-major scan expects one chunk per grid column"
            src = p.reshape(t // c, c * pw)
            cm = lambda b, d, i: i + d * (n - 1 - 2 * i)
            dspec = lambda col: pl.BlockSpec((c, w), lambda b, d, i: (b, cm(b, d, i) * parts + col(d)))
            o_spec = pl.BlockSpec((1, c, w), lambda b, d, i: (d, b, cm(b, d, i)))
            o_shape = (2, t // c, c * w)
        else:
            src = p
            dspec = lambda col: pl.BlockSpec((c, w), lambda b, d, i: (rb(b, d, i), col(d)))
            o_spec = pl.BlockSpec((1, c, w), lambda b, d, i: (d, rb(b, d, i), 0))
            o_shape = (2, t, w)
        st_spec = pl.BlockSpec((1, 1, H_C, HEAD, HEAD), lambda b, d, i: (b, d, 0, 0, 0))
        dir_spec = lambda shp: pl.BlockSpec((1,) + shp, lambda b, d, i: (d,) + (0,) * len(shp))
        in_specs = [dspec(lambda d: 0), dspec(lambda d: 1 + d), dspec(lambda d: 3),
                    dir_spec((1, w)), dir_spec((1, w)), dir_spec((c, c)),
                    dir_spec(sel.shape[1:]), dir_spec(sgn.shape[1:]), dir_spec(msk.shape[1:]), st_spec]
        args = [src, src, src, fb3, lb3, tri, sel, sgn, msk, s]
        aliases = {}
        if o is None:
            in_specs.append(pl.BlockSpec((SUBLANES, LANES), lambda b, d, i: (0, 0)))
            args.append(jnp.zeros((SUBLANES, LANES), F32))
        else:
            in_specs.append(pl.BlockSpec(memory_space=pl.ANY))
            args.append(o.reshape(o_shape))
            aliases = {10: 0}
        o, s = pl.pallas_call(
            _gla_kernel,
            grid=(nb, 2, n),
            in_specs=in_specs,
            out_specs=[o_spec, st_spec],
            out_shape=[jax.ShapeDtypeStruct(o_shape, F32), jax.ShapeDtypeStruct(s.shape, F32)],
            scratch_shapes=[pltpu.VMEM((H_C, HEAD, HEAD), F32)],
            input_output_aliases=aliases,
            compiler_params=_cparams(("parallel", "parallel", "arbitrary")),
            name="gla_" + seg,
        )(*args)
    return o.reshape(2, t, w)


def _route(scores_t, bias):
    s = [scores_t[e:e + 1, :] for e in range(N_EXPERTS)]
    sel = [s[e] + bias[e:e + 1, :] for e in range(N_EXPERTS)]
    per = N_EXPERTS // N_GROUPS
    gscore = []
    for grp in range(N_GROUPS):
        v = sel[grp * per:(grp + 1) * per]
        best = None
        for a in range(per):
            for b in range(a + 1, per):
                pair = v[a] + v[b]
                best = pair if best is None else jnp.maximum(best, pair)
        gscore.append(best)
    best_g = jnp.zeros_like(gscore[0], dtype=jnp.int32)
    best_v = gscore[0]
    for grp in range(1, N_GROUPS):
        better = gscore[grp] > best_v
        best_g = jnp.where(better, grp, best_g)
        best_v = jnp.where(better, gscore[grp], best_v)
    masked = [jnp.where(best_g == e // per, sel[e], -jnp.inf) for e in range(N_EXPERTS)]
    idx1 = jnp.zeros_like(best_g)
    v1 = masked[0]
    for e in range(1, N_EXPERTS):
        better = masked[e] > v1
        idx1 = jnp.where(better, e, idx1)
        v1 = jnp.where(better, masked[e], v1)
    masked2 = [jnp.where(idx1 == e, -jnp.inf, masked[e]) for e in range(N_EXPERTS)]
    idx2 = jnp.zeros_like(best_g)
    v2 = masked2[0]
    for e in range(1, N_EXPERTS):
        better = masked2[e] > v2
        idx2 = jnp.where(better, e, idx2)
        v2 = jnp.where(better, masked2[e], v2)
    w1 = sum(jnp.where(idx1 == e, s[e], 0.0) for e in range(N_EXPERTS))
    w2 = sum(jnp.where(idx2 == e, s[e], 0.0) for e in range(N_EXPERTS))
    tot = w1 + w2
    w1, w2 = w1 / tot, w2 / tot
    return [jnp.where(idx1 == e, w1, 0.0) + jnp.where(idx2 == e, w2, 0.0) for e in range(N_EXPERTS)]


def _outproj_kernel(x_ref, of0, ob0, gt0, of1, ob1, gt1, hn_ref, w_ref, g1_ref, sc_ref, sh_ref, nf_ref,
                    wr_ref, rb_ref, xo_ref, h_ref, gate_ref, *, first_half_sigmoid):
    feats = []
    for half, (of, ob, gt) in enumerate(((of0, ob0, gt0), (of1, ob1, gt1))):
        o = of[0] + ob[0]
        gate = gt[...]
        for h in range(o.shape[1] // HEAD):
            sl = slice(h * HEAD, (h + 1) * HEAD)
            t = o[:, sl]
            y = t * lax.rsqrt(jnp.mean(t * t, axis=-1, keepdims=True) + EPS)
            y = y * hn_ref[:, half * o.shape[1] + h * HEAD: half * o.shape[1] + (h + 1) * HEAD]
            gz = gate[:, sl]
            act = _sigmoid_pair(gz)[0] if (half == 0 and first_half_sigmoid) else _silu(gz)
            feats.append((y * act).astype(BF16))
    feats = jnp.concatenate(feats, axis=1)
    xn = x_ref[...] + g1_ref[0] * _dot(feats, w_ref[...])
    xo_ref[...] = xn
    y = xn * lax.rsqrt(jnp.mean(xn * xn, axis=-1, keepdims=True) + EPS) * nf_ref[...]
    hl = y * (1.0 + sc_ref[0]) + sh_ref[0]
    h_ref[...] = hl.astype(BF16)
    logits = _dot3(hl, wr_ref[...])
    scores_t = _sigmoid_pair(logits)[0].T
    rows = _route(scores_t, rb_ref[...])
    pad = jnp.zeros((LANES - N_EXPERTS, scores_t.shape[1]), F32)
    gate_ref[...] = jnp.concatenate(rows + [pad], axis=0).T


def _outproj(xs, o_halves, gate_src, gate_cols, hn_gain, w_out, mod, layer, nf_gain, wr_pad, rb_col,
             seq, nb, n_rows, first_half_sigmoid):
    t, d = xs.shape
    tm = TOK_TILE
    half = d // 2
    base = layer * SUBLANES * 6

    def mod_spec(which):
        return pl.BlockSpec((1, 1, d), lambda i: (base + _mod_row(i, tm, seq, nb) * 6 + which, 0, 0))

    in_specs = [pl.BlockSpec((tm, d), lambda i: (i, 0))]
    args = [xs]
    for ((arr_f, dir_f), (arr_b, dir_b), col), gcol in zip(o_halves, gate_cols):
        in_specs += [pl.BlockSpec((1, tm, half), lambda i, col=col, dd=dir_f: (dd, i, col)),
                     pl.BlockSpec((1, tm, half), lambda i, col=col, dd=dir_b: (dd, i, col)),
                     pl.BlockSpec((tm, half), lambda i, gcol=gcol: (i, gcol))]
        args += [arr_f, arr_b, gate_src]
    in_specs += [pl.BlockSpec((1, d), lambda i: (0, 0)),
                 pl.BlockSpec((d, d), lambda i: (0, 0)),
                 mod_spec(2), mod_spec(4), mod_spec(3),
                 pl.BlockSpec((1, d), lambda i: (0, 0)),
                 pl.BlockSpec((d, LANES), lambda i: (0, 0)),
                 pl.BlockSpec((N_EXPERTS, 1), lambda i: (0, 0))]
    args += [hn_gain.reshape(1, d), w_out, mod, mod, mod, nf_gain.reshape(1, d), wr_pad, rb_col]
    return pl.pallas_call(
        functools.partial(_outproj_kernel, first_half_sigmoid=first_half_sigmoid),
        grid=(n_rows // tm,),
        in_specs=in_specs,
        out_specs=[pl.BlockSpec((tm, d), lambda i: (i, 0)),
                   pl.BlockSpec((tm, d), lambda i: (i, 0)),
                   pl.BlockSpec((tm, LANES), lambda i: (i, 0))],
        out_shape=[jax.ShapeDtypeStruct((t, d), F32), jax.ShapeDtypeStruct((t, d), BF16),
                   jax.ShapeDtypeStruct((t, LANES), F32)],
        compiler_params=_cparams(("parallel",)),
        name="outproj",
    )(*args)


def _moe_kernel(x_ref, h_ref, gate_ref, w1_ref, w3_ref, w2_ref, g2_ref, nfin_ref, o_ref, acc_ref, *, final_norm):
    e = pl.program_id(1)

    @pl.when(e == 0)
    def _():
        acc_ref[...] = jnp.zeros_like(acc_ref)

    h = h_ref[...]
    lane = lax.broadcasted_iota(jnp.int32, gate_ref.shape, 1)
    gcol = jnp.sum(jnp.where(lane == e, gate_ref[...], 0.0), axis=-1, keepdims=True)
    a = _silu(_dot(h, w1_ref[0])) * _dot(h, w3_ref[0]) * gcol
    acc_ref[...] += _dot(a.astype(BF16), w2_ref[0])

    @pl.when(e == pl.num_programs(1) - 1)
    def _():
        y = x_ref[...] + g2_ref[0] * acc_ref[...]
        if final_norm:
            y = y * lax.rsqrt(jnp.mean(y * y, axis=-1, keepdims=True) + EPS) * nfin_ref[...]
        o_ref[...] = y


def _moe(xs, h, gate, w1, w3, w2, mod, layer, nfin, seq, nb, n_rows, final_norm):
    t, d = xs.shape
    tm = 1024
    e, _, f = w1.shape
    base = layer * SUBLANES * 6
    out_rows = n_rows if final_norm else t
    return pl.pallas_call(
        functools.partial(_moe_kernel, final_norm=final_norm),
        grid=(n_rows // tm, e),
        in_specs=[pl.BlockSpec((tm, d), lambda i, j: (i, 0)),
                  pl.BlockSpec((tm, d), lambda i, j: (i, 0)),
                  pl.BlockSpec((tm, LANES), lambda i, j: (i, 0)),
                  pl.BlockSpec((1, d, f), lambda i, j: (j, 0, 0)),
                  pl.BlockSpec((1, d, f), lambda i, j: (j, 0, 0)),
                  pl.BlockSpec((1, f, d), lambda i, j: (j, 0, 0)),
                  pl.BlockSpec((1, 1, d), lambda i, j: (base + _mod_row(i, tm, seq, nb) * 6 + 5, 0, 0)),
                  pl.BlockSpec((1, d), lambda i, j: (0, 0))],
        out_specs=pl.BlockSpec((tm, d), lambda i, j: (i, 0)),
        out_shape=jax.ShapeDtypeStruct((out_rows, d), F32),
        scratch_shapes=[pltpu.VMEM((tm, d), F32)],
        compiler_params=_cparams(("parallel", "arbitrary")),
        name="moe",
    )(xs, h, gate, w1, w3, w2, mod, nfin.reshape(1, d))


def kernel(x, c, ctx, c_ctx, w_mod, b_mod, norm_mix, norm_ffn, norm_final, ab_w_in, ab_i_bias, ab_f_bias,
           ab_conv, ab_a_log, ab_dt_bias, ab_norm_a, ab_norm_b, ab_w_out, c_w_in, c_f_bias, c_lb_raw, c_norm,
           c_w_out, w_router, router_bias, w1, w3, w2):
    nb, seq, d = x.shape
    n_ctx = ctx.shape[1]
    depth = w_mod.shape[0]
    n_lat = nb * seq
    assert nb + 1 <= SUBLANES and seq % 1024 == 0 and (nb * n_ctx) % 1024 == 0 and n_ctx % 256 == 0

    xs = jnp.concatenate([x.reshape(n_lat, d), ctx.reshape(nb * n_ctx, d)], axis=0)
    cpad = jnp.zeros((SUBLANES, d), F32).at[:nb].set(c).at[nb].set(c_ctx)
    mod = _modulation(cpad, w_mod, b_mod).reshape(depth * SUBLANES * 6, 1, d)

    lb_p = jax.nn.softmax(c_lb_raw.astype(F32), axis=0)
    lb_all = jnp.cumsum(lb_p, axis=0) - lb_p[0:1]

    wr_pad = jnp.zeros((d, LANES), F32).at[:, :N_EXPERTS].set(w_router)
    rb_col = router_bias.astype(F32).reshape(N_EXPERTS, 1)
    w_main = H_AB * HEAD * 8
    hw = H_AB * HEAD

    out = None
    for layer in range(depth):
        last = layer == depth - 1
        j = layer // 2
        if layer % 2 == 0:
            w_in = ab_w_in[j]
            gw = w_in[:, w_main:].reshape(d, 4, 2, H_AB)
            wg = jnp.zeros((2, d, LANES), F32).at[:, :, :4 * H_AB].set(
                jnp.transpose(gw, (2, 0, 1, 3)).reshape(2, d, 4 * H_AB)).astype(BF16)
            zeros = jnp.zeros((2, H_AB), F32)
            bias = jnp.concatenate([ab_i_bias[j], ab_f_bias[j], zeros, ab_dt_bias[j]], axis=1)
            alog = jnp.concatenate([zeros, zeros, zeros, ab_a_log[j]], axis=1)
            gpar = jnp.zeros((2, SUBLANES, LANES), F32).at[:, 0, :4 * H_AB].set(bias).at[:, 1, :4 * H_AB].set(alog)
            p, gates = _inproj(xs, norm_mix[layer], mod, layer, w_in[:, :w_main].astype(BF16), seq, nb, wg, gpar)
            h_a = _mlstm_scan(p, gates, seq, n_ctx, nb)
            qkv = _gdn_prep(p, ab_conv[j], seq, n_ctx, nb)
            ob_f, ob_b = _gdn_scan(qkv, gates, seq, n_ctx, nb)
            o_halves = (((h_a, 0), (h_a, 1), 0), ((ob_f, 0), (ob_b, 0), 0))
            gate_cols = (3, 7)
            hn_gain = jnp.concatenate([ab_norm_a[j], ab_norm_b[j]])
            w_out = ab_w_out[j]
        else:
            (p,) = _inproj(xs, norm_mix[layer], mod, layer, c_w_in[j].astype(BF16), seq, nb)
            o_c = _gla_scan(p, c_f_bias[j], lb_all[j], seq, n_ctx, nb, column_major=(j % 2 == 1))
            o_halves = (((o_c, 0), (o_c, 1), 0), ((o_c, 0), (o_c, 1), 1))
            gate_cols = (8, 9)
            hn_gain = c_norm[j]
            w_out = c_w_out[j]
        n_rows = n_lat if last else n_lat + nb * n_ctx
        xs, h, gate = _outproj(xs, o_halves, p, gate_cols, hn_gain, w_out.astype(BF16), mod, layer,
                               norm_ffn[layer], wr_pad, rb_col, seq, nb, n_rows, layer % 2 == 0)
        res = _moe(xs, h, gate, w1[layer].astype(BF16), w3[layer].astype(BF16), w2[layer].astype(BF16),
                   mod, layer, norm_final, seq, nb, n_rows, last)
        if last:
            out = res
        else:
            xs = res
    return out.reshape(nb, seq, d)
```

```python
import functools

import numpy as np
import jax
import jax.numpy as jnp
from jax import lax
from jax.experimental import pallas as pl
from jax.experimental.pallas import tpu as pltpu

F32 = jnp.float32
BF16 = jnp.bfloat16

EPS = 1e-6
GRID_W = 64
HEAD = 128
H_AB = 4
H_C = 8
CONV_K = 5
N_EXPERTS = 16
N_GROUPS = 4
LANES = 128
SUBLANES = 8
VMEM_LIMIT = 56 * 1024 * 1024

TOK_TILE = 512
MLSTM_CHUNK = 128
GDN_CHUNK = 64
GLA_CHUNK = 128
GDN_INV_BLOCK = 16


def _cparams(sem):
    return pltpu.CompilerParams(dimension_semantics=sem, vmem_limit_bytes=VMEM_LIMIT)


def _dot(a, b):
    return jnp.dot(a, b, preferred_element_type=F32)


def _dot_nt(a, b):
    return lax.dot_general(a, b, (((1,), (1,)), ((), ())), preferred_element_type=F32)


def _dot_tn(a, b):
    return lax.dot_general(a, b, (((0,), (0,)), ((), ())), preferred_element_type=F32)


def _split2(x):
    hi = x.astype(BF16)
    lo = (x - hi.astype(F32)).astype(BF16)
    return hi, lo


def _split3(x):
    hi = x.astype(BF16)
    r = x - hi.astype(F32)
    mid = r.astype(BF16)
    lo = (r - mid.astype(F32)).astype(BF16)
    return hi, mid, lo


def _dot_exact_lhs(m_bf16, x):
    hi, mid, lo = _split3(x)
    return _dot(m_bf16, hi) + _dot(m_bf16, mid) + _dot(m_bf16, lo)


def _dot3(a, b):
    ah, al = _split2(a)
    bh, bl = _split2(b)
    return _dot(ah, bh) + _dot(ah, bl) + _dot(al, bh)


def _dot_nt3(a, b):
    ah, al = _split2(a)
    bh, bl = _split2(b)
    return _dot_nt(ah, bh) + _dot_nt(ah, bl) + _dot_nt(al, bh)


def _sigmoid_pair(z):
    e = jnp.exp(-jnp.abs(z))
    r = 1.0 / (1.0 + e)
    er = e * r
    pos = z >= 0
    return jnp.where(pos, r, er), jnp.where(pos, er, r)


def _silu(z):
    return z * _sigmoid_pair(z)[0]


def _mod_kernel(c_ref, w_ref, b_ref, o_ref):
    s = _silu(c_ref[...]).astype(BF16)
    o_ref[0] = _dot(s, w_ref[0].astype(BF16)) + b_ref[0]


def _modulation(cpad, w_mod, b_mod):
    depth, d, n = w_mod.shape
    tn = 1536
    return pl.pallas_call(
        _mod_kernel,
        grid=(depth, n // tn),
        in_specs=[pl.BlockSpec((SUBLANES, d), lambda l, j: (0, 0)),
                  pl.BlockSpec((1, d, tn), lambda l, j: (l, 0, j)),
                  pl.BlockSpec((1, 1, tn), lambda l, j: (l, 0, j))],
        out_specs=pl.BlockSpec((1, SUBLANES, tn), lambda l, j: (l, 0, j)),
        out_shape=jax.ShapeDtypeStruct((depth, SUBLANES, n), F32),
        compiler_params=_cparams(("parallel", "parallel")),
        name="modulation",
    )(cpad, w_mod, b_mod.reshape(depth, 1, n))


def _gate_activations(raw, par):
    y = raw + par[0:1, :]
    e = jnp.exp(-jnp.abs(y))
    sp = jnp.log1p(e)
    lsig = jnp.minimum(y, 0.0) - sp
    r = 1.0 / (1.0 + e)
    sig = jnp.where(y >= 0, r, e * r)
    gdec = -jnp.exp(par[1:2, :]) * (jnp.maximum(y, 0.0) + sp)
    lane = lax.broadcasted_iota(jnp.int32, raw.shape, 1)
    return jnp.where(lane < 4, y, jnp.where(lane < 8, lsig, jnp.where(lane < 12, sig, gdec)))


def _inproj_kernel(x_ref, gain_ref, sc_ref, sh_ref, w_ref, *rest, with_gates):
    if with_gates:
        wg_ref, gpar_ref, p_ref, g_ref, u_ref = rest
    else:
        p_ref, u_ref = rest

    @pl.when(pl.program_id(1) == 0)
    def _():
        x = x_ref[...]
        y = x * lax.rsqrt(jnp.mean(x * x, axis=-1, keepdims=True) + EPS) * gain_ref[...]
        u = (y * (1.0 + sc_ref[0]) + sh_ref[0]).astype(BF16)
        u_ref[...] = u
        if with_gates:
            for d in range(2):
                g_ref[d] = _gate_activations(_dot(u, wg_ref[d]), gpar_ref[d])

    p_ref[...] = _dot(u_ref[...], w_ref[...])


def _mod_row(i, tm, seq, nb):
    return jnp.minimum((i * tm) // seq, nb)


def _inproj(xs, gain, mod, layer, w, seq, nb, wg=None, gpar=None):
    t, d = xs.shape
    p = w.shape[1]
    tm, tn = TOK_TILE, 1024
    with_gates = wg is not None
    base = layer * SUBLANES * 6

    def mod_spec(which):
        return pl.BlockSpec((1, 1, d), lambda i, j: (base + _mod_row(i, tm, seq, nb) * 6 + which, 0, 0))

    in_specs = [pl.BlockSpec((tm, d), lambda i, j: (i, 0)),
                pl.BlockSpec((1, d), lambda i, j: (0, 0)),
                mod_spec(1), mod_spec(0),
                pl.BlockSpec((d, tn), lambda i, j: (0, j))]
    args = [xs, gain.reshape(1, d), mod, mod, w]
    out_specs = [pl.BlockSpec((tm, tn), lambda i, j: (i, j))]
    out_shape = [jax.ShapeDtypeStruct((t, p), F32)]
    if with_gates:
        in_specs += [pl.BlockSpec((2, d, LANES), lambda i, j: (0, 0, 0)),
                     pl.BlockSpec((2, SUBLANES, LANES), lambda i, j: (0, 0, 0))]
        args += [wg, gpar]
        out_specs.append(pl.BlockSpec((2, tm, LANES), lambda i, j: (0, i, 0)))
        out_shape.append(jax.ShapeDtypeStruct((2, t, LANES), F32))
    return pl.pallas_call(
        functools.partial(_inproj_kernel, with_gates=with_gates),
        grid=(t // tm, p // tn),
        in_specs=in_specs, out_specs=out_specs, out_shape=out_shape,
        scratch_shapes=[pltpu.VMEM((tm, d), BF16)],
        compiler_params=_cparams(("parallel", "arbitrary")),
        name="inproj",
    )(*args)


def _gdn_prep_kernel(prev_ref, x_ref, next_ref, w_ref, o_ref, *, blocks_per_seq, n_lat_blocks, scale):
    i = pl.program_id(0)
    tm = x_ref.shape[0]
    in_lat = i < n_lat_blocks
    first = jnp.logical_or(jnp.logical_not(in_lat), i % blocks_per_seq == 0)
    last = jnp.logical_or(jnp.logical_not(in_lat), i % blocks_per_seq == blocks_per_seq - 1)
    prev = jnp.where(first, 0.0, prev_ref[...])
    nxt = jnp.where(last, 0.0, next_ref[...])
    xe = jnp.concatenate([prev, x_ref[...], nxt], axis=0)
    n = xe.shape[0]
    pad = (CONV_K - 1) // 2
    acc = None
    for j in range(CONV_K):
        shift = (pad - j) % n
        xr = xe if shift == 0 else pltpu.roll(xe, shift, 0)
        term = xr[SUBLANES:SUBLANES + tm] * w_ref[j:j + 1, :]
        acc = term if acc is None else acc + term
    y = _silu(acc)
    part = pl.program_id(1)
    for h in range(H_AB):
        sl = slice(h * HEAD, (h + 1) * HEAD)
        t = y[:, sl]
        inv_norm = lax.rsqrt(jnp.sum(t * t, axis=-1, keepdims=True) + EPS)
        factor = jnp.where(part == 0, inv_norm * scale, jnp.where(part == 1, inv_norm, 1.0))
        o_ref[:, sl] = t * factor


def _gdn_prep(p, conv_w, seq, n_ctx, nb):
    t = p.shape[0]
    tm = 256
    w = H_AB * HEAD
    first_col = 4
    wpad = jnp.zeros((SUBLANES, 3 * w), F32).at[:CONV_K].set(conv_w)
    hb = tm // SUBLANES
    nblk8 = t // SUBLANES
    return pl.pallas_call(
        functools.partial(_gdn_prep_kernel, blocks_per_seq=seq // tm, n_lat_blocks=nb * seq // tm,
                          scale=HEAD ** -0.5),
        grid=(t // tm, 3),
        in_specs=[pl.BlockSpec((SUBLANES, w), lambda i, c: (jnp.maximum(i * hb - 1, 0), first_col + c)),
                  pl.BlockSpec((tm, w), lambda i, c: (i, first_col + c)),
                  pl.BlockSpec((SUBLANES, w), lambda i, c: (jnp.minimum((i + 1) * hb, nblk8 - 1), first_col + c)),
                  pl.BlockSpec((SUBLANES, w), lambda i, c: (0, c))],
        out_specs=pl.BlockSpec((tm, w), lambda i, c: (i, c)),
        out_shape=jax.ShapeDtypeStruct((t, 3 * w), F32),
        compiler_params=_cparams(("parallel", "parallel")),
        name="gdn_prep",
    )(p, p, p, wpad)


def _tri_tables(c):
    idx = np.arange(c)
    fwd = (idx[None, :] <= idx[:, None]).astype(np.float32)
    return jnp.asarray(np.stack([fwd, fwd.T]))


def _to_cm_kernel(x3_ref, x2_ref, o_ref, *, n_lat_tiles):
    i = pl.program_id(0)
    rows = x3_ref.shape[0]

    @pl.when(i < n_lat_tiles)
    def _():
        for wl in range(SUBLANES):
            o_ref[wl * rows:(wl + 1) * rows, :] = x3_ref[:, wl, :]

    @pl.when(i >= n_lat_tiles)
    def _():
        o_ref[...] = x2_ref[...]


def _from_cm_kernel(x_ref, o_ref):
    rows = o_ref.shape[0]
    for wl in range(SUBLANES):
        o_ref[:, wl, :] = x_ref[wl * rows:(wl + 1) * rows, :]


def _to_column_major(xs, n_lat, seq):
    t, d = xs.shape
    rows = seq // GRID_W
    tile = rows * SUBLANES
    assert rows % SUBLANES == 0 and n_lat % tile == 0 and (t - n_lat) % tile == 0
    n_lat_tiles = n_lat // tile
    per_b = GRID_W // SUBLANES
    lat = lambda i: jnp.minimum(i, n_lat_tiles - 1)
    return pl.pallas_call(
        functools.partial(_to_cm_kernel, n_lat_tiles=n_lat_tiles),
        grid=(t // tile,),
        in_specs=[pl.BlockSpec((rows, SUBLANES, d), lambda i: (lat(i) // per_b, lat(i) % per_b, 0)),
                  pl.BlockSpec((tile, d), lambda i: (jnp.maximum(i, n_lat_tiles - 1), 0))],
        out_specs=pl.BlockSpec((tile, d), lambda i: (i, 0)),
        out_shape=jax.ShapeDtypeStruct((t, d), xs.dtype),
        compiler_params=_cparams(("parallel",)),
        name="to_column_major",
    )(xs.reshape(t // GRID_W, GRID_W, d), xs)


def _from_column_major(y, seq):
    n_lat, d = y.shape
    rows = seq // GRID_W
    tile = rows * SUBLANES
    per_b = GRID_W // SUBLANES
    out = pl.pallas_call(
        _from_cm_kernel,
        grid=(n_lat // tile,),
        in_specs=[pl.BlockSpec((tile, d), lambda i: (i, 0))],
        out_specs=pl.BlockSpec((rows, SUBLANES, d), lambda i: (i // per_b, i % per_b, 0)),
        out_shape=jax.ShapeDtypeStruct((n_lat // GRID_W, GRID_W, d), y.dtype),
        compiler_params=_cparams(("parallel",)),
        name="from_column_major",
    )(y)
    return out.reshape(n_lat, d)


def _mlstm_kernel(qf, kf, vf, qb, kb, vb, gf, gb, tri_ref, hf_ref, hb_ref, s_scr, m_scr, *, scale):
    @pl.when(pl.program_id(1) == 0)
    def _():
        s_scr[...] = jnp.zeros_like(s_scr)
        m_scr[...] = jnp.zeros_like(m_scr)

    c = gf.shape[1]
    ones_col = (lax.broadcasted_iota(jnp.int32, (c, HEAD), 1) == 0).astype(BF16)
    qkv = ((qf, kf, vf), (qb, kb, vb))
    h_out = (hf_ref, hb_ref)
    g = [gf[0], gb[0]]
    mask = [tri_ref[d] > 0.5 for d in range(2)]
    b = [_dot_exact_lhs(tri_ref[d].astype(BF16), g[d]) for d in range(2)]
    total = [jnp.sum(x, axis=0, keepdims=True) for x in g]
    g_t = [x.T for x in g]
    b_t = [x.T for x in b]
    chains = [(d, h) for d in range(2) for h in range(H_AB)]
    sl = [slice(h * HEAD, (h + 1) * HEAD) for d, h in chains]
    q = [qkv[d][0][:, sl[i]].astype(BF16) for i, (d, h) in enumerate(chains)]
    k = [qkv[d][1][:, sl[i]] * scale for i, (d, h) in enumerate(chains)]
    v_aug = [jnp.concatenate([qkv[d][2][:, sl[i]].astype(BF16), ones_col], axis=1) for i, (d, h) in enumerate(chains)]
    s = [s_scr[d, h] for d, h in chains]
    qk = [_dot_nt(q[i], k[i].astype(BF16)) for i in range(len(chains))]
    qs = [_dot(q[i], s[i].astype(BF16)) for i in range(len(chains))]
    w_intra, w_inter, m_ts, colvs, m_sts, tots = [], [], [], [], [], []
    for i, (d, h) in enumerate(chains):
        li_c, li_r = g[d][:, h:h + 1], g_t[d][h:h + 1, :]
        b_c, b_r = b[d][:, 4 + h:5 + h], b_t[d][4 + h:5 + h, :]
        tot = total[d][:, 4 + h:5 + h]
        m_st = m_scr[d, h][0:1, 0:1]
        d_log = jnp.where(mask[d], b_c - b_r + li_r, -jnp.inf)
        m_inter = b_c + m_st
        m_t = jnp.maximum(m_inter, jnp.max(d_log, axis=-1, keepdims=True))
        w_inter.append(jnp.exp(m_inter - m_t))
        w_intra.append((jnp.exp(d_log - m_t) * qk[i]).astype(BF16))
        m_ts.append(m_t); colvs.append(tot - b_c + li_c); m_sts.append(m_st); tots.append(tot)
    intra = [_dot(w_intra[i], v_aug[i]) for i in range(len(chains))]
    m_new = [jnp.maximum(tots[i] + m_sts[i], jnp.max(colvs[i], axis=0, keepdims=True)) for i in range(len(chains))]
    upd = [_dot_tn((k[i] * jnp.exp(colvs[i] - m_new[i])).astype(BF16), v_aug[i]) for i in range(len(chains))]
    for i, (d, h) in enumerate(chains):
        num = w_inter[i] * qs[i] + intra[i]
        den = num[:, HEAD:HEAD + 1]
        h_out[d][0, :, sl[i]] = num[:, :HEAD] / jnp.maximum(jnp.abs(den), jnp.exp(-m_ts[i]))
        s_scr[d, h] = jnp.exp(tots[i] + m_sts[i] - m_new[i]) * s[i] + upd[i]
        m_scr[d, h] = jnp.broadcast_to(m_new[i], (SUBLANES, LANES))


def _step_block(chunk, seq, n_ctx, nb):
    n_c, n_l = n_ctx // chunk, seq // chunk
    off = nb * n_l

    def fn(b, d, i):
        j = i - n_c
        ctx_blk = off + b * n_c + (i if d == 0 else n_c - 1 - i)
        lat_blk = b * n_l + (j if d == 0 else n_l - 1 - j)
        return jnp.where(i < n_c, ctx_blk, lat_blk)

    return n_c + n_l, fn


def _mlstm_scan(p, gates, seq, n_ctx, nb):
    t = p.shape[0]
    c = MLSTM_CHUNK
    w = H_AB * HEAD
    n_steps, blk = _step_block(c, seq, n_ctx, nb)
    in_specs = [pl.BlockSpec((c, w), lambda b, i, d=d, col=col: (blk(b, d, i), col)) for d in range(2) for col in range(3)]
    in_specs += [pl.BlockSpec((1, c, LANES), lambda b, i, d=d: (d, blk(b, d, i), 0)) for d in range(2)]
    in_specs.append(pl.BlockSpec((2, c, c), lambda b, i: (0, 0, 0)))
    return pl.pallas_call(
        functools.partial(_mlstm_kernel, scale=HEAD ** -0.5),
        grid=(nb, n_steps),
        in_specs=in_specs,
        out_specs=[pl.BlockSpec((1, c, w), lambda b, i, d=d: (0, blk(b, d, i), 0)) for d in range(2)],
        out_shape=[jax.ShapeDtypeStruct((1, t, w), F32)] * 2,
        scratch_shapes=[pltpu.VMEM((2, H_AB, HEAD, 2 * HEAD), F32), pltpu.VMEM((2, H_AB, SUBLANES, LANES), F32)],
        compiler_params=_cparams(("parallel", "arbitrary")),
        name="mlstm_scan",
    )(p, p, p, p, p, p, gates, gates, _tri_tables(c))


def _gdn_inv_masks(c):
    idx = np.arange(c)
    blk = lambda n: (idx[:, None] // n) == (idx[None, :] // n)
    levels = [blk(GDN_INV_BLOCK)]
    n = GDN_INV_BLOCK
    while n < c:
        levels.append(np.logical_and(blk(2 * n), np.logical_not(blk(n))))
        n *= 2
    levels = np.stack(levels).astype(np.float32)
    return jnp.asarray(np.tile(levels, (1, 1, H_AB)))


def _block_diag(x, n_blocks):
    blk = lax.broadcasted_iota(jnp.int32, x.shape, 1) // (x.shape[1] // n_blocks)
    return jnp.concatenate([jnp.where(blk == j, x, jnp.zeros_like(x)) for j in range(n_blocks)], axis=0)


def _heads_to_lanes(x, first, width):
    return jnp.concatenate([jnp.broadcast_to(x[:, first + h:first + h + 1], (x.shape[0], width))
                            for h in range(H_AB)], axis=1)


def _unit_tri_inverse(a_list, eye, inv_masks):
    prod = lambda xs, ys: [_dot(x.astype(BF16), _block_diag(y.astype(BF16), H_AB)) for x, y in zip(xs, ys)]
    pw = [a * inv_masks[0] for a in a_list]
    inv = [eye - p for p in pw]
    n = 2
    while n < GDN_INV_BLOCK:
        pw = prod(pw, pw)
        inv = [i + t for i, t in zip(inv, prod(inv, pw))]
        n *= 2
    for lvl in range(1, inv_masks.shape[0]):
        t = prod(prod(inv, [a * inv_masks[lvl] for a in a_list]), inv)
        inv = [i - x for i, x in zip(inv, t)]
    return inv


def _gdn_chunk_kernel(q_ref, k_ref, v_ref, g_ref, tri_ref, im_ref, u_ref, w_ref, qd_ref, kd_ref, att_ref):
    c = GDN_CHUNK
    n_chunks = q_ref.shape[0] // c
    inv_masks = im_ref[...]
    row = lax.broadcasted_iota(jnp.int32, (c, H_AB * c), 0)
    col = lax.broadcasted_iota(jnp.int32, (c, H_AB * c), 1) % c
    eye_b = row == col
    eye = eye_b.astype(F32)
    masks = [col <= row, col >= row]
    stricts = [col < row, col > row]
    tris = [tri_ref[d].astype(BF16) for d in range(2)]
    ones = jnp.ones((c, c), BF16)
    chunks = range(n_chunks)
    rows = [slice(ci * c, (ci + 1) * c) for ci in chunks]
    groups = [(ci, d) for ci in chunks for d in range(2)]
    k_hl = [_split2(k_ref[r, :]) for r in rows]
    kbd = [(_block_diag(hi, H_AB), _block_diag(lo, H_AB)) for hi, lo in k_hl]
    r_hi = [_dot_nt(jnp.concatenate([k_hl[ci][0], k_hl[ci][1], q_ref[rows[ci], :].astype(BF16)], axis=0), kbd[ci][0])
            for ci in chunks]
    r_lo = [_dot_nt(k_hl[ci][0], kbd[ci][1]) for ci in chunks]
    kk = [r_hi[ci][:c] + r_hi[ci][c:2 * c] + r_lo[ci] for ci in chunks]
    g = [g_ref[d, rows[ci], :] for ci, d in groups]
    gc = [_dot_exact_lhs(tris[d], g[i]) for i, (ci, d) in enumerate(groups)]
    gc_c = [_heads_to_lanes(x, 12, c) for x in gc]
    gc_r = [_dot_exact_lhs(ones, eye * x) for x in gc_c]
    decay = [jnp.exp(jnp.where(masks[d], gc_c[i] - gc_r[i], -jnp.inf)) for i, (ci, d) in enumerate(groups)]
    a = [jnp.where(stricts[d], _heads_to_lanes(g[i], 8, c) * kk[ci] * decay[i], 0.0)
         for i, (ci, d) in enumerate(groups)]
    inv = _unit_tri_inverse(a, eye, inv_masks)
    inv_hl = [jnp.concatenate(_split2(x), axis=0) for x in inv]
    gc_w = [_heads_to_lanes(x, 12, HEAD) for x in gc]
    eg = [jnp.exp(x) for x in gc_w]
    beta = [_heads_to_lanes(x, 8, HEAD) for x in g]
    res_u = [_dot(inv_hl[i], _block_diag((v_ref[rows[ci], :] * beta[i]).astype(BF16), H_AB))
             for i, (ci, d) in enumerate(groups)]
    res_w = [_dot(inv_hl[i], _block_diag((k_ref[rows[ci], :] * (beta[i] * eg[i])).astype(BF16), H_AB))
             for i, (ci, d) in enumerate(groups)]
    for i, (ci, d) in enumerate(groups):
        r = rows[ci]
        total = _heads_to_lanes(jnp.sum(g[i], axis=0, keepdims=True), 12, HEAD)
        u_ref[d, r, :] = (res_u[i][:c] + res_u[i][c:]).astype(BF16)
        w_ref[d, r, :] = (res_w[i][:c] + res_w[i][c:]).astype(BF16)
        qd_ref[d, r, :] = (q_ref[r, :] * eg[i]).astype(BF16)
        kd_ref[d, r, :] = (k_ref[r, :] * jnp.exp(total - gc_w[i])).astype(BF16)
        att_ref[d, r, :] = (r_hi[ci][2 * c:] * decay[i]).astype(BF16)


def _gdn_scan_kernel(*refs):
    ins, o_refs, s_scr = refs[:12], refs[12:14], refs[14]
    c = GDN_CHUNK

    @pl.when(pl.program_id(1) == 0)
    def _():
        s_scr[...] = jnp.zeros_like(s_scr)

    pw = 2 * HEAD
    rblk = lax.broadcasted_iota(jnp.int32, (pw, pw), 0) // HEAD
    cblk = lax.broadcasted_iota(jnp.int32, (pw, pw), 1) // HEAD
    on_diag = rblk == cblk
    chains = [(d, pr) for d in range(2) for pr in range(H_AB // 2)]
    refs_of = lambda d: ins[d::2]
    sl = [slice(pr * pw, (pr + 1) * pw) for d, pr in chains]
    s = [s_scr[d, pr] for d, pr in chains]
    res = [_dot(jnp.concatenate([refs_of(d)[1][0, :, sl[i]], refs_of(d)[2][0, :, sl[i]]], axis=0), s[i].astype(BF16))
           for i, (d, pr) in enumerate(chains)]
    v_new = [(refs_of(d)[0][0, :, sl[i]].astype(F32) - res[i][:c]).astype(BF16) for i, (d, pr) in enumerate(chains)]
    intra = [_dot(refs_of(d)[4][0, :, pr * 2 * c:(pr + 1) * 2 * c], _block_diag(v_new[i], 2))
             for i, (d, pr) in enumerate(chains)]
    upd = [_dot_tn(refs_of(d)[3][0, :, sl[i]], v_new[i]) for i, (d, pr) in enumerate(chains)]
    totals = [jnp.sum(refs_of(d)[5][0], axis=0, keepdims=True) for d in range(2)]
    for i, (d, pr) in enumerate(chains):
        o_refs[d][0, :, sl[i]] = res[i][c:] + intra[i]
        gl = jnp.concatenate([jnp.broadcast_to(jnp.exp(totals[d][:, 12 + 2 * pr + j:13 + 2 * pr + j]), (1, HEAD))
                              for j in range(2)], axis=1)
        s_scr[d, pr] = gl * s[i] + jnp.where(on_diag, upd[i], 0.0)


def _gdn_scan(qkv, gates, seq, n_ctx, nb):
    t = qkv.shape[0]
    c = GDN_CHUNK
    w = H_AB * HEAD
    tri = _tri_tables(c)
    im = _gdn_inv_masks(c)
    tm = TOK_TILE
    sds = lambda width: jax.ShapeDtypeStruct((2, t, width), BF16)
    u, wv, qd, kd, att = pl.pallas_call(
        _gdn_chunk_kernel,
        grid=(t // tm,),
        in_specs=[pl.BlockSpec((tm, w), lambda i, col=col: (i, col)) for col in range(3)] + [
            pl.BlockSpec((2, tm, LANES), lambda i: (0, i, 0)),
            pl.BlockSpec((2, c, c), lambda i: (0, 0, 0)),
            pl.BlockSpec(im.shape, lambda i: (0, 0, 0))],
        out_specs=[pl.BlockSpec((2, tm, w), lambda i: (0, i, 0))] * 4
        + [pl.BlockSpec((2, tm, H_AB * c), lambda i: (0, i, 0))],
        out_shape=[sds(w)] * 4 + [sds(H_AB * c)],
        compiler_params=_cparams(("parallel",)),
        name="gdn_chunk",
    )(qkv, qkv, qkv, gates, tri, im)

    n_steps, blk = _step_block(c, seq, n_ctx, nb)
    in_specs, args = [], []
    for arr, width in ((u, w), (wv, w), (qd, w), (kd, w), (att, H_AB * c), (gates, LANES)):
        for d in range(2):
            in_specs.append(pl.BlockSpec((1, c, width), lambda b, i, d=d: (d, blk(b, d, i), 0)))
            args.append(arr)
    o_f, o_b = pl.pallas_call(
        _gdn_scan_kernel,
        grid=(nb, n_steps),
        in_specs=in_specs,
        out_specs=[pl.BlockSpec((1, c, w), lambda b, i, d=d: (0, blk(b, d, i), 0)) for d in range(2)],
        out_shape=[jax.ShapeDtypeStruct((1, t, w), F32)] * 2,
        scratch_shapes=[pltpu.VMEM((2, H_AB // 2, 2 * HEAD, 2 * HEAD), F32)],
        compiler_params=_cparams(("parallel", "arbitrary")),
        name="gdn_scan",
    )(*args)
    return o_f, o_b


def _gla_tables(c):
    idx = np.arange(c)
    sel, sgn, msk = [], [], []
    for d in range(2):
        sel_d, sgn_d, msk_d = [], [], []
        bs = c // 2
        while bs >= 1:
            pair = idx // (2 * bs)
            second = (idx // bs) % 2 == 1
            ref_row = pair * 2 * bs + (bs - 1 if d == 0 else bs)
            sel_d.append((idx[None, :] == ref_row[:, None]).astype(np.float32))
            qside = second if d == 0 else np.logical_not(second)
            sgn_d.append(np.where(qside, 1.0, -1.0)[:, None] * np.ones((1, HEAD)))
            same_pair = pair[:, None] == pair[None, :]
            msk_d.append(np.logical_and(same_pair, np.logical_and(qside[:, None], np.logical_not(qside)[None, :])))
            bs //= 2
        msk_d.append(np.eye(c, dtype=bool))
        sel.append(np.concatenate(sel_d, axis=0))
        sgn.append(np.stack(sgn_d))
        msk.append(np.stack(msk_d).astype(np.float32))
    return (jnp.asarray(np.stack(sel), dtype=BF16), jnp.asarray(np.stack(sgn), dtype=F32),
            jnp.asarray(np.stack(msk), dtype=F32))


def _gla_kernel(q_ref, z_ref, v_ref, fb_ref, lb_ref, tri_ref, sel_ref, sgn_ref, msk_ref, o_ref, s_scr):
    @pl.when(pl.program_id(2) == 0)
    def _():
        s_scr[...] = jnp.zeros_like(s_scr)

    c = q_ref.shape[0]
    n_lvl = sgn_ref.shape[1]
    heads = range(H_C)
    sl = [slice(h * HEAD, (h + 1) * HEAD) for h in heads]
    lb = lb_ref[0]
    z = z_ref[...] + fb_ref[0]
    sig_pos, sig_neg = _sigmoid_pair(z)
    g_all = jnp.log(lb + (1.0 - lb) * sig_pos)
    key_all = (1.0 - lb) * sig_neg
    q_all = _silu(q_ref[...])
    gc_all = _dot_exact_lhs(tri_ref[0].astype(BF16), g_all)
    total_all = jnp.sum(g_all, axis=0, keepdims=True)
    refs = _dot(sel_ref[0], gc_all.astype(BF16))
    att = [msk_ref[0, n_lvl] * _dot_nt(q_all[:, s].astype(BF16), key_all[:, s].astype(BF16)) for s in sl]
    for l in range(n_lvl):
        sgn = sgn_ref[0, l]
        q_side = sgn > 0
        ref_l = refs[l * c:(l + 1) * c]
        x = [(jnp.where(q_side, q_all[:, s], key_all[:, s]) * jnp.exp(sgn * (gc_all[:, s] - ref_l[:, s]))).astype(BF16)
             for s in sl]
        prod = [_dot_nt(xh, xh) for xh in x]
        att = [a + msk_ref[0, l] * p for a, p in zip(att, prod)]
    st = [s_scr[h] for h in heads]
    inter = [_dot_nt((q_all[:, s] * jnp.exp(gc_all[:, s])).astype(BF16), st[h].astype(BF16)) for h, s in enumerate(sl)]
    intra = [_dot(att[h].astype(BF16), v_ref[:, s].astype(BF16)) for h, s in enumerate(sl)]
    upd = [_dot_tn(v_ref[:, s].astype(BF16), (key_all[:, s] * jnp.exp(total_all[:, s] - gc_all[:, s])).astype(BF16))
           for s in sl]
    for h, s in enumerate(sl):
        o_ref[0, :, s] = inter[h] + intra[h]
        s_scr[h] = jnp.exp(total_all[:, s]) * st[h] + upd[h]


def _gla_scan(p, f_bias, lb, seq, n_ctx, nb):
    t, pw = p.shape
    c = GLA_CHUNK
    w = H_C * HEAD
    tri = _tri_tables(c)
    sel, sgn, msk = _gla_tables(c)
    n_steps, blk = _step_block(c, seq, n_ctx, nb)
    row = lambda b, d, i: jnp.where(d == 0, blk(b, 0, i), blk(b, 1, i))
    dspec = lambda col: pl.BlockSpec((c, w), lambda b, d, i: (row(b, d, i), col(d)))
    dir_spec = lambda shp: pl.BlockSpec((1,) + shp, lambda b, d, i: (d,) + (0,) * len(shp))
    return pl.pallas_call(
        _gla_kernel,
        grid=(nb, 2, n_steps),
        in_specs=[dspec(lambda d: 0), dspec(lambda d: 1 + d), dspec(lambda d: 3),
                  dir_spec((1, w)), dir_spec((1, w)), dir_spec((c, c)),
                  dir_spec(sel.shape[1:]), dir_spec(sgn.shape[1:]), dir_spec(msk.shape[1:])],
        out_specs=pl.BlockSpec((1, c, w), lambda b, d, i: (d, row(b, d, i), 0)),
        out_shape=jax.ShapeDtypeStruct((2, t, w), F32),
        scratch_shapes=[pltpu.VMEM((H_C, HEAD, HEAD), F32)],
        compiler_params=_cparams(("parallel", "parallel", "arbitrary")),
        name="gla_scan",
    )(p, p, p, f_bias.reshape(2, 1, w), lb.reshape(2, 1, w), tri, sel, sgn, msk)


def _route(scores_t, bias):
    s = [scores_t[e:e + 1, :] for e in range(N_EXPERTS)]
    sel = [s[e] + bias[e:e + 1, :] for e in range(N_EXPERTS)]
    per = N_EXPERTS // N_GROUPS
    gscore = []
    for grp in range(N_GROUPS):
        v = sel[grp * per:(grp + 1) * per]
        best = None
        for a in range(per):
            for b in range(a + 1, per):
                pair = v[a] + v[b]
                best = pair if best is None else jnp.maximum(best, pair)
        gscore.append(best)
    best_g = jnp.zeros_like(gscore[0], dtype=jnp.int32)
    best_v = gscore[0]
    for grp in range(1, N_GROUPS):
        better = gscore[grp] > best_v
        best_g = jnp.where(better, grp, best_g)
        best_v = jnp.where(better, gscore[grp], best_v)
    masked = [jnp.where(best_g == e // per, sel[e], -jnp.inf) for e in range(N_EXPERTS)]
    idx1 = jnp.zeros_like(best_g)
    v1 = masked[0]
    for e in range(1, N_EXPERTS):
        better = masked[e] > v1
        idx1 = jnp.where(better, e, idx1)
        v1 = jnp.where(better, masked[e], v1)
    masked2 = [jnp.where(idx1 == e, -jnp.inf, masked[e]) for e in range(N_EXPERTS)]
    idx2 = jnp.zeros_like(best_g)
    v2 = masked2[0]
    for e in range(1, N_EXPERTS):
        better = masked2[e] > v2
        idx2 = jnp.where(better, e, idx2)
        v2 = jnp.where(better, masked2[e], v2)
    w1 = sum(jnp.where(idx1 == e, s[e], 0.0) for e in range(N_EXPERTS))
    w2 = sum(jnp.where(idx2 == e, s[e], 0.0) for e in range(N_EXPERTS))
    tot = w1 + w2
    w1, w2 = w1 / tot, w2 / tot
    return [jnp.where(idx1 == e, w1, 0.0) + jnp.where(idx2 == e, w2, 0.0) for e in range(N_EXPERTS)]


def _outproj_kernel(x_ref, of0, ob0, gt0, of1, ob1, gt1, hn_ref, w_ref, g1_ref, sc_ref, sh_ref, nf_ref,
                    wr_ref, rb_ref, xo_ref, h_ref, gate_ref, *, first_half_sigmoid):
    feats = []
    for half, (of, ob, gt) in enumerate(((of0, ob0, gt0), (of1, ob1, gt1))):
        o = of[0] + ob[0]
        gate = gt[...]
        for h in range(o.shape[1] // HEAD):
            sl = slice(h * HEAD, (h + 1) * HEAD)
            t = o[:, sl]
            y = t * lax.rsqrt(jnp.mean(t * t, axis=-1, keepdims=True) + EPS)
            y = y * hn_ref[:, half * o.shape[1] + h * HEAD: half * o.shape[1] + (h + 1) * HEAD]
            gz = gate[:, sl]
            act = _sigmoid_pair(gz)[0] if (half == 0 and first_half_sigmoid) else _silu(gz)
            feats.append((y * act).astype(BF16))
    feats = jnp.concatenate(feats, axis=1)
    xn = x_ref[...] + g1_ref[0] * _dot(feats, w_ref[...])
    xo_ref[...] = xn
    y = xn * lax.rsqrt(jnp.mean(xn * xn, axis=-1, keepdims=True) + EPS) * nf_ref[...]
    hl = y * (1.0 + sc_ref[0]) + sh_ref[0]
    h_ref[...] = hl.astype(BF16)
    logits = _dot3(hl, wr_ref[...])
    scores_t = _sigmoid_pair(logits)[0].T
    rows = _route(scores_t, rb_ref[...])
    pad = jnp.zeros((LANES - N_EXPERTS, scores_t.shape[1]), F32)
    gate_ref[...] = jnp.concatenate(rows + [pad], axis=0).T


def _outproj(xs, o_halves, gate_src, gate_cols, hn_gain, w_out, mod, layer, nf_gain, wr_pad, rb_col,
             seq, nb, n_rows, first_half_sigmoid):
    t, d = xs.shape
    tm = TOK_TILE
    half = d // 2
    base = layer * SUBLANES * 6

    def mod_spec(which):
        return pl.BlockSpec((1, 1, d), lambda i: (base + _mod_row(i, tm, seq, nb) * 6 + which, 0, 0))

    in_specs = [pl.BlockSpec((tm, d), lambda i: (i, 0))]
    args = [xs]
    for ((arr_f, dir_f), (arr_b, dir_b), col), gcol in zip(o_halves, gate_cols):
        in_specs += [pl.BlockSpec((1, tm, half), lambda i, col=col, dd=dir_f: (dd, i, col)),
                     pl.BlockSpec((1, tm, half), lambda i, col=col, dd=dir_b: (dd, i, col)),
                     pl.BlockSpec((tm, half), lambda i, gcol=gcol: (i, gcol))]
        args += [arr_f, arr_b, gate_src]
    in_specs += [pl.BlockSpec((1, d), lambda i: (0, 0)),
                 pl.BlockSpec((d, d), lambda i: (0, 0)),
                 mod_spec(2), mod_spec(4), mod_spec(3),
                 pl.BlockSpec((1, d), lambda i: (0, 0)),
                 pl.BlockSpec((d, LANES), lambda i: (0, 0)),
                 pl.BlockSpec((N_EXPERTS, 1), lambda i: (0, 0))]
    args += [hn_gain.reshape(1, d), w_out, mod, mod, mod, nf_gain.reshape(1, d), wr_pad, rb_col]
    return pl.pallas_call(
        functools.partial(_outproj_kernel, first_half_sigmoid=first_half_sigmoid),
        grid=(n_rows // tm,),
        in_specs=in_specs,
        out_specs=[pl.BlockSpec((tm, d), lambda i: (i, 0)),
                   pl.BlockSpec((tm, d), lambda i: (i, 0)),
                   pl.BlockSpec((tm, LANES), lambda i: (i, 0))],
        out_shape=[jax.ShapeDtypeStruct((t, d), F32), jax.ShapeDtypeStruct((t, d), BF16),
                   jax.ShapeDtypeStruct((t, LANES), F32)],
        compiler_params=_cparams(("parallel",)),
        name="outproj",
    )(*args)


def _moe_kernel(x_ref, h_ref, gate_ref, w1_ref, w3_ref, w2_ref, g2_ref, nfin_ref, o_ref, acc_ref, *, final_norm):
    e = pl.program_id(1)

    @pl.when(e == 0)
    def _():
        acc_ref[...] = jnp.zeros_like(acc_ref)

    h = h_ref[...]
    lane = lax.broadcasted_iota(jnp.int32, gate_ref.shape, 1)
    gcol = jnp.sum(jnp.where(lane == e, gate_ref[...], 0.0), axis=-1, keepdims=True)
    a = _silu(_dot(h, w1_ref[0])) * _dot(h, w3_ref[0]) * gcol
    acc_ref[...] += _dot(a.astype(BF16), w2_ref[0])

    @pl.when(e == pl.num_programs(1) - 1)
    def _():
        y = x_ref[...] + g2_ref[0] * acc_ref[...]
        if final_norm:
            y = y * lax.rsqrt(jnp.mean(y * y, axis=-1, keepdims=True) + EPS) * nfin_ref[...]
        o_ref[...] = y


def _moe(xs, h, gate, w1, w3, w2, mod, layer, nfin, seq, nb, n_rows, final_norm):
    t, d = xs.shape
    tm = 1024
    e, _, f = w1.shape
    base = layer * SUBLANES * 6
    out_rows = n_rows if final_norm else t
    return pl.pallas_call(
        functools.partial(_moe_kernel, final_norm=final_norm),
        grid=(n_rows // tm, e),
        in_specs=[pl.BlockSpec((tm, d), lambda i, j: (i, 0)),
                  pl.BlockSpec((tm, d), lambda i, j: (i, 0)),
                  pl.BlockSpec((tm, LANES), lambda i, j: (i, 0)),
                  pl.BlockSpec((1, d, f), lambda i, j: (j, 0, 0)),
                  pl.BlockSpec((1, d, f), lambda i, j: (j, 0, 0)),
                  pl.BlockSpec((1, f, d), lambda i, j: (j, 0, 0)),
                  pl.BlockSpec((1, 1, d), lambda i, j: (base + _mod_row(i, tm, seq, nb) * 6 + 5, 0, 0)),
                  pl.BlockSpec((1, d), lambda i, j: (0, 0))],
        out_specs=pl.BlockSpec((tm, d), lambda i, j: (i, 0)),
        out_shape=jax.ShapeDtypeStruct((out_rows, d), F32),
        scratch_shapes=[pltpu.VMEM((tm, d), F32)],
        compiler_params=_cparams(("parallel", "arbitrary")),
        name="moe",
    )(xs, h, gate, w1, w3, w2, mod, nfin.reshape(1, d))


def kernel(x, c, ctx, c_ctx, w_mod, b_mod, norm_mix, norm_ffn, norm_final, ab_w_in, ab_i_bias, ab_f_bias,
           ab_conv, ab_a_log, ab_dt_bias, ab_norm_a, ab_norm_b, ab_w_out, c_w_in, c_f_bias, c_lb_raw, c_norm,
           c_w_out, w_router, router_bias, w1, w3, w2):
    nb, seq, d = x.shape
    n_ctx = ctx.shape[1]
    depth = w_mod.shape[0]
    n_lat = nb * seq
    assert nb + 1 <= SUBLANES and seq % 1024 == 0 and (nb * n_ctx) % 1024 == 0 and n_ctx % 256 == 0

    xs = jnp.concatenate([x.reshape(n_lat, d), ctx.reshape(nb * n_ctx, d)], axis=0)
    cpad = jnp.zeros((SUBLANES, d), F32).at[:nb].set(c).at[nb].set(c_ctx)
    mod = _modulation(cpad, w_mod, b_mod).reshape(depth * SUBLANES * 6, 1, d)

    lb_p = jax.nn.softmax(c_lb_raw.astype(F32), axis=0)
    lb_all = jnp.cumsum(lb_p, axis=0) - lb_p[0:1]

    wr_pad = jnp.zeros((d, LANES), F32).at[:, :N_EXPERTS].set(w_router)
    rb_col = router_bias.astype(F32).reshape(N_EXPERTS, 1)
    w_main = H_AB * HEAD * 8
    hw = H_AB * HEAD

    out = None
    column_major = False
    for layer in range(depth):
        last = layer == depth - 1
        j = layer // 2
        want_cm = layer % 2 == 1 and j % 2 == 1
        needs_raster = layer % 2 == 0 or not want_cm
        if want_cm and not column_major:
            xs = _to_column_major(xs, n_lat, seq)
            column_major = True
        elif needs_raster and column_major:
            xs = jnp.concatenate([_from_column_major(xs[:n_lat], seq), xs[n_lat:]], axis=0)
            column_major = False
        if layer % 2 == 0:
            w_in = ab_w_in[j]
            gw = w_in[:, w_main:].reshape(d, 4, 2, H_AB)
            wg = jnp.zeros((2, d, LANES), F32).at[:, :, :4 * H_AB].set(
                jnp.transpose(gw, (2, 0, 1, 3)).reshape(2, d, 4 * H_AB)).astype(BF16)
            zeros = jnp.zeros((2, H_AB), F32)
            bias = jnp.concatenate([ab_i_bias[j], ab_f_bias[j], zeros, ab_dt_bias[j]], axis=1)
            alog = jnp.concatenate([zeros, zeros, zeros, ab_a_log[j]], axis=1)
            gpar = jnp.zeros((2, SUBLANES, LANES), F32).at[:, 0, :4 * H_AB].set(bias).at[:, 1, :4 * H_AB].set(alog)
            p, gates = _inproj(xs, norm_mix[layer], mod, layer, w_in[:, :w_main].astype(BF16), seq, nb, wg, gpar)
            ha_f, ha_b = _mlstm_scan(p, gates, seq, n_ctx, nb)
            qkv = _gdn_prep(p, ab_conv[j], seq, n_ctx, nb)
            ob_f, ob_b = _gdn_scan(qkv, gates, seq, n_ctx, nb)
            o_halves = (((ha_f, 0), (ha_b, 0), 0), ((ob_f, 0), (ob_b, 0), 0))
            gate_cols = (3, 7)
            hn_gain = jnp.concatenate([ab_norm_a[j], ab_norm_b[j]])
            w_out = ab_w_out[j]
        else:
            (p,) = _inproj(xs, norm_mix[layer], mod, layer, c_w_in[j].astype(BF16), seq, nb)
            o_c = _gla_scan(p, c_f_bias[j], lb_all[j], seq, n_ctx, nb)
            o_halves = (((o_c, 0), (o_c, 1), 0), ((o_c, 0), (o_c, 1), 1))
            gate_cols = (8, 9)
            hn_gain = c_norm[j]
            w_out = c_w_out[j]
        n_rows = n_lat if last else n_lat + nb * n_ctx
        xs, h, gate = _outproj(xs, o_halves, p, gate_cols, hn_gain, w_out.astype(BF16), mod, layer,
                               norm_ffn[layer], wr_pad, rb_col, seq, nb, n_rows, layer % 2 == 0)
        res = _moe(xs, h, gate, w1[layer].astype(BF16), w3[layer].astype(BF16), w2[layer].astype(BF16),
                   mod, layer, norm_final, seq, nb, n_rows, last)
        if last:
            out = _from_column_major(res, seq) if column_major else res
        else:
            xs = res
    return out.reshape(nb, seq, d)
```

```python
import functools

import numpy as np
import jax
import jax.numpy as jnp
from jax import lax
from jax.experimental import pallas as pl
from jax.experimental.pallas import tpu as pltpu

F32 = jnp.float32
BF16 = jnp.bfloat16

EPS = 1e-6
GRID_W = 64
HEAD = 128
H_AB = 4
H_C = 8
CONV_K = 5
N_EXPERTS = 16
N_GROUPS = 4
EXPERTS_PER_GROUP = N_EXPERTS // N_GROUPS
LANES = 128
SUBLANES = 8
VMEM_LIMIT = 56 * 1024 * 1024

TOK_TILE = 512
MOE_TILE = 1024
MLSTM_CHUNK = 128
GDN_CHUNK = 64
GLA_CHUNK = 128
GDN_INV_BLOCK = 16


def _cparams(sem):
    return pltpu.CompilerParams(dimension_semantics=sem, vmem_limit_bytes=VMEM_LIMIT)


def _dot(a, b):
    return jnp.dot(a, b, preferred_element_type=F32)


def _dot_nt(a, b):
    return lax.dot_general(a, b, (((1,), (1,)), ((), ())), preferred_element_type=F32)


def _dot_tn(a, b):
    return lax.dot_general(a, b, (((0,), (0,)), ((), ())), preferred_element_type=F32)


def _split2(x):
    hi = x.astype(BF16)
    lo = (x - hi.astype(F32)).astype(BF16)
    return hi, lo


def _split3(x):
    hi = x.astype(BF16)
    r = x - hi.astype(F32)
    mid = r.astype(BF16)
    lo = (r - mid.astype(F32)).astype(BF16)
    return hi, mid, lo


def _dot_exact_lhs(m_bf16, x):
    hi, mid, lo = _split3(x)
    return _dot(m_bf16, hi) + _dot(m_bf16, mid) + _dot(m_bf16, lo)


def _dot3(a, b):
    ah, al = _split2(a)
    bh, bl = _split2(b)
    return _dot(ah, bh) + _dot(ah, bl) + _dot(al, bh)


def _dot_nt3(a, b):
    ah, al = _split2(a)
    bh, bl = _split2(b)
    return _dot_nt(ah, bh) + _dot_nt(ah, bl) + _dot_nt(al, bh)


def _sigmoid_pair(z):
    e = jnp.exp(-jnp.abs(z))
    r = 1.0 / (1.0 + e)
    er = e * r
    pos = z >= 0
    return jnp.where(pos, r, er), jnp.where(pos, er, r)


def _silu(z):
    return z * _sigmoid_pair(z)[0]


def _mod_kernel(c_ref, w_ref, b_ref, o_ref):
    s = _silu(c_ref[...]).astype(BF16)
    o_ref[0] = _dot(s, w_ref[0].astype(BF16)) + b_ref[0]


def _modulation(cpad, w_mod, b_mod):
    depth, d, n = w_mod.shape
    tn = 1536
    return pl.pallas_call(
        _mod_kernel,
        grid=(depth, n // tn),
        in_specs=[pl.BlockSpec((SUBLANES, d), lambda l, j: (0, 0)),
                  pl.BlockSpec((1, d, tn), lambda l, j: (l, 0, j)),
                  pl.BlockSpec((1, 1, tn), lambda l, j: (l, 0, j))],
        out_specs=pl.BlockSpec((1, SUBLANES, tn), lambda l, j: (l, 0, j)),
        out_shape=jax.ShapeDtypeStruct((depth, SUBLANES, n), F32),
        compiler_params=_cparams(("parallel", "parallel")),
        name="modulation",
    )(cpad, w_mod, b_mod.reshape(depth, 1, n))


def _gate_activations(raw, par):
    y = raw + par[0:1, :]
    e = jnp.exp(-jnp.abs(y))
    sp = jnp.log1p(e)
    lsig = jnp.minimum(y, 0.0) - sp
    r = 1.0 / (1.0 + e)
    sig = jnp.where(y >= 0, r, e * r)
    gdec = -jnp.exp(par[1:2, :]) * (jnp.maximum(y, 0.0) + sp)
    lane = lax.broadcasted_iota(jnp.int32, raw.shape, 1)
    return jnp.where(lane < 4, y, jnp.where(lane < 8, lsig, jnp.where(lane < 12, sig, gdec)))


def _inproj_kernel(x_ref, gain_ref, sc_ref, sh_ref, w_ref, *rest, with_gates):
    if with_gates:
        wg_ref, gpar_ref, p_ref, g_ref, u_ref = rest
    else:
        p_ref, u_ref = rest

    @pl.when(pl.program_id(1) == 0)
    def _():
        x = x_ref[...]
        y = x * lax.rsqrt(jnp.mean(x * x, axis=-1, keepdims=True) + EPS) * gain_ref[...]
        u = (y * (1.0 + sc_ref[0]) + sh_ref[0]).astype(BF16)
        u_ref[...] = u
        if with_gates:
            for d in range(2):
                g_ref[d] = _gate_activations(_dot(u, wg_ref[d]), gpar_ref[d])

    p_ref[...] = _dot(u_ref[...], w_ref[...])


def _mod_row(i, tm, seq, nb):
    return jnp.minimum((i * tm) // seq, nb)


def _inproj(xs, gain, mod, layer, w, seq, nb, wg=None, gpar=None):
    t, d = xs.shape
    p = w.shape[1]
    tm, tn = TOK_TILE, 1024
    with_gates = wg is not None
    base = layer * SUBLANES * 6

    def mod_spec(which):
        return pl.BlockSpec((1, 1, d), lambda i, j: (base + _mod_row(i, tm, seq, nb) * 6 + which, 0, 0))

    in_specs = [pl.BlockSpec((tm, d), lambda i, j: (i, 0)),
                pl.BlockSpec((1, d), lambda i, j: (0, 0)),
                mod_spec(1), mod_spec(0),
                pl.BlockSpec((d, tn), lambda i, j: (0, j))]
    args = [xs, gain.reshape(1, d), mod, mod, w]
    out_specs = [pl.BlockSpec((tm, tn), lambda i, j: (i, j))]
    out_shape = [jax.ShapeDtypeStruct((t, p), F32)]
    if with_gates:
        in_specs += [pl.BlockSpec((2, d, LANES), lambda i, j: (0, 0, 0)),
                     pl.BlockSpec((2, SUBLANES, LANES), lambda i, j: (0, 0, 0))]
        args += [wg, gpar]
        out_specs.append(pl.BlockSpec((2, tm, LANES), lambda i, j: (0, i, 0)))
        out_shape.append(jax.ShapeDtypeStruct((2, t, LANES), F32))
    return pl.pallas_call(
        functools.partial(_inproj_kernel, with_gates=with_gates),
        grid=(t // tm, p // tn),
        in_specs=in_specs, out_specs=out_specs, out_shape=out_shape,
        scratch_shapes=[pltpu.VMEM((tm, d), BF16)],
        compiler_params=_cparams(("parallel", "arbitrary")),
        name="inproj",
    )(*args)


def _gdn_prep_kernel(prev_ref, x_ref, next_ref, w_ref, o_ref, *, blocks_per_seq, n_lat_blocks, scale):
    i = pl.program_id(0)
    tm = x_ref.shape[0]
    in_lat = i < n_lat_blocks
    first = jnp.logical_or(jnp.logical_not(in_lat), i % blocks_per_seq == 0)
    last = jnp.logical_or(jnp.logical_not(in_lat), i % blocks_per_seq == blocks_per_seq - 1)
    prev = jnp.where(first, 0.0, prev_ref[...])
    nxt = jnp.where(last, 0.0, next_ref[...])
    xe = jnp.concatenate([prev, x_ref[...], nxt], axis=0)
    n = xe.shape[0]
    pad = (CONV_K - 1) // 2
    acc = None
    for j in range(CONV_K):
        shift = (pad - j) % n
        xr = xe if shift == 0 else pltpu.roll(xe, shift, 0)
        term = xr[SUBLANES:SUBLANES + tm] * w_ref[j:j + 1, :]
        acc = term if acc is None else acc + term
    y = _silu(acc)
    part = pl.program_id(1)
    for h in range(H_AB):
        sl = slice(h * HEAD, (h + 1) * HEAD)
        t = y[:, sl]
        inv_norm = lax.rsqrt(jnp.sum(t * t, axis=-1, keepdims=True) + EPS)
        factor = jnp.where(part == 0, inv_norm * scale, jnp.where(part == 1, inv_norm, 1.0))
        o_ref[:, sl] = t * factor


def _gdn_prep(p, conv_w, seq, n_ctx, nb):
    t = p.shape[0]
    tm = 256
    w = H_AB * HEAD
    first_col = 4
    wpad = jnp.zeros((SUBLANES, 3 * w), F32).at[:CONV_K].set(conv_w)
    hb = tm // SUBLANES
    nblk8 = t // SUBLANES
    return pl.pallas_call(
        functools.partial(_gdn_prep_kernel, blocks_per_seq=seq // tm, n_lat_blocks=nb * seq // tm,
                          scale=HEAD ** -0.5),
        grid=(t // tm, 3),
        in_specs=[pl.BlockSpec((SUBLANES, w), lambda i, c: (jnp.maximum(i * hb - 1, 0), first_col + c)),
                  pl.BlockSpec((tm, w), lambda i, c: (i, first_col + c)),
                  pl.BlockSpec((SUBLANES, w), lambda i, c: (jnp.minimum((i + 1) * hb, nblk8 - 1), first_col + c)),
                  pl.BlockSpec((SUBLANES, w), lambda i, c: (0, c))],
        out_specs=pl.BlockSpec((tm, w), lambda i, c: (i, c)),
        out_shape=jax.ShapeDtypeStruct((t, 3 * w), F32),
        compiler_params=_cparams(("parallel", "parallel")),
        name="gdn_prep",
    )(p, p, p, wpad)


def _tri_tables(c):
    idx = np.arange(c)
    fwd = (idx[None, :] <= idx[:, None]).astype(np.float32)
    return jnp.asarray(np.stack([fwd, fwd.T]))


def _to_cm_kernel(x3_ref, x2_ref, o_ref, *, n_lat_tiles):
    i = pl.program_id(0)
    rows = x3_ref.shape[0]

    @pl.when(i < n_lat_tiles)
    def _():
        for wl in range(SUBLANES):
            o_ref[wl * rows:(wl + 1) * rows, :] = x3_ref[:, wl, :]

    @pl.when(i >= n_lat_tiles)
    def _():
        o_ref[...] = x2_ref[...]


def _from_cm_kernel(x_ref, o_ref):
    rows = o_ref.shape[0]
    for wl in range(SUBLANES):
        o_ref[:, wl, :] = x_ref[wl * rows:(wl + 1) * rows, :]


def _to_column_major(xs, n_lat, seq):
    t, d = xs.shape
    rows = seq // GRID_W
    tile = rows * SUBLANES
    assert rows % SUBLANES == 0 and n_lat % tile == 0 and (t - n_lat) % tile == 0
    n_lat_tiles = n_lat // tile
    per_b = GRID_W // SUBLANES
    lat = lambda i: jnp.minimum(i, n_lat_tiles - 1)
    return pl.pallas_call(
        functools.partial(_to_cm_kernel, n_lat_tiles=n_lat_tiles),
        grid=(t // tile,),
        in_specs=[pl.BlockSpec((rows, SUBLANES, d), lambda i: (lat(i) // per_b, lat(i) % per_b, 0)),
                  pl.BlockSpec((tile, d), lambda i: (jnp.maximum(i, n_lat_tiles - 1), 0))],
        out_specs=pl.BlockSpec((tile, d), lambda i: (i, 0)),
        out_shape=jax.ShapeDtypeStruct((t, d), xs.dtype),
        compiler_params=_cparams(("parallel",)),
        name="to_column_major",
    )(xs.reshape(t // GRID_W, GRID_W, d), xs)


def _from_column_major(y, seq):
    n_lat, d = y.shape
    rows = seq // GRID_W
    tile = rows * SUBLANES
    per_b = GRID_W // SUBLANES
    out = pl.pallas_call(
        _from_cm_kernel,
        grid=(n_lat // tile,),
        in_specs=[pl.BlockSpec((tile, d), lambda i: (i, 0))],
        out_specs=pl.BlockSpec((rows, SUBLANES, d), lambda i: (i // per_b, i % per_b, 0)),
        out_shape=jax.ShapeDtypeStruct((n_lat // GRID_W, GRID_W, d), y.dtype),
        compiler_params=_cparams(("parallel",)),
        name="from_column_major",
    )(y)
    return out.reshape(n_lat, d)


def _mlstm_kernel(qf, kf, vf, qb, kb, vb, gf, gb, tri_ref, hf_ref, hb_ref, s_scr, m_scr, *, scale):
    @pl.when(pl.program_id(1) == 0)
    def _():
        s_scr[...] = jnp.zeros_like(s_scr)
        m_scr[...] = jnp.zeros_like(m_scr)

    c = gf.shape[1]
    ones_col = (lax.broadcasted_iota(jnp.int32, (c, HEAD), 1) == 0).astype(BF16)
    qkv = ((qf, kf, vf), (qb, kb, vb))
    h_out = (hf_ref, hb_ref)
    g = [gf[0], gb[0]]
    mask = [tri_ref[d] > 0.5 for d in range(2)]
    b = [_dot_exact_lhs(tri_ref[d].astype(BF16), g[d]) for d in range(2)]
    total = [jnp.sum(x, axis=0, keepdims=True) for x in g]
    g_t = [x.T for x in g]
    b_t = [x.T for x in b]
    chains = [(d, h) for d in range(2) for h in range(H_AB)]
    sl = [slice(h * HEAD, (h + 1) * HEAD) for d, h in chains]
    q = [qkv[d][0][:, sl[i]].astype(BF16) for i, (d, h) in enumerate(chains)]
    k = [qkv[d][1][:, sl[i]] * scale for i, (d, h) in enumerate(chains)]
    v_aug = [jnp.concatenate([qkv[d][2][:, sl[i]].astype(BF16), ones_col], axis=1) for i, (d, h) in enumerate(chains)]
    s = [s_scr[d, h] for d, h in chains]
    qk = [_dot_nt(q[i], k[i].astype(BF16)) for i in range(len(chains))]
    qs = [_dot(q[i], s[i].astype(BF16)) for i in range(len(chains))]
    w_intra, w_inter, m_ts, colvs, m_sts, tots = [], [], [], [], [], []
    for i, (d, h) in enumerate(chains):
        li_c, li_r = g[d][:, h:h + 1], g_t[d][h:h + 1, :]
        b_c, b_r = b[d][:, 4 + h:5 + h], b_t[d][4 + h:5 + h, :]
        tot = total[d][:, 4 + h:5 + h]
        m_st = m_scr[d, h][0:1, 0:1]
        d_log = jnp.where(mask[d], b_c - b_r + li_r, -jnp.inf)
        m_inter = b_c + m_st
        m_t = jnp.maximum(m_inter, jnp.max(d_log, axis=-1, keepdims=True))
        w_inter.append(jnp.exp(m_inter - m_t))
        w_intra.append((jnp.exp(d_log - m_t) * qk[i]).astype(BF16))
        m_ts.append(m_t); colvs.append(tot - b_c + li_c); m_sts.append(m_st); tots.append(tot)
    intra = [_dot(w_intra[i], v_aug[i]) for i in range(len(chains))]
    m_new = [jnp.maximum(tots[i] + m_sts[i], jnp.max(colvs[i], axis=0, keepdims=True)) for i in range(len(chains))]
    upd = [_dot_tn((k[i] * jnp.exp(colvs[i] - m_new[i])).astype(BF16), v_aug[i]) for i in range(len(chains))]
    for i, (d, h) in enumerate(chains):
        num = w_inter[i] * qs[i] + intra[i]
        den = num[:, HEAD:HEAD + 1]
        h_out[d][0, :, sl[i]] = num[:, :HEAD] / jnp.maximum(jnp.abs(den), jnp.exp(-m_ts[i]))
        s_scr[d, h] = jnp.exp(tots[i] + m_sts[i] - m_new[i]) * s[i] + upd[i]
        m_scr[d, h] = jnp.broadcast_to(m_new[i], (SUBLANES, LANES))


def _step_block(chunk, seq, n_ctx, nb):
    n_c, n_l = n_ctx // chunk, seq // chunk
    off = nb * n_l

    def fn(b, d, i):
        j = i - n_c
        ctx_blk = off + b * n_c + (i if d == 0 else n_c - 1 - i)
        lat_blk = b * n_l + (j if d == 0 else n_l - 1 - j)
        return jnp.where(i < n_c, ctx_blk, lat_blk)

    return n_c + n_l, fn


def _mlstm_scan(p, gates, seq, n_ctx, nb):
    t = p.shape[0]
    c = MLSTM_CHUNK
    w = H_AB * HEAD
    n_steps, blk = _step_block(c, seq, n_ctx, nb)
    in_specs = [pl.BlockSpec((c, w), lambda b, i, d=d, col=col: (blk(b, d, i), col)) for d in range(2) for col in range(3)]
    in_specs += [pl.BlockSpec((1, c, LANES), lambda b, i, d=d: (d, blk(b, d, i), 0)) for d in range(2)]
    in_specs.append(pl.BlockSpec((2, c, c), lambda b, i: (0, 0, 0)))
    return pl.pallas_call(
        functools.partial(_mlstm_kernel, scale=HEAD ** -0.5),
        grid=(nb, n_steps),
        in_specs=in_specs,
        out_specs=[pl.BlockSpec((1, c, w), lambda b, i, d=d: (0, blk(b, d, i), 0)) for d in range(2)],
        out_shape=[jax.ShapeDtypeStruct((1, t, w), F32)] * 2,
        scratch_shapes=[pltpu.VMEM((2, H_AB, HEAD, 2 * HEAD), F32), pltpu.VMEM((2, H_AB, SUBLANES, LANES), F32)],
        compiler_params=_cparams(("parallel", "arbitrary")),
        name="mlstm_scan",
    )(p, p, p, p, p, p, gates, gates, _tri_tables(c))


def _gdn_inv_masks(c):
    idx = np.arange(c)
    blk = lambda n: (idx[:, None] // n) == (idx[None, :] // n)
    levels = [blk(GDN_INV_BLOCK)]
    n = GDN_INV_BLOCK
    while n < c:
        levels.append(np.logical_and(blk(2 * n), np.logical_not(blk(n))))
        n *= 2
    levels = np.stack(levels).astype(np.float32)
    return jnp.asarray(np.tile(levels, (1, 1, H_AB)))


def _block_diag(x, n_blocks):
    blk = lax.broadcasted_iota(jnp.int32, x.shape, 1) // (x.shape[1] // n_blocks)
    return jnp.concatenate([jnp.where(blk == j, x, jnp.zeros_like(x)) for j in range(n_blocks)], axis=0)


def _heads_to_lanes(x, first, width):
    return jnp.concatenate([jnp.broadcast_to(x[:, first + h:first + h + 1], (x.shape[0], width))
                            for h in range(H_AB)], axis=1)


def _unit_tri_inverse(a_list, eye, inv_masks):
    prod = lambda xs, ys: [_dot(x.astype(BF16), _block_diag(y.astype(BF16), H_AB)) for x, y in zip(xs, ys)]
    pw = [a * inv_masks[0] for a in a_list]
    inv = [eye - p for p in pw]
    n = 2
    while n < GDN_INV_BLOCK:
        pw = prod(pw, pw)
        inv = [i + t for i, t in zip(inv, prod(inv, pw))]
        n *= 2
    for lvl in range(1, inv_masks.shape[0]):
        t = prod(prod(inv, [a * inv_masks[lvl] for a in a_list]), inv)
        inv = [i - x for i, x in zip(inv, t)]
    return inv


def _gdn_chunk_kernel(q_ref, k_ref, v_ref, g_ref, tri_ref, im_ref, u_ref, w_ref, qd_ref, kd_ref, att_ref):
    c = GDN_CHUNK
    n_chunks = q_ref.shape[0] // c
    inv_masks = im_ref[...]
    row = lax.broadcasted_iota(jnp.int32, (c, H_AB * c), 0)
    col = lax.broadcasted_iota(jnp.int32, (c, H_AB * c), 1) % c
    eye_b = row == col
    eye = eye_b.astype(F32)
    masks = [col <= row, col >= row]
    stricts = [col < row, col > row]
    tris = [tri_ref[d].astype(BF16) for d in range(2)]
    ones = jnp.ones((c, c), BF16)
    chunks = range(n_chunks)
    rows = [slice(ci * c, (ci + 1) * c) for ci in chunks]
    groups = [(ci, d) for ci in chunks for d in range(2)]
    k_hl = [_split2(k_ref[r, :]) for r in rows]
    kbd = [(_block_diag(hi, H_AB), _block_diag(lo, H_AB)) for hi, lo in k_hl]
    r_hi = [_dot_nt(jnp.concatenate([k_hl[ci][0], k_hl[ci][1], q_ref[rows[ci], :].astype(BF16)], axis=0), kbd[ci][0])
            for ci in chunks]
    r_lo = [_dot_nt(k_hl[ci][0], kbd[ci][1]) for ci in chunks]
    kk = [r_hi[ci][:c] + r_hi[ci][c:2 * c] + r_lo[ci] for ci in chunks]
    g = [g_ref[d, rows[ci], :] for ci, d in groups]
    gc = [_dot_exact_lhs(tris[d], g[i]) for i, (ci, d) in enumerate(groups)]
    gc_c = [_heads_to_lanes(x, 12, c) for x in gc]
    gc_r = [_dot_exact_lhs(ones, eye * x) for x in gc_c]
    decay = [jnp.exp(jnp.where(masks[d], gc_c[i] - gc_r[i], -jnp.inf)) for i, (ci, d) in enumerate(groups)]
    a = [jnp.where(stricts[d], _heads_to_lanes(g[i], 8, c) * kk[ci] * decay[i], 0.0)
         for i, (ci, d) in enumerate(groups)]
    inv = _unit_tri_inverse(a, eye, inv_masks)
    inv_hl = [jnp.concatenate(_split2(x), axis=0) for x in inv]
    gc_w = [_heads_to_lanes(x, 12, HEAD) for x in gc]
    eg = [jnp.exp(x) for x in gc_w]
    beta = [_heads_to_lanes(x, 8, HEAD) for x in g]
    res_u = [_dot(inv_hl[i], _block_diag((v_ref[rows[ci], :] * beta[i]).astype(BF16), H_AB))
             for i, (ci, d) in enumerate(groups)]
    res_w = [_dot(inv_hl[i], _block_diag((k_ref[rows[ci], :] * (beta[i] * eg[i])).astype(BF16), H_AB))
             for i, (ci, d) in enumerate(groups)]
    for i, (ci, d) in enumerate(groups):
        r = rows[ci]
        total = _heads_to_lanes(jnp.sum(g[i], axis=0, keepdims=True), 12, HEAD)
        u_ref[d, r, :] = (res_u[i][:c] + res_u[i][c:]).astype(BF16)
        w_ref[d, r, :] = (res_w[i][:c] + res_w[i][c:]).astype(BF16)
        qd_ref[d, r, :] = (q_ref[r, :] * eg[i]).astype(BF16)
        kd_ref[d, r, :] = (k_ref[r, :] * jnp.exp(total - gc_w[i])).astype(BF16)
        att_ref[d, r, :] = (r_hi[ci][2 * c:] * decay[i]).astype(BF16)


def _gdn_scan_kernel(*refs):
    ins, o_refs, s_scr = refs[:12], refs[12:14], refs[14]
    c = GDN_CHUNK

    @pl.when(pl.program_id(1) == 0)
    def _():
        s_scr[...] = jnp.zeros_like(s_scr)

    pw = 2 * HEAD
    rblk = lax.broadcasted_iota(jnp.int32, (pw, pw), 0) // HEAD
    cblk = lax.broadcasted_iota(jnp.int32, (pw, pw), 1) // HEAD
    on_diag = rblk == cblk
    chains = [(d, pr) for d in range(2) for pr in range(H_AB // 2)]
    refs_of = lambda d: ins[d::2]
    sl = [slice(pr * pw, (pr + 1) * pw) for d, pr in chains]
    s = [s_scr[d, pr] for d, pr in chains]
    res = [_dot(jnp.concatenate([refs_of(d)[1][0, :, sl[i]], refs_of(d)[2][0, :, sl[i]]], axis=0), s[i].astype(BF16))
           for i, (d, pr) in enumerate(chains)]
    v_new = [(refs_of(d)[0][0, :, sl[i]].astype(F32) - res[i][:c]).astype(BF16) for i, (d, pr) in enumerate(chains)]
    intra = [_dot(refs_of(d)[4][0, :, pr * 2 * c:(pr + 1) * 2 * c], _block_diag(v_new[i], 2))
             for i, (d, pr) in enumerate(chains)]
    upd = [_dot_tn(refs_of(d)[3][0, :, sl[i]], v_new[i]) for i, (d, pr) in enumerate(chains)]
    totals = [jnp.sum(refs_of(d)[5][0], axis=0, keepdims=True) for d in range(2)]
    for i, (d, pr) in enumerate(chains):
        o_refs[d][0, :, sl[i]] = res[i][c:] + intra[i]
        gl = jnp.concatenate([jnp.broadcast_to(jnp.exp(totals[d][:, 12 + 2 * pr + j:13 + 2 * pr + j]), (1, HEAD))
                              for j in range(2)], axis=1)
        s_scr[d, pr] = gl * s[i] + jnp.where(on_diag, upd[i], 0.0)


def _gdn_scan(qkv, gates, seq, n_ctx, nb):
    t = qkv.shape[0]
    c = GDN_CHUNK
    w = H_AB * HEAD
    tri = _tri_tables(c)
    im = _gdn_inv_masks(c)
    tm = TOK_TILE
    sds = lambda width: jax.ShapeDtypeStruct((2, t, width), BF16)
    u, wv, qd, kd, att = pl.pallas_call(
        _gdn_chunk_kernel,
        grid=(t // tm,),
        in_specs=[pl.BlockSpec((tm, w), lambda i, col=col: (i, col)) for col in range(3)] + [
            pl.BlockSpec((2, tm, LANES), lambda i: (0, i, 0)),
            pl.BlockSpec((2, c, c), lambda i: (0, 0, 0)),
            pl.BlockSpec(im.shape, lambda i: (0, 0, 0))],
        out_specs=[pl.BlockSpec((2, tm, w), lambda i: (0, i, 0))] * 4
        + [pl.BlockSpec((2, tm, H_AB * c), lambda i: (0, i, 0))],
        out_shape=[sds(w)] * 4 + [sds(H_AB * c)],
        compiler_params=_cparams(("parallel",)),
        name="gdn_chunk",
    )(qkv, qkv, qkv, gates, tri, im)

    n_steps, blk = _step_block(c, seq, n_ctx, nb)
    in_specs, args = [], []
    for arr, width in ((u, w), (wv, w), (qd, w), (kd, w), (att, H_AB * c), (gates, LANES)):
        for d in range(2):
            in_specs.append(pl.BlockSpec((1, c, width), lambda b, i, d=d: (d, blk(b, d, i), 0)))
            args.append(arr)
    o_f, o_b = pl.pallas_call(
        _gdn_scan_kernel,
        grid=(nb, n_steps),
        in_specs=in_specs,
        out_specs=[pl.BlockSpec((1, c, w), lambda b, i, d=d: (0, blk(b, d, i), 0)) for d in range(2)],
        out_shape=[jax.ShapeDtypeStruct((1, t, w), F32)] * 2,
        scratch_shapes=[pltpu.VMEM((2, H_AB // 2, 2 * HEAD, 2 * HEAD), F32)],
        compiler_params=_cparams(("parallel", "arbitrary")),
        name="gdn_scan",
    )(*args)
    return o_f, o_b


def _gla_tables(c):
    idx = np.arange(c)
    sel, sgn, msk = [], [], []
    for d in range(2):
        sel_d, sgn_d, msk_d = [], [], []
        bs = c // 2
        while bs >= 1:
            pair = idx // (2 * bs)
            second = (idx // bs) % 2 == 1
            ref_row = pair * 2 * bs + (bs - 1 if d == 0 else bs)
            sel_d.append((idx[None, :] == ref_row[:, None]).astype(np.float32))
            qside = second if d == 0 else np.logical_not(second)
            sgn_d.append(np.where(qside, 1.0, -1.0)[:, None] * np.ones((1, HEAD)))
            same_pair = pair[:, None] == pair[None, :]
            msk_d.append(np.logical_and(same_pair, np.logical_and(qside[:, None], np.logical_not(qside)[None, :])))
            bs //= 2
        msk_d.append(np.eye(c, dtype=bool))
        sel.append(np.concatenate(sel_d, axis=0))
        sgn.append(np.stack(sgn_d))
        msk.append(np.stack(msk_d).astype(np.float32))
    return (jnp.asarray(np.stack(sel), dtype=BF16), jnp.asarray(np.stack(sgn), dtype=F32),
            jnp.asarray(np.stack(msk), dtype=F32))


def _gla_kernel(q_ref, z_ref, v_ref, fb_ref, lb_ref, tri_ref, sel_ref, sgn_ref, msk_ref, o_ref, s_scr):
    @pl.when(pl.program_id(2) == 0)
    def _():
        s_scr[...] = jnp.zeros_like(s_scr)

    c = q_ref.shape[0]
    n_lvl = sgn_ref.shape[1]
    heads = range(H_C)
    sl = [slice(h * HEAD, (h + 1) * HEAD) for h in heads]
    lb = lb_ref[0]
    z = z_ref[...] + fb_ref[0]
    sig_pos, sig_neg = _sigmoid_pair(z)
    g_all = jnp.log(lb + (1.0 - lb) * sig_pos)
    key_all = (1.0 - lb) * sig_neg
    q_all = _silu(q_ref[...])
    gc_all = _dot_exact_lhs(tri_ref[0].astype(BF16), g_all)
    total_all = jnp.sum(g_all, axis=0, keepdims=True)
    refs = _dot(sel_ref[0], gc_all.astype(BF16))
    att = [msk_ref[0, n_lvl] * _dot_nt(q_all[:, s].astype(BF16), key_all[:, s].astype(BF16)) for s in sl]
    for l in range(n_lvl):
        sgn = sgn_ref[0, l]
        q_side = sgn > 0
        ref_l = refs[l * c:(l + 1) * c]
        x = [(jnp.where(q_side, q_all[:, s], key_all[:, s]) * jnp.exp(sgn * (gc_all[:, s] - ref_l[:, s]))).astype(BF16)
             for s in sl]
        prod = [_dot_nt(xh, xh) for xh in x]
        att = [a + msk_ref[0, l] * p for a, p in zip(att, prod)]
    st = [s_scr[h] for h in heads]
    inter = [_dot_nt((q_all[:, s] * jnp.exp(gc_all[:, s])).astype(BF16), st[h].astype(BF16)) for h, s in enumerate(sl)]
    intra = [_dot(att[h].astype(BF16), v_ref[:, s].astype(BF16)) for h, s in enumerate(sl)]
    upd = [_dot_tn(v_ref[:, s].astype(BF16), (key_all[:, s] * jnp.exp(total_all[:, s] - gc_all[:, s])).astype(BF16))
           for s in sl]
    for h, s in enumerate(sl):
        o_ref[0, :, s] = inter[h] + intra[h]
        s_scr[h] = jnp.exp(total_all[:, s]) * st[h] + upd[h]


def _gla_scan(p, f_bias, lb, seq, n_ctx, nb):
    t, pw = p.shape
    c = GLA_CHUNK
    w = H_C * HEAD
    tri = _tri_tables(c)
    sel, sgn, msk = _gla_tables(c)
    n_steps, blk = _step_block(c, seq, n_ctx, nb)
    row = lambda b, d, i: jnp.where(d == 0, blk(b, 0, i), blk(b, 1, i))
    dspec = lambda col: pl.BlockSpec((c, w), lambda b, d, i: (row(b, d, i), col(d)))
    dir_spec = lambda shp: pl.BlockSpec((1,) + shp, lambda b, d, i: (d,) + (0,) * len(shp))
    return pl.pallas_call(
        _gla_kernel,
        grid=(nb, 2, n_steps),
        in_specs=[dspec(lambda d: 0), dspec(lambda d: 1 + d), dspec(lambda d: 3),
                  dir_spec((1, w)), dir_spec((1, w)), dir_spec((c, c)),
                  dir_spec(sel.shape[1:]), dir_spec(sgn.shape[1:]), dir_spec(msk.shape[1:])],
        out_specs=pl.BlockSpec((1, c, w), lambda b, d, i: (d, row(b, d, i), 0)),
        out_shape=jax.ShapeDtypeStruct((2, t, w), F32),
        scratch_shapes=[pltpu.VMEM((H_C, HEAD, HEAD), F32)],
        compiler_params=_cparams(("parallel", "parallel", "arbitrary")),
        name="gla_scan",
    )(p, p, p, f_bias.reshape(2, 1, w), lb.reshape(2, 1, w), tri, sel, sgn, msk)


def _best_group(scores_t, bias):
    sel = [scores_t[e:e + 1, :] + bias[e:e + 1, :] for e in range(N_EXPERTS)]
    gscore = []
    for grp in range(N_GROUPS):
        v = sel[grp * EXPERTS_PER_GROUP:(grp + 1) * EXPERTS_PER_GROUP]
        best = None
        for a in range(EXPERTS_PER_GROUP):
            for b in range(a + 1, EXPERTS_PER_GROUP):
                pair = v[a] + v[b]
                best = pair if best is None else jnp.maximum(best, pair)
        gscore.append(best)
    best_g = jnp.zeros(gscore[0].shape, jnp.int32)
    best_v = gscore[0]
    for grp in range(1, N_GROUPS):
        better = gscore[grp] > best_v
        best_g = jnp.where(better, grp, best_g)
        best_v = jnp.where(better, gscore[grp], best_v)
    return best_g


def _outproj_kernel(x_ref, of0, ob0, gt0, of1, ob1, gt1, hn_ref, w_ref, g1_ref, sc_ref, sh_ref, nf_ref,
                    wr_ref, rb_ref, triu_ref, xo_ref, h_ref, grp_ref, rank_ref, cnt_ref, base_scr,
                    *, first_half_sigmoid):
    @pl.when(pl.program_id(0) == 0)
    def _():
        base_scr[...] = jnp.zeros_like(base_scr)

    feats = []
    for half, (of, ob, gt) in enumerate(((of0, ob0, gt0), (of1, ob1, gt1))):
        o = of[0] + ob[0]
        gate = gt[...]
        for h in range(o.shape[1] // HEAD):
            sl = slice(h * HEAD, (h + 1) * HEAD)
            t = o[:, sl]
            y = t * lax.rsqrt(jnp.mean(t * t, axis=-1, keepdims=True) + EPS)
            y = y * hn_ref[:, half * o.shape[1] + h * HEAD: half * o.shape[1] + (h + 1) * HEAD]
            gz = gate[:, sl]
            act = _sigmoid_pair(gz)[0] if (half == 0 and first_half_sigmoid) else _silu(gz)
            feats.append((y * act).astype(BF16))
    feats = jnp.concatenate(feats, axis=1)
    xn = x_ref[...] + g1_ref[0] * _dot(feats, w_ref[...])
    xo_ref[...] = xn
    y = xn * lax.rsqrt(jnp.mean(xn * xn, axis=-1, keepdims=True) + EPS) * nf_ref[...]
    hl = y * (1.0 + sc_ref[0]) + sh_ref[0]
    h_ref[...] = hl
    logits = _dot3(hl, wr_ref[...])
    scores_t = _sigmoid_pair(logits)[0].T
    best_g = _best_group(scores_t, rb_ref[...])
    tm = best_g.shape[1]
    onehot = jnp.concatenate([(best_g == g).astype(F32) for g in range(N_GROUPS)]
                             + [jnp.zeros((SUBLANES - N_GROUPS, tm), F32)], axis=0)
    before = _dot(onehot.astype(BF16), triu_ref[...])
    base = base_scr[...]
    rank = jnp.sum(onehot * (before + base[:, 0:1]), axis=0, keepdims=True)
    grp_ref[0] = best_g
    rank_ref[0] = rank.astype(jnp.int32)
    base = base + jnp.sum(onehot, axis=1, keepdims=True)
    base_scr[...] = base
    cnt_ref[...] = base


def _outproj(xs, o_halves, gate_src, gate_cols, hn_gain, w_out, mod, layer, nf_gain, wr_pad, rb_col,
             seq, nb, n_rows, first_half_sigmoid):
    t, d = xs.shape
    tm = TOK_TILE
    half = d // 2
    base = layer * SUBLANES * 6

    def mod_spec(which):
        return pl.BlockSpec((1, 1, d), lambda i: (base + _mod_row(i, tm, seq, nb) * 6 + which, 0, 0))

    in_specs = [pl.BlockSpec((tm, d), lambda i: (i, 0))]
    args = [xs]
    for ((arr_f, dir_f), (arr_b, dir_b), col), gcol in zip(o_halves, gate_cols):
        in_specs += [pl.BlockSpec((1, tm, half), lambda i, col=col, dd=dir_f: (dd, i, col)),
                     pl.BlockSpec((1, tm, half), lambda i, col=col, dd=dir_b: (dd, i, col)),
                     pl.BlockSpec((tm, half), lambda i, gcol=gcol: (i, gcol))]
        args += [arr_f, arr_b, gate_src]
    in_specs += [pl.BlockSpec((1, d), lambda i: (0, 0)),
                 pl.BlockSpec((d, d), lambda i: (0, 0)),
                 mod_spec(2), mod_spec(4), mod_spec(3),
                 pl.BlockSpec((1, d), lambda i: (0, 0)),
                 pl.BlockSpec((d, LANES), lambda i: (0, 0)),
                 pl.BlockSpec((N_EXPERTS, 1), lambda i: (0, 0)),
                 pl.BlockSpec((tm, tm), lambda i: (0, 0))]
    idx = np.arange(tm)
    triu = jnp.asarray(idx[:, None] < idx[None, :], dtype=BF16)
    args += [hn_gain.reshape(1, d), w_out, mod, mod, mod, nf_gain.reshape(1, d), wr_pad, rb_col, triu]
    n_tiles = n_rows // tm
    row_i32 = jax.ShapeDtypeStruct((n_tiles, 1, tm), jnp.int32)
    return pl.pallas_call(
        functools.partial(_outproj_kernel, first_half_sigmoid=first_half_sigmoid),
        grid=(n_tiles,),
        in_specs=in_specs,
        out_specs=[pl.BlockSpec((tm, d), lambda i: (i, 0)),
                   pl.BlockSpec((tm, d), lambda i: (i, 0)),
                   pl.BlockSpec((1, 1, tm), lambda i: (i, 0, 0)),
                   pl.BlockSpec((1, 1, tm), lambda i: (i, 0, 0)),
                   pl.BlockSpec((SUBLANES, LANES), lambda i: (0, 0))],
        out_shape=[jax.ShapeDtypeStruct((t, d), F32), jax.ShapeDtypeStruct((n_rows, d), F32), row_i32, row_i32,
                   jax.ShapeDtypeStruct((SUBLANES, LANES), F32)],
        scratch_shapes=[pltpu.VMEM((SUBLANES, LANES), F32)],
        compiler_params=_cparams(("arbitrary",)),
        name="outproj",
    )(*args)


def _rows_copy(src, dst, sem, src_row, dst_row):
    return pltpu.make_async_copy(src.at[pl.ds(src_row, 1), :], dst.at[pl.ds(dst_row, 1), :], sem)


def _dispatch_kernel(pos_ref, cnt_ref, h_ref, o_ref, zero_ref, sem, *, cap):
    i = pl.program_id(0)
    tm = h_ref.shape[0]

    def issue(r, carry):
        _rows_copy(h_ref, o_ref, sem, r, pos_ref[i * tm + r]).start()
        return carry

    lax.fori_loop(0, tm, issue, 0)
    pltpu.make_async_copy(h_ref, o_ref.at[pl.ds(0, tm), :], sem).wait()

    @pl.when(i == pl.num_programs(0) - 1)
    def _():
        zero_ref[...] = jnp.zeros_like(zero_ref)
        pad = zero_ref.shape[0]
        def pad_copies(action):
            for g in range(N_GROUPS):
                cnt = cnt_ref[g]
                up = (cnt + SUBLANES - 1) // SUBLANES * SUBLANES
                for r in range(SUBLANES - 1):
                    @pl.when(cnt + r < up)
                    def _():
                        action(_rows_copy(zero_ref, o_ref, sem, 0, g * cap + cnt + r))
                action(pltpu.make_async_copy(zero_ref, o_ref.at[pl.ds(pl.multiple_of(g * cap + up, SUBLANES), pad), :],
                                             sem))

        pad_copies(lambda cp: cp.start())
        pad_copies(lambda cp: cp.wait())


def _dispatch(h, pos, counts, cap):
    n_rows, d = h.shape
    tm = TOK_TILE
    return pl.pallas_call(
        functools.partial(_dispatch_kernel, cap=cap),
        grid_spec=pltpu.PrefetchScalarGridSpec(
            num_scalar_prefetch=2, grid=(n_rows // tm,),
            in_specs=[pl.BlockSpec((tm, d), lambda i, pos, cnt: (i, 0))],
            out_specs=pl.BlockSpec(memory_space=pl.ANY),
            scratch_shapes=[pltpu.VMEM((MOE_TILE, d), F32), pltpu.SemaphoreType.DMA(())]),
        out_shape=jax.ShapeDtypeStruct((N_GROUPS * cap, d), F32),
        compiler_params=_cparams(("arbitrary",)),
        name="moe_dispatch",
    )(pos, counts, h)


def _group_gates(x, wr, bias_row, grp):
    scores = _sigmoid_pair(_dot3(x, wr))[0]
    lane_i = lax.broadcasted_iota(jnp.int32, scores.shape, 1)
    lane = lane_i.astype(F32)
    m = jnp.where(lane_i // EXPERTS_PER_GROUP == grp, scores + bias_row, -jnp.inf)
    picks = []
    for _ in range(2):
        top = jnp.max(m, axis=-1, keepdims=True)
        idx = jnp.min(jnp.where(m == top, lane, float(LANES)), axis=-1, keepdims=True)
        picks.append(idx)
        m = jnp.where(lane == idx, -jnp.inf, m)
    w = [jnp.sum(jnp.where(lane == idx, scores, 0.0), axis=-1, keepdims=True) for idx in picks]
    tot = w[0] + w[1]
    return jnp.where(lane == picks[0], w[0] / tot, jnp.where(lane == picks[1], w[1] / tot, 0.0))


def _group_ffn_kernel(tg_ref, tb_ref, nv_ref, x_ref, wr_ref, rb_ref, w1_ref, w3_ref, w2_ref, o_ref,
                      xb_scr, gate_scr, acc_scr):
    i, j = pl.program_id(0), pl.program_id(1)

    @pl.when(i < nv_ref[0])
    def _():
        grp = tg_ref[i]

        @pl.when(j == 0)
        def _():
            x = x_ref[...]
            xb_scr[...] = x.astype(BF16)
            gate_scr[...] = _group_gates(x, wr_ref[...], rb_ref[...], grp)
            acc_scr[...] = jnp.zeros_like(acc_scr)

        xb = xb_scr[...]
        lane = lax.broadcasted_iota(jnp.int32, gate_scr.shape, 1)
        gcol = jnp.sum(jnp.where(lane == grp * EXPERTS_PER_GROUP + j, gate_scr[...], 0.0), axis=-1, keepdims=True)
        a = _silu(_dot(xb, w1_ref[0])) * _dot(xb, w3_ref[0]) * gcol
        acc_scr[...] += _dot(a.astype(BF16), w2_ref[0])

        @pl.when(j == pl.num_programs(1) - 1)
        def _():
            o_ref[...] = acc_scr[...]


def _group_ffn(h_sorted, tile_grp, tile_blk, n_valid, wr_pad, rb_row, w1, w3, w2, cap):
    p_rows, d = h_sorted.shape
    tm = MOE_TILE
    f = w1.shape[2]
    n_tiles = tile_grp.shape[0]
    blocks_per_group = cap // tm
    row_blk = lambda i, tg, tb, nv: (tg[i] * blocks_per_group + tb[i], 0)
    expert = lambda i, j, tg, tb, nv: (tg[i] * EXPERTS_PER_GROUP + jnp.where(i < nv[0], j, EXPERTS_PER_GROUP - 1), 0, 0)
    return pl.pallas_call(
        _group_ffn_kernel,
        grid_spec=pltpu.PrefetchScalarGridSpec(
            num_scalar_prefetch=3, grid=(n_tiles, EXPERTS_PER_GROUP),
            in_specs=[pl.BlockSpec((tm, d), lambda i, j, tg, tb, nv: row_blk(i, tg, tb, nv)),
                      pl.BlockSpec((d, LANES), lambda i, j, tg, tb, nv: (0, 0)),
                      pl.BlockSpec((1, LANES), lambda i, j, tg, tb, nv: (0, 0)),
                      pl.BlockSpec((1, d, f), expert), pl.BlockSpec((1, d, f), expert), pl.BlockSpec((1, f, d), expert)],
            out_specs=pl.BlockSpec((tm, d), lambda i, j, tg, tb, nv: row_blk(i, tg, tb, nv)),
            scratch_shapes=[pltpu.VMEM((tm, d), BF16), pltpu.VMEM((tm, LANES), F32), pltpu.VMEM((tm, d), F32)]),
        out_shape=jax.ShapeDtypeStruct((p_rows, d), F32),
        compiler_params=_cparams(("arbitrary", "arbitrary")),
        name="moe_group_ffn",
    )(tile_grp, tile_blk, n_valid, h_sorted, wr_pad, rb_row, w1, w3, w2)


def _combine_kernel(pos_ref, x_ref, y_ref, g2_ref, nfin_ref, o_ref, buf, sem, *, final_norm):
    i = pl.program_id(0)
    tm = x_ref.shape[0]

    def issue(r, carry):
        _rows_copy(y_ref, buf, sem, pos_ref[i * tm + r], r).start()
        return carry

    lax.fori_loop(0, tm, issue, 0)
    pltpu.make_async_copy(y_ref.at[pl.ds(0, tm), :], buf, sem).wait()
    y = x_ref[...] + g2_ref[0] * buf[...]
    if final_norm:
        y = y * lax.rsqrt(jnp.mean(y * y, axis=-1, keepdims=True) + EPS) * nfin_ref[...]
    o_ref[...] = y


def _combine(xs, y_sorted, pos, mod, layer, nfin, seq, nb, n_rows, final_norm):
    t, d = xs.shape
    tm = TOK_TILE
    base = layer * SUBLANES * 6
    out_rows = n_rows if final_norm else t
    return pl.pallas_call(
        functools.partial(_combine_kernel, final_norm=final_norm),
        grid_spec=pltpu.PrefetchScalarGridSpec(
            num_scalar_prefetch=1, grid=(n_rows // tm,),
            in_specs=[pl.BlockSpec((tm, d), lambda i, pos: (i, 0)),
                      pl.BlockSpec(memory_space=pl.ANY),
                      pl.BlockSpec((1, 1, d), lambda i, pos: (base + _mod_row(i, tm, seq, nb) * 6 + 5, 0, 0)),
                      pl.BlockSpec((1, d), lambda i, pos: (0, 0))],
            out_specs=pl.BlockSpec((tm, d), lambda i, pos: (i, 0)),
            scratch_shapes=[pltpu.VMEM((tm, d), F32), pltpu.SemaphoreType.DMA(())]),
        out_shape=jax.ShapeDtypeStruct((out_rows, d), F32),
        compiler_params=_cparams(("arbitrary",)),
        name="moe_combine",
    )(pos, xs, y_sorted, mod, nfin.reshape(1, d))


def _moe(xs, h, grp_rows, rank_rows, cnt, wr_pad, rb_row, w1, w3, w2, mod, layer, nfin, seq, nb, n_rows, final_norm):
    tm = MOE_TILE
    cap = n_rows + tm
    grp, rank = grp_rows.reshape(-1), rank_rows.reshape(-1)
    pos = grp * cap + rank
    counts = cnt[:N_GROUPS, 0].astype(jnp.int32)
    tiles = (counts + tm - 1) // tm
    ends = jnp.cumsum(tiles)
    n_tiles = n_rows // tm + N_GROUPS
    step = jnp.minimum(jnp.arange(n_tiles, dtype=jnp.int32), ends[-1] - 1)
    tile_grp = jnp.sum(step[:, None] >= ends[None, :], axis=1).astype(jnp.int32)
    tile_blk = step - (ends - tiles)[tile_grp]
    h_sorted = _dispatch(h, pos, counts, cap)
    y_sorted = _group_ffn(h_sorted, tile_grp, tile_blk, ends[-1:].astype(jnp.int32), wr_pad, rb_row, w1, w3, w2, cap)
    return _combine(xs, y_sorted, pos, mod, layer, nfin, seq, nb, n_rows, final_norm)


def kernel(x, c, ctx, c_ctx, w_mod, b_mod, norm_mix, norm_ffn, norm_final, ab_w_in, ab_i_bias, ab_f_bias,
           ab_conv, ab_a_log, ab_dt_bias, ab_norm_a, ab_norm_b, ab_w_out, c_w_in, c_f_bias, c_lb_raw, c_norm,
           c_w_out, w_router, router_bias, w1, w3, w2):
    nb, seq, d = x.shape
    n_ctx = ctx.shape[1]
    depth = w_mod.shape[0]
    n_lat = nb * seq
    assert nb + 1 <= SUBLANES and seq % 1024 == 0 and (nb * n_ctx) % 1024 == 0 and n_ctx % 256 == 0

    xs = jnp.concatenate([x.reshape(n_lat, d), ctx.reshape(nb * n_ctx, d)], axis=0)
    cpad = jnp.zeros((SUBLANES, d), F32).at[:nb].set(c).at[nb].set(c_ctx)
    mod = _modulation(cpad, w_mod, b_mod).reshape(depth * SUBLANES * 6, 1, d)

    lb_p = jax.nn.softmax(c_lb_raw.astype(F32), axis=0)
    lb_all = jnp.cumsum(lb_p, axis=0) - lb_p[0:1]

    wr_pad = jnp.zeros((d, LANES), F32).at[:, :N_EXPERTS].set(w_router)
    rb_col = router_bias.astype(F32).reshape(N_EXPERTS, 1)
    rb_row = jnp.zeros((1, LANES), F32).at[0, :N_EXPERTS].set(router_bias.astype(F32))
    w_main = H_AB * HEAD * 8

    out = None
    column_major = False
    for layer in range(depth):
        last = layer == depth - 1
        j = layer // 2
        want_cm = layer % 2 == 1 and j % 2 == 1
        needs_raster = layer % 2 == 0 or not want_cm
        if want_cm and not column_major:
            xs = _to_column_major(xs, n_lat, seq)
            column_major = True
        elif needs_raster and column_major:
            xs = jnp.concatenate([_from_column_major(xs[:n_lat], seq), xs[n_lat:]], axis=0)
            column_major = False
        if layer % 2 == 0:
            w_in = ab_w_in[j]
            gw = w_in[:, w_main:].reshape(d, 4, 2, H_AB)
            wg = jnp.zeros((2, d, LANES), F32).at[:, :, :4 * H_AB].set(
                jnp.transpose(gw, (2, 0, 1, 3)).reshape(2, d, 4 * H_AB)).astype(BF16)
            zeros = jnp.zeros((2, H_AB), F32)
            bias = jnp.concatenate([ab_i_bias[j], ab_f_bias[j], zeros, ab_dt_bias[j]], axis=1)
            alog = jnp.concatenate([zeros, zeros, zeros, ab_a_log[j]], axis=1)
            gpar = jnp.zeros((2, SUBLANES, LANES), F32).at[:, 0, :4 * H_AB].set(bias).at[:, 1, :4 * H_AB].set(alog)
            p, gates = _inproj(xs, norm_mix[layer], mod, layer, w_in[:, :w_main].astype(BF16), seq, nb, wg, gpar)
            ha_f, ha_b = _mlstm_scan(p, gates, seq, n_ctx, nb)
            qkv = _gdn_prep(p, ab_conv[j], seq, n_ctx, nb)
            ob_f, ob_b = _gdn_scan(qkv, gates, seq, n_ctx, nb)
            o_halves = (((ha_f, 0), (ha_b, 0), 0), ((ob_f, 0), (ob_b, 0), 0))
            gate_cols = (3, 7)
            hn_gain = jnp.concatenate([ab_norm_a[j], ab_norm_b[j]])
            w_out = ab_w_out[j]
        else:
            (p,) = _inproj(xs, norm_mix[layer], mod, layer, c_w_in[j].astype(BF16), seq, nb)
            o_c = _gla_scan(p, c_f_bias[j], lb_all[j], seq, n_ctx, nb)
            o_halves = (((o_c, 0), (o_c, 1), 0), ((o_c, 0), (o_c, 1), 1))
            gate_cols = (8, 9)
            hn_gain = c_norm[j]
            w_out = c_w_out[j]
        n_rows = n_lat if last else n_lat + nb * n_ctx
        xs, h, grp_rows, rank_rows, cnt = _outproj(xs, o_halves, p, gate_cols, hn_gain, w_out.astype(BF16), mod, layer,
                                                   norm_ffn[layer], wr_pad, rb_col, seq, nb, n_rows, layer % 2 == 0)
        res = _moe(xs, h, grp_rows, rank_rows, cnt, wr_pad, rb_row, w1[layer].astype(BF16), w3[layer].astype(BF16),
                   w2[layer].astype(BF16), mod, layer, norm_final, seq, nb, n_rows, last)
        if last:
            out = _from_column_major(res, seq) if column_major else res
        else:
            xs = res
    return out.reshape(nb, seq, d)
```

```python
import functools

import numpy as np
import jax
import jax.numpy as jnp
from jax import lax
from jax.experimental import pallas as pl
from jax.experimental.pallas import tpu as pltpu

F32 = jnp.float32
BF16 = jnp.bfloat16

EPS = 1e-6
GRID_W = 64
HEAD = 128
H_AB = 4
H_C = 8
CONV_K = 5
N_EXPERTS = 16
N_GROUPS = 4
EXPERTS_PER_GROUP = N_EXPERTS // N_GROUPS
LANES = 128
SUBLANES = 8
VMEM_LIMIT = 56 * 1024 * 1024

TOK_TILE = 512
MOE_TILE = 512
ROW_DMA_UNROLL = 8
MLSTM_CHUNK = 128
GDN_CHUNK = 64
GLA_CHUNK = 128
GDN_INV_BLOCK = 16


def _cparams(sem):
    return pltpu.CompilerParams(dimension_semantics=sem, vmem_limit_bytes=VMEM_LIMIT)


def _dot(a, b):
    return jnp.dot(a, b, preferred_element_type=F32)


def _dot_nt(a, b):
    return lax.dot_general(a, b, (((1,), (1,)), ((), ())), preferred_element_type=F32)


def _dot_tn(a, b):
    return lax.dot_general(a, b, (((0,), (0,)), ((), ())), preferred_element_type=F32)


def _split2(x):
    hi = x.astype(BF16)
    lo = (x - hi.astype(F32)).astype(BF16)
    return hi, lo


def _split3(x):
    hi = x.astype(BF16)
    r = x - hi.astype(F32)
    mid = r.astype(BF16)
    lo = (r - mid.astype(F32)).astype(BF16)
    return hi, mid, lo


def _dot_exact_lhs(m_bf16, x):
    hi, mid, lo = _split3(x)
    return _dot(m_bf16, hi) + _dot(m_bf16, mid) + _dot(m_bf16, lo)


def _dot3(a, b):
    ah, al = _split2(a)
    bh, bl = _split2(b)
    return _dot(ah, bh) + _dot(ah, bl) + _dot(al, bh)


def _dot_nt3(a, b):
    ah, al = _split2(a)
    bh, bl = _split2(b)
    return _dot_nt(ah, bh) + _dot_nt(ah, bl) + _dot_nt(al, bh)


def _sigmoid_pair(z):
    e = jnp.exp(-jnp.abs(z))
    r = 1.0 / (1.0 + e)
    er = e * r
    pos = z >= 0
    return jnp.where(pos, r, er), jnp.where(pos, er, r)


def _silu(z):
    return z * _sigmoid_pair(z)[0]


def _mod_kernel(c_ref, w_ref, b_ref, o_ref):
    s = _silu(c_ref[...]).astype(BF16)
    o_ref[0] = _dot(s, w_ref[0].astype(BF16)) + b_ref[0]


def _modulation(cpad, w_mod, b_mod):
    depth, d, n = w_mod.shape
    tn = 1536
    return pl.pallas_call(
        _mod_kernel,
        grid=(depth, n // tn),
        in_specs=[pl.BlockSpec((SUBLANES, d), lambda l, j: (0, 0)),
                  pl.BlockSpec((1, d, tn), lambda l, j: (l, 0, j)),
                  pl.BlockSpec((1, 1, tn), lambda l, j: (l, 0, j))],
        out_specs=pl.BlockSpec((1, SUBLANES, tn), lambda l, j: (l, 0, j)),
        out_shape=jax.ShapeDtypeStruct((depth, SUBLANES, n), F32),
        compiler_params=_cparams(("parallel", "parallel")),
        name="modulation",
    )(cpad, w_mod, b_mod.reshape(depth, 1, n))


def _gate_activations(raw, par):
    y = raw + par[0:1, :]
    e = jnp.exp(-jnp.abs(y))
    sp = jnp.log1p(e)
    lsig = jnp.minimum(y, 0.0) - sp
    r = 1.0 / (1.0 + e)
    sig = jnp.where(y >= 0, r, e * r)
    gdec = -jnp.exp(par[1:2, :]) * (jnp.maximum(y, 0.0) + sp)
    lane = lax.broadcasted_iota(jnp.int32, raw.shape, 1)
    return jnp.where(lane < 4, y, jnp.where(lane < 8, lsig, jnp.where(lane < 12, sig, gdec)))


def _inproj_kernel(x_ref, gain_ref, sc_ref, sh_ref, w_ref, *rest, with_gates):
    if with_gates:
        wg_ref, gpar_ref, p_ref, g_ref, u_ref = rest
    else:
        p_ref, u_ref = rest

    @pl.when(pl.program_id(1) == 0)
    def _():
        x = x_ref[...]
        y = x * lax.rsqrt(jnp.mean(x * x, axis=-1, keepdims=True) + EPS) * gain_ref[...]
        u = (y * (1.0 + sc_ref[0]) + sh_ref[0]).astype(BF16)
        u_ref[...] = u
        if with_gates:
            for d in range(2):
                g_ref[d] = _gate_activations(_dot(u, wg_ref[d]), gpar_ref[d])

    p_ref[...] = _dot(u_ref[...], w_ref[...])


def _mod_row(i, tm, seq, nb):
    return jnp.minimum((i * tm) // seq, nb)


def _inproj(xs, gain, mod, layer, w, seq, nb, wg=None, gpar=None):
    t, d = xs.shape
    p = w.shape[1]
    tm, tn = TOK_TILE, 1024
    with_gates = wg is not None
    base = layer * SUBLANES * 6

    def mod_spec(which):
        return pl.BlockSpec((1, 1, d), lambda i, j: (base + _mod_row(i, tm, seq, nb) * 6 + which, 0, 0))

    in_specs = [pl.BlockSpec((tm, d), lambda i, j: (i, 0)),
                pl.BlockSpec((1, d), lambda i, j: (0, 0)),
                mod_spec(1), mod_spec(0),
                pl.BlockSpec((d, tn), lambda i, j: (0, j))]
    args = [xs, gain.reshape(1, d), mod, mod, w]
    out_specs = [pl.BlockSpec((tm, tn), lambda i, j: (i, j))]
    out_shape = [jax.ShapeDtypeStruct((t, p), F32)]
    if with_gates:
        in_specs += [pl.BlockSpec((2, d, LANES), lambda i, j: (0, 0, 0)),
                     pl.BlockSpec((2, SUBLANES, LANES), lambda i, j: (0, 0, 0))]
        args += [wg, gpar]
        out_specs.append(pl.BlockSpec((2, tm, LANES), lambda i, j: (0, i, 0)))
        out_shape.append(jax.ShapeDtypeStruct((2, t, LANES), F32))
    return pl.pallas_call(
        functools.partial(_inproj_kernel, with_gates=with_gates),
        grid=(t // tm, p // tn),
        in_specs=in_specs, out_specs=out_specs, out_shape=out_shape,
        scratch_shapes=[pltpu.VMEM((tm, d), BF16)],
        compiler_params=_cparams(("parallel", "arbitrary")),
        name="inproj",
    )(*args)


def _gdn_prep_kernel(prev_ref, x_ref, next_ref, w_ref, o_ref, *, blocks_per_seq, n_lat_blocks, scale):
    i = pl.program_id(0)
    tm = x_ref.shape[0]
    in_lat = i < n_lat_blocks
    first = jnp.logical_or(jnp.logical_not(in_lat), i % blocks_per_seq == 0)
    last = jnp.logical_or(jnp.logical_not(in_lat), i % blocks_per_seq == blocks_per_seq - 1)
    prev = jnp.where(first, 0.0, prev_ref[...])
    nxt = jnp.where(last, 0.0, next_ref[...])
    xe = jnp.concatenate([prev, x_ref[...], nxt], axis=0)
    n = xe.shape[0]
    pad = (CONV_K - 1) // 2
    acc = None
    for j in range(CONV_K):
        shift = (pad - j) % n
        xr = xe if shift == 0 else pltpu.roll(xe, shift, 0)
        term = xr[SUBLANES:SUBLANES + tm] * w_ref[j:j + 1, :]
        acc = term if acc is None else acc + term
    y = _silu(acc)
    part = pl.program_id(1)
    for h in range(H_AB):
        sl = slice(h * HEAD, (h + 1) * HEAD)
        t = y[:, sl]
        inv_norm = lax.rsqrt(jnp.sum(t * t, axis=-1, keepdims=True) + EPS)
        factor = jnp.where(part == 0, inv_norm * scale, jnp.where(part == 1, inv_norm, 1.0))
        o_ref[:, sl] = t * factor


def _gdn_prep(p, conv_w, seq, n_ctx, nb):
    t = p.shape[0]
    tm = 256
    w = H_AB * HEAD
    first_col = 4
    wpad = jnp.zeros((SUBLANES, 3 * w), F32).at[:CONV_K].set(conv_w)
    hb = tm // SUBLANES
    nblk8 = t // SUBLANES
    return pl.pallas_call(
        functools.partial(_gdn_prep_kernel, blocks_per_seq=seq // tm, n_lat_blocks=nb * seq // tm,
                          scale=HEAD ** -0.5),
        grid=(t // tm, 3),
        in_specs=[pl.BlockSpec((SUBLANES, w), lambda i, c: (jnp.maximum(i * hb - 1, 0), first_col + c)),
                  pl.BlockSpec((tm, w), lambda i, c: (i, first_col + c)),
                  pl.BlockSpec((SUBLANES, w), lambda i, c: (jnp.minimum((i + 1) * hb, nblk8 - 1), first_col + c)),
                  pl.BlockSpec((SUBLANES, w), lambda i, c: (0, c))],
        out_specs=pl.BlockSpec((tm, w), lambda i, c: (i, c)),
        out_shape=jax.ShapeDtypeStruct((t, 3 * w), F32),
        compiler_params=_cparams(("parallel", "parallel")),
        name="gdn_prep",
    )(p, p, p, wpad)


def _tri_tables(c):
    idx = np.arange(c)
    fwd = (idx[None, :] <= idx[:, None]).astype(np.float32)
    return jnp.asarray(np.stack([fwd, fwd.T]))


def _to_cm_kernel(x3_ref, x2_ref, o_ref, *, n_lat_tiles):
    i = pl.program_id(0)
    rows = x3_ref.shape[0]

    @pl.when(i < n_lat_tiles)
    def _():
        for wl in range(SUBLANES):
            o_ref[wl * rows:(wl + 1) * rows, :] = x3_ref[:, wl, :]

    @pl.when(i >= n_lat_tiles)
    def _():
        o_ref[...] = x2_ref[...]


def _from_cm_kernel(x_ref, o_ref):
    rows = o_ref.shape[0]
    for wl in range(SUBLANES):
        o_ref[:, wl, :] = x_ref[wl * rows:(wl + 1) * rows, :]


def _to_column_major(xs, n_lat, seq):
    t, d = xs.shape
    rows = seq // GRID_W
    tile = rows * SUBLANES
    assert rows % SUBLANES == 0 and n_lat % tile == 0 and (t - n_lat) % tile == 0
    n_lat_tiles = n_lat // tile
    per_b = GRID_W // SUBLANES
    lat = lambda i: jnp.minimum(i, n_lat_tiles - 1)
    return pl.pallas_call(
        functools.partial(_to_cm_kernel, n_lat_tiles=n_lat_tiles),
        grid=(t // tile,),
        in_specs=[pl.BlockSpec((rows, SUBLANES, d), lambda i: (lat(i) // per_b, lat(i) % per_b, 0)),
                  pl.BlockSpec((tile, d), lambda i: (jnp.maximum(i, n_lat_tiles - 1), 0))],
        out_specs=pl.BlockSpec((tile, d), lambda i: (i, 0)),
        out_shape=jax.ShapeDtypeStruct((t, d), xs.dtype),
        compiler_params=_cparams(("parallel",)),
        name="to_column_major",
    )(xs.reshape(t // GRID_W, GRID_W, d), xs)


def _from_column_major(y, seq):
    n_lat, d = y.shape
    rows = seq // GRID_W
    tile = rows * SUBLANES
    per_b = GRID_W // SUBLANES
    out = pl.pallas_call(
        _from_cm_kernel,
        grid=(n_lat // tile,),
        in_specs=[pl.BlockSpec((tile, d), lambda i: (i, 0))],
        out_specs=pl.BlockSpec((rows, SUBLANES, d), lambda i: (i // per_b, i % per_b, 0)),
        out_shape=jax.ShapeDtypeStruct((n_lat // GRID_W, GRID_W, d), y.dtype),
        compiler_params=_cparams(("parallel",)),
        name="from_column_major",
    )(y)
    return out.reshape(n_lat, d)


def _mlstm_kernel(qf, kf, vf, qb, kb, vb, gf, gb, tri_ref, hf_ref, hb_ref, s_scr, m_scr, *, scale):
    @pl.when(pl.program_id(1) == 0)
    def _():
        s_scr[...] = jnp.zeros_like(s_scr)
        m_scr[...] = jnp.zeros_like(m_scr)

    c = gf.shape[1]
    ones_col = (lax.broadcasted_iota(jnp.int32, (c, HEAD), 1) == 0).astype(BF16)
    qkv = ((qf, kf, vf), (qb, kb, vb))
    h_out = (hf_ref, hb_ref)
    g = [gf[0], gb[0]]
    mask = [tri_ref[d] > 0.5 for d in range(2)]
    b = [_dot_exact_lhs(tri_ref[d].astype(BF16), g[d]) for d in range(2)]
    total = [jnp.sum(x, axis=0, keepdims=True) for x in g]
    g_t = [x.T for x in g]
    b_t = [x.T for x in b]
    chains = [(d, h) for d in range(2) for h in range(H_AB)]
    sl = [slice(h * HEAD, (h + 1) * HEAD) for d, h in chains]
    q = [qkv[d][0][:, sl[i]].astype(BF16) for i, (d, h) in enumerate(chains)]
    k = [qkv[d][1][:, sl[i]] * scale for i, (d, h) in enumerate(chains)]
    v_aug = [jnp.concatenate([qkv[d][2][:, sl[i]].astype(BF16), ones_col], axis=1) for i, (d, h) in enumerate(chains)]
    s = [s_scr[d, h] for d, h in chains]
    qk = [_dot_nt(q[i], k[i].astype(BF16)) for i in range(len(chains))]
    qs = [_dot(q[i], s[i].astype(BF16)) for i in range(len(chains))]
    w_intra, w_inter, m_ts, colvs, m_sts, tots = [], [], [], [], [], []
    for i, (d, h) in enumerate(chains):
        li_c, li_r = g[d][:, h:h + 1], g_t[d][h:h + 1, :]
        b_c, b_r = b[d][:, 4 + h:5 + h], b_t[d][4 + h:5 + h, :]
        tot = total[d][:, 4 + h:5 + h]
        m_st = m_scr[d, h][0:1, 0:1]
        d_log = jnp.where(mask[d], b_c - b_r + li_r, -jnp.inf)
        m_inter = b_c + m_st
        m_t = jnp.maximum(m_inter, jnp.max(d_log, axis=-1, keepdims=True))
        w_inter.append(jnp.exp(m_inter - m_t))
        w_intra.append((jnp.exp(d_log - m_t) * qk[i]).astype(BF16))
        m_ts.append(m_t); colvs.append(tot - b_c + li_c); m_sts.append(m_st); tots.append(tot)
    intra = [_dot(w_intra[i], v_aug[i]) for i in range(len(chains))]
    m_new = [jnp.maximum(tots[i] + m_sts[i], jnp.max(colvs[i], axis=0, keepdims=True)) for i in range(len(chains))]
    upd = [_dot_tn((k[i] * jnp.exp(colvs[i] - m_new[i])).astype(BF16), v_aug[i]) for i in range(len(chains))]
    for i, (d, h) in enumerate(chains):
        num = w_inter[i] * qs[i] + intra[i]
        den = num[:, HEAD:HEAD + 1]
        h_out[d][0, :, sl[i]] = num[:, :HEAD] / jnp.maximum(jnp.abs(den), jnp.exp(-m_ts[i]))
        s_scr[d, h] = jnp.exp(tots[i] + m_sts[i] - m_new[i]) * s[i] + upd[i]
        m_scr[d, h] = jnp.broadcast_to(m_new[i], (SUBLANES, LANES))


def _step_block(chunk, seq, n_ctx, nb):
    n_c, n_l = n_ctx // chunk, seq // chunk
    off = nb * n_l

    def fn(b, d, i):
        j = i - n_c
        ctx_blk = off + b * n_c + (i if d == 0 else n_c - 1 - i)
        lat_blk = b * n_l + (j if d == 0 else n_l - 1 - j)
        return jnp.where(i < n_c, ctx_blk, lat_blk)

    return n_c + n_l, fn


def _mlstm_scan(p, gates, seq, n_ctx, nb):
    t = p.shape[0]
    c = MLSTM_CHUNK
    w = H_AB * HEAD
    n_steps, blk = _step_block(c, seq, n_ctx, nb)
    in_specs = [pl.BlockSpec((c, w), lambda b, i, d=d, col=col: (blk(b, d, i), col)) for d in range(2) for col in range(3)]
    in_specs += [pl.BlockSpec((1, c, LANES), lambda b, i, d=d: (d, blk(b, d, i), 0)) for d in range(2)]
    in_specs.append(pl.BlockSpec((2, c, c), lambda b, i: (0, 0, 0)))
    return pl.pallas_call(
        functools.partial(_mlstm_kernel, scale=HEAD ** -0.5),
        grid=(nb, n_steps),
        in_specs=in_specs,
        out_specs=[pl.BlockSpec((1, c, w), lambda b, i, d=d: (0, blk(b, d, i), 0)) for d in range(2)],
        out_shape=[jax.ShapeDtypeStruct((1, t, w), F32)] * 2,
        scratch_shapes=[pltpu.VMEM((2, H_AB, HEAD, 2 * HEAD), F32), pltpu.VMEM((2, H_AB, SUBLANES, LANES), F32)],
        compiler_params=_cparams(("parallel", "arbitrary")),
        name="mlstm_scan",
    )(p, p, p, p, p, p, gates, gates, _tri_tables(c))


def _gdn_inv_masks(c):
    idx = np.arange(c)
    blk = lambda n: (idx[:, None] // n) == (idx[None, :] // n)
    levels = [blk(GDN_INV_BLOCK)]
    n = GDN_INV_BLOCK
    while n < c:
        levels.append(np.logical_and(blk(2 * n), np.logical_not(blk(n))))
        n *= 2
    levels = np.stack(levels).astype(np.float32)
    return jnp.asarray(np.tile(levels, (1, 1, H_AB)))


def _block_diag(x, n_blocks):
    blk = lax.broadcasted_iota(jnp.int32, x.shape, 1) // (x.shape[1] // n_blocks)
    return jnp.concatenate([jnp.where(blk == j, x, jnp.zeros_like(x)) for j in range(n_blocks)], axis=0)


def _heads_to_lanes(x, first, width):
    return jnp.concatenate([jnp.broadcast_to(x[:, first + h:first + h + 1], (x.shape[0], width))
                            for h in range(H_AB)], axis=1)


def _unit_tri_inverse(a_list, eye, inv_masks):
    prod = lambda xs, ys: [_dot(x.astype(BF16), _block_diag(y.astype(BF16), H_AB)) for x, y in zip(xs, ys)]
    pw = [a * inv_masks[0] for a in a_list]
    inv = [eye - p for p in pw]
    n = 2
    while n < GDN_INV_BLOCK:
        pw = prod(pw, pw)
        inv = [i + t for i, t in zip(inv, prod(inv, pw))]
        n *= 2
    for lvl in range(1, inv_masks.shape[0]):
        t = prod(prod(inv, [a * inv_masks[lvl] for a in a_list]), inv)
        inv = [i - x for i, x in zip(inv, t)]
    return inv


def _gdn_chunk_kernel(q_ref, k_ref, v_ref, g_ref, tri_ref, im_ref, u_ref, w_ref, qd_ref, kd_ref, att_ref):
    c = GDN_CHUNK
    n_chunks = q_ref.shape[0] // c
    inv_masks = im_ref[...]
    row = lax.broadcasted_iota(jnp.int32, (c, H_AB * c), 0)
    col = lax.broadcasted_iota(jnp.int32, (c, H_AB * c), 1) % c
    eye_b = row == col
    eye = eye_b.astype(F32)
    masks = [col <= row, col >= row]
    stricts = [col < row, col > row]
    tris = [tri_ref[d].astype(BF16) for d in range(2)]
    ones = jnp.ones((c, c), BF16)
    chunks = range(n_chunks)
    rows = [slice(ci * c, (ci + 1) * c) for ci in chunks]
    groups = [(ci, d) for ci in chunks for d in range(2)]
    k_hl = [_split2(k_ref[r, :]) for r in rows]
    kbd = [(_block_diag(hi, H_AB), _block_diag(lo, H_AB)) for hi, lo in k_hl]
    r_hi = [_dot_nt(jnp.concatenate([k_hl[ci][0], k_hl[ci][1], q_ref[rows[ci], :].astype(BF16)], axis=0), kbd[ci][0])
            for ci in chunks]
    r_lo = [_dot_nt(k_hl[ci][0], kbd[ci][1]) for ci in chunks]
    kk = [r_hi[ci][:c] + r_hi[ci][c:2 * c] + r_lo[ci] for ci in chunks]
    g = [g_ref[d, rows[ci], :] for ci, d in groups]
    gc = [_dot_exact_lhs(tris[d], g[i]) for i, (ci, d) in enumerate(groups)]
    gc_c = [_heads_to_lanes(x, 12, c) for x in gc]
    gc_r = [_dot_exact_lhs(ones, eye * x) for x in gc_c]
    decay = [jnp.exp(jnp.where(masks[d], gc_c[i] - gc_r[i], -jnp.inf)) for i, (ci, d) in enumerate(groups)]
    a = [jnp.where(stricts[d], _heads_to_lanes(g[i], 8, c) * kk[ci] * decay[i], 0.0)
         for i, (ci, d) in enumerate(groups)]
    inv = _unit_tri_inverse(a, eye, inv_masks)
    inv_hl = [jnp.concatenate(_split2(x), axis=0) for x in inv]
    gc_w = [_heads_to_lanes(x, 12, HEAD) for x in gc]
    eg = [jnp.exp(x) for x in gc_w]
    beta = [_heads_to_lanes(x, 8, HEAD) for x in g]
    res_u = [_dot(inv_hl[i], _block_diag((v_ref[rows[ci], :] * beta[i]).astype(BF16), H_AB))
             for i, (ci, d) in enumerate(groups)]
    res_w = [_dot(inv_hl[i], _block_diag((k_ref[rows[ci], :] * (beta[i] * eg[i])).astype(BF16), H_AB))
             for i, (ci, d) in enumerate(groups)]
    for i, (ci, d) in enumerate(groups):
        r = rows[ci]
        total = _heads_to_lanes(jnp.sum(g[i], axis=0, keepdims=True), 12, HEAD)
        u_ref[d, r, :] = (res_u[i][:c] + res_u[i][c:]).astype(BF16)
        w_ref[d, r, :] = (res_w[i][:c] + res_w[i][c:]).astype(BF16)
        qd_ref[d, r, :] = (q_ref[r, :] * eg[i]).astype(BF16)
        kd_ref[d, r, :] = (k_ref[r, :] * jnp.exp(total - gc_w[i])).astype(BF16)
        att_ref[d, r, :] = (r_hi[ci][2 * c:] * decay[i]).astype(BF16)


def _gdn_scan_kernel(*refs):
    ins, o_refs, s_scr = refs[:12], refs[12:14], refs[14]
    c = GDN_CHUNK

    @pl.when(pl.program_id(1) == 0)
    def _():
        s_scr[...] = jnp.zeros_like(s_scr)

    pw = 2 * HEAD
    rblk = lax.broadcasted_iota(jnp.int32, (pw, pw), 0) // HEAD
    cblk = lax.broadcasted_iota(jnp.int32, (pw, pw), 1) // HEAD
    on_diag = rblk == cblk
    chains = [(d, pr) for d in range(2) for pr in range(H_AB // 2)]
    refs_of = lambda d: ins[d::2]
    sl = [slice(pr * pw, (pr + 1) * pw) for d, pr in chains]
    s = [s_scr[d, pr] for d, pr in chains]
    res = [_dot(jnp.concatenate([refs_of(d)[1][0, :, sl[i]], refs_of(d)[2][0, :, sl[i]]], axis=0), s[i].astype(BF16))
           for i, (d, pr) in enumerate(chains)]
    v_new = [(refs_of(d)[0][0, :, sl[i]].astype(F32) - res[i][:c]).astype(BF16) for i, (d, pr) in enumerate(chains)]
    intra = [_dot(refs_of(d)[4][0, :, pr * 2 * c:(pr + 1) * 2 * c], _block_diag(v_new[i], 2))
             for i, (d, pr) in enumerate(chains)]
    upd = [_dot_tn(refs_of(d)[3][0, :, sl[i]], v_new[i]) for i, (d, pr) in enumerate(chains)]
    totals = [jnp.sum(refs_of(d)[5][0], axis=0, keepdims=True) for d in range(2)]
    for i, (d, pr) in enumerate(chains):
        o_refs[d][0, :, sl[i]] = res[i][c:] + intra[i]
        gl = jnp.concatenate([jnp.broadcast_to(jnp.exp(totals[d][:, 12 + 2 * pr + j:13 + 2 * pr + j]), (1, HEAD))
                              for j in range(2)], axis=1)
        s_scr[d, pr] = gl * s[i] + jnp.where(on_diag, upd[i], 0.0)


def _gdn_scan(qkv, gates, seq, n_ctx, nb):
    t = qkv.shape[0]
    c = GDN_CHUNK
    w = H_AB * HEAD
    tri = _tri_tables(c)
    im = _gdn_inv_masks(c)
    tm = TOK_TILE
    sds = lambda width: jax.ShapeDtypeStruct((2, t, width), BF16)
    u, wv, qd, kd, att = pl.pallas_call(
        _gdn_chunk_kernel,
        grid=(t // tm,),
        in_specs=[pl.BlockSpec((tm, w), lambda i, col=col: (i, col)) for col in range(3)] + [
            pl.BlockSpec((2, tm, LANES), lambda i: (0, i, 0)),
            pl.BlockSpec((2, c, c), lambda i: (0, 0, 0)),
            pl.BlockSpec(im.shape, lambda i: (0, 0, 0))],
        out_specs=[pl.BlockSpec((2, tm, w), lambda i: (0, i, 0))] * 4
        + [pl.BlockSpec((2, tm, H_AB * c), lambda i: (0, i, 0))],
        out_shape=[sds(w)] * 4 + [sds(H_AB * c)],
        compiler_params=_cparams(("parallel",)),
        name="gdn_chunk",
    )(qkv, qkv, qkv, gates, tri, im)

    n_steps, blk = _step_block(c, seq, n_ctx, nb)
    in_specs, args = [], []
    for arr, width in ((u, w), (wv, w), (qd, w), (kd, w), (att, H_AB * c), (gates, LANES)):
        for d in range(2):
            in_specs.append(pl.BlockSpec((1, c, width), lambda b, i, d=d: (d, blk(b, d, i), 0)))
            args.append(arr)
    o_f, o_b = pl.pallas_call(
        _gdn_scan_kernel,
        grid=(nb, n_steps),
        in_specs=in_specs,
        out_specs=[pl.BlockSpec((1, c, w), lambda b, i, d=d: (0, blk(b, d, i), 0)) for d in range(2)],
        out_shape=[jax.ShapeDtypeStruct((1, t, w), F32)] * 2,
        scratch_shapes=[pltpu.VMEM((2, H_AB // 2, 2 * HEAD, 2 * HEAD), F32)],
        compiler_params=_cparams(("parallel", "arbitrary")),
        name="gdn_scan",
    )(*args)
    return o_f, o_b


def _gla_tables(c):
    idx = np.arange(c)
    sel, sgn, msk = [], [], []
    for d in range(2):
        sel_d, sgn_d, msk_d = [], [], []
        bs = c // 2
        while bs >= 1:
            pair = idx // (2 * bs)
            second = (idx // bs) % 2 == 1
            ref_row = pair * 2 * bs + (bs - 1 if d == 0 else bs)
            sel_d.append((idx[None, :] == ref_row[:, None]).astype(np.float32))
            qside = second if d == 0 else np.logical_not(second)
            sgn_d.append(np.where(qside, 1.0, -1.0)[:, None] * np.ones((1, HEAD)))
            same_pair = pair[:, None] == pair[None, :]
            msk_d.append(np.logical_and(same_pair, np.logical_and(qside[:, None], np.logical_not(qside)[None, :])))
            bs //= 2
        msk_d.append(np.eye(c, dtype=bool))
        sel.append(np.concatenate(sel_d, axis=0))
        sgn.append(np.stack(sgn_d))
        msk.append(np.stack(msk_d).astype(np.float32))
    return (jnp.asarray(np.stack(sel), dtype=BF16), jnp.asarray(np.stack(sgn), dtype=F32),
            jnp.asarray(np.stack(msk), dtype=F32))


def _gla_kernel(q_ref, z_ref, v_ref, fb_ref, lb_ref, tri_ref, sel_ref, sgn_ref, msk_ref, o_ref, s_scr):
    @pl.when(pl.program_id(2) == 0)
    def _():
        s_scr[...] = jnp.zeros_like(s_scr)

    c = q_ref.shape[0]
    n_lvl = sgn_ref.shape[1]
    heads = range(H_C)
    sl = [slice(h * HEAD, (h + 1) * HEAD) for h in heads]
    lb = lb_ref[0]
    z = z_ref[...] + fb_ref[0]
    sig_pos, sig_neg = _sigmoid_pair(z)
    g_all = jnp.log(lb + (1.0 - lb) * sig_pos)
    key_all = (1.0 - lb) * sig_neg
    q_all = _silu(q_ref[...])
    gc_all = _dot_exact_lhs(tri_ref[0].astype(BF16), g_all)
    total_all = jnp.sum(g_all, axis=0, keepdims=True)
    refs = _dot(sel_ref[0], gc_all.astype(BF16))
    diag = msk_ref[0, n_lvl] > 0.5
    att = [jnp.where(diag, _dot_nt(q_all[:, s].astype(BF16), key_all[:, s].astype(BF16)), 0.0) for s in sl]
    for l in range(n_lvl):
        sgn = sgn_ref[0, l]
        q_side = sgn > 0
        ref_l = refs[l * c:(l + 1) * c]
        x = [(jnp.where(q_side, q_all[:, s], key_all[:, s]) * jnp.exp(sgn * (gc_all[:, s] - ref_l[:, s]))).astype(BF16)
             for s in sl]
        prod = [_dot_nt(xh, xh) for xh in x]
        in_level = msk_ref[0, l] > 0.5
        att = [jnp.where(in_level, p, a) for a, p in zip(att, prod)]
    st = [s_scr[h] for h in heads]
    inter = [_dot_nt((q_all[:, s] * jnp.exp(gc_all[:, s])).astype(BF16), st[h].astype(BF16)) for h, s in enumerate(sl)]
    intra = [_dot(att[h].astype(BF16), v_ref[:, s].astype(BF16)) for h, s in enumerate(sl)]
    upd = [_dot_tn(v_ref[:, s].astype(BF16), (key_all[:, s] * jnp.exp(total_all[:, s] - gc_all[:, s])).astype(BF16))
           for s in sl]
    for h, s in enumerate(sl):
        o_ref[0, :, s] = inter[h] + intra[h]
        s_scr[h] = jnp.exp(total_all[:, s]) * st[h] + upd[h]


def _gla_scan(p, f_bias, lb, seq, n_ctx, nb):
    t, pw = p.shape
    c = GLA_CHUNK
    w = H_C * HEAD
    tri = _tri_tables(c)
    sel, sgn, msk = _gla_tables(c)
    n_steps, blk = _step_block(c, seq, n_ctx, nb)
    row = lambda b, d, i: jnp.where(d == 0, blk(b, 0, i), blk(b, 1, i))
    dspec = lambda col: pl.BlockSpec((c, w), lambda b, d, i: (row(b, d, i), col(d)))
    dir_spec = lambda shp: pl.BlockSpec((1,) + shp, lambda b, d, i: (d,) + (0,) * len(shp))
    return pl.pallas_call(
        _gla_kernel,
        grid=(nb, 2, n_steps),
        in_specs=[dspec(lambda d: 0), dspec(lambda d: 1 + d), dspec(lambda d: 3),
                  dir_spec((1, w)), dir_spec((1, w)), dir_spec((c, c)),
                  dir_spec(sel.shape[1:]), dir_spec(sgn.shape[1:]), dir_spec(msk.shape[1:])],
        out_specs=pl.BlockSpec((1, c, w), lambda b, d, i: (d, row(b, d, i), 0)),
        out_shape=jax.ShapeDtypeStruct((2, t, w), F32),
        scratch_shapes=[pltpu.VMEM((H_C, HEAD, HEAD), F32)],
        compiler_params=_cparams(("parallel", "parallel", "arbitrary")),
        name="gla_scan",
    )(p, p, p, f_bias.reshape(2, 1, w), lb.reshape(2, 1, w), tri, sel, sgn, msk)


def _best_group(scores_t, bias):
    sel = [scores_t[e:e + 1, :] + bias[e:e + 1, :] for e in range(N_EXPERTS)]
    gscore = []
    for grp in range(N_GROUPS):
        v = sel[grp * EXPERTS_PER_GROUP:(grp + 1) * EXPERTS_PER_GROUP]
        best = None
        for a in range(EXPERTS_PER_GROUP):
            for b in range(a + 1, EXPERTS_PER_GROUP):
                pair = v[a] + v[b]
                best = pair if best is None else jnp.maximum(best, pair)
        gscore.append(best)
    best_g = jnp.zeros(gscore[0].shape, jnp.int32)
    best_v = gscore[0]
    for grp in range(1, N_GROUPS):
        better = gscore[grp] > best_v
        best_g = jnp.where(better, grp, best_g)
        best_v = jnp.where(better, gscore[grp], best_v)
    return best_g


def _outproj_kernel(x_ref, of0, ob0, gt0, of1, ob1, gt1, hn_ref, w_ref, g1_ref, sc_ref, sh_ref, nf_ref,
                    wr_ref, rb_ref, triu_ref, xo_ref, h_ref, grp_ref, rank_ref, cnt_ref, base_scr,
                    *, first_half_sigmoid):
    @pl.when(pl.program_id(0) == 0)
    def _():
        base_scr[...] = jnp.zeros_like(base_scr)

    feats = []
    for half, (of, ob, gt) in enumerate(((of0, ob0, gt0), (of1, ob1, gt1))):
        o = of[0] + ob[0]
        gate = gt[...]
        for h in range(o.shape[1] // HEAD):
            sl = slice(h * HEAD, (h + 1) * HEAD)
            t = o[:, sl]
            y = t * lax.rsqrt(jnp.mean(t * t, axis=-1, keepdims=True) + EPS)
            y = y * hn_ref[:, half * o.shape[1] + h * HEAD: half * o.shape[1] + (h + 1) * HEAD]
            gz = gate[:, sl]
            act = _sigmoid_pair(gz)[0] if (half == 0 and first_half_sigmoid) else _silu(gz)
            feats.append((y * act).astype(BF16))
    feats = jnp.concatenate(feats, axis=1)
    xn = x_ref[...] + g1_ref[0] * _dot(feats, w_ref[...])
    xo_ref[...] = xn
    y = xn * lax.rsqrt(jnp.mean(xn * xn, axis=-1, keepdims=True) + EPS) * nf_ref[...]
    hl = y * (1.0 + sc_ref[0]) + sh_ref[0]
    h_ref[...] = hl
    logits = _dot3(hl, wr_ref[...])
    scores_t = _sigmoid_pair(logits)[0].T
    best_g = _best_group(scores_t, rb_ref[...])
    tm = best_g.shape[1]
    onehot = jnp.concatenate([(best_g == g).astype(F32) for g in range(N_GROUPS)]
                             + [jnp.zeros((SUBLANES - N_GROUPS, tm), F32)], axis=0)
    before = _dot(onehot.astype(BF16), triu_ref[...])
    base = base_scr[...]
    rank = jnp.sum(onehot * (before + base[:, 0:1]), axis=0, keepdims=True)
    grp_ref[0] = best_g
    rank_ref[0] = rank.astype(jnp.int32)
    base = base + jnp.sum(onehot, axis=1, keepdims=True)
    base_scr[...] = base
    cnt_ref[...] = base


def _outproj(xs, o_halves, gate_src, gate_cols, hn_gain, w_out, mod, layer, nf_gain, wr_pad, rb_col,
             seq, nb, n_rows, first_half_sigmoid):
    t, d = xs.shape
    tm = TOK_TILE
    half = d // 2
    base = layer * SUBLANES * 6

    def mod_spec(which):
        return pl.BlockSpec((1, 1, d), lambda i: (base + _mod_row(i, tm, seq, nb) * 6 + which, 0, 0))

    in_specs = [pl.BlockSpec((tm, d), lambda i: (i, 0))]
    args = [xs]
    for ((arr_f, dir_f), (arr_b, dir_b), col), gcol in zip(o_halves, gate_cols):
        in_specs += [pl.BlockSpec((1, tm, half), lambda i, col=col, dd=dir_f: (dd, i, col)),
                     pl.BlockSpec((1, tm, half), lambda i, col=col, dd=dir_b: (dd, i, col)),
                     pl.BlockSpec((tm, half), lambda i, gcol=gcol: (i, gcol))]
        args += [arr_f, arr_b, gate_src]
    in_specs += [pl.BlockSpec((1, d), lambda i: (0, 0)),
                 pl.BlockSpec((d, d), lambda i: (0, 0)),
                 mod_spec(2), mod_spec(4), mod_spec(3),
                 pl.BlockSpec((1, d), lambda i: (0, 0)),
                 pl.BlockSpec((d, LANES), lambda i: (0, 0)),
                 pl.BlockSpec((N_EXPERTS, 1), lambda i: (0, 0)),
                 pl.BlockSpec((tm, tm), lambda i: (0, 0))]
    idx = np.arange(tm)
    triu = jnp.asarray(idx[:, None] < idx[None, :], dtype=BF16)
    args += [hn_gain.reshape(1, d), w_out, mod, mod, mod, nf_gain.reshape(1, d), wr_pad, rb_col, triu]
    n_tiles = n_rows // tm
    row_i32 = jax.ShapeDtypeStruct((n_tiles, 1, tm), jnp.int32)
    return pl.pallas_call(
        functools.partial(_outproj_kernel, first_half_sigmoid=first_half_sigmoid),
        grid=(n_tiles,),
        in_specs=in_specs,
        out_specs=[pl.BlockSpec((tm, d), lambda i: (i, 0)),
                   pl.BlockSpec((tm, d), lambda i: (i, 0)),
                   pl.BlockSpec((1, 1, tm), lambda i: (i, 0, 0)),
                   pl.BlockSpec((1, 1, tm), lambda i: (i, 0, 0)),
                   pl.BlockSpec((SUBLANES, LANES), lambda i: (0, 0))],
        out_shape=[jax.ShapeDtypeStruct((t, d), F32), jax.ShapeDtypeStruct((n_rows, d), F32), row_i32, row_i32,
                   jax.ShapeDtypeStruct((SUBLANES, LANES), F32)],
        scratch_shapes=[pltpu.VMEM((SUBLANES, LANES), F32)],
        compiler_params=_cparams(("arbitrary",)),
        name="outproj",
    )(*args)


def _rows_copy(src, dst, sem, src_row, dst_row):
    return pltpu.make_async_copy(src.at[pl.ds(src_row, 1), :], dst.at[pl.ds(dst_row, 1), :], sem)


def _dispatch_kernel(pos_ref, cnt_ref, h_ref, o_ref, zero_ref, sem, *, cap):
    i = pl.program_id(0)
    tm = h_ref.shape[0]

    def issue(r, carry):
        _rows_copy(h_ref, o_ref, sem, r, pos_ref[i * tm + r]).start()
        return carry

    lax.fori_loop(0, tm, issue, 0, unroll=ROW_DMA_UNROLL)
    pltpu.make_async_copy(h_ref, o_ref.at[pl.ds(0, tm), :], sem).wait()

    @pl.when(i == pl.num_programs(0) - 1)
    def _():
        zero_ref[...] = jnp.zeros_like(zero_ref)
        pad = zero_ref.shape[0]
        def pad_copies(action):
            for g in range(N_GROUPS):
                cnt = cnt_ref[g]
                up = (cnt + SUBLANES - 1) // SUBLANES * SUBLANES
                for r in range(SUBLANES - 1):
                    @pl.when(cnt + r < up)
                    def _():
                        action(_rows_copy(zero_ref, o_ref, sem, 0, g * cap + cnt + r))
                action(pltpu.make_async_copy(zero_ref, o_ref.at[pl.ds(pl.multiple_of(g * cap + up, SUBLANES), pad), :],
                                             sem))

        pad_copies(lambda cp: cp.start())
        pad_copies(lambda cp: cp.wait())


def _dispatch(h, pos, counts, cap):
    n_rows, d = h.shape
    tm = TOK_TILE
    return pl.pallas_call(
        functools.partial(_dispatch_kernel, cap=cap),
        grid_spec=pltpu.PrefetchScalarGridSpec(
            num_scalar_prefetch=2, grid=(n_rows // tm,),
            in_specs=[pl.BlockSpec((tm, d), lambda i, pos, cnt: (i, 0))],
            out_specs=pl.BlockSpec(memory_space=pl.ANY),
            scratch_shapes=[pltpu.VMEM((MOE_TILE, d), F32), pltpu.SemaphoreType.DMA(())]),
        out_shape=jax.ShapeDtypeStruct((N_GROUPS * cap, d), F32),
        compiler_params=_cparams(("arbitrary",)),
        name="moe_dispatch",
    )(pos, counts, h)


def _group_gates(x, wr, bias_row, grp):
    scores = _sigmoid_pair(_dot3(x, wr))[0]
    lane_i = lax.broadcasted_iota(jnp.int32, scores.shape, 1)
    lane = lane_i.astype(F32)
    m = jnp.where(lane_i // EXPERTS_PER_GROUP == grp, scores + bias_row, -jnp.inf)
    picks = []
    for _ in range(2):
        top = jnp.max(m, axis=-1, keepdims=True)
        idx = jnp.min(jnp.where(m == top, lane, float(LANES)), axis=-1, keepdims=True)
        picks.append(idx)
        m = jnp.where(lane == idx, -jnp.inf, m)
    w = [jnp.sum(jnp.where(lane == idx, scores, 0.0), axis=-1, keepdims=True) for idx in picks]
    tot = w[0] + w[1]
    return jnp.where(lane == picks[0], w[0] / tot, jnp.where(lane == picks[1], w[1] / tot, 0.0))


def _group_ffn_kernel(tg_ref, tb_ref, nv_ref, x_ref, wr_ref, rb_ref, w1_ref, w3_ref, w2_ref, o_ref):
    i = pl.program_id(0)

    @pl.when(i < nv_ref[0])
    def _():
        grp = tg_ref[i]
        x = x_ref[...]
        xb = x.astype(BF16)
        gate = _group_gates(x, wr_ref[...], rb_ref[...], grp)
        lane = lax.broadcasted_iota(jnp.int32, gate.shape, 1)
        acc = None
        for j in range(EXPERTS_PER_GROUP):
            gcol = jnp.sum(jnp.where(lane == grp * EXPERTS_PER_GROUP + j, gate, 0.0), axis=-1, keepdims=True)
            a = _silu(_dot(xb, w1_ref[j])) * _dot(xb, w3_ref[j]) * gcol
            y = _dot(a.astype(BF16), w2_ref[j])
            acc = y if acc is None else acc + y
        o_ref[...] = acc


def _group_ffn(h_sorted, tile_grp, tile_blk, n_valid, wr_pad, rb_row, w1, w3, w2, cap):
    p_rows, d = h_sorted.shape
    tm = MOE_TILE
    f = w1.shape[2]
    n_tiles = tile_grp.shape[0]
    blocks_per_group = cap // tm
    row_blk = lambda i, tg, tb, nv: (tg[i] * blocks_per_group + tb[i], 0)
    experts = lambda i, tg, tb, nv: (tg[i], 0, 0)
    w_in_spec = pl.BlockSpec((EXPERTS_PER_GROUP, d, f), experts)
    return pl.pallas_call(
        _group_ffn_kernel,
        grid_spec=pltpu.PrefetchScalarGridSpec(
            num_scalar_prefetch=3, grid=(n_tiles,),
            in_specs=[pl.BlockSpec((tm, d), row_blk),
                      pl.BlockSpec((d, LANES), lambda i, tg, tb, nv: (0, 0)),
                      pl.BlockSpec((1, LANES), lambda i, tg, tb, nv: (0, 0)),
                      w_in_spec, w_in_spec, pl.BlockSpec((EXPERTS_PER_GROUP, f, d), experts)],
            out_specs=pl.BlockSpec((tm, d), row_blk)),
        out_shape=jax.ShapeDtypeStruct((p_rows, d), F32),
        compiler_params=_cparams(("arbitrary",)),
        name="moe_group_ffn",
    )(tile_grp, tile_blk, n_valid, h_sorted, wr_pad, rb_row, w1, w3, w2)


def _combine_kernel(pos_ref, x_ref, y_ref, g2_ref, nfin_ref, o_ref, buf, sem, *, final_norm):
    i = pl.program_id(0)
    tm = x_ref.shape[0]
    slot = i % 2

    def gather(tile, slot_):
        def issue(r, carry):
            _rows_copy(y_ref, buf.at[slot_], sem.at[slot_], pos_ref[tile * tm + r], r).start()
            return carry
        lax.fori_loop(0, tm, issue, 0, unroll=ROW_DMA_UNROLL)

    @pl.when(i == 0)
    def _():
        gather(0, 0)

    @pl.when(i + 1 < pl.num_programs(0))
    def _():
        gather(i + 1, 1 - slot)

    pltpu.make_async_copy(y_ref.at[pl.ds(0, tm), :], buf.at[slot], sem.at[slot]).wait()
    y = x_ref[...] + g2_ref[0] * buf[slot]
    if final_norm:
        y = y * lax.rsqrt(jnp.mean(y * y, axis=-1, keepdims=True) + EPS) * nfin_ref[...]
    o_ref[...] = y


def _combine(xs, y_sorted, pos, mod, layer, nfin, seq, nb, n_rows, final_norm):
    t, d = xs.shape
    tm = TOK_TILE
    base = layer * SUBLANES * 6
    out_rows = n_rows if final_norm else t
    return pl.pallas_call(
        functools.partial(_combine_kernel, final_norm=final_norm),
        grid_spec=pltpu.PrefetchScalarGridSpec(
            num_scalar_prefetch=1, grid=(n_rows // tm,),
            in_specs=[pl.BlockSpec((tm, d), lambda i, pos: (i, 0)),
                      pl.BlockSpec(memory_space=pl.ANY),
                      pl.BlockSpec((1, 1, d), lambda i, pos: (base + _mod_row(i, tm, seq, nb) * 6 + 5, 0, 0)),
                      pl.BlockSpec((1, d), lambda i, pos: (0, 0))],
            out_specs=pl.BlockSpec((tm, d), lambda i, pos: (i, 0)),
            scratch_shapes=[pltpu.VMEM((2, tm, d), F32), pltpu.SemaphoreType.DMA((2,))]),
        out_shape=jax.ShapeDtypeStruct((out_rows, d), F32),
        compiler_params=_cparams(("arbitrary",)),
        name="moe_combine",
    )(pos, xs, y_sorted, mod, nfin.reshape(1, d))


def _moe(xs, h, grp_rows, rank_rows, cnt, wr_pad, rb_row, w1, w3, w2, mod, layer, nfin, seq, nb, n_rows, final_norm):
    tm = MOE_TILE
    cap = n_rows + tm
    grp, rank = grp_rows.reshape(-1), rank_rows.reshape(-1)
    pos = grp * cap + rank
    counts = cnt[:N_GROUPS, 0].astype(jnp.int32)
    tiles = (counts + tm - 1) // tm
    ends = jnp.cumsum(tiles)
    n_tiles = n_rows // tm + N_GROUPS
    step = jnp.minimum(jnp.arange(n_tiles, dtype=jnp.int32), ends[-1] - 1)
    tile_grp = jnp.sum(step[:, None] >= ends[None, :], axis=1).astype(jnp.int32)
    tile_blk = step - (ends - tiles)[tile_grp]
    h_sorted = _dispatch(h, pos, counts, cap)
    y_sorted = _group_ffn(h_sorted, tile_grp, tile_blk, ends[-1:].astype(jnp.int32), wr_pad, rb_row, w1, w3, w2, cap)
    return _combine(xs, y_sorted, pos, mod, layer, nfin, seq, nb, n_rows, final_norm)


def kernel(x, c, ctx, c_ctx, w_mod, b_mod, norm_mix, norm_ffn, norm_final, ab_w_in, ab_i_bias, ab_f_bias,
           ab_conv, ab_a_log, ab_dt_bias, ab_norm_a, ab_norm_b, ab_w_out, c_w_in, c_f_bias, c_lb_raw, c_norm,
           c_w_out, w_router, router_bias, w1, w3, w2):
    nb, seq, d = x.shape
    n_ctx = ctx.shape[1]
    depth = w_mod.shape[0]
    n_lat = nb * seq
    assert nb + 1 <= SUBLANES and seq % 1024 == 0 and (nb * n_ctx) % 1024 == 0 and n_ctx % 256 == 0

    xs = jnp.concatenate([x.reshape(n_lat, d), ctx.reshape(nb * n_ctx, d)], axis=0)
    cpad = jnp.zeros((SUBLANES, d), F32).at[:nb].set(c).at[nb].set(c_ctx)
    mod = _modulation(cpad, w_mod, b_mod).reshape(depth * SUBLANES * 6, 1, d)

    lb_p = jax.nn.softmax(c_lb_raw.astype(F32), axis=0)
    lb_all = jnp.cumsum(lb_p, axis=0) - lb_p[0:1]

    wr_pad = jnp.zeros((d, LANES), F32).at[:, :N_EXPERTS].set(w_router)
    rb_col = router_bias.astype(F32).reshape(N_EXPERTS, 1)
    rb_row = jnp.zeros((1, LANES), F32).at[0, :N_EXPERTS].set(router_bias.astype(F32))
    w_main = H_AB * HEAD * 8

    out = None
    column_major = False
    for layer in range(depth):
        last = layer == depth - 1
        j = layer // 2
        want_cm = layer % 2 == 1 and j % 2 == 1
        needs_raster = layer % 2 == 0 or not want_cm
        if want_cm and not column_major:
            xs = _to_column_major(xs, n_lat, seq)
            column_major = True
        elif needs_raster and column_major:
            xs = jnp.concatenate([_from_column_major(xs[:n_lat], seq), xs[n_lat:]], axis=0)
            column_major = False
        if layer % 2 == 0:
            w_in = ab_w_in[j]
            gw = w_in[:, w_main:].reshape(d, 4, 2, H_AB)
            wg = jnp.zeros((2, d, LANES), F32).at[:, :, :4 * H_AB].set(
                jnp.transpose(gw, (2, 0, 1, 3)).reshape(2, d, 4 * H_AB)).astype(BF16)
            zeros = jnp.zeros((2, H_AB), F32)
            bias = jnp.concatenate([ab_i_bias[j], ab_f_bias[j], zeros, ab_dt_bias[j]], axis=1)
            alog = jnp.concatenate([zeros, zeros, zeros, ab_a_log[j]], axis=1)
            gpar = jnp.zeros((2, SUBLANES, LANES), F32).at[:, 0, :4 * H_AB].set(bias).at[:, 1, :4 * H_AB].set(alog)
            p, gates = _inproj(xs, norm_mix[layer], mod, layer, w_in[:, :w_main].astype(BF16), seq, nb, wg, gpar)
            ha_f, ha_b = _mlstm_scan(p, gates, seq, n_ctx, nb)
            qkv = _gdn_prep(p, ab_conv[j], seq, n_ctx, nb)
            ob_f, ob_b = _gdn_scan(qkv, gates, seq, n_ctx, nb)
            o_halves = (((ha_f, 0), (ha_b, 0), 0), ((ob_f, 0), (ob_b, 0), 0))
            gate_cols = (3, 7)
            hn_gain = jnp.concatenate([ab_norm_a[j], ab_norm_b[j]])
            w_out = ab_w_out[j]
        else:
            (p,) = _inproj(xs, norm_mix[layer], mod, layer, c_w_in[j].astype(BF16), seq, nb)
            o_c = _gla_scan(p, c_f_bias[j], lb_all[j], seq, n_ctx, nb)
            o_halves = (((o_c, 0), (o_c, 1), 0), ((o_c, 0), (o_c, 1), 1))
            gate_cols = (8, 9)
            hn_gain = c_norm[j]
            w_out = c_w_out[j]
        n_rows = n_lat if last else n_lat + nb * n_ctx
        xs, h, grp_rows, rank_rows, cnt = _outproj(xs, o_halves, p, gate_cols, hn_gain, w_out.astype(BF16), mod, layer,
                                                   norm_ffn[layer], wr_pad, rb_col, seq, nb, n_rows, layer % 2 == 0)
        res = _moe(xs, h, grp_rows, rank_rows, cnt, wr_pad, rb_row, w1[layer].astype(BF16), w3[layer].astype(BF16),
                   w2[layer].astype(BF16), mod, layer, norm_final, seq, nb, n_rows, last)
        if last:
            out = _from_column_major(res, seq) if column_major else res
        else:
            xs = res
    return out.reshape(nb, seq, d)
```

```python
import functools

import numpy as np
import jax
import jax.numpy as jnp
from jax import lax
from jax.experimental import pallas as pl
from jax.experimental.pallas import tpu as pltpu

F32 = jnp.float32
BF16 = jnp.bfloat16

EPS = 1e-6
GRID_W = 64
HEAD = 128
H_AB = 4
H_C = 8
CONV_K = 5
N_EXPERTS = 16
N_GROUPS = 4
EXPERTS_PER_GROUP = N_EXPERTS // N_GROUPS
LANES = 128
SUBLANES = 8
VMEM_LIMIT = 56 * 1024 * 1024

TOK_TILE = 512
MOE_TILE = 512
ROW_DMA_UNROLL = 8
MLSTM_CHUNK = 128
GDN_CHUNK = 64
GDN_SCAN_CHUNKS = 4
GLA_CHUNK = 128
GDN_INV_BLOCK = 16


def _cparams(sem):
    return pltpu.CompilerParams(dimension_semantics=sem, vmem_limit_bytes=VMEM_LIMIT)


def _dot(a, b):
    return jnp.dot(a, b, preferred_element_type=F32)


def _dot_nt(a, b):
    return lax.dot_general(a, b, (((1,), (1,)), ((), ())), preferred_element_type=F32)


def _dot_tn(a, b):
    return lax.dot_general(a, b, (((0,), (0,)), ((), ())), preferred_element_type=F32)


def _split2(x):
    hi = x.astype(BF16)
    lo = (x - hi.astype(F32)).astype(BF16)
    return hi, lo


def _split3(x):
    hi = x.astype(BF16)
    r = x - hi.astype(F32)
    mid = r.astype(BF16)
    lo = (r - mid.astype(F32)).astype(BF16)
    return hi, mid, lo


def _dot_exact_lhs(m_bf16, x):
    hi, mid, lo = _split3(x)
    return _dot(m_bf16, hi) + _dot(m_bf16, mid) + _dot(m_bf16, lo)


def _dot3(a, b):
    ah, al = _split2(a)
    bh, bl = _split2(b)
    return _dot(ah, bh) + _dot(ah, bl) + _dot(al, bh)


def _dot_nt3(a, b):
    ah, al = _split2(a)
    bh, bl = _split2(b)
    return _dot_nt(ah, bh) + _dot_nt(ah, bl) + _dot_nt(al, bh)


def _sigmoid(z):
    return 1.0 / (1.0 + jnp.exp(-z))


def _sigmoid_pair(z):
    return _sigmoid(z), _sigmoid(-z)


def _silu(z):
    return z * _sigmoid(z)


def _mod_kernel(c_ref, w_ref, b_ref, o_ref):
    s = _silu(c_ref[...]).astype(BF16)
    o_ref[0] = _dot(s, w_ref[0].astype(BF16)) + b_ref[0]


def _modulation(cpad, w_mod, b_mod):
    depth, d, n = w_mod.shape
    tn = 1536
    return pl.pallas_call(
        _mod_kernel,
        grid=(depth, n // tn),
        in_specs=[pl.BlockSpec((SUBLANES, d), lambda l, j: (0, 0)),
                  pl.BlockSpec((1, d, tn), lambda l, j: (l, 0, j)),
                  pl.BlockSpec((1, 1, tn), lambda l, j: (l, 0, j))],
        out_specs=pl.BlockSpec((1, SUBLANES, tn), lambda l, j: (l, 0, j)),
        out_shape=jax.ShapeDtypeStruct((depth, SUBLANES, n), F32),
        compiler_params=_cparams(("parallel", "parallel")),
        name="modulation",
    )(cpad, w_mod, b_mod.reshape(depth, 1, n))


def _gate_activations(raw, par):
    y = raw + par[0:1, :]
    e = jnp.exp(-jnp.abs(y))
    sp = jnp.log1p(e)
    lsig = jnp.minimum(y, 0.0) - sp
    r = 1.0 / (1.0 + e)
    sig = jnp.where(y >= 0, r, e * r)
    gdec = -jnp.exp(par[1:2, :]) * (jnp.maximum(y, 0.0) + sp)
    lane = lax.broadcasted_iota(jnp.int32, raw.shape, 1)
    return jnp.where(lane < 4, y, jnp.where(lane < 8, lsig, jnp.where(lane < 12, sig, gdec)))


def _inproj_kernel(x_ref, gain_ref, sc_ref, sh_ref, w_ref, *rest, with_gates):
    if with_gates:
        wg_ref, gpar_ref, p_ref, g_ref, u_ref = rest
    else:
        p_ref, u_ref = rest

    @pl.when(pl.program_id(1) == 0)
    def _():
        x = x_ref[...]
        y = x * lax.rsqrt(jnp.mean(x * x, axis=-1, keepdims=True) + EPS) * gain_ref[...]
        u = (y * (1.0 + sc_ref[0]) + sh_ref[0]).astype(BF16)
        u_ref[...] = u
        if with_gates:
            for d in range(2):
                g_ref[d] = _gate_activations(_dot(u, wg_ref[d]), gpar_ref[d])

    p_ref[...] = _dot(u_ref[...], w_ref[...])


def _mod_row(i, tm, seq, nb):
    return jnp.minimum((i * tm) // seq, nb)


def _inproj(xs, gain, mod, layer, w, seq, nb, wg=None, gpar=None):
    t, d = xs.shape
    p = w.shape[1]
    tm, tn = TOK_TILE, 1024
    with_gates = wg is not None
    base = layer * SUBLANES * 6

    def mod_spec(which):
        return pl.BlockSpec((1, 1, d), lambda i, j: (base + _mod_row(i, tm, seq, nb) * 6 + which, 0, 0))

    in_specs = [pl.BlockSpec((tm, d), lambda i, j: (i, 0)),
                pl.BlockSpec((1, d), lambda i, j: (0, 0)),
                mod_spec(1), mod_spec(0),
                pl.BlockSpec((d, tn), lambda i, j: (0, j))]
    args = [xs, gain.reshape(1, d), mod, mod, w]
    out_specs = [pl.BlockSpec((tm, tn), lambda i, j: (i, j))]
    out_shape = [jax.ShapeDtypeStruct((t, p), F32)]
    if with_gates:
        in_specs += [pl.BlockSpec((2, d, LANES), lambda i, j: (0, 0, 0)),
                     pl.BlockSpec((2, SUBLANES, LANES), lambda i, j: (0, 0, 0))]
        args += [wg, gpar]
        out_specs.append(pl.BlockSpec((2, tm, LANES), lambda i, j: (0, i, 0)))
        out_shape.append(jax.ShapeDtypeStruct((2, t, LANES), F32))
    return pl.pallas_call(
        functools.partial(_inproj_kernel, with_gates=with_gates),
        grid=(t // tm, p // tn),
        in_specs=in_specs, out_specs=out_specs, out_shape=out_shape,
        scratch_shapes=[pltpu.VMEM((tm, d), BF16)],
        compiler_params=_cparams(("parallel", "arbitrary")),
        name="inproj",
    )(*args)


def _gdn_prep_kernel(prev_ref, x_ref, next_ref, w_ref, o_ref, *, blocks_per_seq, n_lat_blocks, scale):
    i = pl.program_id(0)
    tm = x_ref.shape[0]
    in_lat = i < n_lat_blocks
    first = jnp.logical_or(jnp.logical_not(in_lat), i % blocks_per_seq == 0)
    last = jnp.logical_or(jnp.logical_not(in_lat), i % blocks_per_seq == blocks_per_seq - 1)
    prev = jnp.where(first, 0.0, prev_ref[...])
    nxt = jnp.where(last, 0.0, next_ref[...])
    xe = jnp.concatenate([prev, x_ref[...], nxt], axis=0)
    n = xe.shape[0]
    pad = (CONV_K - 1) // 2
    acc = None
    for j in range(CONV_K):
        shift = (pad - j) % n
        xr = xe if shift == 0 else pltpu.roll(xe, shift, 0)
        term = xr[SUBLANES:SUBLANES + tm] * w_ref[j:j + 1, :]
        acc = term if acc is None else acc + term
    y = _silu(acc)
    part = pl.program_id(1)
    for h in range(H_AB):
        sl = slice(h * HEAD, (h + 1) * HEAD)
        t = y[:, sl]
        inv_norm = lax.rsqrt(jnp.sum(t * t, axis=-1, keepdims=True) + EPS)
        factor = jnp.where(part == 0, inv_norm * scale, jnp.where(part == 1, inv_norm, 1.0))
        o_ref[:, sl] = t * factor


def _gdn_prep(p, conv_w, seq, n_ctx, nb):
    t = p.shape[0]
    tm = 256
    w = H_AB * HEAD
    first_col = 4
    wpad = jnp.zeros((SUBLANES, 3 * w), F32).at[:CONV_K].set(conv_w)
    hb = tm // SUBLANES
    nblk8 = t // SUBLANES
    return pl.pallas_call(
        functools.partial(_gdn_prep_kernel, blocks_per_seq=seq // tm, n_lat_blocks=nb * seq // tm,
                          scale=HEAD ** -0.5),
        grid=(t // tm, 3),
        in_specs=[pl.BlockSpec((SUBLANES, w), lambda i, c: (jnp.maximum(i * hb - 1, 0), first_col + c)),
                  pl.BlockSpec((tm, w), lambda i, c: (i, first_col + c)),
                  pl.BlockSpec((SUBLANES, w), lambda i, c: (jnp.minimum((i + 1) * hb, nblk8 - 1), first_col + c)),
                  pl.BlockSpec((SUBLANES, w), lambda i, c: (0, c))],
        out_specs=pl.BlockSpec((tm, w), lambda i, c: (i, c)),
        out_shape=jax.ShapeDtypeStruct((t, 3 * w), F32),
        compiler_params=_cparams(("parallel", "parallel")),
        name="gdn_prep",
    )(p, p, p, wpad)


def _tri_tables(c):
    idx = np.arange(c)
    fwd = (idx[None, :] <= idx[:, None]).astype(np.float32)
    return jnp.asarray(np.stack([fwd, fwd.T]))


def _to_cm_kernel(x3_ref, x2_ref, o_ref, *, n_lat_tiles):
    i = pl.program_id(0)
    rows = x3_ref.shape[0]

    @pl.when(i < n_lat_tiles)
    def _():
        for wl in range(SUBLANES):
            o_ref[wl * rows:(wl + 1) * rows, :] = x3_ref[:, wl, :]

    @pl.when(i >= n_lat_tiles)
    def _():
        o_ref[...] = x2_ref[...]


def _from_cm_kernel(x_ref, o_ref):
    rows = o_ref.shape[0]
    for wl in range(SUBLANES):
        o_ref[:, wl, :] = x_ref[wl * rows:(wl + 1) * rows, :]


def _to_column_major(xs, n_lat, seq):
    t, d = xs.shape
    rows = seq // GRID_W
    tile = rows * SUBLANES
    assert rows % SUBLANES == 0 and n_lat % tile == 0 and (t - n_lat) % tile == 0
    n_lat_tiles = n_lat // tile
    per_b = GRID_W // SUBLANES
    lat = lambda i: jnp.minimum(i, n_lat_tiles - 1)
    return pl.pallas_call(
        functools.partial(_to_cm_kernel, n_lat_tiles=n_lat_tiles),
        grid=(t // tile,),
        in_specs=[pl.BlockSpec((rows, SUBLANES, d), lambda i: (lat(i) // per_b, lat(i) % per_b, 0)),
                  pl.BlockSpec((tile, d), lambda i: (jnp.maximum(i, n_lat_tiles - 1), 0))],
        out_specs=pl.BlockSpec((tile, d), lambda i: (i, 0)),
        out_shape=jax.ShapeDtypeStruct((t, d), xs.dtype),
        compiler_params=_cparams(("parallel",)),
        name="to_column_major",
    )(xs.reshape(t // GRID_W, GRID_W, d), xs)


def _from_column_major(y, seq):
    n_lat, d = y.shape
    rows = seq // GRID_W
    tile = rows * SUBLANES
    per_b = GRID_W // SUBLANES
    out = pl.pallas_call(
        _from_cm_kernel,
        grid=(n_lat // tile,),
        in_specs=[pl.BlockSpec((tile, d), lambda i: (i, 0))],
        out_specs=pl.BlockSpec((rows, SUBLANES, d), lambda i: (i // per_b, i % per_b, 0)),
        out_shape=jax.ShapeDtypeStruct((n_lat // GRID_W, GRID_W, d), y.dtype),
        compiler_params=_cparams(("parallel",)),
        name="from_column_major",
    )(y)
    return out.reshape(n_lat, d)


def _mlstm_kernel(qf, kf, vf, qb, kb, vb, gf, gb, tri_ref, hf_ref, hb_ref, s_scr, m_scr, *, scale):
    @pl.when(pl.program_id(1) == 0)
    def _():
        s_scr[...] = jnp.zeros_like(s_scr)
        m_scr[...] = jnp.zeros_like(m_scr)

    c = gf.shape[1]
    ones_col = (lax.broadcasted_iota(jnp.int32, (c, HEAD), 1) == 0).astype(BF16)
    qkv = ((qf, kf, vf), (qb, kb, vb))
    h_out = (hf_ref, hb_ref)
    g = [gf[0], gb[0]]
    mask = [tri_ref[d] > 0.5 for d in range(2)]
    b = [_dot_exact_lhs(tri_ref[d].astype(BF16), g[d]) for d in range(2)]
    total = [jnp.sum(x, axis=0, keepdims=True) for x in g]
    g_t = [x.T for x in g]
    b_t = [x.T for x in b]
    chains = [(d, h) for d in range(2) for h in range(H_AB)]
    sl = [slice(h * HEAD, (h + 1) * HEAD) for d, h in chains]
    q = [qkv[d][0][:, sl[i]].astype(BF16) for i, (d, h) in enumerate(chains)]
    k = [qkv[d][1][:, sl[i]] * scale for i, (d, h) in enumerate(chains)]
    v_aug = [jnp.concatenate([qkv[d][2][:, sl[i]].astype(BF16), ones_col], axis=1) for i, (d, h) in enumerate(chains)]
    s = [s_scr[d, h] for d, h in chains]
    qk = [_dot_nt(q[i], k[i].astype(BF16)) for i in range(len(chains))]
    qs = [_dot(q[i], s[i].astype(BF16)) for i in range(len(chains))]
    w_intra, w_inter, m_ts, colvs, m_sts, tots = [], [], [], [], [], []
    for i, (d, h) in enumerate(chains):
        li_c, li_r = g[d][:, h:h + 1], g_t[d][h:h + 1, :]
        b_c, b_r = b[d][:, 4 + h:5 + h], b_t[d][4 + h:5 + h, :]
        tot = total[d][:, 4 + h:5 + h]
        m_st = m_scr[d, h][0:1, 0:1]
        d_log = jnp.where(mask[d], b_c - b_r + li_r, -jnp.inf)
        m_inter = b_c + m_st
        m_t = jnp.maximum(m_inter, jnp.max(d_log, axis=-1, keepdims=True))
        w_inter.append(jnp.exp(m_inter - m_t))
        w_intra.append((jnp.exp(d_log - m_t) * qk[i]).astype(BF16))
        m_ts.append(m_t); colvs.append(tot - b_c + li_c); m_sts.append(m_st); tots.append(tot)
    intra = [_dot(w_intra[i], v_aug[i]) for i in range(len(chains))]
    m_new = [jnp.maximum(tots[i] + m_sts[i], jnp.max(colvs[i], axis=0, keepdims=True)) for i in range(len(chains))]
    upd = [_dot_tn((k[i] * jnp.exp(colvs[i] - m_new[i])).astype(BF16), v_aug[i]) for i in range(len(chains))]
    for i, (d, h) in enumerate(chains):
        num = w_inter[i] * qs[i] + intra[i]
        den = num[:, HEAD:HEAD + 1]
        h_out[d][0, :, sl[i]] = num[:, :HEAD] / jnp.maximum(jnp.abs(den), jnp.exp(-m_ts[i]))
        s_scr[d, h] = jnp.exp(tots[i] + m_sts[i] - m_new[i]) * s[i] + upd[i]
        m_scr[d, h] = jnp.broadcast_to(m_new[i], (SUBLANES, LANES))


def _step_block(chunk, seq, n_ctx, nb):
    n_c, n_l = n_ctx // chunk, seq // chunk
    off = nb * n_l

    def fn(b, d, i):
        j = i - n_c
        ctx_blk = off + b * n_c + (i if d == 0 else n_c - 1 - i)
        lat_blk = b * n_l + (j if d == 0 else n_l - 1 - j)
        return jnp.where(i < n_c, ctx_blk, lat_blk)

    return n_c + n_l, fn


def _mlstm_scan(p, gates, seq, n_ctx, nb):
    t = p.shape[0]
    c = MLSTM_CHUNK
    w = H_AB * HEAD
    n_steps, blk = _step_block(c, seq, n_ctx, nb)
    in_specs = [pl.BlockSpec((c, w), lambda b, i, d=d, col=col: (blk(b, d, i), col)) for d in range(2) for col in range(3)]
    in_specs += [pl.BlockSpec((1, c, LANES), lambda b, i, d=d: (d, blk(b, d, i), 0)) for d in range(2)]
    in_specs.append(pl.BlockSpec((2, c, c), lambda b, i: (0, 0, 0)))
    return pl.pallas_call(
        functools.partial(_mlstm_kernel, scale=HEAD ** -0.5),
        grid=(nb, n_steps),
        in_specs=in_specs,
        out_specs=[pl.BlockSpec((1, c, w), lambda b, i, d=d: (0, blk(b, d, i), 0)) for d in range(2)],
        out_shape=[jax.ShapeDtypeStruct((1, t, w), F32)] * 2,
        scratch_shapes=[pltpu.VMEM((2, H_AB, HEAD, 2 * HEAD), F32), pltpu.VMEM((2, H_AB, SUBLANES, LANES), F32)],
        compiler_params=_cparams(("parallel", "arbitrary")),
        name="mlstm_scan",
    )(p, p, p, p, p, p, gates, gates, _tri_tables(c))


def _gdn_inv_masks(c):
    idx = np.arange(c)
    blk = lambda n: (idx[:, None] // n) == (idx[None, :] // n)
    levels = [blk(GDN_INV_BLOCK)]
    n = GDN_INV_BLOCK
    while n < c:
        levels.append(np.logical_and(blk(2 * n), np.logical_not(blk(n))))
        n *= 2
    levels = np.stack(levels).astype(np.float32)
    return jnp.asarray(np.tile(levels, (1, 1, H_AB)))


def _block_diag(x, n_blocks):
    blk = lax.broadcasted_iota(jnp.int32, x.shape, 1) // (x.shape[1] // n_blocks)
    return jnp.concatenate([jnp.where(blk == j, x, jnp.zeros_like(x)) for j in range(n_blocks)], axis=0)


def _heads_to_lanes(x, first, width):
    return jnp.concatenate([jnp.broadcast_to(x[:, first + h:first + h + 1], (x.shape[0], width))
                            for h in range(H_AB)], axis=1)


def _unit_tri_inverse(a_list, eye, inv_masks):
    prod = lambda xs, ys: [_dot(x.astype(BF16), _block_diag(y.astype(BF16), H_AB)) for x, y in zip(xs, ys)]
    pw = [a * inv_masks[0] for a in a_list]
    inv = [eye - p for p in pw]
    n = 2
    while n < GDN_INV_BLOCK:
        pw = prod(pw, pw)
        inv = [i + t for i, t in zip(inv, prod(inv, pw))]
        n *= 2
    for lvl in range(1, inv_masks.shape[0]):
        t = prod(prod(inv, [a * inv_masks[lvl] for a in a_list]), inv)
        inv = [i - x for i, x in zip(inv, t)]
    return inv


def _gdn_chunk_kernel(q_ref, k_ref, v_ref, g_ref, tri_ref, im_ref, u_ref, w_ref, qd_ref, kd_ref, att_ref):
    c = GDN_CHUNK
    n_chunks = q_ref.shape[0] // c
    inv_masks = im_ref[...]
    row = lax.broadcasted_iota(jnp.int32, (c, H_AB * c), 0)
    col = lax.broadcasted_iota(jnp.int32, (c, H_AB * c), 1) % c
    eye_b = row == col
    eye = eye_b.astype(F32)
    masks = [col <= row, col >= row]
    stricts = [col < row, col > row]
    tris = [tri_ref[d].astype(BF16) for d in range(2)]
    ones = jnp.ones((c, c), BF16)
    chunks = range(n_chunks)
    rows = [slice(ci * c, (ci + 1) * c) for ci in chunks]
    groups = [(ci, d) for ci in chunks for d in range(2)]
    k_hl = [_split2(k_ref[r, :]) for r in rows]
    kbd = [(_block_diag(hi, H_AB), _block_diag(lo, H_AB)) for hi, lo in k_hl]
    r_hi = [_dot_nt(jnp.concatenate([k_hl[ci][0], k_hl[ci][1], q_ref[rows[ci], :].astype(BF16)], axis=0), kbd[ci][0])
            for ci in chunks]
    r_lo = [_dot_nt(k_hl[ci][0], kbd[ci][1]) for ci in chunks]
    kk = [r_hi[ci][:c] + r_hi[ci][c:2 * c] + r_lo[ci] for ci in chunks]
    g = [g_ref[d, rows[ci], :] for ci, d in groups]
    gc = [_dot_exact_lhs(tris[d], g[i]) for i, (ci, d) in enumerate(groups)]
    gc_c = [_heads_to_lanes(x, 12, c) for x in gc]
    gc_r = [_dot_exact_lhs(ones, eye * x) for x in gc_c]
    decay = [jnp.exp(jnp.where(masks[d], gc_c[i] - gc_r[i], -jnp.inf)) for i, (ci, d) in enumerate(groups)]
    a = [jnp.where(stricts[d], _heads_to_lanes(g[i], 8, c) * kk[ci] * decay[i], 0.0)
         for i, (ci, d) in enumerate(groups)]
    inv = _unit_tri_inverse(a, eye, inv_masks)
    inv_hl = [jnp.concatenate(_split2(x), axis=0) for x in inv]
    gc_w = [_heads_to_lanes(x, 12, HEAD) for x in gc]
    eg = [jnp.exp(x) for x in gc_w]
    beta = [_heads_to_lanes(x, 8, HEAD) for x in g]
    res_u = [_dot(inv_hl[i], _block_diag((v_ref[rows[ci], :] * beta[i]).astype(BF16), H_AB))
             for i, (ci, d) in enumerate(groups)]
    res_w = [_dot(inv_hl[i], _block_diag((k_ref[rows[ci], :] * (beta[i] * eg[i])).astype(BF16), H_AB))
             for i, (ci, d) in enumerate(groups)]
    for i, (ci, d) in enumerate(groups):
        r = rows[ci]
        total = _heads_to_lanes(jnp.sum(g[i], axis=0, keepdims=True), 12, HEAD)
        u_ref[d, r, :] = (res_u[i][:c] + res_u[i][c:]).astype(BF16)
        w_ref[d, r, :] = (res_w[i][:c] + res_w[i][c:]).astype(BF16)
        qd_ref[d, r, :] = (q_ref[r, :] * eg[i]).astype(BF16)
        kd_ref[d, r, :] = (k_ref[r, :] * jnp.exp(total - gc_w[i])).astype(BF16)
        att_ref[d, r, :] = (r_hi[ci][2 * c:] * decay[i]).astype(BF16)


def _gdn_scan_kernel(*refs):
    ins, o_refs, s_scr = refs[:12], refs[12:14], refs[14]
    c = GDN_CHUNK

    @pl.when(pl.program_id(1) == 0)
    def _():
        s_scr[...] = jnp.zeros_like(s_scr)

    pw = 2 * HEAD
    rblk = lax.broadcasted_iota(jnp.int32, (pw, pw), 0) // HEAD
    cblk = lax.broadcasted_iota(jnp.int32, (pw, pw), 1) // HEAD
    on_diag = rblk == cblk
    chains = [(d, pr) for d in range(2) for pr in range(H_AB // 2)]
    refs_of = lambda d: ins[d::2]
    sl = [slice(pr * pw, (pr + 1) * pw) for d, pr in chains]
    s = [s_scr[d, pr] for d, pr in chains]
    n_sub = o_refs[0].shape[1] // c
    for k in range(n_sub):
        rows = [slice(kk * c, (kk + 1) * c) for kk in (k, n_sub - 1 - k)]
        res = [_dot(jnp.concatenate([refs_of(d)[1][0, rows[d], sl[i]], refs_of(d)[2][0, rows[d], sl[i]]], axis=0),
                    s[i].astype(BF16)) for i, (d, pr) in enumerate(chains)]
        v_new = [(refs_of(d)[0][0, rows[d], sl[i]].astype(F32) - res[i][:c]).astype(BF16)
                 for i, (d, pr) in enumerate(chains)]
        intra = [_dot(refs_of(d)[4][0, rows[d], pr * 2 * c:(pr + 1) * 2 * c], _block_diag(v_new[i], 2))
                 for i, (d, pr) in enumerate(chains)]
        upd = [_dot_tn(refs_of(d)[3][0, rows[d], sl[i]], v_new[i]) for i, (d, pr) in enumerate(chains)]
        totals = [jnp.sum(refs_of(d)[5][0, rows[d], :], axis=0, keepdims=True) for d in range(2)]
        for i, (d, pr) in enumerate(chains):
            o_refs[d][0, rows[d], sl[i]] = res[i][c:] + intra[i]
            gl = jnp.concatenate([jnp.broadcast_to(jnp.exp(totals[d][:, 12 + 2 * pr + j:13 + 2 * pr + j]), (1, HEAD))
                                  for j in range(2)], axis=1)
            s[i] = gl * s[i] + jnp.where(on_diag, upd[i], 0.0)
    for i, (d, pr) in enumerate(chains):
        s_scr[d, pr] = s[i]


def _gdn_scan(qkv, gates, seq, n_ctx, nb):
    t = qkv.shape[0]
    c = GDN_CHUNK
    w = H_AB * HEAD
    tri = _tri_tables(c)
    im = _gdn_inv_masks(c)
    tm = TOK_TILE
    sds = lambda width: jax.ShapeDtypeStruct((2, t, width), BF16)
    u, wv, qd, kd, att = pl.pallas_call(
        _gdn_chunk_kernel,
        grid=(t // tm,),
        in_specs=[pl.BlockSpec((tm, w), lambda i, col=col: (i, col)) for col in range(3)] + [
            pl.BlockSpec((2, tm, LANES), lambda i: (0, i, 0)),
            pl.BlockSpec((2, c, c), lambda i: (0, 0, 0)),
            pl.BlockSpec(im.shape, lambda i: (0, 0, 0))],
        out_specs=[pl.BlockSpec((2, tm, w), lambda i: (0, i, 0))] * 4
        + [pl.BlockSpec((2, tm, H_AB * c), lambda i: (0, i, 0))],
        out_shape=[sds(w)] * 4 + [sds(H_AB * c)],
        compiler_params=_cparams(("parallel",)),
        name="gdn_chunk",
    )(qkv, qkv, qkv, gates, tri, im)

    cs = c * GDN_SCAN_CHUNKS
    n_steps, blk = _step_block(cs, seq, n_ctx, nb)
    in_specs, args = [], []
    for arr, width in ((u, w), (wv, w), (qd, w), (kd, w), (att, H_AB * c), (gates, LANES)):
        for d in range(2):
            in_specs.append(pl.BlockSpec((1, cs, width), lambda b, i, d=d: (d, blk(b, d, i), 0)))
            args.append(arr)
    o_f, o_b = pl.pallas_call(
        _gdn_scan_kernel,
        grid=(nb, n_steps),
        in_specs=in_specs,
        out_specs=[pl.BlockSpec((1, cs, w), lambda b, i, d=d: (0, blk(b, d, i), 0)) for d in range(2)],
        out_shape=[jax.ShapeDtypeStruct((1, t, w), F32)] * 2,
        scratch_shapes=[pltpu.VMEM((2, H_AB // 2, 2 * HEAD, 2 * HEAD), F32)],
        compiler_params=_cparams(("parallel", "arbitrary")),
        name="gdn_scan",
    )(*args)
    return o_f, o_b


def _gla_tables(c):
    idx = np.arange(c)
    sel, sgn, msk = [], [], []
    for d in range(2):
        sel_d, sgn_d, msk_d = [], [], []
        bs = c // 2
        while bs >= 1:
            pair = idx // (2 * bs)
            second = (idx // bs) % 2 == 1
            ref_row = pair * 2 * bs + (bs - 1 if d == 0 else bs)
            sel_d.append((idx[None, :] == ref_row[:, None]).astype(np.float32))
            qside = second if d == 0 else np.logical_not(second)
            sgn_d.append(np.where(qside, 1.0, -1.0)[:, None] * np.ones((1, HEAD)))
            same_pair = pair[:, None] == pair[None, :]
            msk_d.append(np.logical_and(same_pair, np.logical_and(qside[:, None], np.logical_not(qside)[None, :])))
            bs //= 2
        msk_d.append(np.eye(c, dtype=bool))
        sel.append(np.concatenate(sel_d, axis=0))
        sgn.append(np.stack(sgn_d))
        msk.append(np.stack(msk_d).astype(np.float32))
    return (jnp.asarray(np.stack(sel), dtype=BF16), jnp.asarray(np.stack(sgn), dtype=F32),
            jnp.asarray(np.stack(msk), dtype=F32))


def _gla_kernel(qf_ref, zf_ref, vf_ref, qb_ref, zb_ref, vb_ref, fb_ref, lb_ref, tri_ref, sel_ref, sgn_ref, msk_ref,
                of_ref, ob_ref, s_scr):
    @pl.when(pl.program_id(1) == 0)
    def _():
        s_scr[...] = jnp.zeros_like(s_scr)

    c = qf_ref.shape[0]
    n_lvl = sgn_ref.shape[1]
    dirs = range(2)
    qzv = ((qf_ref, zf_ref, vf_ref), (qb_ref, zb_ref, vb_ref))
    o_refs = (of_ref, ob_ref)
    chains = [(d, h) for d in dirs for h in range(H_C)]
    sl = [slice(h * HEAD, (h + 1) * HEAD) for d, h in chains]
    g_all, key_all, q_all = [], [], []
    for d in dirs:
        lb = lb_ref[d]
        sig_pos, sig_neg = _sigmoid_pair(qzv[d][1][...] + fb_ref[d])
        g_all.append(jnp.log(lb + (1.0 - lb) * sig_pos))
        key_all.append((1.0 - lb) * sig_neg)
        q_all.append(_silu(qzv[d][0][...]))
    gc_all = [_dot_exact_lhs(tri_ref[d].astype(BF16), g_all[d]) for d in dirs]
    total_all = [jnp.sum(g, axis=0, keepdims=True) for g in g_all]
    refs = [_dot(sel_ref[d], gc_all[d].astype(BF16)) for d in dirs]
    diag = msk_ref[0, n_lvl] > 0.5
    att = [jnp.where(diag, _dot_nt(q_all[d][:, sl[i]].astype(BF16), key_all[d][:, sl[i]].astype(BF16)), 0.0)
           for i, (d, h) in enumerate(chains)]
    for l in range(n_lvl):
        sgn = [sgn_ref[d, l] for d in dirs]
        x = [(jnp.where(sgn[d] > 0, q_all[d][:, sl[i]], key_all[d][:, sl[i]])
              * jnp.exp(sgn[d] * (gc_all[d][:, sl[i]] - refs[d][l * c:(l + 1) * c, sl[i]]))).astype(BF16)
             for i, (d, h) in enumerate(chains)]
        prod = [_dot_nt(xh, xh) for xh in x]
        in_level = [msk_ref[d, l] > 0.5 for d in dirs]
        att = [jnp.where(in_level[d], prod[i], att[i]) for i, (d, h) in enumerate(chains)]
    st = [s_scr[d, h] for d, h in chains]
    inter = [_dot_nt((q_all[d][:, sl[i]] * jnp.exp(gc_all[d][:, sl[i]])).astype(BF16), st[i].astype(BF16))
             for i, (d, h) in enumerate(chains)]
    intra = [_dot(att[i].astype(BF16), qzv[d][2][:, sl[i]].astype(BF16)) for i, (d, h) in enumerate(chains)]
    upd = [_dot_tn(qzv[d][2][:, sl[i]].astype(BF16),
                   (key_all[d][:, sl[i]] * jnp.exp(total_all[d][:, sl[i]] - gc_all[d][:, sl[i]])).astype(BF16))
           for i, (d, h) in enumerate(chains)]
    for i, (d, h) in enumerate(chains):
        o_refs[d][0, :, sl[i]] = inter[i] + intra[i]
        s_scr[d, h] = jnp.exp(total_all[d][:, sl[i]]) * st[i] + upd[i]


def _gla_scan(p, f_bias, lb, seq, n_ctx, nb):
    t, pw = p.shape
    c = GLA_CHUNK
    w = H_C * HEAD
    tri = _tri_tables(c)
    sel, sgn, msk = _gla_tables(c)
    n_steps, blk = _step_block(c, seq, n_ctx, nb)
    dspec = lambda d, col: pl.BlockSpec((c, w), lambda b, i: (blk(b, d, i), col))
    whole = lambda a: pl.BlockSpec(a.shape, lambda b, i: (0,) * a.ndim)
    fb3, lb3 = f_bias.reshape(2, 1, w), lb.reshape(2, 1, w)
    consts = (fb3, lb3, tri, sel, sgn, msk)
    return pl.pallas_call(
        _gla_kernel,
        grid=(nb, n_steps),
        in_specs=[dspec(0, 0), dspec(0, 1), dspec(0, 3), dspec(1, 0), dspec(1, 2), dspec(1, 3)]
        + [whole(a) for a in consts],
        out_specs=[pl.BlockSpec((1, c, w), lambda b, i, d=d: (0, blk(b, d, i), 0)) for d in range(2)],
        out_shape=[jax.ShapeDtypeStruct((1, t, w), F32)] * 2,
        scratch_shapes=[pltpu.VMEM((2, H_C, HEAD, HEAD), F32)],
        compiler_params=_cparams(("parallel", "arbitrary")),
        name="gla_scan",
    )(p, p, p, p, p, p, *consts)


def _best_group(scores_t, bias):
    sel = [scores_t[e:e + 1, :] + bias[e:e + 1, :] for e in range(N_EXPERTS)]
    gscore = []
    for grp in range(N_GROUPS):
        v = sel[grp * EXPERTS_PER_GROUP:(grp + 1) * EXPERTS_PER_GROUP]
        best = None
        for a in range(EXPERTS_PER_GROUP):
            for b in range(a + 1, EXPERTS_PER_GROUP):
                pair = v[a] + v[b]
                best = pair if best is None else jnp.maximum(best, pair)
        gscore.append(best)
    best_g = jnp.zeros(gscore[0].shape, jnp.int32)
    best_v = gscore[0]
    for grp in range(1, N_GROUPS):
        better = gscore[grp] > best_v
        best_g = jnp.where(better, grp, best_g)
        best_v = jnp.where(better, gscore[grp], best_v)
    return best_g


def _outproj_kernel(x_ref, of0, ob0, gt0, of1, ob1, gt1, hn_ref, w_ref, g1_ref, sc_ref, sh_ref, nf_ref,
                    wr_ref, rb_ref, triu_ref, xo_ref, h_ref, grp_ref, rank_ref, cnt_ref, base_scr,
                    *, first_half_sigmoid):
    @pl.when(pl.program_id(0) == 0)
    def _():
        base_scr[...] = jnp.zeros_like(base_scr)

    feats = []
    for half, (of, ob, gt) in enumerate(((of0, ob0, gt0), (of1, ob1, gt1))):
        o = of[0] + ob[0]
        gate = gt[...]
        for h in range(o.shape[1] // HEAD):
            sl = slice(h * HEAD, (h + 1) * HEAD)
            t = o[:, sl]
            y = t * lax.rsqrt(jnp.mean(t * t, axis=-1, keepdims=True) + EPS)
            y = y * hn_ref[:, half * o.shape[1] + h * HEAD: half * o.shape[1] + (h + 1) * HEAD]
            gz = gate[:, sl]
            act = _sigmoid(gz) if (half == 0 and first_half_sigmoid) else _silu(gz)
            feats.append((y * act).astype(BF16))
    feats = jnp.concatenate(feats, axis=1)
    xn = x_ref[...] + g1_ref[0] * _dot(feats, w_ref[...])
    xo_ref[...] = xn
    y = xn * lax.rsqrt(jnp.mean(xn * xn, axis=-1, keepdims=True) + EPS) * nf_ref[...]
    hl = y * (1.0 + sc_ref[0]) + sh_ref[0]
    h_ref[...] = hl
    logits = _dot3(hl, wr_ref[...])
    scores_t = _sigmoid(logits).T
    best_g = _best_group(scores_t, rb_ref[...])
    tm = best_g.shape[1]
    onehot = jnp.concatenate([(best_g == g).astype(F32) for g in range(N_GROUPS)]
                             + [jnp.zeros((SUBLANES - N_GROUPS, tm), F32)], axis=0)
    before = _dot(onehot.astype(BF16), triu_ref[...])
    base = base_scr[...]
    rank = jnp.sum(onehot * (before + base[:, 0:1]), axis=0, keepdims=True)
    grp_ref[0] = best_g
    rank_ref[0] = rank.astype(jnp.int32)
    base = base + jnp.sum(onehot, axis=1, keepdims=True)
    base_scr[...] = base
    cnt_ref[...] = base


def _outproj(xs, o_halves, gate_src, gate_cols, hn_gain, w_out, mod, layer, nf_gain, wr_pad, rb_col,
             seq, nb, n_rows, first_half_sigmoid):
    t, d = xs.shape
    tm = TOK_TILE
    half = d // 2
    base = layer * SUBLANES * 6

    def mod_spec(which):
        return pl.BlockSpec((1, 1, d), lambda i: (base + _mod_row(i, tm, seq, nb) * 6 + which, 0, 0))

    in_specs = [pl.BlockSpec((tm, d), lambda i: (i, 0))]
    args = [xs]
    for ((arr_f, dir_f), (arr_b, dir_b), col), gcol in zip(o_halves, gate_cols):
        in_specs += [pl.BlockSpec((1, tm, half), lambda i, col=col, dd=dir_f: (dd, i, col)),
                     pl.BlockSpec((1, tm, half), lambda i, col=col, dd=dir_b: (dd, i, col)),
                     pl.BlockSpec((tm, half), lambda i, gcol=gcol: (i, gcol))]
        args += [arr_f, arr_b, gate_src]
    in_specs += [pl.BlockSpec((1, d), lambda i: (0, 0)),
                 pl.BlockSpec((d, d), lambda i: (0, 0)),
                 mod_spec(2), mod_spec(4), mod_spec(3),
                 pl.BlockSpec((1, d), lambda i: (0, 0)),
                 pl.BlockSpec((d, LANES), lambda i: (0, 0)),
                 pl.BlockSpec((N_EXPERTS, 1), lambda i: (0, 0)),
                 pl.BlockSpec((tm, tm), lambda i: (0, 0))]
    idx = np.arange(tm)
    triu = jnp.asarray(idx[:, None] < idx[None, :], dtype=BF16)
    args += [hn_gain.reshape(1, d), w_out, mod, mod, mod, nf_gain.reshape(1, d), wr_pad, rb_col, triu]
    n_tiles = n_rows // tm
    row_i32 = jax.ShapeDtypeStruct((n_tiles, 1, tm), jnp.int32)
    return pl.pallas_call(
        functools.partial(_outproj_kernel, first_half_sigmoid=first_half_sigmoid),
        grid=(n_tiles,),
        in_specs=in_specs,
        out_specs=[pl.BlockSpec((tm, d), lambda i: (i, 0)),
                   pl.BlockSpec((tm, d), lambda i: (i, 0)),
                   pl.BlockSpec((1, 1, tm), lambda i: (i, 0, 0)),
                   pl.BlockSpec((1, 1, tm), lambda i: (i, 0, 0)),
                   pl.BlockSpec((SUBLANES, LANES), lambda i: (0, 0))],
        out_shape=[jax.ShapeDtypeStruct((t, d), F32), jax.ShapeDtypeStruct((n_rows, d), F32), row_i32, row_i32,
                   jax.ShapeDtypeStruct((SUBLANES, LANES), F32)],
        scratch_shapes=[pltpu.VMEM((SUBLANES, LANES), F32)],
        compiler_params=_cparams(("arbitrary",)),
        name="outproj",
    )(*args)


def _rows_copy(src, dst, sem, src_row, dst_row):
    return pltpu.make_async_copy(src.at[pl.ds(src_row, 1), :], dst.at[pl.ds(dst_row, 1), :], sem)


def _dispatch_kernel(pos_ref, cnt_ref, h_ref, o_ref, zero_ref, sem, *, cap):
    i = pl.program_id(0)
    tm = h_ref.shape[0]

    def issue(r, carry):
        _rows_copy(h_ref, o_ref, sem, r, pos_ref[i * tm + r]).start()
        return carry

    lax.fori_loop(0, tm, issue, 0, unroll=ROW_DMA_UNROLL)
    pltpu.make_async_copy(h_ref, o_ref.at[pl.ds(0, tm), :], sem).wait()

    @pl.when(i == pl.num_programs(0) - 1)
    def _():
        zero_ref[...] = jnp.zeros_like(zero_ref)
        pad = zero_ref.shape[0]
        def pad_copies(action):
            for g in range(N_GROUPS):
                cnt = cnt_ref[g]
                up = (cnt + SUBLANES - 1) // SUBLANES * SUBLANES
                for r in range(SUBLANES - 1):
                    @pl.when(cnt + r < up)
                    def _():
                        action(_rows_copy(zero_ref, o_ref, sem, 0, g * cap + cnt + r))
                action(pltpu.make_async_copy(zero_ref, o_ref.at[pl.ds(pl.multiple_of(g * cap + up, SUBLANES), pad), :],
                                             sem))

        pad_copies(lambda cp: cp.start())
        pad_copies(lambda cp: cp.wait())


def _dispatch(h, pos, counts, cap):
    n_rows, d = h.shape
    tm = TOK_TILE
    return pl.pallas_call(
        functools.partial(_dispatch_kernel, cap=cap),
        grid_spec=pltpu.PrefetchScalarGridSpec(
            num_scalar_prefetch=2, grid=(n_rows // tm,),
            in_specs=[pl.BlockSpec((tm, d), lambda i, pos, cnt: (i, 0))],
            out_specs=pl.BlockSpec(memory_space=pl.ANY),
            scratch_shapes=[pltpu.VMEM((MOE_TILE, d), F32), pltpu.SemaphoreType.DMA(())]),
        out_shape=jax.ShapeDtypeStruct((N_GROUPS * cap, d), F32),
        compiler_params=_cparams(("arbitrary",)),
        name="moe_dispatch",
    )(pos, counts, h)


def _group_gates(x, wr, bias_row, grp):
    scores = _sigmoid(_dot3(x, wr))
    lane_i = lax.broadcasted_iota(jnp.int32, scores.shape, 1)
    lane = lane_i.astype(F32)
    m = jnp.where(lane_i // EXPERTS_PER_GROUP == grp, scores + bias_row, -jnp.inf)
    picks = []
    for _ in range(2):
        top = jnp.max(m, axis=-1, keepdims=True)
        idx = jnp.min(jnp.where(m == top, lane, float(LANES)), axis=-1, keepdims=True)
        picks.append(idx)
        m = jnp.where(lane == idx, -jnp.inf, m)
    w = [jnp.sum(jnp.where(lane == idx, scores, 0.0), axis=-1, keepdims=True) for idx in picks]
    tot = w[0] + w[1]
    return jnp.where(lane == picks[0], w[0] / tot, jnp.where(lane == picks[1], w[1] / tot, 0.0))


def _group_ffn_kernel(tg_ref, tb_ref, nv_ref, x_ref, wr_ref, rb_ref, w1_ref, w3_ref, w2_ref, o_ref):
    i = pl.program_id(0)

    @pl.when(i < nv_ref[0])
    def _():
        grp = tg_ref[i]
        x = x_ref[...]
        xb = x.astype(BF16)
        gate = _group_gates(x, wr_ref[...], rb_ref[...], grp)
        lane = lax.broadcasted_iota(jnp.int32, gate.shape, 1)
        acc = None
        for j in range(EXPERTS_PER_GROUP):
            gcol = jnp.sum(jnp.where(lane == grp * EXPERTS_PER_GROUP + j, gate, 0.0), axis=-1, keepdims=True)
            a = _silu(_dot(xb, w1_ref[j])) * _dot(xb, w3_ref[j]) * gcol
            y = _dot(a.astype(BF16), w2_ref[j])
            acc = y if acc is None else acc + y
        o_ref[...] = acc


def _group_ffn(h_sorted, tile_grp, tile_blk, n_valid, wr_pad, rb_row, w1, w3, w2, cap):
    p_rows, d = h_sorted.shape
    tm = MOE_TILE
    f = w1.shape[2]
    n_tiles = tile_grp.shape[0]
    blocks_per_group = cap // tm
    row_blk = lambda i, tg, tb, nv: (tg[i] * blocks_per_group + tb[i], 0)
    experts = lambda i, tg, tb, nv: (tg[i], 0, 0)
    w_in_spec = pl.BlockSpec((EXPERTS_PER_GROUP, d, f), experts)
    return pl.pallas_call(
        _group_ffn_kernel,
        grid_spec=pltpu.PrefetchScalarGridSpec(
            num_scalar_prefetch=3, grid=(n_tiles,),
            in_specs=[pl.BlockSpec((tm, d), row_blk),
                      pl.BlockSpec((d, LANES), lambda i, tg, tb, nv: (0, 0)),
                      pl.BlockSpec((1, LANES), lambda i, tg, tb, nv: (0, 0)),
                      w_in_spec, w_in_spec, pl.BlockSpec((EXPERTS_PER_GROUP, f, d), experts)],
            out_specs=pl.BlockSpec((tm, d), row_blk)),
        out_shape=jax.ShapeDtypeStruct((p_rows, d), F32),
        compiler_params=_cparams(("arbitrary",)),
        name="moe_group_ffn",
    )(tile_grp, tile_blk, n_valid, h_sorted, wr_pad, rb_row, w1, w3, w2)


def _combine_kernel(pos_ref, x_ref, y_ref, g2_ref, nfin_ref, o_ref, buf, sem, *, final_norm):
    i = pl.program_id(0)
    tm = x_ref.shape[0]
    slot = i % 2

    def gather(tile, slot_):
        def issue(r, carry):
            _rows_copy(y_ref, buf.at[slot_], sem.at[slot_], pos_ref[tile * tm + r], r).start()
            return carry
        lax.fori_loop(0, tm, issue, 0, unroll=ROW_DMA_UNROLL)

    @pl.when(i == 0)
    def _():
        gather(0, 0)

    @pl.when(i + 1 < pl.num_programs(0))
    def _():
        gather(i + 1, 1 - slot)

    pltpu.make_async_copy(y_ref.at[pl.ds(0, tm), :], buf.at[slot], sem.at[slot]).wait()
    y = x_ref[...] + g2_ref[0] * buf[slot]
    if final_norm:
        y = y * lax.rsqrt(jnp.mean(y * y, axis=-1, keepdims=True) + EPS) * nfin_ref[...]
    o_ref[...] = y


def _combine(xs, y_sorted, pos, mod, layer, nfin, seq, nb, n_rows, final_norm):
    t, d = xs.shape
    tm = TOK_TILE
    base = layer * SUBLANES * 6
    out_rows = n_rows if final_norm else t
    return pl.pallas_call(
        functools.partial(_combine_kernel, final_norm=final_norm),
        grid_spec=pltpu.PrefetchScalarGridSpec(
            num_scalar_prefetch=1, grid=(n_rows // tm,),
            in_specs=[pl.BlockSpec((tm, d), lambda i, pos: (i, 0)),
                      pl.BlockSpec(memory_space=pl.ANY),
                      pl.BlockSpec((1, 1, d), lambda i, pos: (base + _mod_row(i, tm, seq, nb) * 6 + 5, 0, 0)),
                      pl.BlockSpec((1, d), lambda i, pos: (0, 0))],
            out_specs=pl.BlockSpec((tm, d), lambda i, pos: (i, 0)),
            scratch_shapes=[pltpu.VMEM((2, tm, d), F32), pltpu.SemaphoreType.DMA((2,))]),
        out_shape=jax.ShapeDtypeStruct((out_rows, d), F32),
        compiler_params=_cparams(("arbitrary",)),
        name="moe_combine",
    )(pos, xs, y_sorted, mod, nfin.reshape(1, d))


def _moe(xs, h, grp_rows, rank_rows, cnt, wr_pad, rb_row, w1, w3, w2, mod, layer, nfin, seq, nb, n_rows, final_norm):
    tm = MOE_TILE
    cap = n_rows + tm
    grp, rank = grp_rows.reshape(-1), rank_rows.reshape(-1)
    pos = grp * cap + rank
    counts = cnt[:N_GROUPS, 0].astype(jnp.int32)
    tiles = (counts + tm - 1) // tm
    ends = jnp.cumsum(tiles)
    n_tiles = n_rows // tm + N_GROUPS
    step = jnp.minimum(jnp.arange(n_tiles, dtype=jnp.int32), ends[-1] - 1)
    tile_grp = jnp.sum(step[:, None] >= ends[None, :], axis=1).astype(jnp.int32)
    tile_blk = step - (ends - tiles)[tile_grp]
    h_sorted = _dispatch(h, pos, counts, cap)
    y_sorted = _group_ffn(h_sorted, tile_grp, tile_blk, ends[-1:].astype(jnp.int32), wr_pad, rb_row, w1, w3, w2, cap)
    return _combine(xs, y_sorted, pos, mod, layer, nfin, seq, nb, n_rows, final_norm)


def kernel(x, c, ctx, c_ctx, w_mod, b_mod, norm_mix, norm_ffn, norm_final, ab_w_in, ab_i_bias, ab_f_bias,
           ab_conv, ab_a_log, ab_dt_bias, ab_norm_a, ab_norm_b, ab_w_out, c_w_in, c_f_bias, c_lb_raw, c_norm,
           c_w_out, w_router, router_bias, w1, w3, w2):
    nb, seq, d = x.shape
    n_ctx = ctx.shape[1]
    depth = w_mod.shape[0]
    n_lat = nb * seq
    assert nb + 1 <= SUBLANES and seq % 1024 == 0 and (nb * n_ctx) % 1024 == 0 and n_ctx % 256 == 0

    xs = jnp.concatenate([x.reshape(n_lat, d), ctx.reshape(nb * n_ctx, d)], axis=0)
    cpad = jnp.zeros((SUBLANES, d), F32).at[:nb].set(c).at[nb].set(c_ctx)
    mod = _modulation(cpad, w_mod, b_mod).reshape(depth * SUBLANES * 6, 1, d)

    lb_p = jax.nn.softmax(c_lb_raw.astype(F32), axis=0)
    lb_all = jnp.cumsum(lb_p, axis=0) - lb_p[0:1]

    wr_pad = jnp.zeros((d, LANES), F32).at[:, :N_EXPERTS].set(w_router)
    rb_col = router_bias.astype(F32).reshape(N_EXPERTS, 1)
    rb_row = jnp.zeros((1, LANES), F32).at[0, :N_EXPERTS].set(router_bias.astype(F32))
    w_main = H_AB * HEAD * 8

    out = None
    column_major = False
    for layer in range(depth):
        last = layer == depth - 1
        j = layer // 2
        want_cm = layer % 2 == 1 and j % 2 == 1
        needs_raster = layer % 2 == 0 or not want_cm
        if want_cm and not column_major:
            xs = _to_column_major(xs, n_lat, seq)
            column_major = True
        elif needs_raster and column_major:
            xs = jnp.concatenate([_from_column_major(xs[:n_lat], seq), xs[n_lat:]], axis=0)
            column_major = False
        if layer % 2 == 0:
            w_in = ab_w_in[j]
            gw = w_in[:, w_main:].reshape(d, 4, 2, H_AB)
            wg = jnp.zeros((2, d, LANES), F32).at[:, :, :4 * H_AB].set(
                jnp.transpose(gw, (2, 0, 1, 3)).reshape(2, d, 4 * H_AB)).astype(BF16)
            zeros = jnp.zeros((2, H_AB), F32)
            bias = jnp.concatenate([ab_i_bias[j], ab_f_bias[j], zeros, ab_dt_bias[j]], axis=1)
            alog = jnp.concatenate([zeros, zeros, zeros, ab_a_log[j]], axis=1)
            gpar = jnp.zeros((2, SUBLANES, LANES), F32).at[:, 0, :4 * H_AB].set(bias).at[:, 1, :4 * H_AB].set(alog)
            p, gates = _inproj(xs, norm_mix[layer], mod, layer, w_in[:, :w_main].astype(BF16), seq, nb, wg, gpar)
            ha_f, ha_b = _mlstm_scan(p, gates, seq, n_ctx, nb)
            qkv = _gdn_prep(p, ab_conv[j], seq, n_ctx, nb)
            ob_f, ob_b = _gdn_scan(qkv, gates, seq, n_ctx, nb)
            o_halves = (((ha_f, 0), (ha_b, 0), 0), ((ob_f, 0), (ob_b, 0), 0))
            gate_cols = (3, 7)
            hn_gain = jnp.concatenate([ab_norm_a[j], ab_norm_b[j]])
            w_out = ab_w_out[j]
        else:
            (p,) = _inproj(xs, norm_mix[layer], mod, layer, c_w_in[j].astype(BF16), seq, nb)
            oc_f, oc_b = _gla_scan(p, c_f_bias[j], lb_all[j], seq, n_ctx, nb)
            o_halves = (((oc_f, 0), (oc_b, 0), 0), ((oc_f, 0), (oc_b, 0), 1))
            gate_cols = (8, 9)
            hn_gain = c_norm[j]
            w_out = c_w_out[j]
        n_rows = n_lat if last else n_lat + nb * n_ctx
        xs, h, grp_rows, rank_rows, cnt = _outproj(xs, o_halves, p, gate_cols, hn_gain, w_out.astype(BF16), mod, layer,
                                                   norm_ffn[layer], wr_pad, rb_col, seq, nb, n_rows, layer % 2 == 0)
        res = _moe(xs, h, grp_rows, rank_rows, cnt, wr_pad, rb_row, w1[layer].astype(BF16), w3[layer].astype(BF16),
                   w2[layer].astype(BF16), mod, layer, norm_final, seq, nb, n_rows, last)
        if last:
            out = _from_column_major(res, seq) if column_major else res
        else:
            xs = res
    return out.reshape(nb, seq, d)
```

```python
import functools

import numpy as np
import jax
import jax.numpy as jnp
from jax import lax
from jax.experimental import pallas as pl
from jax.experimental.pallas import tpu as pltpu

F32 = jnp.float32
BF16 = jnp.bfloat16

EPS = 1e-6
GRID_W = 64
HEAD = 128
H_AB = 4
H_C = 8
CONV_K = 5
N_EXPERTS = 16
N_GROUPS = 4
EXPERTS_PER_GROUP = N_EXPERTS // N_GROUPS
LANES = 128
SUBLANES = 8
VMEM_LIMIT = 56 * 1024 * 1024

TOK_TILE = 512
MOE_TILE = 512
ROW_DMA_UNROLL = 8
MLSTM_CHUNK = 128
GDN_CHUNK = 64
GDN_SCAN_CHUNKS = 4
GLA_CHUNK = 128
GDN_INV_BLOCK = 16


def _cparams(sem):
    return pltpu.CompilerParams(dimension_semantics=sem, vmem_limit_bytes=VMEM_LIMIT)


def _dot(a, b):
    return jnp.dot(a, b, preferred_element_type=F32)


def _dot_nt(a, b):
    return lax.dot_general(a, b, (((1,), (1,)), ((), ())), preferred_element_type=F32)


def _dot_tn(a, b):
    return lax.dot_general(a, b, (((0,), (0,)), ((), ())), preferred_element_type=F32)


def _split2(x):
    hi = x.astype(BF16)
    lo = (x - hi.astype(F32)).astype(BF16)
    return hi, lo


def _split3(x):
    hi = x.astype(BF16)
    r = x - hi.astype(F32)
    mid = r.astype(BF16)
    lo = (r - mid.astype(F32)).astype(BF16)
    return hi, mid, lo


def _dot_exact_lhs(m_bf16, x):
    hi, mid, lo = _split3(x)
    return _dot(m_bf16, hi) + _dot(m_bf16, mid) + _dot(m_bf16, lo)


def _dot3(a, b):
    ah, al = _split2(a)
    bh, bl = _split2(b)
    return _dot(ah, bh) + _dot(ah, bl) + _dot(al, bh)


def _dot_nt3(a, b):
    ah, al = _split2(a)
    bh, bl = _split2(b)
    return _dot_nt(ah, bh) + _dot_nt(ah, bl) + _dot_nt(al, bh)


def _sigmoid(z):
    return 1.0 / (1.0 + jnp.exp(-z))


def _sigmoid_pair(z):
    return _sigmoid(z), _sigmoid(-z)


def _silu(z):
    return z * _sigmoid(z)


def _mod_kernel(c_ref, w_ref, b_ref, o_ref):
    s = _silu(c_ref[...]).astype(BF16)
    o_ref[0] = _dot(s, w_ref[0].astype(BF16)) + b_ref[0]


def _modulation(cpad, w_mod, b_mod):
    depth, d, n = w_mod.shape
    tn = 1536
    return pl.pallas_call(
        _mod_kernel,
        grid=(depth, n // tn),
        in_specs=[pl.BlockSpec((SUBLANES, d), lambda l, j: (0, 0)),
                  pl.BlockSpec((1, d, tn), lambda l, j: (l, 0, j)),
                  pl.BlockSpec((1, 1, tn), lambda l, j: (l, 0, j))],
        out_specs=pl.BlockSpec((1, SUBLANES, tn), lambda l, j: (l, 0, j)),
        out_shape=jax.ShapeDtypeStruct((depth, SUBLANES, n), F32),
        compiler_params=_cparams(("parallel", "parallel")),
        name="modulation",
    )(cpad, w_mod, b_mod.reshape(depth, 1, n))


def _gate_activations(raw, par):
    y = raw + par[0:1, :]
    e = jnp.exp(-jnp.abs(y))
    sp = jnp.log1p(e)
    lsig = jnp.minimum(y, 0.0) - sp
    r = 1.0 / (1.0 + e)
    sig = jnp.where(y >= 0, r, e * r)
    gdec = -jnp.exp(par[1:2, :]) * (jnp.maximum(y, 0.0) + sp)
    lane = lax.broadcasted_iota(jnp.int32, raw.shape, 1)
    return jnp.where(lane < 4, y, jnp.where(lane < 8, lsig, jnp.where(lane < 12, sig, gdec)))


def _inproj_kernel(x_ref, gain_ref, sc_ref, sh_ref, w_ref, *rest, with_gates):
    if with_gates:
        wg_ref, gpar_ref, p_ref, g_ref, u_ref = rest
    else:
        p_ref, u_ref = rest

    @pl.when(pl.program_id(1) == 0)
    def _():
        x = x_ref[...]
        y = x * lax.rsqrt(jnp.mean(x * x, axis=-1, keepdims=True) + EPS) * gain_ref[...]
        u = (y * (1.0 + sc_ref[0]) + sh_ref[0]).astype(BF16)
        u_ref[...] = u
        if with_gates:
            for d in range(2):
                g_ref[d] = _gate_activations(_dot(u, wg_ref[d]), gpar_ref[d])

    p_ref[...] = _dot(u_ref[...], w_ref[...])


def _mod_row(i, tm, seq, nb):
    return jnp.minimum((i * tm) // seq, nb)


def _inproj(xs, gain, mod, layer, w, seq, nb, wg=None, gpar=None):
    t, d = xs.shape
    p = w.shape[1]
    tm, tn = TOK_TILE, 1024
    with_gates = wg is not None
    base = layer * SUBLANES * 6

    def mod_spec(which):
        return pl.BlockSpec((1, 1, d), lambda i, j: (base + _mod_row(i, tm, seq, nb) * 6 + which, 0, 0))

    in_specs = [pl.BlockSpec((tm, d), lambda i, j: (i, 0)),
                pl.BlockSpec((1, d), lambda i, j: (0, 0)),
                mod_spec(1), mod_spec(0),
                pl.BlockSpec((d, tn), lambda i, j: (0, j))]
    args = [xs, gain.reshape(1, d), mod, mod, w]
    out_specs = [pl.BlockSpec((tm, tn), lambda i, j: (i, j))]
    out_shape = [jax.ShapeDtypeStruct((t, p), F32)]
    if with_gates:
        in_specs += [pl.BlockSpec((2, d, LANES), lambda i, j: (0, 0, 0)),
                     pl.BlockSpec((2, SUBLANES, LANES), lambda i, j: (0, 0, 0))]
        args += [wg, gpar]
        out_specs.append(pl.BlockSpec((2, tm, LANES), lambda i, j: (0, i, 0)))
        out_shape.append(jax.ShapeDtypeStruct((2, t, LANES), F32))
    return pl.pallas_call(
        functools.partial(_inproj_kernel, with_gates=with_gates),
        grid=(t // tm, p // tn),
        in_specs=in_specs, out_specs=out_specs, out_shape=out_shape,
        scratch_shapes=[pltpu.VMEM((tm, d), BF16)],
        compiler_params=_cparams(("parallel", "arbitrary")),
        name="inproj",
    )(*args)


def _gdn_prep_kernel(prev_ref, x_ref, next_ref, w_ref, o_ref, *, blocks_per_seq, n_lat_blocks, scale):
    i = pl.program_id(0)
    tm = x_ref.shape[0]
    in_lat = i < n_lat_blocks
    first = jnp.logical_or(jnp.logical_not(in_lat), i % blocks_per_seq == 0)
    last = jnp.logical_or(jnp.logical_not(in_lat), i % blocks_per_seq == blocks_per_seq - 1)
    prev = jnp.where(first, 0.0, prev_ref[...])
    nxt = jnp.where(last, 0.0, next_ref[...])
    xe = jnp.concatenate([prev, x_ref[...], nxt], axis=0)
    n = xe.shape[0]
    pad = (CONV_K - 1) // 2
    acc = None
    for j in range(CONV_K):
        shift = (pad - j) % n
        xr = xe if shift == 0 else pltpu.roll(xe, shift, 0)
        term = xr[SUBLANES:SUBLANES + tm] * w_ref[j:j + 1, :]
        acc = term if acc is None else acc + term
    y = _silu(acc)
    part = pl.program_id(1)
    for h in range(H_AB):
        sl = slice(h * HEAD, (h + 1) * HEAD)
        t = y[:, sl]
        inv_norm = lax.rsqrt(jnp.sum(t * t, axis=-1, keepdims=True) + EPS)
        factor = jnp.where(part == 0, inv_norm * scale, jnp.where(part == 1, inv_norm, 1.0))
        o_ref[:, sl] = t * factor


def _gdn_prep(p, conv_w, seq, n_ctx, nb):
    t = p.shape[0]
    tm = 256
    w = H_AB * HEAD
    first_col = 4
    wpad = jnp.zeros((SUBLANES, 3 * w), F32).at[:CONV_K].set(conv_w)
    hb = tm // SUBLANES
    nblk8 = t // SUBLANES
    return pl.pallas_call(
        functools.partial(_gdn_prep_kernel, blocks_per_seq=seq // tm, n_lat_blocks=nb * seq // tm,
                          scale=HEAD ** -0.5),
        grid=(t // tm, 3),
        in_specs=[pl.BlockSpec((SUBLANES, w), lambda i, c: (jnp.maximum(i * hb - 1, 0), first_col + c)),
                  pl.BlockSpec((tm, w), lambda i, c: (i, first_col + c)),
                  pl.BlockSpec((SUBLANES, w), lambda i, c: (jnp.minimum((i + 1) * hb, nblk8 - 1), first_col + c)),
                  pl.BlockSpec((SUBLANES, w), lambda i, c: (0, c))],
        out_specs=pl.BlockSpec((tm, w), lambda i, c: (i, c)),
        out_shape=jax.ShapeDtypeStruct((t, 3 * w), F32),
        compiler_params=_cparams(("parallel", "parallel")),
        name="gdn_prep",
    )(p, p, p, wpad)


def _tri_tables(c):
    idx = np.arange(c)
    fwd = (idx[None, :] <= idx[:, None]).astype(np.float32)
    return jnp.asarray(np.stack([fwd, fwd.T]))


def _to_cm_kernel(x3_ref, x2_ref, o_ref, *, n_lat_tiles):
    i = pl.program_id(0)
    rows = x3_ref.shape[0]

    @pl.when(i < n_lat_tiles)
    def _():
        for wl in range(SUBLANES):
            o_ref[wl * rows:(wl + 1) * rows, :] = x3_ref[:, wl, :]

    @pl.when(i >= n_lat_tiles)
    def _():
        o_ref[...] = x2_ref[...]


def _from_cm_kernel(x_ref, o_ref):
    rows = o_ref.shape[0]
    for wl in range(SUBLANES):
        o_ref[:, wl, :] = x_ref[wl * rows:(wl + 1) * rows, :]


def _to_column_major(xs, n_lat, seq):
    t, d = xs.shape
    rows = seq // GRID_W
    tile = rows * SUBLANES
    assert rows % SUBLANES == 0 and n_lat % tile == 0 and (t - n_lat) % tile == 0
    n_lat_tiles = n_lat // tile
    per_b = GRID_W // SUBLANES
    lat = lambda i: jnp.minimum(i, n_lat_tiles - 1)
    return pl.pallas_call(
        functools.partial(_to_cm_kernel, n_lat_tiles=n_lat_tiles),
        grid=(t // tile,),
        in_specs=[pl.BlockSpec((rows, SUBLANES, d), lambda i: (lat(i) // per_b, lat(i) % per_b, 0)),
                  pl.BlockSpec((tile, d), lambda i: (jnp.maximum(i, n_lat_tiles - 1), 0))],
        out_specs=pl.BlockSpec((tile, d), lambda i: (i, 0)),
        out_shape=jax.ShapeDtypeStruct((t, d), xs.dtype),
        compiler_params=_cparams(("parallel",)),
        name="to_column_major",
    )(xs.reshape(t // GRID_W, GRID_W, d), xs)


def _from_column_major(y, seq):
    n_lat, d = y.shape
    rows = seq // GRID_W
    tile = rows * SUBLANES
    per_b = GRID_W // SUBLANES
    out = pl.pallas_call(
        _from_cm_kernel,
        grid=(n_lat // tile,),
        in_specs=[pl.BlockSpec((tile, d), lambda i: (i, 0))],
        out_specs=pl.BlockSpec((rows, SUBLANES, d), lambda i: (i // per_b, i % per_b, 0)),
        out_shape=jax.ShapeDtypeStruct((n_lat // GRID_W, GRID_W, d), y.dtype),
        compiler_params=_cparams(("parallel",)),
        name="from_column_major",
    )(y)
    return out.reshape(n_lat, d)


def _mlstm_kernel(qf, kf, vf, qb, kb, vb, gf, gb, tri_ref, hf_ref, hb_ref, s_scr, m_scr, *, scale):
    @pl.when(pl.program_id(1) == 0)
    def _():
        s_scr[...] = jnp.zeros_like(s_scr)
        m_scr[...] = jnp.zeros_like(m_scr)

    c = gf.shape[1]
    ones_col = (lax.broadcasted_iota(jnp.int32, (c, HEAD), 1) == 0).astype(BF16)
    qkv = ((qf, kf, vf), (qb, kb, vb))
    h_out = (hf_ref, hb_ref)
    g = [gf[0], gb[0]]
    mask = [tri_ref[d] > 0.5 for d in range(2)]
    b = [_dot_exact_lhs(tri_ref[d].astype(BF16), g[d]) for d in range(2)]
    total = [jnp.sum(x, axis=0, keepdims=True) for x in g]
    g_t = [x.T for x in g]
    b_t = [x.T for x in b]
    chains = [(d, h) for d in range(2) for h in range(H_AB)]
    sl = [slice(h * HEAD, (h + 1) * HEAD) for d, h in chains]
    q = [qkv[d][0][:, sl[i]].astype(BF16) for i, (d, h) in enumerate(chains)]
    k = [qkv[d][1][:, sl[i]] * scale for i, (d, h) in enumerate(chains)]
    v_aug = [jnp.concatenate([qkv[d][2][:, sl[i]].astype(BF16), ones_col], axis=1) for i, (d, h) in enumerate(chains)]
    s = [s_scr[d, h] for d, h in chains]
    qk = [_dot_nt(q[i], k[i].astype(BF16)) for i in range(len(chains))]
    qs = [_dot(q[i], s[i].astype(BF16)) for i in range(len(chains))]
    w_intra, w_inter, m_ts, colvs, m_sts, tots = [], [], [], [], [], []
    for i, (d, h) in enumerate(chains):
        li_c, li_r = g[d][:, h:h + 1], g_t[d][h:h + 1, :]
        b_c, b_r = b[d][:, 4 + h:5 + h], b_t[d][4 + h:5 + h, :]
        tot = total[d][:, 4 + h:5 + h]
        m_st = m_scr[d, h][0:1, 0:1]
        d_log = jnp.where(mask[d], b_c - b_r + li_r, -jnp.inf)
        m_inter = b_c + m_st
        m_t = jnp.maximum(m_inter, jnp.max(d_log, axis=-1, keepdims=True))
        w_inter.append(jnp.exp(m_inter - m_t))
        w_intra.append((jnp.exp(d_log - m_t) * qk[i]).astype(BF16))
        m_ts.append(m_t); colvs.append(tot - b_c + li_c); m_sts.append(m_st); tots.append(tot)
    intra = [_dot(w_intra[i], v_aug[i]) for i in range(len(chains))]
    m_new = [jnp.maximum(tots[i] + m_sts[i], jnp.max(colvs[i], axis=0, keepdims=True)) for i in range(len(chains))]
    upd = [_dot_tn((k[i] * jnp.exp(colvs[i] - m_new[i])).astype(BF16), v_aug[i]) for i in range(len(chains))]
    for i, (d, h) in enumerate(chains):
        num = w_inter[i] * qs[i] + intra[i]
        den = num[:, HEAD:HEAD + 1]
        h_out[d][0, :, sl[i]] = num[:, :HEAD] / jnp.maximum(jnp.abs(den), jnp.exp(-m_ts[i]))
        s_scr[d, h] = jnp.exp(tots[i] + m_sts[i] - m_new[i]) * s[i] + upd[i]
        m_scr[d, h] = jnp.broadcast_to(m_new[i], (SUBLANES, LANES))


def _step_block(chunk, seq, n_ctx, nb):
    n_c, n_l = n_ctx // chunk, seq // chunk
    off = nb * n_l

    def fn(b, d, i):
        j = i - n_c
        ctx_blk = off + b * n_c + (i if d == 0 else n_c - 1 - i)
        lat_blk = b * n_l + (j if d == 0 else n_l - 1 - j)
        return jnp.where(i < n_c, ctx_blk, lat_blk)

    return n_c + n_l, fn


def _mlstm_scan(p, gates, seq, n_ctx, nb):
    t = p.shape[0]
    c = MLSTM_CHUNK
    w = H_AB * HEAD
    n_steps, blk = _step_block(c, seq, n_ctx, nb)
    in_specs = [pl.BlockSpec((c, w), lambda b, i, d=d, col=col: (blk(b, d, i), col)) for d in range(2) for col in range(3)]
    in_specs += [pl.BlockSpec((1, c, LANES), lambda b, i, d=d: (d, blk(b, d, i), 0)) for d in range(2)]
    in_specs.append(pl.BlockSpec((2, c, c), lambda b, i: (0, 0, 0)))
    return pl.pallas_call(
        functools.partial(_mlstm_kernel, scale=HEAD ** -0.5),
        grid=(nb, n_steps),
        in_specs=in_specs,
        out_specs=[pl.BlockSpec((1, c, w), lambda b, i, d=d: (0, blk(b, d, i), 0)) for d in range(2)],
        out_shape=[jax.ShapeDtypeStruct((1, t, w), F32)] * 2,
        scratch_shapes=[pltpu.VMEM((2, H_AB, HEAD, 2 * HEAD), F32), pltpu.VMEM((2, H_AB, SUBLANES, LANES), F32)],
        compiler_params=_cparams(("parallel", "arbitrary")),
        name="mlstm_scan",
    )(p, p, p, p, p, p, gates, gates, _tri_tables(c))


def _gdn_inv_masks(c):
    idx = np.arange(c)
    blk = lambda n: (idx[:, None] // n) == (idx[None, :] // n)
    levels = [blk(GDN_INV_BLOCK)]
    n = GDN_INV_BLOCK
    while n < c:
        levels.append(np.logical_and(blk(2 * n), np.logical_not(blk(n))))
        n *= 2
    levels = np.stack(levels).astype(np.float32)
    return jnp.asarray(np.tile(levels, (1, 1, H_AB)))


def _block_diag(x, n_blocks):
    blk = lax.broadcasted_iota(jnp.int32, x.shape, 1) // (x.shape[1] // n_blocks)
    return jnp.concatenate([jnp.where(blk == j, x, jnp.zeros_like(x)) for j in range(n_blocks)], axis=0)


def _heads_to_lanes(x, first, width):
    return jnp.concatenate([jnp.broadcast_to(x[:, first + h:first + h + 1], (x.shape[0], width))
                            for h in range(H_AB)], axis=1)


def _unit_tri_inverse(a_list, eye, inv_masks):
    prod = lambda xs, ys: [_dot(x.astype(BF16), _block_diag(y.astype(BF16), H_AB)) for x, y in zip(xs, ys)]
    pw = [a * inv_masks[0] for a in a_list]
    inv = [eye - p for p in pw]
    n = 2
    while n < GDN_INV_BLOCK:
        pw = prod(pw, pw)
        inv = [i + t for i, t in zip(inv, prod(inv, pw))]
        n *= 2
    for lvl in range(1, inv_masks.shape[0]):
        t = prod(prod(inv, [a * inv_masks[lvl] for a in a_list]), inv)
        inv = [i - x for i, x in zip(inv, t)]
    return inv


def _gdn_chunk_kernel(q_ref, k_ref, v_ref, g_ref, tri_ref, im_ref, u_ref, w_ref, qd_ref, kd_ref, att_ref):
    c = GDN_CHUNK
    n_chunks = q_ref.shape[0] // c
    inv_masks = im_ref[...]
    row = lax.broadcasted_iota(jnp.int32, (c, H_AB * c), 0)
    col = lax.broadcasted_iota(jnp.int32, (c, H_AB * c), 1) % c
    eye_b = row == col
    eye = eye_b.astype(F32)
    masks = [col <= row, col >= row]
    stricts = [col < row, col > row]
    tris = [tri_ref[d].astype(BF16) for d in range(2)]
    ones = jnp.ones((c, c), BF16)
    chunks = range(n_chunks)
    rows = [slice(ci * c, (ci + 1) * c) for ci in chunks]
    groups = [(ci, d) for ci in chunks for d in range(2)]
    k_hl = [_split2(k_ref[r, :]) for r in rows]
    kbd = [(_block_diag(hi, H_AB), _block_diag(lo, H_AB)) for hi, lo in k_hl]
    r_hi = [_dot_nt(jnp.concatenate([k_hl[ci][0], k_hl[ci][1], q_ref[rows[ci], :].astype(BF16)], axis=0), kbd[ci][0])
            for ci in chunks]
    r_lo = [_dot_nt(k_hl[ci][0], kbd[ci][1]) for ci in chunks]
    kk = [r_hi[ci][:c] + r_hi[ci][c:2 * c] + r_lo[ci] for ci in chunks]
    g = [g_ref[d, rows[ci], :] for ci, d in groups]
    gc = [_dot_exact_lhs(tris[d], g[i]) for i, (ci, d) in enumerate(groups)]
    gc_c = [_heads_to_lanes(x, 12, c) for x in gc]
    gc_r = [_dot_exact_lhs(ones, eye * x) for x in gc_c]
    decay = [jnp.exp(jnp.where(masks[d], gc_c[i] - gc_r[i], -jnp.inf)) for i, (ci, d) in enumerate(groups)]
    a = [jnp.where(stricts[d], _heads_to_lanes(g[i], 8, c) * kk[ci] * decay[i], 0.0)
         for i, (ci, d) in enumerate(groups)]
    inv = _unit_tri_inverse(a, eye, inv_masks)
    inv_hl = [jnp.concatenate(_split2(x), axis=0) for x in inv]
    gc_w = [_heads_to_lanes(x, 12, HEAD) for x in gc]
    eg = [jnp.exp(x) for x in gc_w]
    beta = [_heads_to_lanes(x, 8, HEAD) for x in g]
    res_u = [_dot(inv_hl[i], _block_diag((v_ref[rows[ci], :] * beta[i]).astype(BF16), H_AB))
             for i, (ci, d) in enumerate(groups)]
    res_w = [_dot(inv_hl[i], _block_diag((k_ref[rows[ci], :] * (beta[i] * eg[i])).astype(BF16), H_AB))
             for i, (ci, d) in enumerate(groups)]
    for i, (ci, d) in enumerate(groups):
        r = rows[ci]
        total = _heads_to_lanes(jnp.sum(g[i], axis=0, keepdims=True), 12, HEAD)
        u_ref[d, r, :] = (res_u[i][:c] + res_u[i][c:]).astype(BF16)
        w_ref[d, r, :] = (res_w[i][:c] + res_w[i][c:]).astype(BF16)
        qd_ref[d, r, :] = (q_ref[r, :] * eg[i]).astype(BF16)
        kd_ref[d, r, :] = (k_ref[r, :] * jnp.exp(total - gc_w[i])).astype(BF16)
        att_ref[d, r, :] = (r_hi[ci][2 * c:] * decay[i]).astype(BF16)


def _gdn_scan_kernel(*refs):
    ins, o_refs, s_scr = refs[:12], refs[12:14], refs[14]
    c = GDN_CHUNK

    @pl.when(pl.program_id(1) == 0)
    def _():
        s_scr[...] = jnp.zeros_like(s_scr)

    pw = 2 * HEAD
    rblk = lax.broadcasted_iota(jnp.int32, (pw, pw), 0) // HEAD
    cblk = lax.broadcasted_iota(jnp.int32, (pw, pw), 1) // HEAD
    on_diag = rblk == cblk
    chains = [(d, pr) for d in range(2) for pr in range(H_AB // 2)]
    refs_of = lambda d: ins[d::2]
    sl = [slice(pr * pw, (pr + 1) * pw) for d, pr in chains]
    s = [s_scr[d, pr] for d, pr in chains]
    n_sub = o_refs[0].shape[1] // c
    for k in range(n_sub):
        rows = [slice(kk * c, (kk + 1) * c) for kk in (k, n_sub - 1 - k)]
        res = [_dot(jnp.concatenate([refs_of(d)[1][0, rows[d], sl[i]], refs_of(d)[2][0, rows[d], sl[i]]], axis=0),
                    s[i].astype(BF16)) for i, (d, pr) in enumerate(chains)]
        v_new = [(refs_of(d)[0][0, rows[d], sl[i]].astype(F32) - res[i][:c]).astype(BF16)
                 for i, (d, pr) in enumerate(chains)]
        intra = [_dot(refs_of(d)[4][0, rows[d], pr * 2 * c:(pr + 1) * 2 * c], _block_diag(v_new[i], 2))
                 for i, (d, pr) in enumerate(chains)]
        upd = [_dot_tn(refs_of(d)[3][0, rows[d], sl[i]], v_new[i]) for i, (d, pr) in enumerate(chains)]
        totals = [jnp.sum(refs_of(d)[5][0, rows[d], :], axis=0, keepdims=True) for d in range(2)]
        for i, (d, pr) in enumerate(chains):
            o_refs[d][0, rows[d], sl[i]] = res[i][c:] + intra[i]
            gl = jnp.concatenate([jnp.broadcast_to(jnp.exp(totals[d][:, 12 + 2 * pr + j:13 + 2 * pr + j]), (1, HEAD))
                                  for j in range(2)], axis=1)
            s[i] = gl * s[i] + jnp.where(on_diag, upd[i], 0.0)
    for i, (d, pr) in enumerate(chains):
        s_scr[d, pr] = s[i]


def _gdn_scan(qkv, gates, seq, n_ctx, nb):
    t = qkv.shape[0]
    c = GDN_CHUNK
    w = H_AB * HEAD
    tri = _tri_tables(c)
    im = _gdn_inv_masks(c)
    tm = TOK_TILE
    sds = lambda width: jax.ShapeDtypeStruct((2, t, width), BF16)
    u, wv, qd, kd, att = pl.pallas_call(
        _gdn_chunk_kernel,
        grid=(t // tm,),
        in_specs=[pl.BlockSpec((tm, w), lambda i, col=col: (i, col)) for col in range(3)] + [
            pl.BlockSpec((2, tm, LANES), lambda i: (0, i, 0)),
            pl.BlockSpec((2, c, c), lambda i: (0, 0, 0)),
            pl.BlockSpec(im.shape, lambda i: (0, 0, 0))],
        out_specs=[pl.BlockSpec((2, tm, w), lambda i: (0, i, 0))] * 4
        + [pl.BlockSpec((2, tm, H_AB * c), lambda i: (0, i, 0))],
        out_shape=[sds(w)] * 4 + [sds(H_AB * c)],
        compiler_params=_cparams(("parallel",)),
        name="gdn_chunk",
    )(qkv, qkv, qkv, gates, tri, im)

    cs = c * GDN_SCAN_CHUNKS
    n_steps, blk = _step_block(cs, seq, n_ctx, nb)
    in_specs, args = [], []
    for arr, width in ((u, w), (wv, w), (qd, w), (kd, w), (att, H_AB * c), (gates, LANES)):
        for d in range(2):
            in_specs.append(pl.BlockSpec((1, cs, width), lambda b, i, d=d: (d, blk(b, d, i), 0)))
            args.append(arr)
    o_f, o_b = pl.pallas_call(
        _gdn_scan_kernel,
        grid=(nb, n_steps),
        in_specs=in_specs,
        out_specs=[pl.BlockSpec((1, cs, w), lambda b, i, d=d: (0, blk(b, d, i), 0)) for d in range(2)],
        out_shape=[jax.ShapeDtypeStruct((1, t, w), F32)] * 2,
        scratch_shapes=[pltpu.VMEM((2, H_AB // 2, 2 * HEAD, 2 * HEAD), F32)],
        compiler_params=_cparams(("parallel", "arbitrary")),
        name="gdn_scan",
    )(*args)
    return o_f, o_b


def _gla_tables(c):
    idx = np.arange(c)
    sel, sgn, msk = [], [], []
    for d in range(2):
        sel_d, sgn_d, msk_d = [], [], []
        bs = c // 2
        while bs >= 1:
            pair = idx // (2 * bs)
            second = (idx // bs) % 2 == 1
            ref_row = pair * 2 * bs + (bs - 1 if d == 0 else bs)
            sel_d.append((idx[None, :] == ref_row[:, None]).astype(np.float32))
            qside = second if d == 0 else np.logical_not(second)
            sgn_d.append(np.where(qside, 1.0, -1.0)[:, None] * np.ones((1, HEAD)))
            same_pair = pair[:, None] == pair[None, :]
            msk_d.append(np.logical_and(same_pair, np.logical_and(qside[:, None], np.logical_not(qside)[None, :])))
            bs //= 2
        msk_d.append(np.eye(c, dtype=bool))
        sel.append(np.concatenate(sel_d, axis=0))
        sgn.append(np.stack(sgn_d))
        msk.append(np.stack(msk_d).astype(np.float32))
    return (jnp.asarray(np.stack(sel), dtype=BF16), jnp.asarray(np.stack(sgn), dtype=F32),
            jnp.asarray(np.stack(msk), dtype=F32))


def _gla_kernel(qf_ref, zf_ref, vf_ref, qb_ref, zb_ref, vb_ref, fb_ref, lb_ref, tri_ref, sel_ref, sgn_ref, msk_ref,
                of_ref, ob_ref, s_scr):
    @pl.when(pl.program_id(1) == 0)
    def _():
        s_scr[...] = jnp.zeros_like(s_scr)

    c = qf_ref.shape[0]
    n_lvl = sgn_ref.shape[1]
    dirs = range(2)
    qzv = ((qf_ref, zf_ref, vf_ref), (qb_ref, zb_ref, vb_ref))
    o_refs = (of_ref, ob_ref)
    chains = [(d, h) for d in dirs for h in range(H_C)]
    sl = [slice(h * HEAD, (h + 1) * HEAD) for d, h in chains]
    g_all, key_all, q_all = [], [], []
    for d in dirs:
        lb = lb_ref[d]
        sig_pos, sig_neg = _sigmoid_pair(qzv[d][1][...] + fb_ref[d])
        g_all.append(jnp.log(lb + (1.0 - lb) * sig_pos))
        key_all.append((1.0 - lb) * sig_neg)
        q_all.append(_silu(qzv[d][0][...]))
    gc_all = [_dot_exact_lhs(tri_ref[d].astype(BF16), g_all[d]) for d in dirs]
    total_all = [jnp.sum(g, axis=0, keepdims=True) for g in g_all]
    refs = [_dot(sel_ref[d], gc_all[d].astype(BF16)) for d in dirs]
    diag = msk_ref[0, n_lvl] > 0.5
    att = [jnp.where(diag, _dot_nt(q_all[d][:, sl[i]].astype(BF16), key_all[d][:, sl[i]].astype(BF16)), 0.0)
           for i, (d, h) in enumerate(chains)]
    for l in range(n_lvl):
        sgn = [sgn_ref[d, l] for d in dirs]
        x = [(jnp.where(sgn[d] > 0, q_all[d][:, sl[i]], key_all[d][:, sl[i]])
              * jnp.exp(sgn[d] * (gc_all[d][:, sl[i]] - refs[d][l * c:(l + 1) * c, sl[i]]))).astype(BF16)
             for i, (d, h) in enumerate(chains)]
        prod = [_dot_nt(xh, xh) for xh in x]
        in_level = [msk_ref[d, l] > 0.5 for d in dirs]
        att = [jnp.where(in_level[d], prod[i], att[i]) for i, (d, h) in enumerate(chains)]
    st = [s_scr[d, h] for d, h in chains]
    inter = [_dot_nt((q_all[d][:, sl[i]] * jnp.exp(gc_all[d][:, sl[i]])).astype(BF16), st[i].astype(BF16))
             for i, (d, h) in enumerate(chains)]
    intra = [_dot(att[i].astype(BF16), qzv[d][2][:, sl[i]].astype(BF16)) for i, (d, h) in enumerate(chains)]
    upd = [_dot_tn(qzv[d][2][:, sl[i]].astype(BF16),
                   (key_all[d][:, sl[i]] * jnp.exp(total_all[d][:, sl[i]] - gc_all[d][:, sl[i]])).astype(BF16))
           for i, (d, h) in enumerate(chains)]
    for i, (d, h) in enumerate(chains):
        o_refs[d][0, :, sl[i]] = inter[i] + intra[i]
        s_scr[d, h] = jnp.exp(total_all[d][:, sl[i]]) * st[i] + upd[i]


def _gla_scan(p, f_bias, lb, seq, n_ctx, nb):
    t, pw = p.shape
    c = GLA_CHUNK
    w = H_C * HEAD
    tri = _tri_tables(c)
    sel, sgn, msk = _gla_tables(c)
    n_steps, blk = _step_block(c, seq, n_ctx, nb)
    dspec = lambda d, col: pl.BlockSpec((c, w), lambda b, i: (blk(b, d, i), col))
    whole = lambda a: pl.BlockSpec(a.shape, lambda b, i: (0,) * a.ndim)
    fb3, lb3 = f_bias.reshape(2, 1, w), lb.reshape(2, 1, w)
    consts = (fb3, lb3, tri, sel, sgn, msk)
    return pl.pallas_call(
        _gla_kernel,
        grid=(nb, n_steps),
        in_specs=[dspec(0, 0), dspec(0, 1), dspec(0, 3), dspec(1, 0), dspec(1, 2), dspec(1, 3)]
        + [whole(a) for a in consts],
        out_specs=[pl.BlockSpec((1, c, w), lambda b, i, d=d: (0, blk(b, d, i), 0)) for d in range(2)],
        out_shape=[jax.ShapeDtypeStruct((1, t, w), F32)] * 2,
        scratch_shapes=[pltpu.VMEM((2, H_C, HEAD, HEAD), F32)],
        compiler_params=_cparams(("parallel", "arbitrary")),
        name="gla_scan",
    )(p, p, p, p, p, p, *consts)


def _best_group(scores_t, bias):
    sel = [scores_t[e:e + 1, :] + bias[e:e + 1, :] for e in range(N_EXPERTS)]
    gscore = []
    for grp in range(N_GROUPS):
        v = sel[grp * EXPERTS_PER_GROUP:(grp + 1) * EXPERTS_PER_GROUP]
        best = None
        for a in range(EXPERTS_PER_GROUP):
            for b in range(a + 1, EXPERTS_PER_GROUP):
                pair = v[a] + v[b]
                best = pair if best is None else jnp.maximum(best, pair)
        gscore.append(best)
    best_g = jnp.zeros(gscore[0].shape, jnp.int32)
    best_v = gscore[0]
    for grp in range(1, N_GROUPS):
        better = gscore[grp] > best_v
        best_g = jnp.where(better, grp, best_g)
        best_v = jnp.where(better, gscore[grp], best_v)
    return best_g


def _rows_copy(src, dst, sem, src_row, dst_row):
    return pltpu.make_async_copy(src.at[pl.ds(src_row, 1), :], dst.at[pl.ds(dst_row, 1), :], sem)


def _outproj_kernel(x_ref, of0, ob0, gt0, of1, ob1, gt1, hn_ref, w_ref, g1_ref, sc_ref, sh_ref, nf_ref,
                    wr_ref, rb_ref, triu_ref, xo_ref, pos_ref, cnt_ref, hs_ref,
                    base_scr, h_scr, pos_v, pos_s, cnt_v, cnt_s, row_sem, meta_sem, *, first_half_sigmoid, cap):
    i = pl.program_id(0)
    n_steps = pl.num_programs(0)
    tm = x_ref.shape[0]
    slot = i % 2
    prev = 1 - slot
    dump = N_GROUPS * cap

    @pl.when(i == 0)
    def _():
        base_scr[...] = jnp.zeros_like(base_scr)
        h_scr[...] = jnp.zeros_like(h_scr)

        def init(r, carry):
            pos_s[1, r] = dump + r
            return carry
        lax.fori_loop(0, tm, init, 0)

    for r in range(tm):
        _rows_copy(h_scr.at[prev], hs_ref, row_sem, r, pos_s[prev, r]).start()

    feats = []
    for half, (of, ob, gt) in enumerate(((of0, ob0, gt0), (of1, ob1, gt1))):
        o = of[0] + ob[0]
        gate = gt[...]
        for h in range(o.shape[1] // HEAD):
            sl = slice(h * HEAD, (h + 1) * HEAD)
            t = o[:, sl]
            y = t * lax.rsqrt(jnp.mean(t * t, axis=-1, keepdims=True) + EPS)
            y = y * hn_ref[:, half * o.shape[1] + h * HEAD: half * o.shape[1] + (h + 1) * HEAD]
            gz = gate[:, sl]
            act = _sigmoid(gz) if (half == 0 and first_half_sigmoid) else _silu(gz)
            feats.append((y * act).astype(BF16))
    feats = jnp.concatenate(feats, axis=1)
    xn = x_ref[...] + g1_ref[0] * _dot(feats, w_ref[...])
    xo_ref[...] = xn
    y = xn * lax.rsqrt(jnp.mean(xn * xn, axis=-1, keepdims=True) + EPS) * nf_ref[...]
    hl = y * (1.0 + sc_ref[0]) + sh_ref[0]
    logits = _dot3(hl, wr_ref[...])
    scores_t = _sigmoid(logits).T
    best_g = _best_group(scores_t, rb_ref[...])
    onehot = jnp.concatenate([(best_g == g).astype(F32) for g in range(N_GROUPS)]
                             + [jnp.zeros((SUBLANES - N_GROUPS, tm), F32)], axis=0)
    before = _dot(onehot.astype(BF16), triu_ref[...])
    base = base_scr[...]
    rank = jnp.sum(onehot * (before + base[:, 0:1]), axis=0, keepdims=True)
    pos = best_g * cap + rank.astype(jnp.int32)
    pos_ref[0] = pos
    base = base + jnp.sum(onehot, axis=1, keepdims=True)
    base_scr[...] = base
    cnt_ref[...] = base

    pltpu.make_async_copy(h_scr.at[prev], hs_ref.at[pl.ds(0, tm), :], row_sem).wait()
    h_scr[slot] = hl
    pos_v[...] = pos
    to_smem = pltpu.make_async_copy(pos_v, pos_s.at[pl.ds(slot, 1), :], meta_sem)
    to_smem.start()
    to_smem.wait()

    @pl.when(i == n_steps - 1)
    def _():
        def issue(r, carry):
            _rows_copy(h_scr.at[slot], hs_ref, row_sem, r, pos_s[slot, r]).start()
            return carry
        lax.fori_loop(0, tm, issue, 0, unroll=ROW_DMA_UNROLL)
        pltpu.make_async_copy(h_scr.at[slot], hs_ref.at[pl.ds(0, tm), :], row_sem).wait()
        cnt_v[...] = base.astype(jnp.int32)
        counts = pltpu.make_async_copy(cnt_v, cnt_s, meta_sem)
        counts.start()
        counts.wait()
        h_scr[prev] = jnp.zeros((tm, h_scr.shape[2]), F32)
        zero = h_scr.at[prev]

        def pad_copies(action):
            for g in range(N_GROUPS):
                cnt = cnt_s[g, 0]
                up = (cnt + SUBLANES - 1) // SUBLANES * SUBLANES
                for r in range(SUBLANES - 1):
                    @pl.when(cnt + r < up)
                    def _():
                        action(_rows_copy(zero, hs_ref, row_sem, 0, g * cap + cnt + r))
                action(pltpu.make_async_copy(zero, hs_ref.at[pl.ds(pl.multiple_of(g * cap + up, SUBLANES), tm), :],
                                             row_sem))

        pad_copies(lambda cp: cp.start())
        pad_copies(lambda cp: cp.wait())


def _outproj(xs, o_halves, gate_src, gate_cols, hn_gain, w_out, mod, layer, nf_gain, wr_pad, rb_col,
             seq, nb, n_rows, first_half_sigmoid):
    t, d = xs.shape
    tm = TOK_TILE
    half = d // 2
    base = layer * SUBLANES * 6

    def mod_spec(which):
        return pl.BlockSpec((1, 1, d), lambda i: (base + _mod_row(i, tm, seq, nb) * 6 + which, 0, 0))

    in_specs = [pl.BlockSpec((tm, d), lambda i: (i, 0))]
    args = [xs]
    for ((arr_f, dir_f), (arr_b, dir_b), col), gcol in zip(o_halves, gate_cols):
        in_specs += [pl.BlockSpec((1, tm, half), lambda i, col=col, dd=dir_f: (dd, i, col)),
                     pl.BlockSpec((1, tm, half), lambda i, col=col, dd=dir_b: (dd, i, col)),
                     pl.BlockSpec((tm, half), lambda i, gcol=gcol: (i, gcol))]
        args += [arr_f, arr_b, gate_src]
    in_specs += [pl.BlockSpec((1, d), lambda i: (0, 0)),
                 pl.BlockSpec((d, d), lambda i: (0, 0)),
                 mod_spec(2), mod_spec(4), mod_spec(3),
                 pl.BlockSpec((1, d), lambda i: (0, 0)),
                 pl.BlockSpec((d, LANES), lambda i: (0, 0)),
                 pl.BlockSpec((N_EXPERTS, 1), lambda i: (0, 0)),
                 pl.BlockSpec((tm, tm), lambda i: (0, 0))]
    idx = np.arange(tm)
    triu = jnp.asarray(idx[:, None] < idx[None, :], dtype=BF16)
    args += [hn_gain.reshape(1, d), w_out, mod, mod, mod, nf_gain.reshape(1, d), wr_pad, rb_col, triu]
    n_tiles = n_rows // tm
    assert tm == MOE_TILE
    cap = n_rows + MOE_TILE
    return pl.pallas_call(
        functools.partial(_outproj_kernel, first_half_sigmoid=first_half_sigmoid, cap=cap),
        grid=(n_tiles,),
        in_specs=in_specs,
        out_specs=[pl.BlockSpec((tm, d), lambda i: (i, 0)),
                   pl.BlockSpec((1, 1, tm), lambda i: (i, 0, 0)),
                   pl.BlockSpec((SUBLANES, LANES), lambda i: (0, 0)),
                   pl.BlockSpec(memory_space=pl.ANY)],
        out_shape=[jax.ShapeDtypeStruct((t, d), F32), jax.ShapeDtypeStruct((n_tiles, 1, tm), jnp.int32),
                   jax.ShapeDtypeStruct((SUBLANES, LANES), F32),
                   jax.ShapeDtypeStruct((N_GROUPS * cap + tm, d), F32)],
        scratch_shapes=[pltpu.VMEM((SUBLANES, LANES), F32), pltpu.VMEM((2, tm, d), F32),
                        pltpu.VMEM((1, tm), jnp.int32), pltpu.SMEM((2, tm), jnp.int32),
                        pltpu.VMEM((SUBLANES, LANES), jnp.int32), pltpu.SMEM((SUBLANES, LANES), jnp.int32),
                        pltpu.SemaphoreType.DMA(()), pltpu.SemaphoreType.DMA(())],
        compiler_params=_cparams(("arbitrary",)),
        name="outproj",
    )(*args)


def _group_gates(x, wr, bias_row, grp):
    scores = _sigmoid(_dot3(x, wr))
    lane_i = lax.broadcasted_iota(jnp.int32, scores.shape, 1)
    lane = lane_i.astype(F32)
    m = jnp.where(lane_i // EXPERTS_PER_GROUP == grp, scores + bias_row, -jnp.inf)
    picks = []
    for _ in range(2):
        top = jnp.max(m, axis=-1, keepdims=True)
        idx = jnp.min(jnp.where(m == top, lane, float(LANES)), axis=-1, keepdims=True)
        picks.append(idx)
        m = jnp.where(lane == idx, -jnp.inf, m)
    w = [jnp.sum(jnp.where(lane == idx, scores, 0.0), axis=-1, keepdims=True) for idx in picks]
    tot = w[0] + w[1]
    return jnp.where(lane == picks[0], w[0] / tot, jnp.where(lane == picks[1], w[1] / tot, 0.0))


def _group_ffn_kernel(tg_ref, tb_ref, nv_ref, x_ref, wr_ref, rb_ref, w1_ref, w3_ref, w2_ref, o_ref):
    i = pl.program_id(0)

    @pl.when(i < nv_ref[0])
    def _():
        grp = tg_ref[i]
        x = x_ref[...]
        xb = x.astype(BF16)
        gate = _group_gates(x, wr_ref[...], rb_ref[...], grp)
        lane = lax.broadcasted_iota(jnp.int32, gate.shape, 1)
        acc = None
        for j in range(EXPERTS_PER_GROUP):
            gcol = jnp.sum(jnp.where(lane == grp * EXPERTS_PER_GROUP + j, gate, 0.0), axis=-1, keepdims=True)
            a = _silu(_dot(xb, w1_ref[j])) * _dot(xb, w3_ref[j]) * gcol
            y = _dot(a.astype(BF16), w2_ref[j])
            acc = y if acc is None else acc + y
        o_ref[...] = acc


def _group_ffn(h_sorted, tile_grp, tile_blk, n_valid, wr_pad, rb_row, w1, w3, w2, cap):
    p_rows, d = h_sorted.shape
    tm = MOE_TILE
    f = w1.shape[2]
    n_tiles = tile_grp.shape[0]
    blocks_per_group = cap // tm
    row_blk = lambda i, tg, tb, nv: (tg[i] * blocks_per_group + tb[i], 0)
    experts = lambda i, tg, tb, nv: (tg[i], 0, 0)
    w_in_spec = pl.BlockSpec((EXPERTS_PER_GROUP, d, f), experts)
    return pl.pallas_call(
        _group_ffn_kernel,
        grid_spec=pltpu.PrefetchScalarGridSpec(
            num_scalar_prefetch=3, grid=(n_tiles,),
            in_specs=[pl.BlockSpec((tm, d), row_blk),
                      pl.BlockSpec((d, LANES), lambda i, tg, tb, nv: (0, 0)),
                      pl.BlockSpec((1, LANES), lambda i, tg, tb, nv: (0, 0)),
                      w_in_spec, w_in_spec, pl.BlockSpec((EXPERTS_PER_GROUP, f, d), experts)],
            out_specs=pl.BlockSpec((tm, d), row_blk)),
        out_shape=jax.ShapeDtypeStruct((p_rows, d), F32),
        compiler_params=_cparams(("arbitrary",)),
        name="moe_group_ffn",
    )(tile_grp, tile_blk, n_valid, h_sorted, wr_pad, rb_row, w1, w3, w2)


def _combine_kernel(pos_ref, x_ref, y_ref, g2_ref, nfin_ref, o_ref, buf, sem, *, final_norm):
    i = pl.program_id(0)
    tm = x_ref.shape[0]
    slot = i % 2

    def gather(tile, slot_):
        def issue(r, carry):
            _rows_copy(y_ref, buf.at[slot_], sem.at[slot_], pos_ref[tile * tm + r], r).start()
            return carry
        lax.fori_loop(0, tm, issue, 0, unroll=ROW_DMA_UNROLL)

    @pl.when(i == 0)
    def _():
        gather(0, 0)

    @pl.when(i + 1 < pl.num_programs(0))
    def _():
        gather(i + 1, 1 - slot)

    pltpu.make_async_copy(y_ref.at[pl.ds(0, tm), :], buf.at[slot], sem.at[slot]).wait()
    y = x_ref[...] + g2_ref[0] * buf[slot]
    if final_norm:
        y = y * lax.rsqrt(jnp.mean(y * y, axis=-1, keepdims=True) + EPS) * nfin_ref[...]
    o_ref[...] = y


def _combine(xs, y_sorted, pos, mod, layer, nfin, seq, nb, n_rows, final_norm):
    t, d = xs.shape
    tm = TOK_TILE
    base = layer * SUBLANES * 6
    out_rows = n_rows if final_norm else t
    return pl.pallas_call(
        functools.partial(_combine_kernel, final_norm=final_norm),
        grid_spec=pltpu.PrefetchScalarGridSpec(
            num_scalar_prefetch=1, grid=(n_rows // tm,),
            in_specs=[pl.BlockSpec((tm, d), lambda i, pos: (i, 0)),
                      pl.BlockSpec(memory_space=pl.ANY),
                      pl.BlockSpec((1, 1, d), lambda i, pos: (base + _mod_row(i, tm, seq, nb) * 6 + 5, 0, 0)),
                      pl.BlockSpec((1, d), lambda i, pos: (0, 0))],
            out_specs=pl.BlockSpec((tm, d), lambda i, pos: (i, 0)),
            scratch_shapes=[pltpu.VMEM((2, tm, d), F32), pltpu.SemaphoreType.DMA((2,))]),
        out_shape=jax.ShapeDtypeStruct((out_rows, d), F32),
        compiler_params=_cparams(("arbitrary",)),
        name="moe_combine",
    )(pos, xs, y_sorted, mod, nfin.reshape(1, d))


def _moe(xs, h_sorted, pos_rows, cnt, wr_pad, rb_row, w1, w3, w2, mod, layer, nfin, seq, nb, n_rows, final_norm):
    tm = MOE_TILE
    cap = n_rows + tm
    pos = pos_rows.reshape(-1)
    counts = cnt[:N_GROUPS, 0].astype(jnp.int32)
    tiles = (counts + tm - 1) // tm
    ends = jnp.cumsum(tiles)
    n_tiles = n_rows // tm + N_GROUPS
    step = jnp.minimum(jnp.arange(n_tiles, dtype=jnp.int32), ends[-1] - 1)
    tile_grp = jnp.sum(step[:, None] >= ends[None, :], axis=1).astype(jnp.int32)
    tile_blk = step - (ends - tiles)[tile_grp]
    y_sorted = _group_ffn(h_sorted, tile_grp, tile_blk, ends[-1:].astype(jnp.int32), wr_pad, rb_row, w1, w3, w2, cap)
    return _combine(xs, y_sorted, pos, mod, layer, nfin, seq, nb, n_rows, final_norm)


def kernel(x, c, ctx, c_ctx, w_mod, b_mod, norm_mix, norm_ffn, norm_final, ab_w_in, ab_i_bias, ab_f_bias,
           ab_conv, ab_a_log, ab_dt_bias, ab_norm_a, ab_norm_b, ab_w_out, c_w_in, c_f_bias, c_lb_raw, c_norm,
           c_w_out, w_router, router_bias, w1, w3, w2):
    nb, seq, d = x.shape
    n_ctx = ctx.shape[1]
    depth = w_mod.shape[0]
    n_lat = nb * seq
    assert nb + 1 <= SUBLANES and seq % 1024 == 0 and (nb * n_ctx) % 1024 == 0 and n_ctx % 256 == 0

    xs = jnp.concatenate([x.reshape(n_lat, d), ctx.reshape(nb * n_ctx, d)], axis=0)
    cpad = jnp.zeros((SUBLANES, d), F32).at[:nb].set(c).at[nb].set(c_ctx)
    mod = _modulation(cpad, w_mod, b_mod).reshape(depth * SUBLANES * 6, 1, d)

    lb_p = jax.nn.softmax(c_lb_raw.astype(F32), axis=0)
    lb_all = jnp.cumsum(lb_p, axis=0) - lb_p[0:1]

    wr_pad = jnp.zeros((d, LANES), F32).at[:, :N_EXPERTS].set(w_router)
    rb_col = router_bias.astype(F32).reshape(N_EXPERTS, 1)
    rb_row = jnp.zeros((1, LANES), F32).at[0, :N_EXPERTS].set(router_bias.astype(F32))
    w_main = H_AB * HEAD * 8

    out = None
    column_major = False
    for layer in range(depth):
        last = layer == depth - 1
        j = layer // 2
        want_cm = layer % 2 == 1 and j % 2 == 1
        needs_raster = layer % 2 == 0 or not want_cm
        if want_cm and not column_major:
            xs = _to_column_major(xs, n_lat, seq)
            column_major = True
        elif needs_raster and column_major:
            xs = jnp.concatenate([_from_column_major(xs[:n_lat], seq), xs[n_lat:]], axis=0)
            column_major = False
        if layer % 2 == 0:
            w_in = ab_w_in[j]
            gw = w_in[:, w_main:].reshape(d, 4, 2, H_AB)
            wg = jnp.zeros((2, d, LANES), F32).at[:, :, :4 * H_AB].set(
                jnp.transpose(gw, (2, 0, 1, 3)).reshape(2, d, 4 * H_AB)).astype(BF16)
            zeros = jnp.zeros((2, H_AB), F32)
            bias = jnp.concatenate([ab_i_bias[j], ab_f_bias[j], zeros, ab_dt_bias[j]], axis=1)
            alog = jnp.concatenate([zeros, zeros, zeros, ab_a_log[j]], axis=1)
            gpar = jnp.zeros((2, SUBLANES, LANES), F32).at[:, 0, :4 * H_AB].set(bias).at[:, 1, :4 * H_AB].set(alog)
            p, gates = _inproj(xs, norm_mix[layer], mod, layer, w_in[:, :w_main].astype(BF16), seq, nb, wg, gpar)
            ha_f, ha_b = _mlstm_scan(p, gates, seq, n_ctx, nb)
            qkv = _gdn_prep(p, ab_conv[j], seq, n_ctx, nb)
            ob_f, ob_b = _gdn_scan(qkv, gates, seq, n_ctx, nb)
            o_halves = (((ha_f, 0), (ha_b, 0), 0), ((ob_f, 0), (ob_b, 0), 0))
            gate_cols = (3, 7)
            hn_gain = jnp.concatenate([ab_norm_a[j], ab_norm_b[j]])
            w_out = ab_w_out[j]
        else:
            (p,) = _inproj(xs, norm_mix[layer], mod, layer, c_w_in[j].astype(BF16), seq, nb)
            oc_f, oc_b = _gla_scan(p, c_f_bias[j], lb_all[j], seq, n_ctx, nb)
            o_halves = (((oc_f, 0), (oc_b, 0), 0), ((oc_f, 0), (oc_b, 0), 1))
            gate_cols = (8, 9)
            hn_gain = c_norm[j]
            w_out = c_w_out[j]
        n_rows = n_lat if last else n_lat + nb * n_ctx
        xs, pos_rows, cnt, h_sorted = _outproj(xs, o_halves, p, gate_cols, hn_gain, w_out.astype(BF16), mod, layer,
                                               norm_ffn[layer], wr_pad, rb_col, seq, nb, n_rows, layer % 2 == 0)
        res = _moe(xs, h_sorted, pos_rows, cnt, wr_pad, rb_row, w1[layer].astype(BF16), w3[layer].astype(BF16),
                   w2[layer].astype(BF16), mod, layer, norm_final, seq, nb, n_rows, last)
        if last:
            out = _from_column_major(res, seq) if column_major else res
        else:
            xs = res
    return out.reshape(nb, seq, d)
```

```python
import functools

import numpy as np
import jax
import jax.numpy as jnp
from jax import lax
from jax.experimental import pallas as pl
from jax.experimental.pallas import tpu as pltpu

F32 = jnp.float32
BF16 = jnp.bfloat16

EPS = 1e-6
GRID_W = 64
HEAD = 128
H_AB = 4
H_C = 8
CONV_K = 5
N_EXPERTS = 16
N_GROUPS = 4
EXPERTS_PER_GROUP = N_EXPERTS // N_GROUPS
LANES = 128
SUBLANES = 8
VMEM_LIMIT = 56 * 1024 * 1024

TOK_TILE = 512
MOE_TILE = 512
ROW_DMA_UNROLL = 8
MLSTM_CHUNK = 128
GDN_CHUNK = 64
GDN_SCAN_CHUNKS = 4
GLA_CHUNK = 128
GDN_INV_BLOCK = 16


def _cparams(sem):
    return pltpu.CompilerParams(dimension_semantics=sem, vmem_limit_bytes=VMEM_LIMIT)


def _dot(a, b):
    return jnp.dot(a, b, preferred_element_type=F32)


def _dot_nt(a, b):
    return lax.dot_general(a, b, (((1,), (1,)), ((), ())), preferred_element_type=F32)


def _dot_tn(a, b):
    return lax.dot_general(a, b, (((0,), (0,)), ((), ())), preferred_element_type=F32)


def _split2(x):
    hi = x.astype(BF16)
    lo = (x - hi.astype(F32)).astype(BF16)
    return hi, lo


def _split3(x):
    hi = x.astype(BF16)
    r = x - hi.astype(F32)
    mid = r.astype(BF16)
    lo = (r - mid.astype(F32)).astype(BF16)
    return hi, mid, lo


def _dot_exact_lhs(m_bf16, x):
    hi, mid, lo = _split3(x)
    return _dot(m_bf16, hi) + _dot(m_bf16, mid) + _dot(m_bf16, lo)


def _dot3(a, b):
    ah, al = _split2(a)
    bh, bl = _split2(b)
    return _dot(ah, bh) + _dot(ah, bl) + _dot(al, bh)


def _dot_nt3(a, b):
    ah, al = _split2(a)
    bh, bl = _split2(b)
    return _dot_nt(ah, bh) + _dot_nt(ah, bl) + _dot_nt(al, bh)


def _sigmoid(z):
    return 1.0 / (1.0 + jnp.exp(-z))


def _sigmoid_pair(z):
    return _sigmoid(z), _sigmoid(-z)


def _silu(z):
    return z * _sigmoid(z)


def _mod_kernel(c_ref, w_ref, b_ref, o_ref):
    s = _silu(c_ref[...]).astype(BF16)
    o_ref[0] = _dot(s, w_ref[0].astype(BF16)) + b_ref[0]


def _modulation(cpad, w_mod, b_mod):
    depth, d, n = w_mod.shape
    tn = 1536
    return pl.pallas_call(
        _mod_kernel,
        grid=(depth, n // tn),
        in_specs=[pl.BlockSpec((SUBLANES, d), lambda l, j: (0, 0)),
                  pl.BlockSpec((1, d, tn), lambda l, j: (l, 0, j)),
                  pl.BlockSpec((1, 1, tn), lambda l, j: (l, 0, j))],
        out_specs=pl.BlockSpec((1, SUBLANES, tn), lambda l, j: (l, 0, j)),
        out_shape=jax.ShapeDtypeStruct((depth, SUBLANES, n), F32),
        compiler_params=_cparams(("parallel", "parallel")),
        name="modulation",
    )(cpad, w_mod, b_mod.reshape(depth, 1, n))


def _gate_activations(raw, par):
    y = raw + par[0:1, :]
    e = jnp.exp(-jnp.abs(y))
    sp = jnp.log1p(e)
    lsig = jnp.minimum(y, 0.0) - sp
    r = 1.0 / (1.0 + e)
    sig = jnp.where(y >= 0, r, e * r)
    gdec = -jnp.exp(par[1:2, :]) * (jnp.maximum(y, 0.0) + sp)
    lane = lax.broadcasted_iota(jnp.int32, raw.shape, 1)
    return jnp.where(lane < 4, y, jnp.where(lane < 8, lsig, jnp.where(lane < 12, sig, gdec)))


def _inproj_kernel(x_ref, gain_ref, sc_ref, sh_ref, w_ref, *rest, with_gates):
    if with_gates:
        wg_ref, gpar_ref, p_ref, g_ref, u_ref = rest
    else:
        p_ref, u_ref = rest

    @pl.when(pl.program_id(1) == 0)
    def _():
        x = x_ref[...]
        y = x * lax.rsqrt(jnp.mean(x * x, axis=-1, keepdims=True) + EPS) * gain_ref[...]
        u = (y * (1.0 + sc_ref[0]) + sh_ref[0]).astype(BF16)
        u_ref[...] = u
        if with_gates:
            for d in range(2):
                g_ref[d] = _gate_activations(_dot(u, wg_ref[d]), gpar_ref[d])

    p_ref[...] = _dot(u_ref[...], w_ref[...])


def _mod_row(i, tm, seq, nb):
    return jnp.minimum((i * tm) // seq, nb)


def _inproj(xs, gain, mod, layer, w_all, w_idx, p, seq, nb, wg=None, gpar=None):
    t, d = xs.shape
    tm, tn = TOK_TILE, 1024
    with_gates = wg is not None
    base = layer * SUBLANES * 6

    def mod_spec(which):
        return pl.BlockSpec((1, 1, d), lambda i, j: (base + _mod_row(i, tm, seq, nb) * 6 + which, 0, 0))

    in_specs = [pl.BlockSpec((tm, d), lambda i, j: (i, 0)),
                pl.BlockSpec((1, d), lambda i, j: (0, 0)),
                mod_spec(1), mod_spec(0),
                pl.BlockSpec((None, d, tn), lambda i, j: (w_idx, 0, j))]
    args = [xs, gain.reshape(1, d), mod, mod, w_all]
    out_specs = [pl.BlockSpec((tm, tn), lambda i, j: (i, j))]
    out_shape = [jax.ShapeDtypeStruct((t, p), F32)]
    if with_gates:
        in_specs += [pl.BlockSpec((2, d, LANES), lambda i, j: (0, 0, 0)),
                     pl.BlockSpec((2, SUBLANES, LANES), lambda i, j: (0, 0, 0))]
        args += [wg, gpar]
        out_specs.append(pl.BlockSpec((2, tm, LANES), lambda i, j: (0, i, 0)))
        out_shape.append(jax.ShapeDtypeStruct((2, t, LANES), F32))
    return pl.pallas_call(
        functools.partial(_inproj_kernel, with_gates=with_gates),
        grid=(t // tm, p // tn),
        in_specs=in_specs, out_specs=out_specs, out_shape=out_shape,
        scratch_shapes=[pltpu.VMEM((tm, d), BF16)],
        compiler_params=_cparams(("parallel", "arbitrary")),
        name="inproj",
    )(*args)


def _gdn_prep_kernel(*refs, blocks_per_seq, n_lat_blocks, scale):
    w_ref, o_ref = refs[9], refs[10]
    i = pl.program_id(0)
    tm = o_ref.shape[0]
    w = refs[1].shape[1]
    in_lat = i < n_lat_blocks
    first = jnp.logical_or(jnp.logical_not(in_lat), i % blocks_per_seq == 0)
    last = jnp.logical_or(jnp.logical_not(in_lat), i % blocks_per_seq == blocks_per_seq - 1)
    pad = (CONV_K - 1) // 2
    for part in range(3):
        prev_ref, x_ref, next_ref = refs[3 * part:3 * part + 3]
        prev = jnp.where(first, 0.0, prev_ref[...])
        nxt = jnp.where(last, 0.0, next_ref[...])
        xe = jnp.concatenate([prev, x_ref[...], nxt], axis=0)
        n = xe.shape[0]
        acc = None
        for j in range(CONV_K):
            shift = (pad - j) % n
            xr = xe if shift == 0 else pltpu.roll(xe, shift, 0)
            term = xr[SUBLANES:SUBLANES + tm] * w_ref[j:j + 1, part * w:(part + 1) * w]
            acc = term if acc is None else acc + term
        y = _silu(acc)
        for h in range(H_AB):
            t = y[:, h * HEAD:(h + 1) * HEAD]
            if part < 2:
                t = t * lax.rsqrt(jnp.sum(t * t, axis=-1, keepdims=True) + EPS)
            if part == 0:
                t = t * scale
            o_ref[:, part * w + h * HEAD:part * w + (h + 1) * HEAD] = t


def _gdn_prep(p, conv_w, seq, n_ctx, nb):
    t = p.shape[0]
    tm = 256
    w = H_AB * HEAD
    first_col = 4
    wpad = jnp.zeros((SUBLANES, 3 * w), F32).at[:CONV_K].set(conv_w)
    hb = tm // SUBLANES
    nblk8 = t // SUBLANES
    in_specs = []
    for c in range(3):
        in_specs += [pl.BlockSpec((SUBLANES, w), lambda i, c=c: (jnp.maximum(i * hb - 1, 0), first_col + c)),
                     pl.BlockSpec((tm, w), lambda i, c=c: (i, first_col + c)),
                     pl.BlockSpec((SUBLANES, w), lambda i, c=c: (jnp.minimum((i + 1) * hb, nblk8 - 1), first_col + c))]
    in_specs.append(pl.BlockSpec((SUBLANES, 3 * w), lambda i: (0, 0)))
    return pl.pallas_call(
        functools.partial(_gdn_prep_kernel, blocks_per_seq=seq // tm, n_lat_blocks=nb * seq // tm,
                          scale=HEAD ** -0.5),
        grid=(t // tm,),
        in_specs=in_specs,
        out_specs=pl.BlockSpec((tm, 3 * w), lambda i: (i, 0)),
        out_shape=jax.ShapeDtypeStruct((t, 3 * w), F32),
        compiler_params=_cparams(("parallel",)),
        name="gdn_prep",
    )(*([p] * 9), wpad)


def _tri_tables(c):
    idx = np.arange(c)
    fwd = (idx[None, :] <= idx[:, None]).astype(np.float32)
    return jnp.asarray(np.stack([fwd, fwd.T]))


def _to_cm_kernel(x3_ref, x2_ref, o_ref, *, n_lat_tiles):
    i = pl.program_id(0)
    rows = x3_ref.shape[0]

    @pl.when(i < n_lat_tiles)
    def _():
        for wl in range(SUBLANES):
            o_ref[wl * rows:(wl + 1) * rows, :] = x3_ref[:, wl, :]

    @pl.when(i >= n_lat_tiles)
    def _():
        o_ref[...] = x2_ref[...]


def _from_cm_kernel(x_ref, o_ref):
    rows = o_ref.shape[0]
    for wl in range(SUBLANES):
        o_ref[:, wl, :] = x_ref[wl * rows:(wl + 1) * rows, :]


def _to_column_major(xs, n_lat, seq):
    t, d = xs.shape
    rows = seq // GRID_W
    tile = rows * SUBLANES
    assert rows % SUBLANES == 0 and n_lat % tile == 0 and (t - n_lat) % tile == 0
    n_lat_tiles = n_lat // tile
    per_b = GRID_W // SUBLANES
    lat = lambda i: jnp.minimum(i, n_lat_tiles - 1)
    return pl.pallas_call(
        functools.partial(_to_cm_kernel, n_lat_tiles=n_lat_tiles),
        grid=(t // tile,),
        in_specs=[pl.BlockSpec((rows, SUBLANES, d), lambda i: (lat(i) // per_b, lat(i) % per_b, 0)),
                  pl.BlockSpec((tile, d), lambda i: (jnp.maximum(i, n_lat_tiles - 1), 0))],
        out_specs=pl.BlockSpec((tile, d), lambda i: (i, 0)),
        out_shape=jax.ShapeDtypeStruct((t, d), xs.dtype),
        compiler_params=_cparams(("parallel",)),
        name="to_column_major",
    )(xs.reshape(t // GRID_W, GRID_W, d), xs)


def _from_column_major(y, seq):
    n_lat, d = y.shape
    rows = seq // GRID_W
    tile = rows * SUBLANES
    per_b = GRID_W // SUBLANES
    out = pl.pallas_call(
        _from_cm_kernel,
        grid=(n_lat // tile,),
        in_specs=[pl.BlockSpec((tile, d), lambda i: (i, 0))],
        out_specs=pl.BlockSpec((rows, SUBLANES, d), lambda i: (i // per_b, i % per_b, 0)),
        out_shape=jax.ShapeDtypeStruct((n_lat // GRID_W, GRID_W, d), y.dtype),
        compiler_params=_cparams(("parallel",)),
        name="from_column_major",
    )(y)
    return out.reshape(n_lat, d)


def _mlstm_kernel(qf, kf, vf, qb, kb, vb, gf, gb, tri_ref, hf_ref, hb_ref, s_scr, m_scr, *, scale):
    @pl.when(pl.program_id(1) == 0)
    def _():
        s_scr[...] = jnp.zeros_like(s_scr)
        m_scr[...] = jnp.zeros_like(m_scr)

    c = gf.shape[1]
    ones_col = (lax.broadcasted_iota(jnp.int32, (c, HEAD), 1) == 0).astype(BF16)
    qkv = ((qf, kf, vf), (qb, kb, vb))
    h_out = (hf_ref, hb_ref)
    g = [gf[0], gb[0]]
    mask = [tri_ref[d] > 0.5 for d in range(2)]
    b = [_dot_exact_lhs(tri_ref[d].astype(BF16), g[d]) for d in range(2)]
    total = [jnp.sum(x, axis=0, keepdims=True) for x in g]
    g_t = [x.T for x in g]
    b_t = [x.T for x in b]
    chains = [(d, h) for d in range(2) for h in range(H_AB)]
    sl = [slice(h * HEAD, (h + 1) * HEAD) for d, h in chains]
    q = [qkv[d][0][:, sl[i]].astype(BF16) for i, (d, h) in enumerate(chains)]
    k = [qkv[d][1][:, sl[i]] * scale for i, (d, h) in enumerate(chains)]
    v_aug = [jnp.concatenate([qkv[d][2][:, sl[i]].astype(BF16), ones_col], axis=1) for i, (d, h) in enumerate(chains)]
    s = [s_scr[d, h] for d, h in chains]
    qk = [_dot_nt(q[i], k[i].astype(BF16)) for i in range(len(chains))]
    qs = [_dot(q[i], s[i].astype(BF16)) for i in range(len(chains))]
    w_intra, w_inter, m_ts, colvs, m_sts, tots = [], [], [], [], [], []
    for i, (d, h) in enumerate(chains):
        li_c, li_r = g[d][:, h:h + 1], g_t[d][h:h + 1, :]
        b_c, b_r = b[d][:, 4 + h:5 + h], b_t[d][4 + h:5 + h, :]
        tot = total[d][:, 4 + h:5 + h]
        m_st = m_scr[d, h][0:1, 0:1]
        d_log = jnp.where(mask[d], b_c - b_r + li_r, -jnp.inf)
        m_inter = b_c + m_st
        m_t = jnp.maximum(m_inter, jnp.max(d_log, axis=-1, keepdims=True))
        w_inter.append(jnp.exp(m_inter - m_t))
        w_intra.append((jnp.exp(d_log - m_t) * qk[i]).astype(BF16))
        m_ts.append(m_t); colvs.append(tot - b_c + li_c); m_sts.append(m_st); tots.append(tot)
    intra = [_dot(w_intra[i], v_aug[i]) for i in range(len(chains))]
    m_new = [jnp.maximum(tots[i] + m_sts[i], jnp.max(colvs[i], axis=0, keepdims=True)) for i in range(len(chains))]
    upd = [_dot_tn((k[i] * jnp.exp(colvs[i] - m_new[i])).astype(BF16), v_aug[i]) for i in range(len(chains))]
    for i, (d, h) in enumerate(chains):
        num = w_inter[i] * qs[i] + intra[i]
        den = num[:, HEAD:HEAD + 1]
        h_out[d][0, :, sl[i]] = num[:, :HEAD] / jnp.maximum(jnp.abs(den), jnp.exp(-m_ts[i]))
        s_scr[d, h] = jnp.exp(tots[i] + m_sts[i] - m_new[i]) * s[i] + upd[i]
        m_scr[d, h] = jnp.broadcast_to(m_new[i], (SUBLANES, LANES))


def _step_block(chunk, seq, n_ctx, nb):
    n_c, n_l = n_ctx // chunk, seq // chunk
    off = nb * n_l

    def fn(b, d, i):
        j = i - n_c
        ctx_blk = off + b * n_c + (i if d == 0 else n_c - 1 - i)
        lat_blk = b * n_l + (j if d == 0 else n_l - 1 - j)
        return jnp.where(i < n_c, ctx_blk, lat_blk)

    return n_c + n_l, fn


def _mlstm_scan(p, gates, seq, n_ctx, nb):
    t = p.shape[0]
    c = MLSTM_CHUNK
    w = H_AB * HEAD
    n_steps, blk = _step_block(c, seq, n_ctx, nb)
    in_specs = [pl.BlockSpec((c, w), lambda b, i, d=d, col=col: (blk(b, d, i), col)) for d in range(2) for col in range(3)]
    in_specs += [pl.BlockSpec((1, c, LANES), lambda b, i, d=d: (d, blk(b, d, i), 0)) for d in range(2)]
    in_specs.append(pl.BlockSpec((2, c, c), lambda b, i: (0, 0, 0)))
    return pl.pallas_call(
        functools.partial(_mlstm_kernel, scale=HEAD ** -0.5),
        grid=(nb, n_steps),
        in_specs=in_specs,
        out_specs=[pl.BlockSpec((1, c, w), lambda b, i, d=d: (0, blk(b, d, i), 0)) for d in range(2)],
        out_shape=[jax.ShapeDtypeStruct((1, t, w), F32)] * 2,
        scratch_shapes=[pltpu.VMEM((2, H_AB, HEAD, 2 * HEAD), F32), pltpu.VMEM((2, H_AB, SUBLANES, LANES), F32)],
        compiler_params=_cparams(("parallel", "arbitrary")),
        name="mlstm_scan",
    )(p, p, p, p, p, p, gates, gates, _tri_tables(c))


def _gdn_inv_masks(c):
    idx = np.arange(c)
    blk = lambda n: (idx[:, None] // n) == (idx[None, :] // n)
    levels = [blk(GDN_INV_BLOCK)]
    n = GDN_INV_BLOCK
    while n < c:
        levels.append(np.logical_and(blk(2 * n), np.logical_not(blk(n))))
        n *= 2
    levels = np.stack(levels).astype(np.float32)
    return jnp.asarray(np.tile(levels, (1, 1, H_AB)))


def _block_diag(x, n_blocks):
    blk = lax.broadcasted_iota(jnp.int32, x.shape, 1) // (x.shape[1] // n_blocks)
    return jnp.concatenate([jnp.where(blk == j, x, jnp.zeros_like(x)) for j in range(n_blocks)], axis=0)


def _heads_to_lanes(x, first, width):
    return jnp.concatenate([jnp.broadcast_to(x[:, first + h:first + h + 1], (x.shape[0], width))
                            for h in range(H_AB)], axis=1)


def _unit_tri_inverse(a_list, eye, inv_masks):
    prod = lambda xs, ys: [_dot(x.astype(BF16), _block_diag(y.astype(BF16), H_AB)) for x, y in zip(xs, ys)]
    pw = [a * inv_masks[0] for a in a_list]
    inv = [eye - p for p in pw]
    n = 2
    while n < GDN_INV_BLOCK:
        pw = prod(pw, pw)
        inv = [i + t for i, t in zip(inv, prod(inv, pw))]
        n *= 2
    for lvl in range(1, inv_masks.shape[0]):
        t = prod(prod(inv, [a * inv_masks[lvl] for a in a_list]), inv)
        inv = [i - x for i, x in zip(inv, t)]
    return inv


def _gdn_chunk_kernel(q_ref, k_ref, v_ref, g_ref, tri_ref, im_ref, u_ref, w_ref, qd_ref, kd_ref, att_ref):
    c = GDN_CHUNK
    n_chunks = q_ref.shape[0] // c
    inv_masks = im_ref[...]
    row = lax.broadcasted_iota(jnp.int32, (c, H_AB * c), 0)
    col = lax.broadcasted_iota(jnp.int32, (c, H_AB * c), 1) % c
    eye_b = row == col
    eye = eye_b.astype(F32)
    masks = [col <= row, col >= row]
    stricts = [col < row, col > row]
    tris = [tri_ref[d].astype(BF16) for d in range(2)]
    ones = jnp.ones((c, c), BF16)
    chunks = range(n_chunks)
    rows = [slice(ci * c, (ci + 1) * c) for ci in chunks]
    groups = [(ci, d) for ci in chunks for d in range(2)]
    k_hl = [_split2(k_ref[r, :]) for r in rows]
    kbd = [(_block_diag(hi, H_AB), _block_diag(lo, H_AB)) for hi, lo in k_hl]
    r_hi = [_dot_nt(jnp.concatenate([k_hl[ci][0], k_hl[ci][1], q_ref[rows[ci], :].astype(BF16)], axis=0), kbd[ci][0])
            for ci in chunks]
    r_lo = [_dot_nt(k_hl[ci][0], kbd[ci][1]) for ci in chunks]
    kk = [r_hi[ci][:c] + r_hi[ci][c:2 * c] + r_lo[ci] for ci in chunks]
    g = [g_ref[d, rows[ci], :] for ci, d in groups]
    gc = [_dot_exact_lhs(tris[d], g[i]) for i, (ci, d) in enumerate(groups)]
    gc_c = [_heads_to_lanes(x, 12, c) for x in gc]
    gc_r = [_dot_exact_lhs(ones, eye * x) for x in gc_c]
    decay = [jnp.exp(jnp.where(masks[d], gc_c[i] - gc_r[i], -jnp.inf)) for i, (ci, d) in enumerate(groups)]
    a = [jnp.where(stricts[d], _heads_to_lanes(g[i], 8, c) * kk[ci] * decay[i], 0.0)
         for i, (ci, d) in enumerate(groups)]
    inv = _unit_tri_inverse(a, eye, inv_masks)
    inv_hl = [jnp.concatenate(_split2(x), axis=0) for x in inv]
    gc_w = [_heads_to_lanes(x, 12, HEAD) for x in gc]
    eg = [jnp.exp(x) for x in gc_w]
    beta = [_heads_to_lanes(x, 8, HEAD) for x in g]
    res_u = [_dot(inv_hl[i], _block_diag((v_ref[rows[ci], :] * beta[i]).astype(BF16), H_AB))
             for i, (ci, d) in enumerate(groups)]
    res_w = [_dot(inv_hl[i], _block_diag((k_ref[rows[ci], :] * (beta[i] * eg[i])).astype(BF16), H_AB))
             for i, (ci, d) in enumerate(groups)]
    for i, (ci, d) in enumerate(groups):
        r = rows[ci]
        total = _heads_to_lanes(jnp.sum(g[i], axis=0, keepdims=True), 12, HEAD)
        u_ref[d, r, :] = (res_u[i][:c] + res_u[i][c:]).astype(BF16)
        w_ref[d, r, :] = (res_w[i][:c] + res_w[i][c:]).astype(BF16)
        qd_ref[d, r, :] = (q_ref[r, :] * eg[i]).astype(BF16)
        kd_ref[d, r, :] = (k_ref[r, :] * jnp.exp(total - gc_w[i])).astype(BF16)
        att_ref[d, r, :] = (r_hi[ci][2 * c:] * decay[i]).astype(BF16)


def _gdn_scan_kernel(*refs):
    ins, o_refs, s_scr = refs[:12], refs[12:14], refs[14]
    c = GDN_CHUNK

    @pl.when(pl.program_id(1) == 0)
    def _():
        s_scr[...] = jnp.zeros_like(s_scr)

    pw = 2 * HEAD
    rblk = lax.broadcasted_iota(jnp.int32, (pw, pw), 0) // HEAD
    cblk = lax.broadcasted_iota(jnp.int32, (pw, pw), 1) // HEAD
    on_diag = rblk == cblk
    chains = [(d, pr) for d in range(2) for pr in range(H_AB // 2)]
    refs_of = lambda d: ins[d::2]
    sl = [slice(pr * pw, (pr + 1) * pw) for d, pr in chains]
    s = [s_scr[d, pr] for d, pr in chains]
    n_sub = o_refs[0].shape[1] // c
    for k in range(n_sub):
        rows = [slice(kk * c, (kk + 1) * c) for kk in (k, n_sub - 1 - k)]
        res = [_dot(jnp.concatenate([refs_of(d)[1][0, rows[d], sl[i]], refs_of(d)[2][0, rows[d], sl[i]]], axis=0),
                    s[i].astype(BF16)) for i, (d, pr) in enumerate(chains)]
        v_new = [(refs_of(d)[0][0, rows[d], sl[i]].astype(F32) - res[i][:c]).astype(BF16)
                 for i, (d, pr) in enumerate(chains)]
        intra = [_dot(refs_of(d)[4][0, rows[d], pr * 2 * c:(pr + 1) * 2 * c], _block_diag(v_new[i], 2))
                 for i, (d, pr) in enumerate(chains)]
        upd = [_dot_tn(refs_of(d)[3][0, rows[d], sl[i]], v_new[i]) for i, (d, pr) in enumerate(chains)]
        totals = [jnp.sum(refs_of(d)[5][0, rows[d], :], axis=0, keepdims=True) for d in range(2)]
        for i, (d, pr) in enumerate(chains):
            o_refs[d][0, rows[d], sl[i]] = res[i][c:] + intra[i]
            gl = jnp.concatenate([jnp.broadcast_to(jnp.exp(totals[d][:, 12 + 2 * pr + j:13 + 2 * pr + j]), (1, HEAD))
                                  for j in range(2)], axis=1)
            s[i] = gl * s[i] + jnp.where(on_diag, upd[i], 0.0)
    for i, (d, pr) in enumerate(chains):
        s_scr[d, pr] = s[i]


def _gdn_scan(qkv, gates, seq, n_ctx, nb):
    t = qkv.shape[0]
    c = GDN_CHUNK
    w = H_AB * HEAD
    tri = _tri_tables(c)
    im = _gdn_inv_masks(c)
    tm = TOK_TILE
    sds = lambda width: jax.ShapeDtypeStruct((2, t, width), BF16)
    u, wv, qd, kd, att = pl.pallas_call(
        _gdn_chunk_kernel,
        grid=(t // tm,),
        in_specs=[pl.BlockSpec((tm, w), lambda i, col=col: (i, col)) for col in range(3)] + [
            pl.BlockSpec((2, tm, LANES), lambda i: (0, i, 0)),
            pl.BlockSpec((2, c, c), lambda i: (0, 0, 0)),
            pl.BlockSpec(im.shape, lambda i: (0, 0, 0))],
        out_specs=[pl.BlockSpec((2, tm, w), lambda i: (0, i, 0))] * 4
        + [pl.BlockSpec((2, tm, H_AB * c), lambda i: (0, i, 0))],
        out_shape=[sds(w)] * 4 + [sds(H_AB * c)],
        compiler_params=_cparams(("parallel",)),
        name="gdn_chunk",
    )(qkv, qkv, qkv, gates, tri, im)

    cs = c * GDN_SCAN_CHUNKS
    n_steps, blk = _step_block(cs, seq, n_ctx, nb)
    in_specs, args = [], []
    for arr, width in ((u, w), (wv, w), (qd, w), (kd, w), (att, H_AB * c), (gates, LANES)):
        for d in range(2):
            in_specs.append(pl.BlockSpec((1, cs, width), lambda b, i, d=d: (d, blk(b, d, i), 0)))
            args.append(arr)
    o_f, o_b = pl.pallas_call(
        _gdn_scan_kernel,
        grid=(nb, n_steps),
        in_specs=in_specs,
        out_specs=[pl.BlockSpec((1, cs, w), lambda b, i, d=d: (0, blk(b, d, i), 0)) for d in range(2)],
        out_shape=[jax.ShapeDtypeStruct((1, t, w), F32)] * 2,
        scratch_shapes=[pltpu.VMEM((2, H_AB // 2, 2 * HEAD, 2 * HEAD), F32)],
        compiler_params=_cparams(("parallel", "arbitrary")),
        name="gdn_scan",
    )(*args)
    return o_f, o_b


def _gla_tables(c):
    idx = np.arange(c)
    sel, sgn, msk = [], [], []
    for d in range(2):
        sel_d, sgn_d, msk_d = [], [], []
        bs = c // 2
        while bs >= 1:
            pair = idx // (2 * bs)
            second = (idx // bs) % 2 == 1
            ref_row = pair * 2 * bs + (bs - 1 if d == 0 else bs)
            sel_d.append((idx[None, :] == ref_row[:, None]).astype(np.float32))
            qside = second if d == 0 else np.logical_not(second)
            sgn_d.append(np.where(qside, 1.0, -1.0)[:, None] * np.ones((1, HEAD)))
            same_pair = pair[:, None] == pair[None, :]
            msk_d.append(np.logical_and(same_pair, np.logical_and(qside[:, None], np.logical_not(qside)[None, :])))
            bs //= 2
        msk_d.append(np.eye(c, dtype=bool))
        sel.append(np.concatenate(sel_d, axis=0))
        sgn.append(np.stack(sgn_d))
        msk.append(np.stack(msk_d).astype(np.float32))
    return (jnp.asarray(np.stack(sel), dtype=BF16), jnp.asarray(np.stack(sgn), dtype=F32),
            jnp.asarray(np.stack(msk), dtype=F32))


def _gla_kernel(qf_ref, zf_ref, vf_ref, qb_ref, zb_ref, vb_ref, fb_ref, lb_ref, tri_ref, sel_ref, sgn_ref, msk_ref,
                of_ref, ob_ref, s_scr):
    @pl.when(pl.program_id(1) == 0)
    def _():
        s_scr[...] = jnp.zeros_like(s_scr)

    c = qf_ref.shape[0]
    n_lvl = sgn_ref.shape[1]
    dirs = range(2)
    qzv = ((qf_ref, zf_ref, vf_ref), (qb_ref, zb_ref, vb_ref))
    o_refs = (of_ref, ob_ref)
    chains = [(d, h) for d in dirs for h in range(H_C)]
    sl = [slice(h * HEAD, (h + 1) * HEAD) for d, h in chains]
    g_all, key_all, q_all = [], [], []
    for d in dirs:
        lb = lb_ref[d]
        sig_pos, sig_neg = _sigmoid_pair(qzv[d][1][...] + fb_ref[d])
        g_all.append(jnp.log(lb + (1.0 - lb) * sig_pos))
        key_all.append((1.0 - lb) * sig_neg)
        q_all.append(_silu(qzv[d][0][...]))
    gc_all = [_dot_exact_lhs(tri_ref[d].astype(BF16), g_all[d]) for d in dirs]
    total_all = [jnp.sum(g, axis=0, keepdims=True) for g in g_all]
    refs = [_dot(sel_ref[d], gc_all[d].astype(BF16)) for d in dirs]
    diag = msk_ref[0, n_lvl] > 0.5
    att = [jnp.where(diag, _dot_nt(q_all[d][:, sl[i]].astype(BF16), key_all[d][:, sl[i]].astype(BF16)), 0.0)
           for i, (d, h) in enumerate(chains)]
    for l in range(n_lvl):
        sgn = [sgn_ref[d, l] for d in dirs]
        x = [(jnp.where(sgn[d] > 0, q_all[d][:, sl[i]], key_all[d][:, sl[i]])
              * jnp.exp(sgn[d] * (gc_all[d][:, sl[i]] - refs[d][l * c:(l + 1) * c, sl[i]]))).astype(BF16)
             for i, (d, h) in enumerate(chains)]
        prod = [_dot_nt(xh, xh) for xh in x]
        in_level = [msk_ref[d, l] > 0.5 for d in dirs]
        att = [jnp.where(in_level[d], prod[i], att[i]) for i, (d, h) in enumerate(chains)]
    st = [s_scr[d, h] for d, h in chains]
    inter = [_dot_nt((q_all[d][:, sl[i]] * jnp.exp(gc_all[d][:, sl[i]])).astype(BF16), st[i].astype(BF16))
             for i, (d, h) in enumerate(chains)]
    intra = [_dot(att[i].astype(BF16), qzv[d][2][:, sl[i]].astype(BF16)) for i, (d, h) in enumerate(chains)]
    upd = [_dot_tn(qzv[d][2][:, sl[i]].astype(BF16),
                   (key_all[d][:, sl[i]] * jnp.exp(total_all[d][:, sl[i]] - gc_all[d][:, sl[i]])).astype(BF16))
           for i, (d, h) in enumerate(chains)]
    for i, (d, h) in enumerate(chains):
        o_refs[d][0, :, sl[i]] = inter[i] + intra[i]
        s_scr[d, h] = jnp.exp(total_all[d][:, sl[i]]) * st[i] + upd[i]


def _gla_scan(p, f_bias, lb, seq, n_ctx, nb):
    t, pw = p.shape
    c = GLA_CHUNK
    w = H_C * HEAD
    tri = _tri_tables(c)
    sel, sgn, msk = _gla_tables(c)
    n_steps, blk = _step_block(c, seq, n_ctx, nb)
    dspec = lambda d, col: pl.BlockSpec((c, w), lambda b, i: (blk(b, d, i), col))
    whole = lambda a: pl.BlockSpec(a.shape, lambda b, i: (0,) * a.ndim)
    fb3, lb3 = f_bias.reshape(2, 1, w), lb.reshape(2, 1, w)
    consts = (fb3, lb3, tri, sel, sgn, msk)
    return pl.pallas_call(
        _gla_kernel,
        grid=(nb, n_steps),
        in_specs=[dspec(0, 0), dspec(0, 1), dspec(0, 3), dspec(1, 0), dspec(1, 2), dspec(1, 3)]
        + [whole(a) for a in consts],
        out_specs=[pl.BlockSpec((1, c, w), lambda b, i, d=d: (0, blk(b, d, i), 0)) for d in range(2)],
        out_shape=[jax.ShapeDtypeStruct((1, t, w), F32)] * 2,
        scratch_shapes=[pltpu.VMEM((2, H_C, HEAD, HEAD), F32)],
        compiler_params=_cparams(("parallel", "arbitrary")),
        name="gla_scan",
    )(p, p, p, p, p, p, *consts)


def _best_group(scores_t, bias):
    sel = [scores_t[e:e + 1, :] + bias[e:e + 1, :] for e in range(N_EXPERTS)]
    gscore = []
    for grp in range(N_GROUPS):
        v = sel[grp * EXPERTS_PER_GROUP:(grp + 1) * EXPERTS_PER_GROUP]
        best = None
        for a in range(EXPERTS_PER_GROUP):
            for b in range(a + 1, EXPERTS_PER_GROUP):
                pair = v[a] + v[b]
                best = pair if best is None else jnp.maximum(best, pair)
        gscore.append(best)
    best_g = jnp.zeros(gscore[0].shape, jnp.int32)
    best_v = gscore[0]
    for grp in range(1, N_GROUPS):
        better = gscore[grp] > best_v
        best_g = jnp.where(better, grp, best_g)
        best_v = jnp.where(better, gscore[grp], best_v)
    return best_g


def _rows_copy(src, dst, sem, src_row, dst_row):
    return pltpu.make_async_copy(src.at[pl.ds(src_row, 1), :], dst.at[pl.ds(dst_row, 1), :], sem)


def _outproj_kernel(x_ref, of0, ob0, gt0, of1, ob1, gt1, hn_ref, w_ref, g1_ref, sc_ref, sh_ref, nf_ref,
                    wr_ref, rb_ref, triu_ref, xo_ref, pos_ref, cnt_ref, hs_ref,
                    base_scr, h_scr, pos_v, pos_s, cnt_v, cnt_s, row_sem, meta_sem, *, first_half_sigmoid, cap):
    i = pl.program_id(0)
    n_steps = pl.num_programs(0)
    tm = x_ref.shape[0]
    slot = i % 2
    prev = 1 - slot
    dump = N_GROUPS * cap

    @pl.when(i == 0)
    def _():
        base_scr[...] = jnp.zeros_like(base_scr)
        h_scr[...] = jnp.zeros_like(h_scr)

        def init(r, carry):
            pos_s[1, r] = dump + r
            return carry
        lax.fori_loop(0, tm, init, 0)

    for r in range(tm):
        _rows_copy(h_scr.at[prev], hs_ref, row_sem, r, pos_s[prev, r]).start()

    feats = []
    for half, (of, ob, gt) in enumerate(((of0, ob0, gt0), (of1, ob1, gt1))):
        o = of[0] + ob[0]
        gate = gt[...]
        for h in range(o.shape[1] // HEAD):
            sl = slice(h * HEAD, (h + 1) * HEAD)
            t = o[:, sl]
            y = t * lax.rsqrt(jnp.mean(t * t, axis=-1, keepdims=True) + EPS)
            y = y * hn_ref[:, half * o.shape[1] + h * HEAD: half * o.shape[1] + (h + 1) * HEAD]
            gz = gate[:, sl]
            act = _sigmoid(gz) if (half == 0 and first_half_sigmoid) else _silu(gz)
            feats.append((y * act).astype(BF16))
    feats = jnp.concatenate(feats, axis=1)
    xn = x_ref[...] + g1_ref[0] * _dot(feats, w_ref[...])
    xo_ref[...] = xn
    y = xn * lax.rsqrt(jnp.mean(xn * xn, axis=-1, keepdims=True) + EPS) * nf_ref[...]
    hl = y * (1.0 + sc_ref[0]) + sh_ref[0]
    logits = _dot3(hl, wr_ref[...])
    scores_t = _sigmoid(logits).T
    best_g = _best_group(scores_t, rb_ref[...])
    onehot = jnp.concatenate([(best_g == g).astype(F32) for g in range(N_GROUPS)]
                             + [jnp.zeros((SUBLANES - N_GROUPS, tm), F32)], axis=0)
    before = _dot(onehot.astype(BF16), triu_ref[...])
    base = base_scr[...]
    rank = jnp.sum(onehot * (before + base[:, 0:1]), axis=0, keepdims=True)
    pos = best_g * cap + rank.astype(jnp.int32)
    pos_ref[0] = pos
    base = base + jnp.sum(onehot, axis=1, keepdims=True)
    base_scr[...] = base
    cnt_ref[...] = base

    pltpu.make_async_copy(h_scr.at[prev], hs_ref.at[pl.ds(0, tm), :], row_sem).wait()
    h_scr[slot] = hl
    pos_v[...] = pos
    to_smem = pltpu.make_async_copy(pos_v, pos_s.at[pl.ds(slot, 1), :], meta_sem)
    to_smem.start()
    to_smem.wait()

    @pl.when(i == n_steps - 1)
    def _():
        def issue(r, carry):
            _rows_copy(h_scr.at[slot], hs_ref, row_sem, r, pos_s[slot, r]).start()
            return carry
        lax.fori_loop(0, tm, issue, 0, unroll=ROW_DMA_UNROLL)
        pltpu.make_async_copy(h_scr.at[slot], hs_ref.at[pl.ds(0, tm), :], row_sem).wait()
        cnt_v[...] = base.astype(jnp.int32)
        counts = pltpu.make_async_copy(cnt_v, cnt_s, meta_sem)
        counts.start()
        counts.wait()
        h_scr[prev] = jnp.zeros((tm, h_scr.shape[2]), F32)
        zero = h_scr.at[prev]

        def pad_copies(action):
            for g in range(N_GROUPS):
                cnt = cnt_s[g, 0]
                up = (cnt + SUBLANES - 1) // SUBLANES * SUBLANES
                for r in range(SUBLANES - 1):
                    @pl.when(cnt + r < up)
                    def _():
                        action(_rows_copy(zero, hs_ref, row_sem, 0, g * cap + cnt + r))
                action(pltpu.make_async_copy(zero, hs_ref.at[pl.ds(pl.multiple_of(g * cap + up, SUBLANES), tm), :],
                                             row_sem))

        pad_copies(lambda cp: cp.start())
        pad_copies(lambda cp: cp.wait())


def _outproj(xs, o_halves, gate_src, gate_cols, hn_gain, w_out, w_idx, mod, layer, nf_gain, wr_pad, rb_col,
             seq, nb, n_rows, first_half_sigmoid):
    t, d = xs.shape
    tm = TOK_TILE
    half = d // 2
    base = layer * SUBLANES * 6

    def mod_spec(which):
        return pl.BlockSpec((1, 1, d), lambda i: (base + _mod_row(i, tm, seq, nb) * 6 + which, 0, 0))

    in_specs = [pl.BlockSpec((tm, d), lambda i: (i, 0))]
    args = [xs]
    for ((arr_f, dir_f), (arr_b, dir_b), col), gcol in zip(o_halves, gate_cols):
        in_specs += [pl.BlockSpec((1, tm, half), lambda i, col=col, dd=dir_f: (dd, i, col)),
                     pl.BlockSpec((1, tm, half), lambda i, col=col, dd=dir_b: (dd, i, col)),
                     pl.BlockSpec((tm, half), lambda i, gcol=gcol: (i, gcol))]
        args += [arr_f, arr_b, gate_src]
    in_specs += [pl.BlockSpec((1, d), lambda i: (0, 0)),
                 pl.BlockSpec((None, d, d), lambda i: (w_idx, 0, 0)),
                 mod_spec(2), mod_spec(4), mod_spec(3),
                 pl.BlockSpec((1, d), lambda i: (0, 0)),
                 pl.BlockSpec((d, LANES), lambda i: (0, 0)),
                 pl.BlockSpec((N_EXPERTS, 1), lambda i: (0, 0)),
                 pl.BlockSpec((tm, tm), lambda i: (0, 0))]
    idx = np.arange(tm)
    triu = jnp.asarray(idx[:, None] < idx[None, :], dtype=BF16)
    args += [hn_gain.reshape(1, d), w_out, mod, mod, mod, nf_gain.reshape(1, d), wr_pad, rb_col, triu]
    n_tiles = n_rows // tm
    assert tm == MOE_TILE
    cap = n_rows + MOE_TILE
    return pl.pallas_call(
        functools.partial(_outproj_kernel, first_half_sigmoid=first_half_sigmoid, cap=cap),
        grid=(n_tiles,),
        in_specs=in_specs,
        out_specs=[pl.BlockSpec((tm, d), lambda i: (i, 0)),
                   pl.BlockSpec((1, 1, tm), lambda i: (i, 0, 0)),
                   pl.BlockSpec((SUBLANES, LANES), lambda i: (0, 0)),
                   pl.BlockSpec(memory_space=pl.ANY)],
        out_shape=[jax.ShapeDtypeStruct((t, d), F32), jax.ShapeDtypeStruct((n_tiles, 1, tm), jnp.int32),
                   jax.ShapeDtypeStruct((SUBLANES, LANES), F32),
                   jax.ShapeDtypeStruct((N_GROUPS * cap + tm, d), F32)],
        scratch_shapes=[pltpu.VMEM((SUBLANES, LANES), F32), pltpu.VMEM((2, tm, d), F32),
                        pltpu.VMEM((1, tm), jnp.int32), pltpu.SMEM((2, tm), jnp.int32),
                        pltpu.VMEM((SUBLANES, LANES), jnp.int32), pltpu.SMEM((SUBLANES, LANES), jnp.int32),
                        pltpu.SemaphoreType.DMA(()), pltpu.SemaphoreType.DMA(())],
        compiler_params=_cparams(("arbitrary",)),
        name="outproj",
    )(*args)


def _group_gates(x, wr, bias_row, grp):
    scores = _sigmoid(_dot3(x, wr))
    lane_i = lax.broadcasted_iota(jnp.int32, scores.shape, 1)
    lane = lane_i.astype(F32)
    m = jnp.where(lane_i // EXPERTS_PER_GROUP == grp, scores + bias_row, -jnp.inf)
    picks = []
    for _ in range(2):
        top = jnp.max(m, axis=-1, keepdims=True)
        idx = jnp.min(jnp.where(m == top, lane, float(LANES)), axis=-1, keepdims=True)
        picks.append(idx)
        m = jnp.where(lane == idx, -jnp.inf, m)
    w = [jnp.sum(jnp.where(lane == idx, scores, 0.0), axis=-1, keepdims=True) for idx in picks]
    tot = w[0] + w[1]
    return jnp.where(lane == picks[0], w[0] / tot, jnp.where(lane == picks[1], w[1] / tot, 0.0))


def _group_ffn_kernel(tg_ref, tb_ref, nv_ref, x_ref, wr_ref, rb_ref, w1_ref, w3_ref, w2_ref, o_ref):
    i = pl.program_id(0)

    @pl.when(i < nv_ref[0])
    def _():
        grp = tg_ref[i]
        x = x_ref[...]
        xb = x.astype(BF16)
        gate = _group_gates(x, wr_ref[...], rb_ref[...], grp)
        lane = lax.broadcasted_iota(jnp.int32, gate.shape, 1)
        acc = None
        for j in range(EXPERTS_PER_GROUP):
            gcol = jnp.sum(jnp.where(lane == grp * EXPERTS_PER_GROUP + j, gate, 0.0), axis=-1, keepdims=True)
            a = _silu(_dot(xb, w1_ref[j])) * _dot(xb, w3_ref[j]) * gcol
            y = _dot(a.astype(BF16), w2_ref[j])
            acc = y if acc is None else acc + y
        o_ref[...] = acc


def _group_ffn(h_sorted, tile_grp, tile_blk, n_valid, wr_pad, rb_row, w1, w3, w2, cap, layer):
    p_rows, d = h_sorted.shape
    tm = MOE_TILE
    f = w1.shape[2]
    n_tiles = tile_grp.shape[0]
    blocks_per_group = cap // tm
    row_blk = lambda i, tg, tb, nv: (tg[i] * blocks_per_group + tb[i], 0)
    experts = lambda i, tg, tb, nv: (layer * N_GROUPS + tg[i], 0, 0)
    w_in_spec = pl.BlockSpec((EXPERTS_PER_GROUP, d, f), experts)
    return pl.pallas_call(
        _group_ffn_kernel,
        grid_spec=pltpu.PrefetchScalarGridSpec(
            num_scalar_prefetch=3, grid=(n_tiles,),
            in_specs=[pl.BlockSpec((tm, d), row_blk),
                      pl.BlockSpec((d, LANES), lambda i, tg, tb, nv: (0, 0)),
                      pl.BlockSpec((1, LANES), lambda i, tg, tb, nv: (0, 0)),
                      w_in_spec, w_in_spec, pl.BlockSpec((EXPERTS_PER_GROUP, f, d), experts)],
            out_specs=pl.BlockSpec((tm, d), row_blk)),
        out_shape=jax.ShapeDtypeStruct((p_rows, d), F32),
        compiler_params=_cparams(("arbitrary",)),
        name="moe_group_ffn",
    )(tile_grp, tile_blk, n_valid, h_sorted, wr_pad, rb_row, w1, w3, w2)


def _combine_kernel(pos_ref, x_ref, y_ref, g2_ref, nfin_ref, o_ref, buf, sem, *, final_norm):
    i = pl.program_id(0)
    tm = x_ref.shape[0]
    slot = i % 2

    def gather(tile, slot_):
        def issue(r, carry):
            _rows_copy(y_ref, buf.at[slot_], sem.at[slot_], pos_ref[tile * tm + r], r).start()
            return carry
        lax.fori_loop(0, tm, issue, 0, unroll=ROW_DMA_UNROLL)

    @pl.when(i == 0)
    def _():
        gather(0, 0)

    @pl.when(i + 1 < pl.num_programs(0))
    def _():
        gather(i + 1, 1 - slot)

    pltpu.make_async_copy(y_ref.at[pl.ds(0, tm), :], buf.at[slot], sem.at[slot]).wait()
    y = x_ref[...] + g2_ref[0] * buf[slot]
    if final_norm:
        y = y * lax.rsqrt(jnp.mean(y * y, axis=-1, keepdims=True) + EPS) * nfin_ref[...]
    o_ref[...] = y


def _combine(xs, y_sorted, pos, mod, layer, nfin, seq, nb, n_rows, final_norm):
    t, d = xs.shape
    tm = TOK_TILE
    base = layer * SUBLANES * 6
    out_rows = n_rows if final_norm else t
    return pl.pallas_call(
        functools.partial(_combine_kernel, final_norm=final_norm),
        grid_spec=pltpu.PrefetchScalarGridSpec(
            num_scalar_prefetch=1, grid=(n_rows // tm,),
            in_specs=[pl.BlockSpec((tm, d), lambda i, pos: (i, 0)),
                      pl.BlockSpec(memory_space=pl.ANY),
                      pl.BlockSpec((1, 1, d), lambda i, pos: (base + _mod_row(i, tm, seq, nb) * 6 + 5, 0, 0)),
                      pl.BlockSpec((1, d), lambda i, pos: (0, 0))],
            out_specs=pl.BlockSpec((tm, d), lambda i, pos: (i, 0)),
            scratch_shapes=[pltpu.VMEM((2, tm, d), F32), pltpu.SemaphoreType.DMA((2,))]),
        out_shape=jax.ShapeDtypeStruct((out_rows, d), F32),
        compiler_params=_cparams(("arbitrary",)),
        name="moe_combine",
    )(pos, xs, y_sorted, mod, nfin.reshape(1, d))


def _moe(xs, h_sorted, pos_rows, cnt, wr_pad, rb_row, w1, w3, w2, mod, layer, nfin, seq, nb, n_rows, final_norm):
    tm = MOE_TILE
    cap = n_rows + tm
    pos = pos_rows.reshape(-1)
    counts = cnt[:N_GROUPS, 0].astype(jnp.int32)
    tiles = (counts + tm - 1) // tm
    ends = jnp.cumsum(tiles)
    n_tiles = n_rows // tm + N_GROUPS
    step = jnp.minimum(jnp.arange(n_tiles, dtype=jnp.int32), ends[-1] - 1)
    tile_grp = jnp.sum(step[:, None] >= ends[None, :], axis=1).astype(jnp.int32)
    tile_blk = step - (ends - tiles)[tile_grp]
    y_sorted = _group_ffn(h_sorted, tile_grp, tile_blk, ends[-1:].astype(jnp.int32), wr_pad, rb_row, w1, w3, w2, cap,
                          layer)
    return _combine(xs, y_sorted, pos, mod, layer, nfin, seq, nb, n_rows, final_norm)


def kernel(x, c, ctx, c_ctx, w_mod, b_mod, norm_mix, norm_ffn, norm_final, ab_w_in, ab_i_bias, ab_f_bias,
           ab_conv, ab_a_log, ab_dt_bias, ab_norm_a, ab_norm_b, ab_w_out, c_w_in, c_f_bias, c_lb_raw, c_norm,
           c_w_out, w_router, router_bias, w1, w3, w2):
    nb, seq, d = x.shape
    n_ctx = ctx.shape[1]
    depth = w_mod.shape[0]
    n_lat = nb * seq
    assert nb + 1 <= SUBLANES and seq % 1024 == 0 and (nb * n_ctx) % 1024 == 0 and n_ctx % 256 == 0

    xs = jnp.concatenate([x.reshape(n_lat, d), ctx.reshape(nb * n_ctx, d)], axis=0)
    cpad = jnp.zeros((SUBLANES, d), F32).at[:nb].set(c).at[nb].set(c_ctx)
    mod = _modulation(cpad, w_mod, b_mod).reshape(depth * SUBLANES * 6, 1, d)

    lb_p = jax.nn.softmax(c_lb_raw.astype(F32), axis=0)
    lb_all = jnp.cumsum(lb_p, axis=0) - lb_p[0:1]

    wr_pad = jnp.zeros((d, LANES), F32).at[:, :N_EXPERTS].set(w_router)
    rb_col = router_bias.astype(F32).reshape(N_EXPERTS, 1)
    rb_row = jnp.zeros((1, LANES), F32).at[0, :N_EXPERTS].set(router_bias.astype(F32))
    w_main = H_AB * HEAD * 8
    ab_w_in_b, c_w_in_b = ab_w_in.astype(BF16), c_w_in.astype(BF16)
    ab_w_out_b, c_w_out_b = ab_w_out.astype(BF16), c_w_out.astype(BF16)
    n_exp, _, d_ff = w1.shape[1:]
    w1_b = w1.astype(BF16).reshape(depth * n_exp, d, d_ff)
    w3_b = w3.astype(BF16).reshape(depth * n_exp, d, d_ff)
    w2_b = w2.astype(BF16).reshape(depth * n_exp, d_ff, d)

    out = None
    column_major = False
    for layer in range(depth):
        last = layer == depth - 1
        j = layer // 2
        want_cm = layer % 2 == 1 and j % 2 == 1
        needs_raster = layer % 2 == 0 or not want_cm
        if want_cm and not column_major:
            xs = _to_column_major(xs, n_lat, seq)
            column_major = True
        elif needs_raster and column_major:
            xs = jnp.concatenate([_from_column_major(xs[:n_lat], seq), xs[n_lat:]], axis=0)
            column_major = False
        if layer % 2 == 0:
            w_in = ab_w_in[j]
            gw = w_in[:, w_main:].reshape(d, 4, 2, H_AB)
            wg = jnp.zeros((2, d, LANES), F32).at[:, :, :4 * H_AB].set(
                jnp.transpose(gw, (2, 0, 1, 3)).reshape(2, d, 4 * H_AB)).astype(BF16)
            zeros = jnp.zeros((2, H_AB), F32)
            bias = jnp.concatenate([ab_i_bias[j], ab_f_bias[j], zeros, ab_dt_bias[j]], axis=1)
            alog = jnp.concatenate([zeros, zeros, zeros, ab_a_log[j]], axis=1)
            gpar = jnp.zeros((2, SUBLANES, LANES), F32).at[:, 0, :4 * H_AB].set(bias).at[:, 1, :4 * H_AB].set(alog)
            p, gates = _inproj(xs, norm_mix[layer], mod, layer, ab_w_in_b, j, w_main, seq, nb, wg, gpar)
            ha_f, ha_b = _mlstm_scan(p, gates, seq, n_ctx, nb)
            qkv = _gdn_prep(p, ab_conv[j], seq, n_ctx, nb)
            ob_f, ob_b = _gdn_scan(qkv, gates, seq, n_ctx, nb)
            o_halves = (((ha_f, 0), (ha_b, 0), 0), ((ob_f, 0), (ob_b, 0), 0))
            gate_cols = (3, 7)
            hn_gain = jnp.concatenate([ab_norm_a[j], ab_norm_b[j]])
            w_out = ab_w_out_b
        else:
            (p,) = _inproj(xs, norm_mix[layer], mod, layer, c_w_in_b, j, c_w_in.shape[2], seq, nb)
            oc_f, oc_b = _gla_scan(p, c_f_bias[j], lb_all[j], seq, n_ctx, nb)
            o_halves = (((oc_f, 0), (oc_b, 0), 0), ((oc_f, 0), (oc_b, 0), 1))
            gate_cols = (8, 9)
            hn_gain = c_norm[j]
            w_out = c_w_out_b
        n_rows = n_lat if last else n_lat + nb * n_ctx
        xs, pos_rows, cnt, h_sorted = _outproj(xs, o_halves, p, gate_cols, hn_gain, w_out, j, mod, layer,
                                               norm_ffn[layer], wr_pad, rb_col, seq, nb, n_rows, layer % 2 == 0)
        res = _moe(xs, h_sorted, pos_rows, cnt, wr_pad, rb_row, w1_b, w3_b, w2_b, mod, layer, norm_final, seq, nb,
                   n_rows, last)
        if last:
            out = _from_column_major(res, seq) if column_major else res
        else:
            xs = res
    return out.reshape(nb, seq, d)
```

```python
import functools

import numpy as np
import jax
import jax.numpy as jnp
from jax import lax
from jax.experimental import pallas as pl
from jax.experimental.pallas import tpu as pltpu

F32 = jnp.float32
BF16 = jnp.bfloat16

EPS = 1e-6
GRID_W = 64
HEAD = 128
H_AB = 4
H_C = 8
CONV_K = 5
N_EXPERTS = 16
N_GROUPS = 4
EXPERTS_PER_GROUP = N_EXPERTS // N_GROUPS
LANES = 128
SUBLANES = 8
VMEM_LIMIT = 56 * 1024 * 1024

TOK_TILE = 512
MOE_TILE = 512
ROW_DMA_UNROLL = 8
MLSTM_CHUNK = 128
GDN_CHUNK = 64
GDN_SCAN_CHUNKS = 4
GLA_CHUNK = 128
GDN_INV_BLOCK = 16


def _cparams(sem):
    return pltpu.CompilerParams(dimension_semantics=sem, vmem_limit_bytes=VMEM_LIMIT)


def _dot(a, b):
    return jnp.dot(a, b, preferred_element_type=F32)


def _dot_nt(a, b):
    return lax.dot_general(a, b, (((1,), (1,)), ((), ())), preferred_element_type=F32)


def _dot_tn(a, b):
    return lax.dot_general(a, b, (((0,), (0,)), ((), ())), preferred_element_type=F32)


def _split2(x):
    hi = x.astype(BF16)
    lo = (x - hi.astype(F32)).astype(BF16)
    return hi, lo


def _split3(x):
    hi = x.astype(BF16)
    r = x - hi.astype(F32)
    mid = r.astype(BF16)
    lo = (r - mid.astype(F32)).astype(BF16)
    return hi, mid, lo


def _dot_exact_lhs(m_bf16, x):
    hi, mid, lo = _split3(x)
    return _dot(m_bf16, hi) + _dot(m_bf16, mid) + _dot(m_bf16, lo)


def _dot3(a, b):
    ah, al = _split2(a)
    bh, bl = _split2(b)
    return _dot(ah, bh) + _dot(ah, bl) + _dot(al, bh)


def _dot_nt3(a, b):
    ah, al = _split2(a)
    bh, bl = _split2(b)
    return _dot_nt(ah, bh) + _dot_nt(ah, bl) + _dot_nt(al, bh)


def _sigmoid(z):
    return 1.0 / (1.0 + jnp.exp(-z))


def _sigmoid_pair(z):
    return _sigmoid(z), _sigmoid(-z)


def _silu(z):
    return z * _sigmoid(z)


def _mod_kernel(c_ref, w_ref, b_ref, o_ref):
    s = _silu(c_ref[...]).astype(BF16)
    o_ref[0] = _dot(s, w_ref[0].astype(BF16)) + b_ref[0]


def _modulation(cpad, w_mod, b_mod):
    depth, d, n = w_mod.shape
    tn = 1536
    return pl.pallas_call(
        _mod_kernel,
        grid=(depth, n // tn),
        in_specs=[pl.BlockSpec((SUBLANES, d), lambda l, j: (0, 0)),
                  pl.BlockSpec((1, d, tn), lambda l, j: (l, 0, j)),
                  pl.BlockSpec((1, 1, tn), lambda l, j: (l, 0, j))],
        out_specs=pl.BlockSpec((1, SUBLANES, tn), lambda l, j: (l, 0, j)),
        out_shape=jax.ShapeDtypeStruct((depth, SUBLANES, n), F32),
        compiler_params=_cparams(("parallel", "parallel")),
        name="modulation",
    )(cpad, w_mod, b_mod.reshape(depth, 1, n))


def _gate_activations(raw, par):
    y = raw + par[0:1, :]
    e = jnp.exp(-jnp.abs(y))
    sp = jnp.log1p(e)
    lsig = jnp.minimum(y, 0.0) - sp
    r = 1.0 / (1.0 + e)
    sig = jnp.where(y >= 0, r, e * r)
    gdec = -jnp.exp(par[1:2, :]) * (jnp.maximum(y, 0.0) + sp)
    lane = lax.broadcasted_iota(jnp.int32, raw.shape, 1)
    return jnp.where(lane < 4, y, jnp.where(lane < 8, lsig, jnp.where(lane < 12, sig, gdec)))


def _inproj_kernel(x_ref, gain_ref, sc_ref, sh_ref, w_ref, *rest, with_gates, p_cols, silu_cols):
    if with_gates:
        wg_ref, gpar_ref, p_ref, g_ref = rest
    else:
        (p_ref,) = rest
    x = x_ref[...]
    y = x * lax.rsqrt(jnp.mean(x * x, axis=-1, keepdims=True) + EPS) * gain_ref[...]
    u = (y * (1.0 + sc_ref[0]) + sh_ref[0]).astype(BF16)
    if with_gates:
        for d in range(2):
            g_ref[d] = _gate_activations(_dot(u, wg_ref[d]), gpar_ref[d])
    tn = 1024
    for j in range(p_cols // tn):
        res = _dot(u, w_ref[:, j * tn:(j + 1) * tn])
        if (j + 1) * tn <= silu_cols:
            res = _silu(res)
        p_ref[:, j * tn:(j + 1) * tn] = res


def _mod_row(i, tm, seq, nb):
    return jnp.minimum((i * tm) // seq, nb)


def _inproj(xs, gain, mod, layer, w_all, w_idx, p, seq, nb, wg=None, gpar=None, silu_cols=0):
    t, d = xs.shape
    tm = TOK_TILE
    with_gates = wg is not None
    base = layer * SUBLANES * 6

    def mod_spec(which):
        return pl.BlockSpec((1, 1, d), lambda i: (base + _mod_row(i, tm, seq, nb) * 6 + which, 0, 0))

    in_specs = [pl.BlockSpec((tm, d), lambda i: (i, 0)),
                pl.BlockSpec((1, d), lambda i: (0, 0)),
                mod_spec(1), mod_spec(0),
                pl.BlockSpec((None, d, p), lambda i: (w_idx, 0, 0), pipeline_mode=pl.Buffered(1))]
    args = [xs, gain.reshape(1, d), mod, mod, w_all]
    out_specs = [pl.BlockSpec((tm, p), lambda i: (i, 0))]
    out_shape = [jax.ShapeDtypeStruct((t, p), F32)]
    if with_gates:
        in_specs += [pl.BlockSpec((2, d, LANES), lambda i: (0, 0, 0)),
                     pl.BlockSpec((2, SUBLANES, LANES), lambda i: (0, 0, 0))]
        args += [wg, gpar]
        out_specs.append(pl.BlockSpec((2, tm, LANES), lambda i: (0, i, 0)))
        out_shape.append(jax.ShapeDtypeStruct((2, t, LANES), F32))
    return pl.pallas_call(
        functools.partial(_inproj_kernel, with_gates=with_gates, p_cols=p, silu_cols=silu_cols),
        grid=(t // tm,),
        in_specs=in_specs, out_specs=out_specs, out_shape=out_shape,
        compiler_params=_cparams(("parallel",)),
        name="inproj",
    )(*args)


def _gdn_prep_kernel(*refs, blocks_per_seq, n_lat_blocks, scale):
    w_ref, o_ref = refs[9], refs[10]
    i = pl.program_id(0)
    tm = o_ref.shape[0]
    w = refs[1].shape[1]
    in_lat = i < n_lat_blocks
    first = jnp.logical_or(jnp.logical_not(in_lat), i % blocks_per_seq == 0)
    last = jnp.logical_or(jnp.logical_not(in_lat), i % blocks_per_seq == blocks_per_seq - 1)
    pad = (CONV_K - 1) // 2
    for part in range(3):
        prev_ref, x_ref, next_ref = refs[3 * part:3 * part + 3]
        prev = jnp.where(first, 0.0, prev_ref[...])
        nxt = jnp.where(last, 0.0, next_ref[...])
        xe = jnp.concatenate([prev, x_ref[...], nxt], axis=0)
        n = xe.shape[0]
        acc = None
        for j in range(CONV_K):
            shift = (pad - j) % n
            xr = xe if shift == 0 else pltpu.roll(xe, shift, 0)
            term = xr[SUBLANES:SUBLANES + tm] * w_ref[j:j + 1, part * w:(part + 1) * w]
            acc = term if acc is None else acc + term
        y = _silu(acc)
        for h in range(H_AB):
            t = y[:, h * HEAD:(h + 1) * HEAD]
            if part < 2:
                t = t * lax.rsqrt(jnp.sum(t * t, axis=-1, keepdims=True) + EPS)
            if part == 0:
                t = t * scale
            o_ref[:, part * w + h * HEAD:part * w + (h + 1) * HEAD] = t


def _gdn_prep(p, conv_w, seq, n_ctx, nb):
    t = p.shape[0]
    tm = 256
    w = H_AB * HEAD
    first_col = 4
    wpad = jnp.zeros((SUBLANES, 3 * w), F32).at[:CONV_K].set(conv_w)
    hb = tm // SUBLANES
    nblk8 = t // SUBLANES
    in_specs = []
    for c in range(3):
        in_specs += [pl.BlockSpec((SUBLANES, w), lambda i, c=c: (jnp.maximum(i * hb - 1, 0), first_col + c)),
                     pl.BlockSpec((tm, w), lambda i, c=c: (i, first_col + c)),
                     pl.BlockSpec((SUBLANES, w), lambda i, c=c: (jnp.minimum((i + 1) * hb, nblk8 - 1), first_col + c))]
    in_specs.append(pl.BlockSpec((SUBLANES, 3 * w), lambda i: (0, 0)))
    return pl.pallas_call(
        functools.partial(_gdn_prep_kernel, blocks_per_seq=seq // tm, n_lat_blocks=nb * seq // tm,
                          scale=HEAD ** -0.5),
        grid=(t // tm,),
        in_specs=in_specs,
        out_specs=pl.BlockSpec((tm, 3 * w), lambda i: (i, 0)),
        out_shape=jax.ShapeDtypeStruct((t, 3 * w), F32),
        compiler_params=_cparams(("parallel",)),
        name="gdn_prep",
    )(*([p] * 9), wpad)


def _tri_tables(c):
    idx = np.arange(c)
    fwd = (idx[None, :] <= idx[:, None]).astype(np.float32)
    return jnp.asarray(np.stack([fwd, fwd.T]))


def _to_cm_kernel(x3_ref, x2_ref, o_ref, *, n_lat_tiles):
    i = pl.program_id(0)
    rows = x3_ref.shape[0]

    @pl.when(i < n_lat_tiles)
    def _():
        for wl in range(SUBLANES):
            o_ref[wl * rows:(wl + 1) * rows, :] = x3_ref[:, wl, :]

    @pl.when(i >= n_lat_tiles)
    def _():
        o_ref[...] = x2_ref[...]


def _from_cm_kernel(x_ref, o_ref):
    rows = o_ref.shape[0]
    for wl in range(SUBLANES):
        o_ref[:, wl, :] = x_ref[wl * rows:(wl + 1) * rows, :]


def _to_column_major(xs, n_lat, seq):
    t, d = xs.shape
    rows = seq // GRID_W
    tile = rows * SUBLANES
    assert rows % SUBLANES == 0 and n_lat % tile == 0 and (t - n_lat) % tile == 0
    n_lat_tiles = n_lat // tile
    per_b = GRID_W // SUBLANES
    lat = lambda i: jnp.minimum(i, n_lat_tiles - 1)
    return pl.pallas_call(
        functools.partial(_to_cm_kernel, n_lat_tiles=n_lat_tiles),
        grid=(t // tile,),
        in_specs=[pl.BlockSpec((rows, SUBLANES, d), lambda i: (lat(i) // per_b, lat(i) % per_b, 0)),
                  pl.BlockSpec((tile, d), lambda i: (jnp.maximum(i, n_lat_tiles - 1), 0))],
        out_specs=pl.BlockSpec((tile, d), lambda i: (i, 0)),
        out_shape=jax.ShapeDtypeStruct((t, d), xs.dtype),
        compiler_params=_cparams(("parallel",)),
        name="to_column_major",
    )(xs.reshape(t // GRID_W, GRID_W, d), xs)


def _from_column_major(y, seq):
    n_lat, d = y.shape
    rows = seq // GRID_W
    tile = rows * SUBLANES
    per_b = GRID_W // SUBLANES
    out = pl.pallas_call(
        _from_cm_kernel,
        grid=(n_lat // tile,),
        in_specs=[pl.BlockSpec((tile, d), lambda i: (i, 0))],
        out_specs=pl.BlockSpec((rows, SUBLANES, d), lambda i: (i // per_b, i % per_b, 0)),
        out_shape=jax.ShapeDtypeStruct((n_lat // GRID_W, GRID_W, d), y.dtype),
        compiler_params=_cparams(("parallel",)),
        name="from_column_major",
    )(y)
    return out.reshape(n_lat, d)


def _mlstm_kernel(qf, kf, vf, qb, kb, vb, gf, gb, tri_ref, hf_ref, hb_ref, s_scr, m_scr, *, scale):
    @pl.when(pl.program_id(1) == 0)
    def _():
        s_scr[...] = jnp.zeros_like(s_scr)
        m_scr[...] = jnp.zeros_like(m_scr)

    c = gf.shape[1]
    ones_col = (lax.broadcasted_iota(jnp.int32, (c, HEAD), 1) == 0).astype(BF16)
    qkv = ((qf, kf, vf), (qb, kb, vb))
    h_out = (hf_ref, hb_ref)
    g = [gf[0], gb[0]]
    mask = [tri_ref[d] > 0.5 for d in range(2)]
    b = [_dot_exact_lhs(tri_ref[d].astype(BF16), g[d]) for d in range(2)]
    total = [jnp.sum(x, axis=0, keepdims=True) for x in g]
    g_t = [x.T for x in g]
    b_t = [x.T for x in b]
    chains = [(d, h) for d in range(2) for h in range(H_AB)]
    sl = [slice(h * HEAD, (h + 1) * HEAD) for d, h in chains]
    q = [qkv[d][0][:, sl[i]].astype(BF16) for i, (d, h) in enumerate(chains)]
    k = [qkv[d][1][:, sl[i]] * scale for i, (d, h) in enumerate(chains)]
    v_aug = [jnp.concatenate([qkv[d][2][:, sl[i]].astype(BF16), ones_col], axis=1) for i, (d, h) in enumerate(chains)]
    s = [s_scr[d, h] for d, h in chains]
    qk = [_dot_nt(q[i], k[i].astype(BF16)) for i in range(len(chains))]
    qs = [_dot(q[i], s[i].astype(BF16)) for i in range(len(chains))]
    w_intra, w_inter, m_ts, colvs, m_sts, tots = [], [], [], [], [], []
    for i, (d, h) in enumerate(chains):
        li_c, li_r = g[d][:, h:h + 1], g_t[d][h:h + 1, :]
        b_c, b_r = b[d][:, 4 + h:5 + h], b_t[d][4 + h:5 + h, :]
        tot = total[d][:, 4 + h:5 + h]
        m_st = m_scr[d, h][0:1, 0:1]
        d_log = jnp.where(mask[d], b_c - b_r + li_r, -jnp.inf)
        m_inter = b_c + m_st
        m_t = jnp.maximum(m_inter, jnp.max(d_log, axis=-1, keepdims=True))
        w_inter.append(jnp.exp(m_inter - m_t))
        w_intra.append((jnp.exp(d_log - m_t) * qk[i]).astype(BF16))
        m_ts.append(m_t); colvs.append(tot - b_c + li_c); m_sts.append(m_st); tots.append(tot)
    intra = [_dot(w_intra[i], v_aug[i]) for i in range(len(chains))]
    m_new = [jnp.maximum(tots[i] + m_sts[i], jnp.max(colvs[i], axis=0, keepdims=True)) for i in range(len(chains))]
    upd = [_dot_tn((k[i] * jnp.exp(colvs[i] - m_new[i])).astype(BF16), v_aug[i]) for i in range(len(chains))]
    for i, (d, h) in enumerate(chains):
        num = w_inter[i] * qs[i] + intra[i]
        den = num[:, HEAD:HEAD + 1]
        h_out[d][0, :, sl[i]] = num[:, :HEAD] / jnp.maximum(jnp.abs(den), jnp.exp(-m_ts[i]))
        s_scr[d, h] = jnp.exp(tots[i] + m_sts[i] - m_new[i]) * s[i] + upd[i]
        m_scr[d, h] = jnp.broadcast_to(m_new[i], (SUBLANES, LANES))


def _step_block(chunk, seq, n_ctx, nb):
    n_c, n_l = n_ctx // chunk, seq // chunk
    off = nb * n_l

    def fn(b, d, i):
        j = i - n_c
        ctx_blk = off + b * n_c + (i if d == 0 else n_c - 1 - i)
        lat_blk = b * n_l + (j if d == 0 else n_l - 1 - j)
        return jnp.where(i < n_c, ctx_blk, lat_blk)

    return n_c + n_l, fn


def _mlstm_scan(p, gates, seq, n_ctx, nb):
    t = p.shape[0]
    c = MLSTM_CHUNK
    w = H_AB * HEAD
    n_steps, blk = _step_block(c, seq, n_ctx, nb)
    in_specs = [pl.BlockSpec((c, w), lambda b, i, d=d, col=col: (blk(b, d, i), col)) for d in range(2) for col in range(3)]
    in_specs += [pl.BlockSpec((1, c, LANES), lambda b, i, d=d: (d, blk(b, d, i), 0)) for d in range(2)]
    in_specs.append(pl.BlockSpec((2, c, c), lambda b, i: (0, 0, 0)))
    return pl.pallas_call(
        functools.partial(_mlstm_kernel, scale=HEAD ** -0.5),
        grid=(nb, n_steps),
        in_specs=in_specs,
        out_specs=[pl.BlockSpec((1, c, w), lambda b, i, d=d: (0, blk(b, d, i), 0)) for d in range(2)],
        out_shape=[jax.ShapeDtypeStruct((1, t, w), F32)] * 2,
        scratch_shapes=[pltpu.VMEM((2, H_AB, HEAD, 2 * HEAD), F32), pltpu.VMEM((2, H_AB, SUBLANES, LANES), F32)],
        compiler_params=_cparams(("parallel", "arbitrary")),
        name="mlstm_scan",
    )(p, p, p, p, p, p, gates, gates, _tri_tables(c))


def _gdn_inv_masks(c):
    idx = np.arange(c)
    blk = lambda n: (idx[:, None] // n) == (idx[None, :] // n)
    levels = [blk(GDN_INV_BLOCK)]
    n = GDN_INV_BLOCK
    while n < c:
        levels.append(np.logical_and(blk(2 * n), np.logical_not(blk(n))))
        n *= 2
    levels = np.stack(levels).astype(np.float32)
    return jnp.asarray(np.tile(levels, (1, 1, H_AB)))


def _block_diag(x, n_blocks):
    blk = lax.broadcasted_iota(jnp.int32, x.shape, 1) // (x.shape[1] // n_blocks)
    return jnp.concatenate([jnp.where(blk == j, x, jnp.zeros_like(x)) for j in range(n_blocks)], axis=0)


def _heads_to_lanes(x, first, width):
    return jnp.concatenate([jnp.broadcast_to(x[:, first + h:first + h + 1], (x.shape[0], width))
                            for h in range(H_AB)], axis=1)


def _unit_tri_inverse(a_list, eye, inv_masks):
    prod = lambda xs, ys: [_dot(x.astype(BF16), _block_diag(y.astype(BF16), H_AB)) for x, y in zip(xs, ys)]
    pw = [a * inv_masks[0] for a in a_list]
    inv = [eye - p for p in pw]
    n = 2
    while n < GDN_INV_BLOCK:
        pw = prod(pw, pw)
        inv = [i + t for i, t in zip(inv, prod(inv, pw))]
        n *= 2
    for lvl in range(1, inv_masks.shape[0]):
        t = prod(prod(inv, [a * inv_masks[lvl] for a in a_list]), inv)
        inv = [i - x for i, x in zip(inv, t)]
    return inv


def _gdn_chunk_kernel(q_ref, k_ref, v_ref, g_ref, tri_ref, im_ref, u_ref, w_ref, qd_ref, kd_ref, att_ref):
    c = GDN_CHUNK
    n_chunks = q_ref.shape[0] // c
    inv_masks = im_ref[...]
    row = lax.broadcasted_iota(jnp.int32, (c, H_AB * c), 0)
    col = lax.broadcasted_iota(jnp.int32, (c, H_AB * c), 1) % c
    eye_b = row == col
    eye = eye_b.astype(F32)
    masks = [col <= row, col >= row]
    stricts = [col < row, col > row]
    tris = [tri_ref[d].astype(BF16) for d in range(2)]
    ones = jnp.ones((c, c), BF16)
    chunks = range(n_chunks)
    rows = [slice(ci * c, (ci + 1) * c) for ci in chunks]
    groups = [(ci, d) for ci in chunks for d in range(2)]
    k_hl = [_split2(k_ref[r, :]) for r in rows]
    kbd = [(_block_diag(hi, H_AB), _block_diag(lo, H_AB)) for hi, lo in k_hl]
    r_hi = [_dot_nt(jnp.concatenate([k_hl[ci][0], k_hl[ci][1], q_ref[rows[ci], :].astype(BF16)], axis=0), kbd[ci][0])
            for ci in chunks]
    r_lo = [_dot_nt(k_hl[ci][0], kbd[ci][1]) for ci in chunks]
    kk = [r_hi[ci][:c] + r_hi[ci][c:2 * c] + r_lo[ci] for ci in chunks]
    g = [g_ref[d, rows[ci], :] for ci, d in groups]
    gc = [_dot_exact_lhs(tris[d], g[i]) for i, (ci, d) in enumerate(groups)]
    gc_c = [_heads_to_lanes(x, 12, c) for x in gc]
    gc_r = [_dot_exact_lhs(ones, eye * x) for x in gc_c]
    decay = [jnp.exp(jnp.where(masks[d], gc_c[i] - gc_r[i], -jnp.inf)) for i, (ci, d) in enumerate(groups)]
    a = [jnp.where(stricts[d], _heads_to_lanes(g[i], 8, c) * kk[ci] * decay[i], 0.0)
         for i, (ci, d) in enumerate(groups)]
    inv = _unit_tri_inverse(a, eye, inv_masks)
    inv_hl = [jnp.concatenate(_split2(x), axis=0) for x in inv]
    gc_w = [_heads_to_lanes(x, 12, HEAD) for x in gc]
    eg = [jnp.exp(x) for x in gc_w]
    beta = [_heads_to_lanes(x, 8, HEAD) for x in g]
    res_u = [_dot(inv_hl[i], _block_diag((v_ref[rows[ci], :] * beta[i]).astype(BF16), H_AB))
             for i, (ci, d) in enumerate(groups)]
    res_w = [_dot(inv_hl[i], _block_diag((k_ref[rows[ci], :] * (beta[i] * eg[i])).astype(BF16), H_AB))
             for i, (ci, d) in enumerate(groups)]
    for i, (ci, d) in enumerate(groups):
        r = rows[ci]
        total = _heads_to_lanes(jnp.sum(g[i], axis=0, keepdims=True), 12, HEAD)
        u_ref[d, r, :] = (res_u[i][:c] + res_u[i][c:]).astype(BF16)
        w_ref[d, r, :] = (res_w[i][:c] + res_w[i][c:]).astype(BF16)
        qd_ref[d, r, :] = (q_ref[r, :] * eg[i]).astype(BF16)
        kd_ref[d, r, :] = (k_ref[r, :] * jnp.exp(total - gc_w[i])).astype(BF16)
        att_ref[d, r, :] = (r_hi[ci][2 * c:] * decay[i]).astype(BF16)


def _gdn_scan_kernel(*refs):
    ins, o_refs, s_scr = refs[:12], refs[12:14], refs[14]
    c = GDN_CHUNK

    @pl.when(pl.program_id(1) == 0)
    def _():
        s_scr[...] = jnp.zeros_like(s_scr)

    pw = 2 * HEAD
    rblk = lax.broadcasted_iota(jnp.int32, (pw, pw), 0) // HEAD
    cblk = lax.broadcasted_iota(jnp.int32, (pw, pw), 1) // HEAD
    on_diag = rblk == cblk
    chains = [(d, pr) for d in range(2) for pr in range(H_AB // 2)]
    refs_of = lambda d: ins[d::2]
    sl = [slice(pr * pw, (pr + 1) * pw) for d, pr in chains]
    s = [s_scr[d, pr] for d, pr in chains]
    n_sub = o_refs[0].shape[1] // c
    for k in range(n_sub):
        rows = [slice(kk * c, (kk + 1) * c) for kk in (k, n_sub - 1 - k)]
        res = [_dot(jnp.concatenate([refs_of(d)[1][0, rows[d], sl[i]], refs_of(d)[2][0, rows[d], sl[i]]], axis=0),
                    s[i].astype(BF16)) for i, (d, pr) in enumerate(chains)]
        v_new = [(refs_of(d)[0][0, rows[d], sl[i]].astype(F32) - res[i][:c]).astype(BF16)
                 for i, (d, pr) in enumerate(chains)]
        intra = [_dot(refs_of(d)[4][0, rows[d], pr * 2 * c:(pr + 1) * 2 * c], _block_diag(v_new[i], 2))
                 for i, (d, pr) in enumerate(chains)]
        upd = [_dot_tn(refs_of(d)[3][0, rows[d], sl[i]], v_new[i]) for i, (d, pr) in enumerate(chains)]
        totals = [jnp.sum(refs_of(d)[5][0, rows[d], :], axis=0, keepdims=True) for d in range(2)]
        for i, (d, pr) in enumerate(chains):
            o_refs[d][0, rows[d], sl[i]] = res[i][c:] + intra[i]
            gl = jnp.concatenate([jnp.broadcast_to(jnp.exp(totals[d][:, 12 + 2 * pr + j:13 + 2 * pr + j]), (1, HEAD))
                                  for j in range(2)], axis=1)
            s[i] = gl * s[i] + jnp.where(on_diag, upd[i], 0.0)
    for i, (d, pr) in enumerate(chains):
        s_scr[d, pr] = s[i]


def _gdn_scan(qkv, gates, seq, n_ctx, nb):
    t = qkv.shape[0]
    c = GDN_CHUNK
    w = H_AB * HEAD
    tri = _tri_tables(c)
    im = _gdn_inv_masks(c)
    tm = TOK_TILE
    sds = lambda width: jax.ShapeDtypeStruct((2, t, width), BF16)
    u, wv, qd, kd, att = pl.pallas_call(
        _gdn_chunk_kernel,
        grid=(t // tm,),
        in_specs=[pl.BlockSpec((tm, w), lambda i, col=col: (i, col)) for col in range(3)] + [
            pl.BlockSpec((2, tm, LANES), lambda i: (0, i, 0)),
            pl.BlockSpec((2, c, c), lambda i: (0, 0, 0)),
            pl.BlockSpec(im.shape, lambda i: (0, 0, 0))],
        out_specs=[pl.BlockSpec((2, tm, w), lambda i: (0, i, 0))] * 4
        + [pl.BlockSpec((2, tm, H_AB * c), lambda i: (0, i, 0))],
        out_shape=[sds(w)] * 4 + [sds(H_AB * c)],
        compiler_params=_cparams(("parallel",)),
        name="gdn_chunk",
    )(qkv, qkv, qkv, gates, tri, im)

    cs = c * GDN_SCAN_CHUNKS
    n_steps, blk = _step_block(cs, seq, n_ctx, nb)
    in_specs, args = [], []
    for arr, width in ((u, w), (wv, w), (qd, w), (kd, w), (att, H_AB * c), (gates, LANES)):
        for d in range(2):
            in_specs.append(pl.BlockSpec((1, cs, width), lambda b, i, d=d: (d, blk(b, d, i), 0)))
            args.append(arr)
    o_f, o_b = pl.pallas_call(
        _gdn_scan_kernel,
        grid=(nb, n_steps),
        in_specs=in_specs,
        out_specs=[pl.BlockSpec((1, cs, w), lambda b, i, d=d: (0, blk(b, d, i), 0)) for d in range(2)],
        out_shape=[jax.ShapeDtypeStruct((1, t, w), F32)] * 2,
        scratch_shapes=[pltpu.VMEM((2, H_AB // 2, 2 * HEAD, 2 * HEAD), F32)],
        compiler_params=_cparams(("parallel", "arbitrary")),
        name="gdn_scan",
    )(*args)
    return o_f, o_b


def _gla_tables(c):
    idx = np.arange(c)
    sel, sgn, msk = [], [], []
    for d in range(2):
        sel_d, sgn_d, msk_d = [], [], []
        bs = c // 2
        while bs >= 1:
            pair = idx // (2 * bs)
            second = (idx // bs) % 2 == 1
            ref_row = pair * 2 * bs + (bs - 1 if d == 0 else bs)
            sel_d.append((idx[None, :] == ref_row[:, None]).astype(np.float32))
            qside = second if d == 0 else np.logical_not(second)
            sgn_d.append(np.where(qside, 1.0, -1.0)[:, None] * np.ones((1, HEAD)))
            same_pair = pair[:, None] == pair[None, :]
            msk_d.append(np.logical_and(same_pair, np.logical_and(qside[:, None], np.logical_not(qside)[None, :])))
            bs //= 2
        msk_d.append(np.eye(c, dtype=bool))
        sel.append(np.concatenate(sel_d, axis=0))
        sgn.append(np.stack(sgn_d))
        msk.append(np.stack(msk_d).astype(np.float32))
    return (jnp.asarray(np.stack(sel), dtype=BF16), jnp.asarray(np.stack(sgn), dtype=F32),
            jnp.asarray(np.stack(msk), dtype=F32))


def _gla_kernel(qf_ref, zf_ref, vf_ref, qb_ref, zb_ref, vb_ref, fb_ref, lb_ref, tri_ref, sel_ref, sgn_ref, msk_ref,
                of_ref, ob_ref, s_scr):
    @pl.when(pl.program_id(1) == 0)
    def _():
        s_scr[...] = jnp.zeros_like(s_scr)

    c = qf_ref.shape[0]
    n_lvl = sgn_ref.shape[1]
    dirs = range(2)
    qzv = ((qf_ref, zf_ref, vf_ref), (qb_ref, zb_ref, vb_ref))
    o_refs = (of_ref, ob_ref)
    chains = [(d, h) for d in dirs for h in range(H_C)]
    sl = [slice(h * HEAD, (h + 1) * HEAD) for d, h in chains]
    g_all, key_all, q_all = [], [], []
    for d in dirs:
        lb = lb_ref[d]
        sig_pos, sig_neg = _sigmoid_pair(qzv[d][1][...] + fb_ref[d])
        g_all.append(jnp.log(lb + (1.0 - lb) * sig_pos))
        key_all.append((1.0 - lb) * sig_neg)
        q_all.append(qzv[d][0][...])
    gc_all = [_dot_exact_lhs(tri_ref[d].astype(BF16), g_all[d]) for d in dirs]
    total_all = [jnp.sum(g, axis=0, keepdims=True) for g in g_all]
    refs = [_dot(sel_ref[d], gc_all[d].astype(BF16)) for d in dirs]
    diag = msk_ref[0, n_lvl] > 0.5
    att = [jnp.where(diag, _dot_nt(q_all[d][:, sl[i]].astype(BF16), key_all[d][:, sl[i]].astype(BF16)), 0.0)
           for i, (d, h) in enumerate(chains)]
    for l in range(n_lvl):
        sgn = [sgn_ref[d, l] for d in dirs]
        x = [(jnp.where(sgn[d] > 0, q_all[d][:, sl[i]], key_all[d][:, sl[i]])
              * jnp.exp(sgn[d] * (gc_all[d][:, sl[i]] - refs[d][l * c:(l + 1) * c, sl[i]]))).astype(BF16)
             for i, (d, h) in enumerate(chains)]
        prod = [_dot_nt(xh, xh) for xh in x]
        in_level = [msk_ref[d, l] > 0.5 for d in dirs]
        att = [jnp.where(in_level[d], prod[i], att[i]) for i, (d, h) in enumerate(chains)]
    st = [s_scr[d, h] for d, h in chains]
    inter = [_dot_nt((q_all[d][:, sl[i]] * jnp.exp(gc_all[d][:, sl[i]])).astype(BF16), st[i].astype(BF16))
             for i, (d, h) in enumerate(chains)]
    intra = [_dot(att[i].astype(BF16), qzv[d][2][:, sl[i]].astype(BF16)) for i, (d, h) in enumerate(chains)]
    upd = [_dot_tn(qzv[d][2][:, sl[i]].astype(BF16),
                   (key_all[d][:, sl[i]] * jnp.exp(total_all[d][:, sl[i]] - gc_all[d][:, sl[i]])).astype(BF16))
           for i, (d, h) in enumerate(chains)]
    for i, (d, h) in enumerate(chains):
        o_refs[d][0, :, sl[i]] = inter[i] + intra[i]
        s_scr[d, h] = jnp.exp(total_all[d][:, sl[i]]) * st[i] + upd[i]


def _gla_scan(p, f_bias, lb, seq, n_ctx, nb):
    t, pw = p.shape
    c = GLA_CHUNK
    w = H_C * HEAD
    tri = _tri_tables(c)
    sel, sgn, msk = _gla_tables(c)
    n_steps, blk = _step_block(c, seq, n_ctx, nb)
    dspec = lambda d, col: pl.BlockSpec((c, w), lambda b, i: (blk(b, d, i), col))
    whole = lambda a: pl.BlockSpec(a.shape, lambda b, i: (0,) * a.ndim)
    fb3, lb3 = f_bias.reshape(2, 1, w), lb.reshape(2, 1, w)
    consts = (fb3, lb3, tri, sel, sgn, msk)
    return pl.pallas_call(
        _gla_kernel,
        grid=(nb, n_steps),
        in_specs=[dspec(0, 0), dspec(0, 1), dspec(0, 3), dspec(1, 0), dspec(1, 2), dspec(1, 3)]
        + [whole(a) for a in consts],
        out_specs=[pl.BlockSpec((1, c, w), lambda b, i, d=d: (0, blk(b, d, i), 0)) for d in range(2)],
        out_shape=[jax.ShapeDtypeStruct((1, t, w), F32)] * 2,
        scratch_shapes=[pltpu.VMEM((2, H_C, HEAD, HEAD), F32)],
        compiler_params=_cparams(("parallel", "arbitrary")),
        name="gla_scan",
    )(p, p, p, p, p, p, *consts)


def _best_group(scores_t, bias):
    sel = [scores_t[e:e + 1, :] + bias[e:e + 1, :] for e in range(N_EXPERTS)]
    gscore = []
    for grp in range(N_GROUPS):
        v = sel[grp * EXPERTS_PER_GROUP:(grp + 1) * EXPERTS_PER_GROUP]
        best = None
        for a in range(EXPERTS_PER_GROUP):
            for b in range(a + 1, EXPERTS_PER_GROUP):
                pair = v[a] + v[b]
                best = pair if best is None else jnp.maximum(best, pair)
        gscore.append(best)
    best_g = jnp.zeros(gscore[0].shape, jnp.int32)
    best_v = gscore[0]
    for grp in range(1, N_GROUPS):
        better = gscore[grp] > best_v
        best_g = jnp.where(better, grp, best_g)
        best_v = jnp.where(better, gscore[grp], best_v)
    return best_g


def _rows_copy(src, dst, sem, src_row, dst_row):
    return pltpu.make_async_copy(src.at[pl.ds(src_row, 1), :], dst.at[pl.ds(dst_row, 1), :], sem)


def _outproj_kernel(x_ref, of0, ob0, gt0, of1, ob1, gt1, hn_ref, w_ref, g1_ref, sc_ref, sh_ref, nf_ref,
                    wr_ref, rb_ref, triu_ref, xo_ref, pos_ref, cnt_ref, hs_ref,
                    base_scr, h_scr, pos_v, pos_s, cnt_v, cnt_s, row_sem, meta_sem, *, first_half_sigmoid, cap):
    i = pl.program_id(0)
    n_steps = pl.num_programs(0)
    tm = x_ref.shape[0]
    slot = i % 2
    prev = 1 - slot
    dump = N_GROUPS * cap

    @pl.when(i == 0)
    def _():
        base_scr[...] = jnp.zeros_like(base_scr)
        h_scr[...] = jnp.zeros_like(h_scr)

        def init(r, carry):
            pos_s[1, r] = dump + r
            return carry
        lax.fori_loop(0, tm, init, 0)

    for r in range(tm):
        _rows_copy(h_scr.at[prev], hs_ref, row_sem, r, pos_s[prev, r]).start()

    feats = []
    for half, (of, ob, gt) in enumerate(((of0, ob0, gt0), (of1, ob1, gt1))):
        o = of[0] + ob[0]
        gate = gt[...]
        for h in range(o.shape[1] // HEAD):
            sl = slice(h * HEAD, (h + 1) * HEAD)
            t = o[:, sl]
            y = t * lax.rsqrt(jnp.mean(t * t, axis=-1, keepdims=True) + EPS)
            y = y * hn_ref[:, half * o.shape[1] + h * HEAD: half * o.shape[1] + (h + 1) * HEAD]
            gz = gate[:, sl]
            act = _sigmoid(gz) if (half == 0 and first_half_sigmoid) else _silu(gz)
            feats.append((y * act).astype(BF16))
    feats = jnp.concatenate(feats, axis=1)
    xn = x_ref[...] + g1_ref[0] * _dot(feats, w_ref[...])
    xo_ref[...] = xn
    y = xn * lax.rsqrt(jnp.mean(xn * xn, axis=-1, keepdims=True) + EPS) * nf_ref[...]
    hl = y * (1.0 + sc_ref[0]) + sh_ref[0]
    logits = _dot3(hl, wr_ref[...])
    scores_t = _sigmoid(logits).T
    best_g = _best_group(scores_t, rb_ref[...])
    onehot = jnp.concatenate([(best_g == g).astype(F32) for g in range(N_GROUPS)]
                             + [jnp.zeros((SUBLANES - N_GROUPS, tm), F32)], axis=0)
    before = _dot(onehot.astype(BF16), triu_ref[...])
    base = base_scr[...]
    rank = jnp.sum(onehot * (before + base[:, 0:1]), axis=0, keepdims=True)
    pos = best_g * cap + rank.astype(jnp.int32)
    pos_ref[0] = pos
    base = base + jnp.sum(onehot, axis=1, keepdims=True)
    base_scr[...] = base
    cnt_ref[...] = base

    pltpu.make_async_copy(h_scr.at[prev], hs_ref.at[pl.ds(0, tm), :], row_sem).wait()
    h_scr[slot] = hl
    pos_v[...] = pos
    to_smem = pltpu.make_async_copy(pos_v, pos_s.at[pl.ds(slot, 1), :], meta_sem)
    to_smem.start()
    to_smem.wait()

    @pl.when(i == n_steps - 1)
    def _():
        def issue(r, carry):
            _rows_copy(h_scr.at[slot], hs_ref, row_sem, r, pos_s[slot, r]).start()
            return carry
        lax.fori_loop(0, tm, issue, 0, unroll=ROW_DMA_UNROLL)
        pltpu.make_async_copy(h_scr.at[slot], hs_ref.at[pl.ds(0, tm), :], row_sem).wait()
        cnt_v[...] = base.astype(jnp.int32)
        counts = pltpu.make_async_copy(cnt_v, cnt_s, meta_sem)
        counts.start()
        counts.wait()
        h_scr[prev] = jnp.zeros((tm, h_scr.shape[2]), F32)
        zero = h_scr.at[prev]

        def pad_copies(action):
            for g in range(N_GROUPS):
                cnt = cnt_s[g, 0]
                up = (cnt + SUBLANES - 1) // SUBLANES * SUBLANES
                for r in range(SUBLANES - 1):
                    @pl.when(cnt + r < up)
                    def _():
                        action(_rows_copy(zero, hs_ref, row_sem, 0, g * cap + cnt + r))
                action(pltpu.make_async_copy(zero, hs_ref.at[pl.ds(pl.multiple_of(g * cap + up, SUBLANES), tm), :],
                                             row_sem))

        pad_copies(lambda cp: cp.start())
        pad_copies(lambda cp: cp.wait())


def _outproj(xs, o_halves, gate_src, gate_cols, hn_gain, w_out, w_idx, mod, layer, nf_gain, wr_pad, rb_col,
             seq, nb, n_rows, first_half_sigmoid):
    t, d = xs.shape
    tm = TOK_TILE
    half = d // 2
    base = layer * SUBLANES * 6

    def mod_spec(which):
        return pl.BlockSpec((1, 1, d), lambda i: (base + _mod_row(i, tm, seq, nb) * 6 + which, 0, 0))

    in_specs = [pl.BlockSpec((tm, d), lambda i: (i, 0))]
    args = [xs]
    for ((arr_f, dir_f), (arr_b, dir_b), col), gcol in zip(o_halves, gate_cols):
        in_specs += [pl.BlockSpec((1, tm, half), lambda i, col=col, dd=dir_f: (dd, i, col)),
                     pl.BlockSpec((1, tm, half), lambda i, col=col, dd=dir_b: (dd, i, col)),
                     pl.BlockSpec((tm, half), lambda i, gcol=gcol: (i, gcol))]
        args += [arr_f, arr_b, gate_src]
    in_specs += [pl.BlockSpec((1, d), lambda i: (0, 0)),
                 pl.BlockSpec((None, d, d), lambda i: (w_idx, 0, 0)),
                 mod_spec(2), mod_spec(4), mod_spec(3),
                 pl.BlockSpec((1, d), lambda i: (0, 0)),
                 pl.BlockSpec((d, LANES), lambda i: (0, 0)),
                 pl.BlockSpec((N_EXPERTS, 1), lambda i: (0, 0)),
                 pl.BlockSpec((tm, tm), lambda i: (0, 0))]
    idx = np.arange(tm)
    triu = jnp.asarray(idx[:, None] < idx[None, :], dtype=BF16)
    args += [hn_gain.reshape(1, d), w_out, mod, mod, mod, nf_gain.reshape(1, d), wr_pad, rb_col, triu]
    n_tiles = n_rows // tm
    assert tm == MOE_TILE
    cap = n_rows + MOE_TILE
    return pl.pallas_call(
        functools.partial(_outproj_kernel, first_half_sigmoid=first_half_sigmoid, cap=cap),
        grid=(n_tiles,),
        in_specs=in_specs,
        out_specs=[pl.BlockSpec((tm, d), lambda i: (i, 0)),
                   pl.BlockSpec((1, 1, tm), lambda i: (i, 0, 0)),
                   pl.BlockSpec((SUBLANES, LANES), lambda i: (0, 0)),
                   pl.BlockSpec(memory_space=pl.ANY)],
        out_shape=[jax.ShapeDtypeStruct((t, d), F32), jax.ShapeDtypeStruct((n_tiles, 1, tm), jnp.int32),
                   jax.ShapeDtypeStruct((SUBLANES, LANES), F32),
                   jax.ShapeDtypeStruct((N_GROUPS * cap + tm, d), F32)],
        scratch_shapes=[pltpu.VMEM((SUBLANES, LANES), F32), pltpu.VMEM((2, tm, d), F32),
                        pltpu.VMEM((1, tm), jnp.int32), pltpu.SMEM((2, tm), jnp.int32),
                        pltpu.VMEM((SUBLANES, LANES), jnp.int32), pltpu.SMEM((SUBLANES, LANES), jnp.int32),
                        pltpu.SemaphoreType.DMA(()), pltpu.SemaphoreType.DMA(())],
        compiler_params=_cparams(("arbitrary",)),
        name="outproj",
    )(*args)


def _group_gates(x, wr, bias_row, grp):
    scores = _sigmoid(_dot3(x, wr))
    lane_i = lax.broadcasted_iota(jnp.int32, scores.shape, 1)
    lane = lane_i.astype(F32)
    m = jnp.where(lane_i // EXPERTS_PER_GROUP == grp, scores + bias_row, -jnp.inf)
    picks = []
    for _ in range(2):
        top = jnp.max(m, axis=-1, keepdims=True)
        idx = jnp.min(jnp.where(m == top, lane, float(LANES)), axis=-1, keepdims=True)
        picks.append(idx)
        m = jnp.where(lane == idx, -jnp.inf, m)
    w = [jnp.sum(jnp.where(lane == idx, scores, 0.0), axis=-1, keepdims=True) for idx in picks]
    tot = w[0] + w[1]
    return jnp.where(lane == picks[0], w[0] / tot, jnp.where(lane == picks[1], w[1] / tot, 0.0))


def _group_ffn_kernel(tg_ref, tb_ref, nv_ref, x_ref, wr_ref, rb_ref, w1_ref, w3_ref, w2_ref, o_ref):
    i = pl.program_id(0)

    @pl.when(i < nv_ref[0])
    def _():
        grp = tg_ref[i]
        x = x_ref[...]
        xb = x.astype(BF16)
        gate = _group_gates(x, wr_ref[...], rb_ref[...], grp)
        lane = lax.broadcasted_iota(jnp.int32, gate.shape, 1)
        acc = None
        for j in range(EXPERTS_PER_GROUP):
            gcol = jnp.sum(jnp.where(lane == grp * EXPERTS_PER_GROUP + j, gate, 0.0), axis=-1, keepdims=True)
            a = _silu(_dot(xb, w1_ref[j])) * _dot(xb, w3_ref[j]) * gcol
            y = _dot(a.astype(BF16), w2_ref[j])
            acc = y if acc is None else acc + y
        o_ref[...] = acc


def _group_ffn(h_sorted, tile_grp, tile_blk, n_valid, wr_pad, rb_row, w1, w3, w2, cap, layer):
    p_rows, d = h_sorted.shape
    tm = MOE_TILE
    f = w1.shape[2]
    n_tiles = tile_grp.shape[0]
    blocks_per_group = cap // tm
    row_blk = lambda i, tg, tb, nv: (tg[i] * blocks_per_group + tb[i], 0)
    experts = lambda i, tg, tb, nv: (layer * N_GROUPS + tg[i], 0, 0)
    w_in_spec = pl.BlockSpec((EXPERTS_PER_GROUP, d, f), experts)
    return pl.pallas_call(
        _group_ffn_kernel,
        grid_spec=pltpu.PrefetchScalarGridSpec(
            num_scalar_prefetch=3, grid=(n_tiles,),
            in_specs=[pl.BlockSpec((tm, d), row_blk),
                      pl.BlockSpec((d, LANES), lambda i, tg, tb, nv: (0, 0)),
                      pl.BlockSpec((1, LANES), lambda i, tg, tb, nv: (0, 0)),
                      w_in_spec, w_in_spec, pl.BlockSpec((EXPERTS_PER_GROUP, f, d), experts)],
            out_specs=pl.BlockSpec((tm, d), row_blk)),
        out_shape=jax.ShapeDtypeStruct((p_rows, d), F32),
        compiler_params=_cparams(("arbitrary",)),
        name="moe_group_ffn",
    )(tile_grp, tile_blk, n_valid, h_sorted, wr_pad, rb_row, w1, w3, w2)


def _combine_kernel(pos_ref, x_ref, y_ref, g2_ref, nfin_ref, o_ref, buf, sem, *, final_norm):
    i = pl.program_id(0)
    tm = x_ref.shape[0]
    slot = i % 2

    def gather(tile, slot_):
        def issue(r, carry):
            _rows_copy(y_ref, buf.at[slot_], sem.at[slot_], pos_ref[tile * tm + r], r).start()
            return carry
        lax.fori_loop(0, tm, issue, 0, unroll=ROW_DMA_UNROLL)

    @pl.when(i == 0)
    def _():
        gather(0, 0)

    @pl.when(i + 1 < pl.num_programs(0))
    def _():
        gather(i + 1, 1 - slot)

    pltpu.make_async_copy(y_ref.at[pl.ds(0, tm), :], buf.at[slot], sem.at[slot]).wait()
    y = x_ref[...] + g2_ref[0] * buf[slot]
    if final_norm:
        y = y * lax.rsqrt(jnp.mean(y * y, axis=-1, keepdims=True) + EPS) * nfin_ref[...]
    o_ref[...] = y


def _combine(xs, y_sorted, pos, mod, layer, nfin, seq, nb, n_rows, final_norm):
    t, d = xs.shape
    tm = TOK_TILE
    base = layer * SUBLANES * 6
    out_rows = n_rows if final_norm else t
    return pl.pallas_call(
        functools.partial(_combine_kernel, final_norm=final_norm),
        grid_spec=pltpu.PrefetchScalarGridSpec(
            num_scalar_prefetch=1, grid=(n_rows // tm,),
            in_specs=[pl.BlockSpec((tm, d), lambda i, pos: (i, 0)),
                      pl.BlockSpec(memory_space=pl.ANY),
                      pl.BlockSpec((1, 1, d), lambda i, pos: (base + _mod_row(i, tm, seq, nb) * 6 + 5, 0, 0)),
                      pl.BlockSpec((1, d), lambda i, pos: (0, 0))],
            out_specs=pl.BlockSpec((tm, d), lambda i, pos: (i, 0)),
            scratch_shapes=[pltpu.VMEM((2, tm, d), F32), pltpu.SemaphoreType.DMA((2,))]),
        out_shape=jax.ShapeDtypeStruct((out_rows, d), F32),
        compiler_params=_cparams(("arbitrary",)),
        name="moe_combine",
    )(pos, xs, y_sorted, mod, nfin.reshape(1, d))


def _moe(xs, h_sorted, pos_rows, cnt, wr_pad, rb_row, w1, w3, w2, mod, layer, nfin, seq, nb, n_rows, final_norm):
    tm = MOE_TILE
    cap = n_rows + tm
    pos = pos_rows.reshape(-1)
    counts = cnt[:N_GROUPS, 0].astype(jnp.int32)
    tiles = (counts + tm - 1) // tm
    ends = jnp.cumsum(tiles)
    n_tiles = n_rows // tm + N_GROUPS
    step = jnp.minimum(jnp.arange(n_tiles, dtype=jnp.int32), ends[-1] - 1)
    tile_grp = jnp.sum(step[:, None] >= ends[None, :], axis=1).astype(jnp.int32)
    tile_blk = step - (ends - tiles)[tile_grp]
    y_sorted = _group_ffn(h_sorted, tile_grp, tile_blk, ends[-1:].astype(jnp.int32), wr_pad, rb_row, w1, w3, w2, cap,
                          layer)
    return _combine(xs, y_sorted, pos, mod, layer, nfin, seq, nb, n_rows, final_norm)


def kernel(x, c, ctx, c_ctx, w_mod, b_mod, norm_mix, norm_ffn, norm_final, ab_w_in, ab_i_bias, ab_f_bias,
           ab_conv, ab_a_log, ab_dt_bias, ab_norm_a, ab_norm_b, ab_w_out, c_w_in, c_f_bias, c_lb_raw, c_norm,
           c_w_out, w_router, router_bias, w1, w3, w2):
    nb, seq, d = x.shape
    n_ctx = ctx.shape[1]
    depth = w_mod.shape[0]
    n_lat = nb * seq
    assert nb + 1 <= SUBLANES and seq % 1024 == 0 and (nb * n_ctx) % 1024 == 0 and n_ctx % 256 == 0

    xs = jnp.concatenate([x.reshape(n_lat, d), ctx.reshape(nb * n_ctx, d)], axis=0)
    cpad = jnp.zeros((SUBLANES, d), F32).at[:nb].set(c).at[nb].set(c_ctx)
    mod = _modulation(cpad, w_mod, b_mod).reshape(depth * SUBLANES * 6, 1, d)

    lb_p = jax.nn.softmax(c_lb_raw.astype(F32), axis=0)
    lb_all = jnp.cumsum(lb_p, axis=0) - lb_p[0:1]

    wr_pad = jnp.zeros((d, LANES), F32).at[:, :N_EXPERTS].set(w_router)
    rb_col = router_bias.astype(F32).reshape(N_EXPERTS, 1)
    rb_row = jnp.zeros((1, LANES), F32).at[0, :N_EXPERTS].set(router_bias.astype(F32))
    w_main = H_AB * HEAD * 8
    ab_w_in_b, c_w_in_b = ab_w_in.astype(BF16), c_w_in.astype(BF16)
    ab_w_out_b, c_w_out_b = ab_w_out.astype(BF16), c_w_out.astype(BF16)
    n_exp, _, d_ff = w1.shape[1:]
    w1_b = w1.astype(BF16).reshape(depth * n_exp, d, d_ff)
    w3_b = w3.astype(BF16).reshape(depth * n_exp, d, d_ff)
    w2_b = w2.astype(BF16).reshape(depth * n_exp, d_ff, d)

    out = None
    column_major = False
    for layer in range(depth):
        last = layer == depth - 1
        j = layer // 2
        want_cm = layer % 2 == 1 and j % 2 == 1
        needs_raster = layer % 2 == 0 or not want_cm
        if want_cm and not column_major:
            xs = _to_column_major(xs, n_lat, seq)
            column_major = True
        elif needs_raster and column_major:
            xs = jnp.concatenate([_from_column_major(xs[:n_lat], seq), xs[n_lat:]], axis=0)
            column_major = False
        if layer % 2 == 0:
            w_in = ab_w_in[j]
            gw = w_in[:, w_main:].reshape(d, 4, 2, H_AB)
            wg = jnp.zeros((2, d, LANES), F32).at[:, :, :4 * H_AB].set(
                jnp.transpose(gw, (2, 0, 1, 3)).reshape(2, d, 4 * H_AB)).astype(BF16)
            zeros = jnp.zeros((2, H_AB), F32)
            bias = jnp.concatenate([ab_i_bias[j], ab_f_bias[j], zeros, ab_dt_bias[j]], axis=1)
            alog = jnp.concatenate([zeros, zeros, zeros, ab_a_log[j]], axis=1)
            gpar = jnp.zeros((2, SUBLANES, LANES), F32).at[:, 0, :4 * H_AB].set(bias).at[:, 1, :4 * H_AB].set(alog)
            p, gates = _inproj(xs, norm_mix[layer], mod, layer, ab_w_in_b, j, w_main, seq, nb, wg, gpar)
            ha_f, ha_b = _mlstm_scan(p, gates, seq, n_ctx, nb)
            qkv = _gdn_prep(p, ab_conv[j], seq, n_ctx, nb)
            ob_f, ob_b = _gdn_scan(qkv, gates, seq, n_ctx, nb)
            o_halves = (((ha_f, 0), (ha_b, 0), 0), ((ob_f, 0), (ob_b, 0), 0))
            gate_cols = (3, 7)
            hn_gain = jnp.concatenate([ab_norm_a[j], ab_norm_b[j]])
            w_out = ab_w_out_b
        else:
            (p,) = _inproj(xs, norm_mix[layer], mod, layer, c_w_in_b, j, c_w_in.shape[2], seq, nb,
                           silu_cols=H_C * HEAD)
            oc_f, oc_b = _gla_scan(p, c_f_bias[j], lb_all[j], seq, n_ctx, nb)
            o_halves = (((oc_f, 0), (oc_b, 0), 0), ((oc_f, 0), (oc_b, 0), 1))
            gate_cols = (8, 9)
            hn_gain = c_norm[j]
            w_out = c_w_out_b
        n_rows = n_lat if last else n_lat + nb * n_ctx
        xs, pos_rows, cnt, h_sorted = _outproj(xs, o_halves, p, gate_cols, hn_gain, w_out, j, mod, layer,
                                               norm_ffn[layer], wr_pad, rb_col, seq, nb, n_rows, layer % 2 == 0)
        res = _moe(xs, h_sorted, pos_rows, cnt, wr_pad, rb_row, w1_b, w3_b, w2_b, mod, layer, norm_final, seq, nb,
                   n_rows, last)
        if last:
            out = _from_column_major(res, seq) if column_major else res
        else:
            xs = res
    return out.reshape(nb, seq, d)
```

```python
import functools

import numpy as np
import jax
import jax.numpy as jnp
from jax import lax
from jax.experimental import pallas as pl
from jax.experimental.pallas import tpu as pltpu

F32 = jnp.float32
BF16 = jnp.bfloat16

EPS = 1e-6
GRID_W = 64
HEAD = 128
H_AB = 4
H_C = 8
CONV_K = 5
N_EXPERTS = 16
N_GROUPS = 4
EXPERTS_PER_GROUP = N_EXPERTS // N_GROUPS
LANES = 128
SUBLANES = 8
VMEM_LIMIT = 56 * 1024 * 1024

TOK_TILE = 512
MOE_TILE = 512
ROW_DMA_UNROLL = 8
MLSTM_CHUNK = 128
GDN_CHUNK = 64
GDN_SCAN_CHUNKS = 4
GLA_CHUNK = 128
GDN_INV_BLOCK = 16


def _cparams(sem):
    return pltpu.CompilerParams(dimension_semantics=sem, vmem_limit_bytes=VMEM_LIMIT)


def _dot(a, b):
    return jnp.dot(a, b, preferred_element_type=F32)


def _dot_nt(a, b):
    return lax.dot_general(a, b, (((1,), (1,)), ((), ())), preferred_element_type=F32)


def _dot_tn(a, b):
    return lax.dot_general(a, b, (((0,), (0,)), ((), ())), preferred_element_type=F32)


def _split2(x):
    hi = x.astype(BF16)
    lo = (x - hi.astype(F32)).astype(BF16)
    return hi, lo


def _split3(x):
    hi = x.astype(BF16)
    r = x - hi.astype(F32)
    mid = r.astype(BF16)
    lo = (r - mid.astype(F32)).astype(BF16)
    return hi, mid, lo


def _dot_exact_lhs(m_bf16, x):
    hi, mid, lo = _split3(x)
    return _dot(m_bf16, hi) + _dot(m_bf16, mid) + _dot(m_bf16, lo)


def _dot3(a, b):
    ah, al = _split2(a)
    bh, bl = _split2(b)
    return _dot(ah, bh) + _dot(ah, bl) + _dot(al, bh)


def _dot_nt3(a, b):
    ah, al = _split2(a)
    bh, bl = _split2(b)
    return _dot_nt(ah, bh) + _dot_nt(ah, bl) + _dot_nt(al, bh)


def _sigmoid(z):
    return 1.0 / (1.0 + jnp.exp(-z))


def _sigmoid_pair(z):
    return _sigmoid(z), _sigmoid(-z)


def _silu(z):
    return z * _sigmoid(z)


def _mod_kernel(c_ref, w_ref, b_ref, o_ref):
    s = _silu(c_ref[...]).astype(BF16)
    o_ref[0] = _dot(s, w_ref[0].astype(BF16)) + b_ref[0]


def _modulation(cpad, w_mod, b_mod):
    depth, d, n = w_mod.shape
    tn = 1536
    return pl.pallas_call(
        _mod_kernel,
        grid=(depth, n // tn),
        in_specs=[pl.BlockSpec((SUBLANES, d), lambda l, j: (0, 0)),
                  pl.BlockSpec((1, d, tn), lambda l, j: (l, 0, j)),
                  pl.BlockSpec((1, 1, tn), lambda l, j: (l, 0, j))],
        out_specs=pl.BlockSpec((1, SUBLANES, tn), lambda l, j: (l, 0, j)),
        out_shape=jax.ShapeDtypeStruct((depth, SUBLANES, n), F32),
        compiler_params=_cparams(("parallel", "parallel")),
        name="modulation",
    )(cpad, w_mod, b_mod.reshape(depth, 1, n))


def _gate_activations(raw, par):
    y = raw + par[0:1, :]
    e = jnp.exp(-jnp.abs(y))
    sp = jnp.log1p(e)
    lsig = jnp.minimum(y, 0.0) - sp
    r = 1.0 / (1.0 + e)
    sig = jnp.where(y >= 0, r, e * r)
    gdec = -jnp.exp(par[1:2, :]) * (jnp.maximum(y, 0.0) + sp)
    lane = lax.broadcasted_iota(jnp.int32, raw.shape, 1)
    return jnp.where(lane < 4, y, jnp.where(lane < 8, lsig, jnp.where(lane < 12, sig, gdec)))


def _rows_copy(src, dst, sem, src_row, dst_row):
    return pltpu.make_async_copy(src.at[pl.ds(src_row, 1), :], dst.at[pl.ds(dst_row, 1), :], sem)


def _inproj_kernel(*refs, with_gates, p_cols, silu_cols, fused_combine):
    refs = list(refs)
    pos_ref = refs.pop(0) if fused_combine else None
    x_ref, gain_ref, sc_ref, sh_ref, w_ref = refs[:5]
    rest = refs[5:]
    if with_gates:
        wg_ref, gpar_ref = rest[:2]
        rest = rest[2:]
    if fused_combine:
        y_ref, g2_ref = rest[:2]
        rest = rest[2:]
    p_ref = rest.pop(0)
    if with_gates:
        g_ref = rest.pop(0)
    x = x_ref[...]
    if fused_combine:
        xs_ref, buf, sem = rest
        i = pl.program_id(0)
        n_steps = pl.num_programs(0)
        tm = x_ref.shape[0]
        slot = i % 2

        @pl.when(i == 0)
        def _():
            def issue(r, carry):
                _rows_copy(y_ref, buf.at[0], sem.at[0], pos_ref[r], r).start()
                return carry
            lax.fori_loop(0, tm, issue, 0, unroll=ROW_DMA_UNROLL)

        nxt = jnp.minimum(i + 1, n_steps - 1)
        for r in range(tm):
            _rows_copy(y_ref, buf.at[1 - slot], sem.at[1 - slot], pos_ref[nxt * tm + r], r).start()
        pltpu.make_async_copy(y_ref.at[pl.ds(0, tm), :], buf.at[slot], sem.at[slot]).wait()
        x = x + g2_ref[0] * buf[slot]
        xs_ref[...] = x

        @pl.when(i == n_steps - 1)
        def _():
            pltpu.make_async_copy(y_ref.at[pl.ds(0, tm), :], buf.at[1 - slot], sem.at[1 - slot]).wait()
    y = x * lax.rsqrt(jnp.mean(x * x, axis=-1, keepdims=True) + EPS) * gain_ref[...]
    u = (y * (1.0 + sc_ref[0]) + sh_ref[0]).astype(BF16)
    if with_gates:
        for d in range(2):
            g_ref[d] = _gate_activations(_dot(u, wg_ref[d]), gpar_ref[d])
    tn = 1024
    for j in range(p_cols // tn):
        res = _dot(u, w_ref[:, j * tn:(j + 1) * tn])
        if (j + 1) * tn <= silu_cols:
            res = _silu(res)
        p_ref[:, j * tn:(j + 1) * tn] = res


def _mod_row(i, tm, seq, nb):
    return jnp.minimum((i * tm) // seq, nb)


def _inproj(xs, gain, mod, layer, w_all, w_idx, p, seq, nb, wg=None, gpar=None, silu_cols=0, pending=None):
    t, d = xs.shape
    tm = TOK_TILE
    with_gates = wg is not None
    fused = pending is not None
    base = layer * SUBLANES * 6

    def mod_spec(which, base_=base):
        return pl.BlockSpec((1, 1, d), lambda i, *_: (base_ + _mod_row(i, tm, seq, nb) * 6 + which, 0, 0))

    in_specs = [pl.BlockSpec((tm, d), lambda i, *_: (i, 0)),
                pl.BlockSpec((1, d), lambda i, *_: (0, 0)),
                mod_spec(1), mod_spec(0),
                pl.BlockSpec((None, d, p), lambda i, *_: (w_idx, 0, 0), pipeline_mode=pl.Buffered(1))]
    args = [xs, gain.reshape(1, d), mod, mod, w_all]
    out_specs = [pl.BlockSpec((tm, p), lambda i, *_: (i, 0))]
    out_shape = [jax.ShapeDtypeStruct((t, p), F32)]
    scratch, prefetch = [], []
    if with_gates:
        in_specs += [pl.BlockSpec((2, d, LANES), lambda i, *_: (0, 0, 0)),
                     pl.BlockSpec((2, SUBLANES, LANES), lambda i, *_: (0, 0, 0))]
        args += [wg, gpar]
        out_specs.append(pl.BlockSpec((2, tm, LANES), lambda i, *_: (0, i, 0)))
        out_shape.append(jax.ShapeDtypeStruct((2, t, LANES), F32))
    if fused:
        y_sorted, pos, prev_layer = pending
        in_specs += [pl.BlockSpec(memory_space=pl.ANY), mod_spec(5, prev_layer * SUBLANES * 6)]
        args += [y_sorted, mod]
        out_specs.append(pl.BlockSpec((tm, d), lambda i, *_: (i, 0)))
        out_shape.append(jax.ShapeDtypeStruct((t, d), F32))
        scratch = [pltpu.VMEM((2, tm, d), F32), pltpu.SemaphoreType.DMA((2,))]
        prefetch = [pos]
    return pl.pallas_call(
        functools.partial(_inproj_kernel, with_gates=with_gates, p_cols=p, silu_cols=silu_cols, fused_combine=fused),
        grid_spec=pltpu.PrefetchScalarGridSpec(
            num_scalar_prefetch=len(prefetch), grid=(t // tm,),
            in_specs=in_specs, out_specs=out_specs, scratch_shapes=scratch),
        out_shape=out_shape,
        compiler_params=_cparams(("arbitrary",) if fused else ("parallel",)),
        name="inproj",
    )(*prefetch, *args)


def _gdn_prep_kernel(*refs, blocks_per_seq, n_lat_blocks, scale):
    w_ref, o_ref = refs[9], refs[10]
    i = pl.program_id(0)
    tm = o_ref.shape[0]
    w = refs[1].shape[1]
    in_lat = i < n_lat_blocks
    first = jnp.logical_or(jnp.logical_not(in_lat), i % blocks_per_seq == 0)
    last = jnp.logical_or(jnp.logical_not(in_lat), i % blocks_per_seq == blocks_per_seq - 1)
    pad = (CONV_K - 1) // 2
    for part in range(3):
        prev_ref, x_ref, next_ref = refs[3 * part:3 * part + 3]
        prev = jnp.where(first, 0.0, prev_ref[...])
        nxt = jnp.where(last, 0.0, next_ref[...])
        xe = jnp.concatenate([prev, x_ref[...], nxt], axis=0)
        n = xe.shape[0]
        acc = None
        for j in range(CONV_K):
            shift = (pad - j) % n
            xr = xe if shift == 0 else pltpu.roll(xe, shift, 0)
            term = xr[SUBLANES:SUBLANES + tm] * w_ref[j:j + 1, part * w:(part + 1) * w]
            acc = term if acc is None else acc + term
        y = _silu(acc)
        for h in range(H_AB):
            t = y[:, h * HEAD:(h + 1) * HEAD]
            if part < 2:
                t = t * lax.rsqrt(jnp.sum(t * t, axis=-1, keepdims=True) + EPS)
            if part == 0:
                t = t * scale
            o_ref[:, part * w + h * HEAD:part * w + (h + 1) * HEAD] = t


def _gdn_prep(p, conv_w, seq, n_ctx, nb):
    t = p.shape[0]
    tm = 256
    w = H_AB * HEAD
    first_col = 4
    wpad = jnp.zeros((SUBLANES, 3 * w), F32).at[:CONV_K].set(conv_w)
    hb = tm // SUBLANES
    nblk8 = t // SUBLANES
    in_specs = []
    for c in range(3):
        in_specs += [pl.BlockSpec((SUBLANES, w), lambda i, c=c: (jnp.maximum(i * hb - 1, 0), first_col + c)),
                     pl.BlockSpec((tm, w), lambda i, c=c: (i, first_col + c)),
                     pl.BlockSpec((SUBLANES, w), lambda i, c=c: (jnp.minimum((i + 1) * hb, nblk8 - 1), first_col + c))]
    in_specs.append(pl.BlockSpec((SUBLANES, 3 * w), lambda i: (0, 0)))
    return pl.pallas_call(
        functools.partial(_gdn_prep_kernel, blocks_per_seq=seq // tm, n_lat_blocks=nb * seq // tm,
                          scale=HEAD ** -0.5),
        grid=(t // tm,),
        in_specs=in_specs,
        out_specs=pl.BlockSpec((tm, 3 * w), lambda i: (i, 0)),
        out_shape=jax.ShapeDtypeStruct((t, 3 * w), F32),
        compiler_params=_cparams(("parallel",)),
        name="gdn_prep",
    )(*([p] * 9), wpad)


def _tri_tables(c):
    idx = np.arange(c)
    fwd = (idx[None, :] <= idx[:, None]).astype(np.float32)
    return jnp.asarray(np.stack([fwd, fwd.T]))


def _to_cm_kernel(x3_ref, x2_ref, o_ref, *, n_lat_tiles):
    i = pl.program_id(0)
    rows = x3_ref.shape[0]

    @pl.when(i < n_lat_tiles)
    def _():
        for wl in range(SUBLANES):
            o_ref[wl * rows:(wl + 1) * rows, :] = x3_ref[:, wl, :]

    @pl.when(i >= n_lat_tiles)
    def _():
        o_ref[...] = x2_ref[...]


def _from_cm_kernel(x_ref, o_ref):
    rows = o_ref.shape[0]
    for wl in range(SUBLANES):
        o_ref[:, wl, :] = x_ref[wl * rows:(wl + 1) * rows, :]


def _to_column_major(xs, n_lat, seq):
    t, d = xs.shape
    rows = seq // GRID_W
    tile = rows * SUBLANES
    assert rows % SUBLANES == 0 and n_lat % tile == 0 and (t - n_lat) % tile == 0
    n_lat_tiles = n_lat // tile
    per_b = GRID_W // SUBLANES
    lat = lambda i: jnp.minimum(i, n_lat_tiles - 1)
    return pl.pallas_call(
        functools.partial(_to_cm_kernel, n_lat_tiles=n_lat_tiles),
        grid=(t // tile,),
        in_specs=[pl.BlockSpec((rows, SUBLANES, d), lambda i: (lat(i) // per_b, lat(i) % per_b, 0)),
                  pl.BlockSpec((tile, d), lambda i: (jnp.maximum(i, n_lat_tiles - 1), 0))],
        out_specs=pl.BlockSpec((tile, d), lambda i: (i, 0)),
        out_shape=jax.ShapeDtypeStruct((t, d), xs.dtype),
        compiler_params=_cparams(("parallel",)),
        name="to_column_major",
    )(xs.reshape(t // GRID_W, GRID_W, d), xs)


def _from_column_major(y, seq):
    n_lat, d = y.shape
    rows = seq // GRID_W
    tile = rows * SUBLANES
    per_b = GRID_W // SUBLANES
    out = pl.pallas_call(
        _from_cm_kernel,
        grid=(n_lat // tile,),
        in_specs=[pl.BlockSpec((tile, d), lambda i: (i, 0))],
        out_specs=pl.BlockSpec((rows, SUBLANES, d), lambda i: (i // per_b, i % per_b, 0)),
        out_shape=jax.ShapeDtypeStruct((n_lat // GRID_W, GRID_W, d), y.dtype),
        compiler_params=_cparams(("parallel",)),
        name="from_column_major",
    )(y)
    return out.reshape(n_lat, d)


def _mlstm_kernel(qf, kf, vf, qb, kb, vb, gf, gb, tri_ref, hf_ref, hb_ref, s_scr, m_scr, *, scale):
    @pl.when(pl.program_id(1) == 0)
    def _():
        s_scr[...] = jnp.zeros_like(s_scr)
        m_scr[...] = jnp.zeros_like(m_scr)

    c = gf.shape[1]
    ones_col = (lax.broadcasted_iota(jnp.int32, (c, HEAD), 1) == 0).astype(BF16)
    qkv = ((qf, kf, vf), (qb, kb, vb))
    h_out = (hf_ref, hb_ref)
    g = [gf[0], gb[0]]
    mask = [tri_ref[d] > 0.5 for d in range(2)]
    b = [_dot_exact_lhs(tri_ref[d].astype(BF16), g[d]) for d in range(2)]
    total = [jnp.sum(x, axis=0, keepdims=True) for x in g]
    g_t = [x.T for x in g]
    b_t = [x.T for x in b]
    chains = [(d, h) for d in range(2) for h in range(H_AB)]
    sl = [slice(h * HEAD, (h + 1) * HEAD) for d, h in chains]
    q = [qkv[d][0][:, sl[i]].astype(BF16) for i, (d, h) in enumerate(chains)]
    k = [qkv[d][1][:, sl[i]] * scale for i, (d, h) in enumerate(chains)]
    v_aug = [jnp.concatenate([qkv[d][2][:, sl[i]].astype(BF16), ones_col], axis=1) for i, (d, h) in enumerate(chains)]
    s = [s_scr[d, h] for d, h in chains]
    qk = [_dot_nt(q[i], k[i].astype(BF16)) for i in range(len(chains))]
    qs = [_dot(q[i], s[i].astype(BF16)) for i in range(len(chains))]
    w_intra, w_inter, m_ts, colvs, m_sts, tots = [], [], [], [], [], []
    for i, (d, h) in enumerate(chains):
        li_c, li_r = g[d][:, h:h + 1], g_t[d][h:h + 1, :]
        b_c, b_r = b[d][:, 4 + h:5 + h], b_t[d][4 + h:5 + h, :]
        tot = total[d][:, 4 + h:5 + h]
        m_st = m_scr[d, h][0:1, 0:1]
        d_log = jnp.where(mask[d], b_c - b_r + li_r, -jnp.inf)
        m_inter = b_c + m_st
        m_t = jnp.maximum(m_inter, jnp.max(d_log, axis=-1, keepdims=True))
        w_inter.append(jnp.exp(m_inter - m_t))
        w_intra.append((jnp.exp(d_log - m_t) * qk[i]).astype(BF16))
        m_ts.append(m_t); colvs.append(tot - b_c + li_c); m_sts.append(m_st); tots.append(tot)
    intra = [_dot(w_intra[i], v_aug[i]) for i in range(len(chains))]
    m_new = [jnp.maximum(tots[i] + m_sts[i], jnp.max(colvs[i], axis=0, keepdims=True)) for i in range(len(chains))]
    upd = [_dot_tn((k[i] * jnp.exp(colvs[i] - m_new[i])).astype(BF16), v_aug[i]) for i in range(len(chains))]
    for i, (d, h) in enumerate(chains):
        num = w_inter[i] * qs[i] + intra[i]
        den = num[:, HEAD:HEAD + 1]
        h_out[d][0, :, sl[i]] = num[:, :HEAD] / jnp.maximum(jnp.abs(den), jnp.exp(-m_ts[i]))
        s_scr[d, h] = jnp.exp(tots[i] + m_sts[i] - m_new[i]) * s[i] + upd[i]
        m_scr[d, h] = jnp.broadcast_to(m_new[i], (SUBLANES, LANES))


def _step_block(chunk, seq, n_ctx, nb):
    n_c, n_l = n_ctx // chunk, seq // chunk
    off = nb * n_l

    def fn(b, d, i):
        j = i - n_c
        ctx_blk = off + b * n_c + (i if d == 0 else n_c - 1 - i)
        lat_blk = b * n_l + (j if d == 0 else n_l - 1 - j)
        return jnp.where(i < n_c, ctx_blk, lat_blk)

    return n_c + n_l, fn


def _mlstm_scan(p, gates, seq, n_ctx, nb):
    t = p.shape[0]
    c = MLSTM_CHUNK
    w = H_AB * HEAD
    n_steps, blk = _step_block(c, seq, n_ctx, nb)
    in_specs = [pl.BlockSpec((c, w), lambda b, i, d=d, col=col: (blk(b, d, i), col)) for d in range(2) for col in range(3)]
    in_specs += [pl.BlockSpec((1, c, LANES), lambda b, i, d=d: (d, blk(b, d, i), 0)) for d in range(2)]
    in_specs.append(pl.BlockSpec((2, c, c), lambda b, i: (0, 0, 0)))
    return pl.pallas_call(
        functools.partial(_mlstm_kernel, scale=HEAD ** -0.5),
        grid=(nb, n_steps),
        in_specs=in_specs,
        out_specs=[pl.BlockSpec((1, c, w), lambda b, i, d=d: (0, blk(b, d, i), 0)) for d in range(2)],
        out_shape=[jax.ShapeDtypeStruct((1, t, w), F32)] * 2,
        scratch_shapes=[pltpu.VMEM((2, H_AB, HEAD, 2 * HEAD), F32), pltpu.VMEM((2, H_AB, SUBLANES, LANES), F32)],
        compiler_params=_cparams(("parallel", "arbitrary")),
        name="mlstm_scan",
    )(p, p, p, p, p, p, gates, gates, _tri_tables(c))


def _gdn_inv_masks(c):
    idx = np.arange(c)
    blk = lambda n: (idx[:, None] // n) == (idx[None, :] // n)
    levels = [blk(GDN_INV_BLOCK)]
    n = GDN_INV_BLOCK
    while n < c:
        levels.append(np.logical_and(blk(2 * n), np.logical_not(blk(n))))
        n *= 2
    levels = np.stack(levels).astype(np.float32)
    return jnp.asarray(np.tile(levels, (1, 1, H_AB)))


def _block_diag(x, n_blocks):
    blk = lax.broadcasted_iota(jnp.int32, x.shape, 1) // (x.shape[1] // n_blocks)
    return jnp.concatenate([jnp.where(blk == j, x, jnp.zeros_like(x)) for j in range(n_blocks)], axis=0)


def _heads_to_lanes(x, first, width):
    return jnp.concatenate([jnp.broadcast_to(x[:, first + h:first + h + 1], (x.shape[0], width))
                            for h in range(H_AB)], axis=1)


def _unit_tri_inverse(a_list, eye, inv_masks):
    prod = lambda xs, ys: [_dot(x.astype(BF16), _block_diag(y.astype(BF16), H_AB)) for x, y in zip(xs, ys)]
    pw = [a * inv_masks[0] for a in a_list]
    inv = [eye - p for p in pw]
    n = 2
    while n < GDN_INV_BLOCK:
        pw = prod(pw, pw)
        inv = [i + t for i, t in zip(inv, prod(inv, pw))]
        n *= 2
    for lvl in range(1, inv_masks.shape[0]):
        t = prod(prod(inv, [a * inv_masks[lvl] for a in a_list]), inv)
        inv = [i - x for i, x in zip(inv, t)]
    return inv


def _gdn_chunk_kernel(q_ref, k_ref, v_ref, g_ref, tri_ref, im_ref, u_ref, w_ref, qd_ref, kd_ref, att_ref):
    c = GDN_CHUNK
    n_chunks = q_ref.shape[0] // c
    inv_masks = im_ref[...]
    row = lax.broadcasted_iota(jnp.int32, (c, H_AB * c), 0)
    col = lax.broadcasted_iota(jnp.int32, (c, H_AB * c), 1) % c
    eye_b = row == col
    eye = eye_b.astype(F32)
    masks = [col <= row, col >= row]
    stricts = [col < row, col > row]
    tris = [tri_ref[d].astype(BF16) for d in range(2)]
    ones = jnp.ones((c, c), BF16)
    chunks = range(n_chunks)
    rows = [slice(ci * c, (ci + 1) * c) for ci in chunks]
    groups = [(ci, d) for ci in chunks for d in range(2)]
    k_hl = [_split2(k_ref[r, :]) for r in rows]
    kbd = [(_block_diag(hi, H_AB), _block_diag(lo, H_AB)) for hi, lo in k_hl]
    r_hi = [_dot_nt(jnp.concatenate([k_hl[ci][0], k_hl[ci][1], q_ref[rows[ci], :].astype(BF16)], axis=0), kbd[ci][0])
            for ci in chunks]
    r_lo = [_dot_nt(k_hl[ci][0], kbd[ci][1]) for ci in chunks]
    kk = [r_hi[ci][:c] + r_hi[ci][c:2 * c] + r_lo[ci] for ci in chunks]
    g = [g_ref[d, rows[ci], :] for ci, d in groups]
    gc = [_dot_exact_lhs(tris[d], g[i]) for i, (ci, d) in enumerate(groups)]
    gc_c = [_heads_to_lanes(x, 12, c) for x in gc]
    gc_r = [_dot_exact_lhs(ones, eye * x) for x in gc_c]
    decay = [jnp.exp(jnp.where(masks[d], gc_c[i] - gc_r[i], -jnp.inf)) for i, (ci, d) in enumerate(groups)]
    a = [jnp.where(stricts[d], _heads_to_lanes(g[i], 8, c) * kk[ci] * decay[i], 0.0)
         for i, (ci, d) in enumerate(groups)]
    inv = _unit_tri_inverse(a, eye, inv_masks)
    inv_hl = [jnp.concatenate(_split2(x), axis=0) for x in inv]
    gc_w = [_heads_to_lanes(x, 12, HEAD) for x in gc]
    eg = [jnp.exp(x) for x in gc_w]
    beta = [_heads_to_lanes(x, 8, HEAD) for x in g]
    res_u = [_dot(inv_hl[i], _block_diag((v_ref[rows[ci], :] * beta[i]).astype(BF16), H_AB))
             for i, (ci, d) in enumerate(groups)]
    res_w = [_dot(inv_hl[i], _block_diag((k_ref[rows[ci], :] * (beta[i] * eg[i])).astype(BF16), H_AB))
             for i, (ci, d) in enumerate(groups)]
    for i, (ci, d) in enumerate(groups):
        r = rows[ci]
        total = _heads_to_lanes(jnp.sum(g[i], axis=0, keepdims=True), 12, HEAD)
        u_ref[d, r, :] = (res_u[i][:c] + res_u[i][c:]).astype(BF16)
        w_ref[d, r, :] = (res_w[i][:c] + res_w[i][c:]).astype(BF16)
        qd_ref[d, r, :] = (q_ref[r, :] * eg[i]).astype(BF16)
        kd_ref[d, r, :] = (k_ref[r, :] * jnp.exp(total - gc_w[i])).astype(BF16)
        att_ref[d, r, :] = (r_hi[ci][2 * c:] * decay[i]).astype(BF16)


def _gdn_scan_kernel(*refs):
    ins, o_refs, s_scr = refs[:12], refs[12:14], refs[14]
    c = GDN_CHUNK

    @pl.when(pl.program_id(1) == 0)
    def _():
        s_scr[...] = jnp.zeros_like(s_scr)

    pw = 2 * HEAD
    rblk = lax.broadcasted_iota(jnp.int32, (pw, pw), 0) // HEAD
    cblk = lax.broadcasted_iota(jnp.int32, (pw, pw), 1) // HEAD
    on_diag = rblk == cblk
    chains = [(d, pr) for d in range(2) for pr in range(H_AB // 2)]
    refs_of = lambda d: ins[d::2]
    sl = [slice(pr * pw, (pr + 1) * pw) for d, pr in chains]
    s = [s_scr[d, pr] for d, pr in chains]
    n_sub = o_refs[0].shape[1] // c
    for k in range(n_sub):
        rows = [slice(kk * c, (kk + 1) * c) for kk in (k, n_sub - 1 - k)]
        res = [_dot(jnp.concatenate([refs_of(d)[1][0, rows[d], sl[i]], refs_of(d)[2][0, rows[d], sl[i]]], axis=0),
                    s[i].astype(BF16)) for i, (d, pr) in enumerate(chains)]
        v_new = [(refs_of(d)[0][0, rows[d], sl[i]].astype(F32) - res[i][:c]).astype(BF16)
                 for i, (d, pr) in enumerate(chains)]
        intra = [_dot(refs_of(d)[4][0, rows[d], pr * 2 * c:(pr + 1) * 2 * c], _block_diag(v_new[i], 2))
                 for i, (d, pr) in enumerate(chains)]
        upd = [_dot_tn(refs_of(d)[3][0, rows[d], sl[i]], v_new[i]) for i, (d, pr) in enumerate(chains)]
        totals = [jnp.sum(refs_of(d)[5][0, rows[d], :], axis=0, keepdims=True) for d in range(2)]
        for i, (d, pr) in enumerate(chains):
            o_refs[d][0, rows[d], sl[i]] = res[i][c:] + intra[i]
            gl = jnp.concatenate([jnp.broadcast_to(jnp.exp(totals[d][:, 12 + 2 * pr + j:13 + 2 * pr + j]), (1, HEAD))
                                  for j in range(2)], axis=1)
            s[i] = gl * s[i] + jnp.where(on_diag, upd[i], 0.0)
    for i, (d, pr) in enumerate(chains):
        s_scr[d, pr] = s[i]


def _gdn_scan(qkv, gates, seq, n_ctx, nb):
    t = qkv.shape[0]
    c = GDN_CHUNK
    w = H_AB * HEAD
    tri = _tri_tables(c)
    im = _gdn_inv_masks(c)
    tm = TOK_TILE
    sds = lambda width: jax.ShapeDtypeStruct((2, t, width), BF16)
    u, wv, qd, kd, att = pl.pallas_call(
        _gdn_chunk_kernel,
        grid=(t // tm,),
        in_specs=[pl.BlockSpec((tm, w), lambda i, col=col: (i, col)) for col in range(3)] + [
            pl.BlockSpec((2, tm, LANES), lambda i: (0, i, 0)),
            pl.BlockSpec((2, c, c), lambda i: (0, 0, 0)),
            pl.BlockSpec(im.shape, lambda i: (0, 0, 0))],
        out_specs=[pl.BlockSpec((2, tm, w), lambda i: (0, i, 0))] * 4
        + [pl.BlockSpec((2, tm, H_AB * c), lambda i: (0, i, 0))],
        out_shape=[sds(w)] * 4 + [sds(H_AB * c)],
        compiler_params=_cparams(("parallel",)),
        name="gdn_chunk",
    )(qkv, qkv, qkv, gates, tri, im)

    cs = c * GDN_SCAN_CHUNKS
    n_steps, blk = _step_block(cs, seq, n_ctx, nb)
    in_specs, args = [], []
    for arr, width in ((u, w), (wv, w), (qd, w), (kd, w), (att, H_AB * c), (gates, LANES)):
        for d in range(2):
            in_specs.append(pl.BlockSpec((1, cs, width), lambda b, i, d=d: (d, blk(b, d, i), 0)))
            args.append(arr)
    o_f, o_b = pl.pallas_call(
        _gdn_scan_kernel,
        grid=(nb, n_steps),
        in_specs=in_specs,
        out_specs=[pl.BlockSpec((1, cs, w), lambda b, i, d=d: (0, blk(b, d, i), 0)) for d in range(2)],
        out_shape=[jax.ShapeDtypeStruct((1, t, w), F32)] * 2,
        scratch_shapes=[pltpu.VMEM((2, H_AB // 2, 2 * HEAD, 2 * HEAD), F32)],
        compiler_params=_cparams(("parallel", "arbitrary")),
        name="gdn_scan",
    )(*args)
    return o_f, o_b


def _gla_tables(c):
    idx = np.arange(c)
    sel, sgn, msk = [], [], []
    for d in range(2):
        sel_d, sgn_d, msk_d = [], [], []
        bs = c // 2
        while bs >= 1:
            pair = idx // (2 * bs)
            second = (idx // bs) % 2 == 1
            ref_row = pair * 2 * bs + (bs - 1 if d == 0 else bs)
            sel_d.append((idx[None, :] == ref_row[:, None]).astype(np.float32))
            qside = second if d == 0 else np.logical_not(second)
            sgn_d.append(np.where(qside, 1.0, -1.0)[:, None] * np.ones((1, HEAD)))
            same_pair = pair[:, None] == pair[None, :]
            msk_d.append(np.logical_and(same_pair, np.logical_and(qside[:, None], np.logical_not(qside)[None, :])))
            bs //= 2
        msk_d.append(np.eye(c, dtype=bool))
        sel.append(np.concatenate(sel_d, axis=0))
        sgn.append(np.stack(sgn_d))
        msk.append(np.stack(msk_d).astype(np.float32))
    return (jnp.asarray(np.stack(sel), dtype=BF16), jnp.asarray(np.stack(sgn), dtype=F32),
            jnp.asarray(np.stack(msk), dtype=F32))


def _gla_kernel(qf_ref, zf_ref, vf_ref, qb_ref, zb_ref, vb_ref, fb_ref, lb_ref, tri_ref, sel_ref, sgn_ref, msk_ref,
                of_ref, ob_ref, s_scr):
    @pl.when(pl.program_id(1) == 0)
    def _():
        s_scr[...] = jnp.zeros_like(s_scr)

    c = qf_ref.shape[0]
    n_lvl = sgn_ref.shape[1]
    dirs = range(2)
    qzv = ((qf_ref, zf_ref, vf_ref), (qb_ref, zb_ref, vb_ref))
    o_refs = (of_ref, ob_ref)
    chains = [(d, h) for d in dirs for h in range(H_C)]
    sl = [slice(h * HEAD, (h + 1) * HEAD) for d, h in chains]
    g_all, key_all, q_all = [], [], []
    for d in dirs:
        lb = lb_ref[d]
        sig_pos, sig_neg = _sigmoid_pair(qzv[d][1][...] + fb_ref[d])
        g_all.append(jnp.log(lb + (1.0 - lb) * sig_pos))
        key_all.append((1.0 - lb) * sig_neg)
        q_all.append(qzv[d][0][...])
    gc_all = [_dot_exact_lhs(tri_ref[d].astype(BF16), g_all[d]) for d in dirs]
    total_all = [jnp.sum(g, axis=0, keepdims=True) for g in g_all]
    refs = [_dot(sel_ref[d], gc_all[d].astype(BF16)) for d in dirs]
    diag = msk_ref[0, n_lvl] > 0.5
    att = [jnp.where(diag, _dot_nt(q_all[d][:, sl[i]].astype(BF16), key_all[d][:, sl[i]].astype(BF16)), 0.0)
           for i, (d, h) in enumerate(chains)]
    for l in range(n_lvl):
        sgn = [sgn_ref[d, l] for d in dirs]
        x = [(jnp.where(sgn[d] > 0, q_all[d][:, sl[i]], key_all[d][:, sl[i]])
              * jnp.exp(sgn[d] * (gc_all[d][:, sl[i]] - refs[d][l * c:(l + 1) * c, sl[i]]))).astype(BF16)
             for i, (d, h) in enumerate(chains)]
        prod = [_dot_nt(xh, xh) for xh in x]
        in_level = [msk_ref[d, l] > 0.5 for d in dirs]
        att = [jnp.where(in_level[d], prod[i], att[i]) for i, (d, h) in enumerate(chains)]
    st = [s_scr[d, h] for d, h in chains]
    inter = [_dot_nt((q_all[d][:, sl[i]] * jnp.exp(gc_all[d][:, sl[i]])).astype(BF16), st[i].astype(BF16))
             for i, (d, h) in enumerate(chains)]
    intra = [_dot(att[i].astype(BF16), qzv[d][2][:, sl[i]].astype(BF16)) for i, (d, h) in enumerate(chains)]
    upd = [_dot_tn(qzv[d][2][:, sl[i]].astype(BF16),
                   (key_all[d][:, sl[i]] * jnp.exp(total_all[d][:, sl[i]] - gc_all[d][:, sl[i]])).astype(BF16))
           for i, (d, h) in enumerate(chains)]
    for i, (d, h) in enumerate(chains):
        o_refs[d][0, :, sl[i]] = inter[i] + intra[i]
        s_scr[d, h] = jnp.exp(total_all[d][:, sl[i]]) * st[i] + upd[i]


def _gla_scan(p, f_bias, lb, seq, n_ctx, nb):
    t, pw = p.shape
    c = GLA_CHUNK
    w = H_C * HEAD
    tri = _tri_tables(c)
    sel, sgn, msk = _gla_tables(c)
    n_steps, blk = _step_block(c, seq, n_ctx, nb)
    dspec = lambda d, col: pl.BlockSpec((c, w), lambda b, i: (blk(b, d, i), col))
    whole = lambda a: pl.BlockSpec(a.shape, lambda b, i: (0,) * a.ndim)
    fb3, lb3 = f_bias.reshape(2, 1, w), lb.reshape(2, 1, w)
    consts = (fb3, lb3, tri, sel, sgn, msk)
    return pl.pallas_call(
        _gla_kernel,
        grid=(nb, n_steps),
        in_specs=[dspec(0, 0), dspec(0, 1), dspec(0, 3), dspec(1, 0), dspec(1, 2), dspec(1, 3)]
        + [whole(a) for a in consts],
        out_specs=[pl.BlockSpec((1, c, w), lambda b, i, d=d: (0, blk(b, d, i), 0)) for d in range(2)],
        out_shape=[jax.ShapeDtypeStruct((1, t, w), F32)] * 2,
        scratch_shapes=[pltpu.VMEM((2, H_C, HEAD, HEAD), F32)],
        compiler_params=_cparams(("parallel", "arbitrary")),
        name="gla_scan",
    )(p, p, p, p, p, p, *consts)


def _best_group(scores_t, bias):
    sel = [scores_t[e:e + 1, :] + bias[e:e + 1, :] for e in range(N_EXPERTS)]
    gscore = []
    for grp in range(N_GROUPS):
        v = sel[grp * EXPERTS_PER_GROUP:(grp + 1) * EXPERTS_PER_GROUP]
        best = None
        for a in range(EXPERTS_PER_GROUP):
            for b in range(a + 1, EXPERTS_PER_GROUP):
                pair = v[a] + v[b]
                best = pair if best is None else jnp.maximum(best, pair)
        gscore.append(best)
    best_g = jnp.zeros(gscore[0].shape, jnp.int32)
    best_v = gscore[0]
    for grp in range(1, N_GROUPS):
        better = gscore[grp] > best_v
        best_g = jnp.where(better, grp, best_g)
        best_v = jnp.where(better, gscore[grp], best_v)
    return best_g


def _outproj_kernel(x_ref, of0, ob0, gt0, of1, ob1, gt1, hn_ref, w_ref, g1_ref, sc_ref, sh_ref, nf_ref,
                    wr_ref, rb_ref, triu_ref, xo_ref, pos_ref, cnt_ref, hs_ref,
                    base_scr, h_scr, pos_v, pos_s, cnt_v, cnt_s, row_sem, meta_sem, *, first_half_sigmoid, cap):
    i = pl.program_id(0)
    n_steps = pl.num_programs(0)
    tm = x_ref.shape[0]
    slot = i % 2
    prev = 1 - slot
    dump = N_GROUPS * cap

    @pl.when(i == 0)
    def _():
        base_scr[...] = jnp.zeros_like(base_scr)
        h_scr[...] = jnp.zeros_like(h_scr)

        def init(r, carry):
            pos_s[1, r] = dump + r
            return carry
        lax.fori_loop(0, tm, init, 0)

    @pl.when(i > 0)
    def _():
        pltpu.make_async_copy(pos_v, pos_s.at[pl.ds(prev, 1), :], meta_sem).wait()

    for r in range(tm):
        _rows_copy(h_scr.at[prev], hs_ref, row_sem, r, pos_s[prev, r]).start()

    feats = []
    for half, (of, ob, gt) in enumerate(((of0, ob0, gt0), (of1, ob1, gt1))):
        o = of[0] + ob[0]
        gate = gt[...]
        for h in range(o.shape[1] // HEAD):
            sl = slice(h * HEAD, (h + 1) * HEAD)
            t = o[:, sl]
            y = t * lax.rsqrt(jnp.mean(t * t, axis=-1, keepdims=True) + EPS)
            y = y * hn_ref[:, half * o.shape[1] + h * HEAD: half * o.shape[1] + (h + 1) * HEAD]
            gz = gate[:, sl]
            act = _sigmoid(gz) if (half == 0 and first_half_sigmoid) else _silu(gz)
            feats.append((y * act).astype(BF16))
    feats = jnp.concatenate(feats, axis=1)
    xn = x_ref[...] + g1_ref[0] * _dot(feats, w_ref[...])
    xo_ref[...] = xn
    y = xn * lax.rsqrt(jnp.mean(xn * xn, axis=-1, keepdims=True) + EPS) * nf_ref[...]
    hl = y * (1.0 + sc_ref[0]) + sh_ref[0]
    logits = _dot3(hl, wr_ref[...])
    scores_t = _sigmoid(logits).T
    best_g = _best_group(scores_t, rb_ref[...])
    onehot = jnp.concatenate([(best_g == g).astype(F32) for g in range(N_GROUPS)]
                             + [jnp.zeros((SUBLANES - N_GROUPS, tm), F32)], axis=0)
    before = _dot(onehot.astype(BF16), triu_ref[...])
    base = base_scr[...]
    rank = jnp.sum(onehot * (before + base[:, 0:1]), axis=0, keepdims=True)
    pos = best_g * cap + rank.astype(jnp.int32)
    pos_ref[0] = pos
    base = base + jnp.sum(onehot, axis=1, keepdims=True)
    base_scr[...] = base
    cnt_ref[...] = base

    pltpu.make_async_copy(h_scr.at[prev], hs_ref.at[pl.ds(0, tm), :], row_sem).wait()
    h_scr[slot] = hl
    pos_v[...] = pos
    pltpu.make_async_copy(pos_v, pos_s.at[pl.ds(slot, 1), :], meta_sem).start()

    @pl.when(i == n_steps - 1)
    def _():
        pltpu.make_async_copy(pos_v, pos_s.at[pl.ds(slot, 1), :], meta_sem).wait()

        def issue(r, carry):
            _rows_copy(h_scr.at[slot], hs_ref, row_sem, r, pos_s[slot, r]).start()
            return carry
        lax.fori_loop(0, tm, issue, 0, unroll=ROW_DMA_UNROLL)
        pltpu.make_async_copy(h_scr.at[slot], hs_ref.at[pl.ds(0, tm), :], row_sem).wait()
        cnt_v[...] = base.astype(jnp.int32)
        counts = pltpu.make_async_copy(cnt_v, cnt_s, meta_sem)
        counts.start()
        counts.wait()
        h_scr[prev] = jnp.zeros((tm, h_scr.shape[2]), F32)
        zero = h_scr.at[prev]

        def pad_copies(action):
            for g in range(N_GROUPS):
                cnt = cnt_s[g, 0]
                up = (cnt + SUBLANES - 1) // SUBLANES * SUBLANES
                for r in range(SUBLANES - 1):
                    @pl.when(cnt + r < up)
                    def _():
                        action(_rows_copy(zero, hs_ref, row_sem, 0, g * cap + cnt + r))
                action(pltpu.make_async_copy(zero, hs_ref.at[pl.ds(pl.multiple_of(g * cap + up, SUBLANES), tm), :],
                                             row_sem))

        pad_copies(lambda cp: cp.start())
        pad_copies(lambda cp: cp.wait())


def _outproj(xs, o_halves, gate_src, gate_cols, hn_gain, w_out, w_idx, mod, layer, nf_gain, wr_pad, rb_col,
             seq, nb, n_rows, first_half_sigmoid):
    t, d = xs.shape
    tm = TOK_TILE
    half = d // 2
    base = layer * SUBLANES * 6

    def mod_spec(which):
        return pl.BlockSpec((1, 1, d), lambda i: (base + _mod_row(i, tm, seq, nb) * 6 + which, 0, 0))

    in_specs = [pl.BlockSpec((tm, d), lambda i: (i, 0))]
    args = [xs]
    for ((arr_f, dir_f), (arr_b, dir_b), col), gcol in zip(o_halves, gate_cols):
        in_specs += [pl.BlockSpec((1, tm, half), lambda i, col=col, dd=dir_f: (dd, i, col)),
                     pl.BlockSpec((1, tm, half), lambda i, col=col, dd=dir_b: (dd, i, col)),
                     pl.BlockSpec((tm, half), lambda i, gcol=gcol: (i, gcol))]
        args += [arr_f, arr_b, gate_src]
    in_specs += [pl.BlockSpec((1, d), lambda i: (0, 0)),
                 pl.BlockSpec((None, d, d), lambda i: (w_idx, 0, 0)),
                 mod_spec(2), mod_spec(4), mod_spec(3),
                 pl.BlockSpec((1, d), lambda i: (0, 0)),
                 pl.BlockSpec((d, LANES), lambda i: (0, 0)),
                 pl.BlockSpec((N_EXPERTS, 1), lambda i: (0, 0)),
                 pl.BlockSpec((tm, tm), lambda i: (0, 0))]
    idx = np.arange(tm)
    triu = jnp.asarray(idx[:, None] < idx[None, :], dtype=BF16)
    args += [hn_gain.reshape(1, d), w_out, mod, mod, mod, nf_gain.reshape(1, d), wr_pad, rb_col, triu]
    n_tiles = n_rows // tm
    assert tm == MOE_TILE
    cap = n_rows + MOE_TILE
    return pl.pallas_call(
        functools.partial(_outproj_kernel, first_half_sigmoid=first_half_sigmoid, cap=cap),
        grid=(n_tiles,),
        in_specs=in_specs,
        out_specs=[pl.BlockSpec((tm, d), lambda i: (i, 0)),
                   pl.BlockSpec((1, 1, tm), lambda i: (i, 0, 0)),
                   pl.BlockSpec((SUBLANES, LANES), lambda i: (0, 0)),
                   pl.BlockSpec(memory_space=pl.ANY)],
        out_shape=[jax.ShapeDtypeStruct((t, d), F32), jax.ShapeDtypeStruct((n_tiles, 1, tm), jnp.int32),
                   jax.ShapeDtypeStruct((SUBLANES, LANES), F32),
                   jax.ShapeDtypeStruct((N_GROUPS * cap + tm, d), F32)],
        scratch_shapes=[pltpu.VMEM((SUBLANES, LANES), F32), pltpu.VMEM((2, tm, d), F32),
                        pltpu.VMEM((1, tm), jnp.int32), pltpu.SMEM((2, tm), jnp.int32),
                        pltpu.VMEM((SUBLANES, LANES), jnp.int32), pltpu.SMEM((SUBLANES, LANES), jnp.int32),
                        pltpu.SemaphoreType.DMA(()), pltpu.SemaphoreType.DMA(())],
        compiler_params=_cparams(("arbitrary",)),
        name="outproj",
    )(*args)


def _group_gates(x, wr, bias_row, grp):
    scores = _sigmoid(_dot3(x, wr))
    lane_i = lax.broadcasted_iota(jnp.int32, scores.shape, 1)
    lane = lane_i.astype(F32)
    m = jnp.where(lane_i // EXPERTS_PER_GROUP == grp, scores + bias_row, -jnp.inf)
    picks = []
    for _ in range(2):
        top = jnp.max(m, axis=-1, keepdims=True)
        idx = jnp.min(jnp.where(m == top, lane, float(LANES)), axis=-1, keepdims=True)
        picks.append(idx)
        m = jnp.where(lane == idx, -jnp.inf, m)
    w = [jnp.sum(jnp.where(lane == idx, scores, 0.0), axis=-1, keepdims=True) for idx in picks]
    tot = w[0] + w[1]
    return jnp.where(lane == picks[0], w[0] / tot, jnp.where(lane == picks[1], w[1] / tot, 0.0))


def _group_ffn_kernel(tg_ref, tb_ref, nv_ref, x_ref, wr_ref, rb_ref, w1_ref, w3_ref, w2_ref, o_ref):
    i = pl.program_id(0)

    @pl.when(i < nv_ref[0])
    def _():
        grp = tg_ref[i]
        x = x_ref[...]
        xb = x.astype(BF16)
        gate = _group_gates(x, wr_ref[...], rb_ref[...], grp)
        lane = lax.broadcasted_iota(jnp.int32, gate.shape, 1)
        acc = None
        for j in range(EXPERTS_PER_GROUP):
            gcol = jnp.sum(jnp.where(lane == grp * EXPERTS_PER_GROUP + j, gate, 0.0), axis=-1, keepdims=True)
            a = _silu(_dot(xb, w1_ref[j])) * _dot(xb, w3_ref[j]) * gcol
            y = _dot(a.astype(BF16), w2_ref[j])
            acc = y if acc is None else acc + y
        o_ref[...] = acc


def _group_ffn(h_sorted, tile_grp, tile_blk, n_valid, wr_pad, rb_row, w1, w3, w2, cap, layer):
    p_rows, d = h_sorted.shape
    tm = MOE_TILE
    f = w1.shape[2]
    n_tiles = tile_grp.shape[0]
    blocks_per_group = cap // tm
    row_blk = lambda i, tg, tb, nv: (tg[i] * blocks_per_group + tb[i], 0)
    experts = lambda i, tg, tb, nv: (layer * N_GROUPS + tg[i], 0, 0)
    w_in_spec = pl.BlockSpec((EXPERTS_PER_GROUP, d, f), experts)
    return pl.pallas_call(
        _group_ffn_kernel,
        grid_spec=pltpu.PrefetchScalarGridSpec(
            num_scalar_prefetch=3, grid=(n_tiles,),
            in_specs=[pl.BlockSpec((tm, d), row_blk),
                      pl.BlockSpec((d, LANES), lambda i, tg, tb, nv: (0, 0)),
                      pl.BlockSpec((1, LANES), lambda i, tg, tb, nv: (0, 0)),
                      w_in_spec, w_in_spec, pl.BlockSpec((EXPERTS_PER_GROUP, f, d), experts)],
            out_specs=pl.BlockSpec((tm, d), row_blk)),
        out_shape=jax.ShapeDtypeStruct((p_rows, d), F32),
        compiler_params=_cparams(("arbitrary",)),
        name="moe_group_ffn",
    )(tile_grp, tile_blk, n_valid, h_sorted, wr_pad, rb_row, w1, w3, w2)


def _combine_kernel(pos_ref, x_ref, y_ref, g2_ref, nfin_ref, o_ref, buf, sem, *, final_norm):
    i = pl.program_id(0)
    tm = x_ref.shape[0]
    slot = i % 2

    def gather(tile, slot_):
        def issue(r, carry):
            _rows_copy(y_ref, buf.at[slot_], sem.at[slot_], pos_ref[tile * tm + r], r).start()
            return carry
        lax.fori_loop(0, tm, issue, 0, unroll=ROW_DMA_UNROLL)

    @pl.when(i == 0)
    def _():
        gather(0, 0)

    @pl.when(i + 1 < pl.num_programs(0))
    def _():
        gather(i + 1, 1 - slot)

    pltpu.make_async_copy(y_ref.at[pl.ds(0, tm), :], buf.at[slot], sem.at[slot]).wait()
    y = x_ref[...] + g2_ref[0] * buf[slot]
    if final_norm:
        y = y * lax.rsqrt(jnp.mean(y * y, axis=-1, keepdims=True) + EPS) * nfin_ref[...]
    o_ref[...] = y


def _combine(xs, y_sorted, pos, mod, layer, nfin, seq, nb, n_rows, final_norm):
    t, d = xs.shape
    tm = TOK_TILE
    base = layer * SUBLANES * 6
    out_rows = n_rows if final_norm else t
    return pl.pallas_call(
        functools.partial(_combine_kernel, final_norm=final_norm),
        grid_spec=pltpu.PrefetchScalarGridSpec(
            num_scalar_prefetch=1, grid=(n_rows // tm,),
            in_specs=[pl.BlockSpec((tm, d), lambda i, pos: (i, 0)),
                      pl.BlockSpec(memory_space=pl.ANY),
                      pl.BlockSpec((1, 1, d), lambda i, pos: (base + _mod_row(i, tm, seq, nb) * 6 + 5, 0, 0)),
                      pl.BlockSpec((1, d), lambda i, pos: (0, 0))],
            out_specs=pl.BlockSpec((tm, d), lambda i, pos: (i, 0)),
            scratch_shapes=[pltpu.VMEM((2, tm, d), F32), pltpu.SemaphoreType.DMA((2,))]),
        out_shape=jax.ShapeDtypeStruct((out_rows, d), F32),
        compiler_params=_cparams(("arbitrary",)),
        name="moe_combine",
    )(pos, xs, y_sorted, mod, nfin.reshape(1, d))


def _moe_ffn(h_sorted, pos_rows, cnt, wr_pad, rb_row, w1, w3, w2, layer, n_rows):
    tm = MOE_TILE
    cap = n_rows + tm
    pos = pos_rows.reshape(-1)
    counts = cnt[:N_GROUPS, 0].astype(jnp.int32)
    tiles = (counts + tm - 1) // tm
    ends = jnp.cumsum(tiles)
    n_tiles = n_rows // tm + N_GROUPS
    step = jnp.minimum(jnp.arange(n_tiles, dtype=jnp.int32), ends[-1] - 1)
    tile_grp = jnp.sum(step[:, None] >= ends[None, :], axis=1).astype(jnp.int32)
    tile_blk = step - (ends - tiles)[tile_grp]
    y_sorted = _group_ffn(h_sorted, tile_grp, tile_blk, ends[-1:].astype(jnp.int32), wr_pad, rb_row, w1, w3, w2, cap,
                          layer)
    return y_sorted, pos


def kernel(x, c, ctx, c_ctx, w_mod, b_mod, norm_mix, norm_ffn, norm_final, ab_w_in, ab_i_bias, ab_f_bias,
           ab_conv, ab_a_log, ab_dt_bias, ab_norm_a, ab_norm_b, ab_w_out, c_w_in, c_f_bias, c_lb_raw, c_norm,
           c_w_out, w_router, router_bias, w1, w3, w2):
    nb, seq, d = x.shape
    n_ctx = ctx.shape[1]
    depth = w_mod.shape[0]
    n_lat = nb * seq
    assert nb + 1 <= SUBLANES and seq % 1024 == 0 and (nb * n_ctx) % 1024 == 0 and n_ctx % 256 == 0

    xs = jnp.concatenate([x.reshape(n_lat, d), ctx.reshape(nb * n_ctx, d)], axis=0)
    cpad = jnp.zeros((SUBLANES, d), F32).at[:nb].set(c).at[nb].set(c_ctx)
    mod = _modulation(cpad, w_mod, b_mod).reshape(depth * SUBLANES * 6, 1, d)

    lb_p = jax.nn.softmax(c_lb_raw.astype(F32), axis=0)
    lb_all = jnp.cumsum(lb_p, axis=0) - lb_p[0:1]

    wr_pad = jnp.zeros((d, LANES), F32).at[:, :N_EXPERTS].set(w_router)
    rb_col = router_bias.astype(F32).reshape(N_EXPERTS, 1)
    rb_row = jnp.zeros((1, LANES), F32).at[0, :N_EXPERTS].set(router_bias.astype(F32))
    w_main = H_AB * HEAD * 8
    ab_w_in_b, c_w_in_b = ab_w_in.astype(BF16), c_w_in.astype(BF16)
    ab_w_out_b, c_w_out_b = ab_w_out.astype(BF16), c_w_out.astype(BF16)
    n_exp, _, d_ff = w1.shape[1:]
    w1_b = w1.astype(BF16).reshape(depth * n_exp, d, d_ff)
    w3_b = w3.astype(BF16).reshape(depth * n_exp, d, d_ff)
    w2_b = w2.astype(BF16).reshape(depth * n_exp, d_ff, d)

    out = None
    column_major = False
    pending = None
    for layer in range(depth):
        last = layer == depth - 1
        j = layer // 2
        want_cm = layer % 2 == 1 and j % 2 == 1
        if want_cm != column_major:
            if pending is not None:
                xs = _combine(xs, *pending[:2], mod, pending[2], norm_final, seq, nb, xs.shape[0], False)
                pending = None
            if want_cm:
                xs = _to_column_major(xs, n_lat, seq)
            else:
                xs = jnp.concatenate([_from_column_major(xs[:n_lat], seq), xs[n_lat:]], axis=0)
            column_major = want_cm
        if layer % 2 == 0:
            w_in = ab_w_in[j]
            gw = w_in[:, w_main:].reshape(d, 4, 2, H_AB)
            wg = jnp.zeros((2, d, LANES), F32).at[:, :, :4 * H_AB].set(
                jnp.transpose(gw, (2, 0, 1, 3)).reshape(2, d, 4 * H_AB)).astype(BF16)
            zeros = jnp.zeros((2, H_AB), F32)
            bias = jnp.concatenate([ab_i_bias[j], ab_f_bias[j], zeros, ab_dt_bias[j]], axis=1)
            alog = jnp.concatenate([zeros, zeros, zeros, ab_a_log[j]], axis=1)
            gpar = jnp.zeros((2, SUBLANES, LANES), F32).at[:, 0, :4 * H_AB].set(bias).at[:, 1, :4 * H_AB].set(alog)
            p, gates, *xs_new = _inproj(xs, norm_mix[layer], mod, layer, ab_w_in_b, j, w_main, seq, nb, wg, gpar,
                                        pending=pending)
            ha_f, ha_b = _mlstm_scan(p, gates, seq, n_ctx, nb)
            qkv = _gdn_prep(p, ab_conv[j], seq, n_ctx, nb)
            ob_f, ob_b = _gdn_scan(qkv, gates, seq, n_ctx, nb)
            o_halves = (((ha_f, 0), (ha_b, 0), 0), ((ob_f, 0), (ob_b, 0), 0))
            gate_cols = (3, 7)
            hn_gain = jnp.concatenate([ab_norm_a[j], ab_norm_b[j]])
            w_out = ab_w_out_b
        else:
            p, *xs_new = _inproj(xs, norm_mix[layer], mod, layer, c_w_in_b, j, c_w_in.shape[2], seq, nb,
                                 silu_cols=H_C * HEAD, pending=pending)
            oc_f, oc_b = _gla_scan(p, c_f_bias[j], lb_all[j], seq, n_ctx, nb)
            o_halves = (((oc_f, 0), (oc_b, 0), 0), ((oc_f, 0), (oc_b, 0), 1))
            gate_cols = (8, 9)
            hn_gain = c_norm[j]
            w_out = c_w_out_b
        if pending is not None:
            (xs,) = xs_new
            pending = None
        n_rows = n_lat if last else n_lat + nb * n_ctx
        xs, pos_rows, cnt, h_sorted = _outproj(xs, o_halves, p, gate_cols, hn_gain, w_out, j, mod, layer,
                                               norm_ffn[layer], wr_pad, rb_col, seq, nb, n_rows, layer % 2 == 0)
        y_sorted, pos = _moe_ffn(h_sorted, pos_rows, cnt, wr_pad, rb_row, w1_b, w3_b, w2_b, layer, n_rows)
        if last:
            res = _combine(xs, y_sorted, pos, mod, layer, norm_final, seq, nb, n_rows, True)
            out = _from_column_major(res, seq) if column_major else res
        else:
            pending = (y_sorted, pos, layer)
    return out.reshape(nb, seq, d)
```

```python
import functools

import numpy as np
import jax
import jax.numpy as jnp
from jax import lax
from jax.experimental import pallas as pl
from jax.experimental.pallas import tpu as pltpu

F32 = jnp.float32
BF16 = jnp.bfloat16

EPS = 1e-6
GRID_W = 64
HEAD = 128
H_AB = 4
H_C = 8
CONV_K = 5
N_EXPERTS = 16
N_GROUPS = 4
EXPERTS_PER_GROUP = N_EXPERTS // N_GROUPS
LANES = 128
SUBLANES = 8
VMEM_LIMIT = 56 * 1024 * 1024

TOK_TILE = 512
MOE_TILE = 512
ROW_DMA_UNROLL = 8
MLSTM_CHUNK = 128
GDN_CHUNK = 64
GDN_SCAN_CHUNKS = 4
GLA_CHUNK = 128
GDN_INV_BLOCK = 16


def _cparams(sem):
    return pltpu.CompilerParams(dimension_semantics=sem, vmem_limit_bytes=VMEM_LIMIT)


def _dot(a, b):
    return jnp.dot(a, b, preferred_element_type=F32)


def _dot_nt(a, b):
    return lax.dot_general(a, b, (((1,), (1,)), ((), ())), preferred_element_type=F32)


def _dot_tn(a, b):
    return lax.dot_general(a, b, (((0,), (0,)), ((), ())), preferred_element_type=F32)


def _split2(x):
    hi = x.astype(BF16)
    lo = (x - hi.astype(F32)).astype(BF16)
    return hi, lo


def _split3(x):
    hi = x.astype(BF16)
    r = x - hi.astype(F32)
    mid = r.astype(BF16)
    lo = (r - mid.astype(F32)).astype(BF16)
    return hi, mid, lo


def _dot_exact_lhs(m_bf16, x):
    hi, mid, lo = _split3(x)
    return _dot(m_bf16, hi) + _dot(m_bf16, mid) + _dot(m_bf16, lo)


def _dot3(a, b):
    ah, al = _split2(a)
    bh, bl = _split2(b)
    return _dot(ah, bh) + _dot(ah, bl) + _dot(al, bh)


def _dot_nt3(a, b):
    ah, al = _split2(a)
    bh, bl = _split2(b)
    return _dot_nt(ah, bh) + _dot_nt(ah, bl) + _dot_nt(al, bh)


def _sigmoid(z):
    return 1.0 / (1.0 + jnp.exp(-z))


def _sigmoid_pair(z):
    return _sigmoid(z), _sigmoid(-z)


def _silu(z):
    return z * _sigmoid(z)


def _mod_kernel(c_ref, w_ref, b_ref, o_ref):
    s = _silu(c_ref[...]).astype(BF16)
    o_ref[0] = _dot(s, w_ref[0].astype(BF16)) + b_ref[0]


def _modulation(cpad, w_mod, b_mod):
    depth, d, n = w_mod.shape
    tn = 1536
    return pl.pallas_call(
        _mod_kernel,
        grid=(depth, n // tn),
        in_specs=[pl.BlockSpec((SUBLANES, d), lambda l, j: (0, 0)),
                  pl.BlockSpec((1, d, tn), lambda l, j: (l, 0, j)),
                  pl.BlockSpec((1, 1, tn), lambda l, j: (l, 0, j))],
        out_specs=pl.BlockSpec((1, SUBLANES, tn), lambda l, j: (l, 0, j)),
        out_shape=jax.ShapeDtypeStruct((depth, SUBLANES, n), F32),
        compiler_params=_cparams(("parallel", "parallel")),
        name="modulation",
    )(cpad, w_mod, b_mod.reshape(depth, 1, n))


def _gate_activations(raw, par):
    y = raw + par[0:1, :]
    e = jnp.exp(-jnp.abs(y))
    sp = jnp.log1p(e)
    lsig = jnp.minimum(y, 0.0) - sp
    r = 1.0 / (1.0 + e)
    sig = jnp.where(y >= 0, r, e * r)
    gdec = -jnp.exp(par[1:2, :]) * (jnp.maximum(y, 0.0) + sp)
    lane = lax.broadcasted_iota(jnp.int32, raw.shape, 1)
    return jnp.where(lane < 4, y, jnp.where(lane < 8, lsig, jnp.where(lane < 12, sig, gdec)))


def _rows_copy(src, dst, sem, src_row, dst_row):
    return pltpu.make_async_copy(src.at[pl.ds(src_row, 1), :], dst.at[pl.ds(dst_row, 1), :], sem)


def _inproj_kernel(*refs, with_gates, p_cols, silu_cols, fused_combine):
    refs = list(refs)
    pos_ref = refs.pop(0) if fused_combine else None
    x_ref, gain_ref, sc_ref, sh_ref, w_ref = refs[:5]
    rest = refs[5:]
    if with_gates:
        wg_ref, gpar_ref = rest[:2]
        rest = rest[2:]
    if fused_combine:
        y_ref, g2_ref = rest[:2]
        rest = rest[2:]
    p_ref = rest.pop(0)
    if with_gates:
        g_ref = rest.pop(0)
    x = x_ref[...]
    if fused_combine:
        xs_ref, buf, sem = rest
        i = pl.program_id(0)
        n_steps = pl.num_programs(0)
        tm = x_ref.shape[0]
        slot = i % 2

        @pl.when(i == 0)
        def _():
            def issue(r, carry):
                _rows_copy(y_ref, buf.at[0], sem.at[0], pos_ref[r], r).start()
                return carry
            lax.fori_loop(0, tm, issue, 0, unroll=ROW_DMA_UNROLL)

        nxt = jnp.minimum(i + 1, n_steps - 1)
        for r in range(tm):
            _rows_copy(y_ref, buf.at[1 - slot], sem.at[1 - slot], pos_ref[nxt * tm + r], r).start()
        pltpu.make_async_copy(y_ref.at[pl.ds(0, tm), :], buf.at[slot], sem.at[slot]).wait()
        x = x + g2_ref[0] * buf[slot]
        xs_ref[...] = x

        @pl.when(i == n_steps - 1)
        def _():
            pltpu.make_async_copy(y_ref.at[pl.ds(0, tm), :], buf.at[1 - slot], sem.at[1 - slot]).wait()
    y = x * lax.rsqrt(jnp.mean(x * x, axis=-1, keepdims=True) + EPS) * gain_ref[...]
    u = (y * (1.0 + sc_ref[0]) + sh_ref[0]).astype(BF16)
    if with_gates:
        for d in range(2):
            g_ref[d] = _gate_activations(_dot(u, wg_ref[d]), gpar_ref[d])
    tn = 1024
    for j in range(p_cols // tn):
        res = _dot(u, w_ref[:, j * tn:(j + 1) * tn])
        if (j + 1) * tn <= silu_cols:
            res = _silu(res)
        p_ref[:, j * tn:(j + 1) * tn] = res


def _mod_row(i, tm, seq, nb):
    return jnp.minimum((i * tm) // seq, nb)


def _inproj(xs, gain, mod, layer, w_all, w_idx, p, seq, nb, wg=None, gpar=None, silu_cols=0, pending=None):
    t, d = xs.shape
    tm = TOK_TILE
    with_gates = wg is not None
    fused = pending is not None
    base = layer * SUBLANES * 6

    def mod_spec(which, base_=base):
        return pl.BlockSpec((1, 1, d), lambda i, *_: (base_ + _mod_row(i, tm, seq, nb) * 6 + which, 0, 0))

    in_specs = [pl.BlockSpec((tm, d), lambda i, *_: (i, 0)),
                pl.BlockSpec((1, d), lambda i, *_: (0, 0)),
                mod_spec(1), mod_spec(0),
                pl.BlockSpec((None, d, p), lambda i, *_: (w_idx, 0, 0), pipeline_mode=pl.Buffered(1))]
    args = [xs, gain.reshape(1, d), mod, mod, w_all]
    out_specs = [pl.BlockSpec((tm, p), lambda i, *_: (i, 0))]
    out_shape = [jax.ShapeDtypeStruct((t, p), F32)]
    scratch, prefetch = [], []
    if with_gates:
        in_specs += [pl.BlockSpec((2, d, LANES), lambda i, *_: (0, 0, 0)),
                     pl.BlockSpec((2, SUBLANES, LANES), lambda i, *_: (0, 0, 0))]
        args += [wg, gpar]
        out_specs.append(pl.BlockSpec((2, tm, LANES), lambda i, *_: (0, i, 0)))
        out_shape.append(jax.ShapeDtypeStruct((2, t, LANES), F32))
    if fused:
        y_sorted, pos, prev_layer = pending
        in_specs += [pl.BlockSpec(memory_space=pl.ANY), mod_spec(5, prev_layer * SUBLANES * 6)]
        args += [y_sorted, mod]
        out_specs.append(pl.BlockSpec((tm, d), lambda i, *_: (i, 0)))
        out_shape.append(jax.ShapeDtypeStruct((t, d), F32))
        scratch = [pltpu.VMEM((2, tm, d), F32), pltpu.SemaphoreType.DMA((2,))]
        prefetch = [pos]
    return pl.pallas_call(
        functools.partial(_inproj_kernel, with_gates=with_gates, p_cols=p, silu_cols=silu_cols, fused_combine=fused),
        grid_spec=pltpu.PrefetchScalarGridSpec(
            num_scalar_prefetch=len(prefetch), grid=(t // tm,),
            in_specs=in_specs, out_specs=out_specs, scratch_shapes=scratch),
        out_shape=out_shape,
        compiler_params=_cparams(("arbitrary",) if fused else ("parallel",)),
        name="inproj",
    )(*prefetch, *args)


def _gdn_prep_kernel(*refs, blocks_per_seq, n_lat_blocks, scale):
    w_ref, o_ref = refs[9], refs[10]
    i = pl.program_id(0)
    tm = o_ref.shape[0]
    w = refs[1].shape[1]
    in_lat = i < n_lat_blocks
    first = jnp.logical_or(jnp.logical_not(in_lat), i % blocks_per_seq == 0)
    last = jnp.logical_or(jnp.logical_not(in_lat), i % blocks_per_seq == blocks_per_seq - 1)
    pad = (CONV_K - 1) // 2
    for part in range(3):
        prev_ref, x_ref, next_ref = refs[3 * part:3 * part + 3]
        prev = jnp.where(first, 0.0, prev_ref[...])
        nxt = jnp.where(last, 0.0, next_ref[...])
        xe = jnp.concatenate([prev, x_ref[...], nxt], axis=0)
        n = xe.shape[0]
        acc = None
        for j in range(CONV_K):
            shift = (pad - j) % n
            xr = xe if shift == 0 else pltpu.roll(xe, shift, 0)
            term = xr[SUBLANES:SUBLANES + tm] * w_ref[j:j + 1, part * w:(part + 1) * w]
            acc = term if acc is None else acc + term
        y = _silu(acc)
        for h in range(H_AB):
            t = y[:, h * HEAD:(h + 1) * HEAD]
            if part < 2:
                t = t * lax.rsqrt(jnp.sum(t * t, axis=-1, keepdims=True) + EPS)
            if part == 0:
                t = t * scale
            o_ref[:, part * w + h * HEAD:part * w + (h + 1) * HEAD] = t


def _gdn_prep(p, conv_w, seq, n_ctx, nb):
    t = p.shape[0]
    tm = 256
    w = H_AB * HEAD
    first_col = 4
    wpad = jnp.zeros((SUBLANES, 3 * w), F32).at[:CONV_K].set(conv_w)
    hb = tm // SUBLANES
    nblk8 = t // SUBLANES
    in_specs = []
    for c in range(3):
        in_specs += [pl.BlockSpec((SUBLANES, w), lambda i, c=c: (jnp.maximum(i * hb - 1, 0), first_col + c)),
                     pl.BlockSpec((tm, w), lambda i, c=c: (i, first_col + c)),
                     pl.BlockSpec((SUBLANES, w), lambda i, c=c: (jnp.minimum((i + 1) * hb, nblk8 - 1), first_col + c))]
    in_specs.append(pl.BlockSpec((SUBLANES, 3 * w), lambda i: (0, 0)))
    return pl.pallas_call(
        functools.partial(_gdn_prep_kernel, blocks_per_seq=seq // tm, n_lat_blocks=nb * seq // tm,
                          scale=HEAD ** -0.5),
        grid=(t // tm,),
        in_specs=in_specs,
        out_specs=pl.BlockSpec((tm, 3 * w), lambda i: (i, 0)),
        out_shape=jax.ShapeDtypeStruct((t, 3 * w), F32),
        compiler_params=_cparams(("parallel",)),
        name="gdn_prep",
    )(*([p] * 9), wpad)


def _tri_tables(c):
    idx = np.arange(c)
    fwd = (idx[None, :] <= idx[:, None]).astype(np.float32)
    return jnp.asarray(np.stack([fwd, fwd.T]))


def _to_cm_kernel(x3_ref, x2_ref, o_ref, *, n_lat_tiles):
    i = pl.program_id(0)
    rows = x3_ref.shape[0]

    @pl.when(i < n_lat_tiles)
    def _():
        for wl in range(SUBLANES):
            o_ref[wl * rows:(wl + 1) * rows, :] = x3_ref[:, wl, :]

    @pl.when(i >= n_lat_tiles)
    def _():
        o_ref[...] = x2_ref[...]


def _from_cm_kernel(x_ref, o_ref):
    rows = o_ref.shape[0]
    for wl in range(SUBLANES):
        o_ref[:, wl, :] = x_ref[wl * rows:(wl + 1) * rows, :]


def _to_column_major(xs, n_lat, seq):
    t, d = xs.shape
    rows = seq // GRID_W
    tile = rows * SUBLANES
    assert rows % SUBLANES == 0 and n_lat % tile == 0 and (t - n_lat) % tile == 0
    n_lat_tiles = n_lat // tile
    per_b = GRID_W // SUBLANES
    lat = lambda i: jnp.minimum(i, n_lat_tiles - 1)
    return pl.pallas_call(
        functools.partial(_to_cm_kernel, n_lat_tiles=n_lat_tiles),
        grid=(t // tile,),
        in_specs=[pl.BlockSpec((rows, SUBLANES, d), lambda i: (lat(i) // per_b, lat(i) % per_b, 0)),
                  pl.BlockSpec((tile, d), lambda i: (jnp.maximum(i, n_lat_tiles - 1), 0))],
        out_specs=pl.BlockSpec((tile, d), lambda i: (i, 0)),
        out_shape=jax.ShapeDtypeStruct((t, d), xs.dtype),
        compiler_params=_cparams(("parallel",)),
        name="to_column_major",
    )(xs.reshape(t // GRID_W, GRID_W, d), xs)


def _from_column_major(y, seq):
    n_lat, d = y.shape
    rows = seq // GRID_W
    tile = rows * SUBLANES
    per_b = GRID_W // SUBLANES
    out = pl.pallas_call(
        _from_cm_kernel,
        grid=(n_lat // tile,),
        in_specs=[pl.BlockSpec((tile, d), lambda i: (i, 0))],
        out_specs=pl.BlockSpec((rows, SUBLANES, d), lambda i: (i // per_b, i % per_b, 0)),
        out_shape=jax.ShapeDtypeStruct((n_lat // GRID_W, GRID_W, d), y.dtype),
        compiler_params=_cparams(("parallel",)),
        name="from_column_major",
    )(y)
    return out.reshape(n_lat, d)


def _mlstm_kernel(qf, kf, vf, qb, kb, vb, gf, gb, tri_ref, hf_ref, hb_ref, s_scr, m_scr, *, scale):
    @pl.when(pl.program_id(1) == 0)
    def _():
        s_scr[...] = jnp.zeros_like(s_scr)
        m_scr[...] = jnp.zeros_like(m_scr)

    c = gf.shape[1]
    ones_col = (lax.broadcasted_iota(jnp.int32, (c, HEAD), 1) == 0).astype(BF16)
    qkv = ((qf, kf, vf), (qb, kb, vb))
    h_out = (hf_ref, hb_ref)
    g = [gf[0], gb[0]]
    mask = [tri_ref[d] > 0.5 for d in range(2)]
    b = [_dot_exact_lhs(tri_ref[d].astype(BF16), g[d]) for d in range(2)]
    total = [jnp.sum(x, axis=0, keepdims=True) for x in g]
    g_t = [x.T for x in g]
    b_t = [x.T for x in b]
    chains = [(d, h) for d in range(2) for h in range(H_AB)]
    sl = [slice(h * HEAD, (h + 1) * HEAD) for d, h in chains]
    q = [qkv[d][0][:, sl[i]].astype(BF16) for i, (d, h) in enumerate(chains)]
    k = [qkv[d][1][:, sl[i]] * scale for i, (d, h) in enumerate(chains)]
    v_aug = [jnp.concatenate([qkv[d][2][:, sl[i]].astype(BF16), ones_col], axis=1) for i, (d, h) in enumerate(chains)]
    s = [s_scr[d, h] for d, h in chains]
    qk = [_dot_nt(q[i], k[i].astype(BF16)) for i in range(len(chains))]
    qs = [_dot(q[i], s[i].astype(BF16)) for i in range(len(chains))]
    w_intra, w_inter, m_ts, colvs, m_sts, tots = [], [], [], [], [], []
    for i, (d, h) in enumerate(chains):
        li_c, li_r = g[d][:, h:h + 1], g_t[d][h:h + 1, :]
        b_c, b_r = b[d][:, 4 + h:5 + h], b_t[d][4 + h:5 + h, :]
        tot = total[d][:, 4 + h:5 + h]
        m_st = m_scr[d, h][0:1, 0:1]
        d_log = jnp.where(mask[d], b_c - b_r + li_r, -jnp.inf)
        m_inter = b_c + m_st
        m_t = jnp.maximum(m_inter, jnp.max(d_log, axis=-1, keepdims=True))
        w_inter.append(jnp.exp(m_inter - m_t))
        w_intra.append((jnp.exp(d_log - m_t) * qk[i]).astype(BF16))
        m_ts.append(m_t); colvs.append(tot - b_c + li_c); m_sts.append(m_st); tots.append(tot)
    intra = [_dot(w_intra[i], v_aug[i]) for i in range(len(chains))]
    m_new = [jnp.maximum(tots[i] + m_sts[i], jnp.max(colvs[i], axis=0, keepdims=True)) for i in range(len(chains))]
    upd = [_dot_tn((k[i] * jnp.exp(colvs[i] - m_new[i])).astype(BF16), v_aug[i]) for i in range(len(chains))]
    for i, (d, h) in enumerate(chains):
        num = w_inter[i] * qs[i] + intra[i]
        den = num[:, HEAD:HEAD + 1]
        h_out[d][0, :, sl[i]] = num[:, :HEAD] / jnp.maximum(jnp.abs(den), jnp.exp(-m_ts[i]))
        s_scr[d, h] = jnp.exp(tots[i] + m_sts[i] - m_new[i]) * s[i] + upd[i]
        m_scr[d, h] = jnp.broadcast_to(m_new[i], (SUBLANES, LANES))


def _step_block(chunk, seq, n_ctx, nb):
    n_c, n_l = n_ctx // chunk, seq // chunk
    off = nb * n_l

    def fn(b, d, i):
        j = i - n_c
        ctx_blk = off + b * n_c + (i if d == 0 else n_c - 1 - i)
        lat_blk = b * n_l + (j if d == 0 else n_l - 1 - j)
        return jnp.where(i < n_c, ctx_blk, lat_blk)

    return n_c + n_l, fn


def _mlstm_scan(p, gates, seq, n_ctx, nb):
    t = p.shape[0]
    c = MLSTM_CHUNK
    w = H_AB * HEAD
    n_steps, blk = _step_block(c, seq, n_ctx, nb)
    in_specs = [pl.BlockSpec((c, w), lambda b, i, d=d, col=col: (blk(b, d, i), col)) for d in range(2) for col in range(3)]
    in_specs += [pl.BlockSpec((1, c, LANES), lambda b, i, d=d: (d, blk(b, d, i), 0)) for d in range(2)]
    in_specs.append(pl.BlockSpec((2, c, c), lambda b, i: (0, 0, 0)))
    return pl.pallas_call(
        functools.partial(_mlstm_kernel, scale=HEAD ** -0.5),
        grid=(nb, n_steps),
        in_specs=in_specs,
        out_specs=[pl.BlockSpec((1, c, w), lambda b, i, d=d: (0, blk(b, d, i), 0)) for d in range(2)],
        out_shape=[jax.ShapeDtypeStruct((1, t, w), F32)] * 2,
        scratch_shapes=[pltpu.VMEM((2, H_AB, HEAD, 2 * HEAD), F32), pltpu.VMEM((2, H_AB, SUBLANES, LANES), F32)],
        compiler_params=_cparams(("parallel", "arbitrary")),
        name="mlstm_scan",
    )(p, p, p, p, p, p, gates, gates, _tri_tables(c))


def _gdn_inv_masks(c):
    idx = np.arange(c)
    blk = lambda n: (idx[:, None] // n) == (idx[None, :] // n)
    levels = [blk(GDN_INV_BLOCK)]
    n = GDN_INV_BLOCK
    while n < c:
        levels.append(np.logical_and(blk(2 * n), np.logical_not(blk(n))))
        n *= 2
    levels = np.stack(levels).astype(np.float32)
    return jnp.asarray(np.tile(levels, (1, 1, H_AB)))


def _block_diag(x, n_blocks):
    blk = lax.broadcasted_iota(jnp.int32, x.shape, 1) // (x.shape[1] // n_blocks)
    return jnp.concatenate([jnp.where(blk == j, x, jnp.zeros_like(x)) for j in range(n_blocks)], axis=0)


def _heads_to_lanes(x, first, width):
    return jnp.concatenate([jnp.broadcast_to(x[:, first + h:first + h + 1], (x.shape[0], width))
                            for h in range(H_AB)], axis=1)


def _unit_tri_inverse(a_list, eye, inv_masks):
    prod = lambda xs, ys: [_dot(x.astype(BF16), _block_diag(y.astype(BF16), H_AB)) for x, y in zip(xs, ys)]
    pw = [a * inv_masks[0] for a in a_list]
    inv = [eye - p for p in pw]
    n = 2
    while n < GDN_INV_BLOCK:
        pw = prod(pw, pw)
        inv = [i + t for i, t in zip(inv, prod(inv, pw))]
        n *= 2
    for lvl in range(1, inv_masks.shape[0]):
        t = prod(prod(inv, [a * inv_masks[lvl] for a in a_list]), inv)
        inv = [i - x for i, x in zip(inv, t)]
    return inv


def _gdn_chunk_kernel(q_ref, k_ref, v_ref, g_ref, tri_ref, im_ref, u_ref, w_ref, qd_ref, kd_ref, att_ref):
    c = GDN_CHUNK
    n_chunks = q_ref.shape[0] // c
    inv_masks = im_ref[...]
    row = lax.broadcasted_iota(jnp.int32, (c, H_AB * c), 0)
    col = lax.broadcasted_iota(jnp.int32, (c, H_AB * c), 1) % c
    eye_b = row == col
    eye = eye_b.astype(F32)
    masks = [col <= row, col >= row]
    stricts = [col < row, col > row]
    tris = [tri_ref[d].astype(BF16) for d in range(2)]
    ones = jnp.ones((c, c), BF16)
    chunks = range(n_chunks)
    rows = [slice(ci * c, (ci + 1) * c) for ci in chunks]
    groups = [(ci, d) for ci in chunks for d in range(2)]
    k_hl = [_split2(k_ref[r, :]) for r in rows]
    kbd = [(_block_diag(hi, H_AB), _block_diag(lo, H_AB)) for hi, lo in k_hl]
    r_hi = [_dot_nt(jnp.concatenate([k_hl[ci][0], k_hl[ci][1], q_ref[rows[ci], :].astype(BF16)], axis=0), kbd[ci][0])
            for ci in chunks]
    r_lo = [_dot_nt(k_hl[ci][0], kbd[ci][1]) for ci in chunks]
    kk = [r_hi[ci][:c] + r_hi[ci][c:2 * c] + r_lo[ci] for ci in chunks]
    g = [g_ref[d, rows[ci], :] for ci, d in groups]
    gc = [_dot_exact_lhs(tris[d], g[i]) for i, (ci, d) in enumerate(groups)]
    gc_c = [_heads_to_lanes(x, 12, c) for x in gc]
    gc_r = [_dot_exact_lhs(ones, eye * x) for x in gc_c]
    decay = [jnp.exp(jnp.where(masks[d], gc_c[i] - gc_r[i], -jnp.inf)) for i, (ci, d) in enumerate(groups)]
    a = [jnp.where(stricts[d], _heads_to_lanes(g[i], 8, c) * kk[ci] * decay[i], 0.0)
         for i, (ci, d) in enumerate(groups)]
    inv = _unit_tri_inverse(a, eye, inv_masks)
    inv_hl = [jnp.concatenate(_split2(x), axis=0) for x in inv]
    gc_w = [_heads_to_lanes(x, 12, HEAD) for x in gc]
    eg = [jnp.exp(x) for x in gc_w]
    beta = [_heads_to_lanes(x, 8, HEAD) for x in g]
    res_u = [_dot(inv_hl[i], _block_diag((v_ref[rows[ci], :] * beta[i]).astype(BF16), H_AB))
             for i, (ci, d) in enumerate(groups)]
    res_w = [_dot(inv_hl[i], _block_diag((k_ref[rows[ci], :] * (beta[i] * eg[i])).astype(BF16), H_AB))
             for i, (ci, d) in enumerate(groups)]
    for i, (ci, d) in enumerate(groups):
        r = rows[ci]
        total = _heads_to_lanes(jnp.sum(g[i], axis=0, keepdims=True), 12, HEAD)
        u_ref[d, r, :] = (res_u[i][:c] + res_u[i][c:]).astype(BF16)
        w_ref[d, r, :] = (res_w[i][:c] + res_w[i][c:]).astype(BF16)
        qd_ref[d, r, :] = (q_ref[r, :] * eg[i]).astype(BF16)
        kd_ref[d, r, :] = (k_ref[r, :] * jnp.exp(total - gc_w[i])).astype(BF16)
        att_ref[d, r, :] = (r_hi[ci][2 * c:] * decay[i]).astype(BF16)


def _gdn_scan_kernel(*refs):
    ins, o_refs, s_scr = refs[:12], refs[12:14], refs[14]
    c = GDN_CHUNK

    @pl.when(pl.program_id(1) == 0)
    def _():
        s_scr[...] = jnp.zeros_like(s_scr)

    pw = 2 * HEAD
    rblk = lax.broadcasted_iota(jnp.int32, (pw, pw), 0) // HEAD
    cblk = lax.broadcasted_iota(jnp.int32, (pw, pw), 1) // HEAD
    on_diag = rblk == cblk
    chains = [(d, pr) for d in range(2) for pr in range(H_AB // 2)]
    refs_of = lambda d: ins[d::2]
    sl = [slice(pr * pw, (pr + 1) * pw) for d, pr in chains]
    s = [s_scr[d, pr] for d, pr in chains]
    n_sub = o_refs[0].shape[1] // c
    for k in range(n_sub):
        rows = [slice(kk * c, (kk + 1) * c) for kk in (k, n_sub - 1 - k)]
        res = [_dot(jnp.concatenate([refs_of(d)[1][0, rows[d], sl[i]], refs_of(d)[2][0, rows[d], sl[i]]], axis=0),
                    s[i].astype(BF16)) for i, (d, pr) in enumerate(chains)]
        v_new = [(refs_of(d)[0][0, rows[d], sl[i]].astype(F32) - res[i][:c]).astype(BF16)
                 for i, (d, pr) in enumerate(chains)]
        intra = [_dot(refs_of(d)[4][0, rows[d], pr * 2 * c:(pr + 1) * 2 * c], _block_diag(v_new[i], 2))
                 for i, (d, pr) in enumerate(chains)]
        upd = [_dot_tn(refs_of(d)[3][0, rows[d], sl[i]], v_new[i]) for i, (d, pr) in enumerate(chains)]
        totals = [jnp.sum(refs_of(d)[5][0, rows[d], :], axis=0, keepdims=True) for d in range(2)]
        for i, (d, pr) in enumerate(chains):
            o_refs[d][0, rows[d], sl[i]] = res[i][c:] + intra[i]
            gl = jnp.concatenate([jnp.broadcast_to(jnp.exp(totals[d][:, 12 + 2 * pr + j:13 + 2 * pr + j]), (1, HEAD))
                                  for j in range(2)], axis=1)
            s[i] = gl * s[i] + jnp.where(on_diag, upd[i], 0.0)
    for i, (d, pr) in enumerate(chains):
        s_scr[d, pr] = s[i]


def _gdn_scan(qkv, gates, seq, n_ctx, nb):
    t = qkv.shape[0]
    c = GDN_CHUNK
    w = H_AB * HEAD
    tri = _tri_tables(c)
    im = _gdn_inv_masks(c)
    tm = TOK_TILE
    sds = lambda width: jax.ShapeDtypeStruct((2, t, width), BF16)
    u, wv, qd, kd, att = pl.pallas_call(
        _gdn_chunk_kernel,
        grid=(t // tm,),
        in_specs=[pl.BlockSpec((tm, w), lambda i, col=col: (i, col)) for col in range(3)] + [
            pl.BlockSpec((2, tm, LANES), lambda i: (0, i, 0)),
            pl.BlockSpec((2, c, c), lambda i: (0, 0, 0)),
            pl.BlockSpec(im.shape, lambda i: (0, 0, 0))],
        out_specs=[pl.BlockSpec((2, tm, w), lambda i: (0, i, 0))] * 4
        + [pl.BlockSpec((2, tm, H_AB * c), lambda i: (0, i, 0))],
        out_shape=[sds(w)] * 4 + [sds(H_AB * c)],
        compiler_params=_cparams(("parallel",)),
        name="gdn_chunk",
    )(qkv, qkv, qkv, gates, tri, im)

    cs = c * GDN_SCAN_CHUNKS
    n_steps, blk = _step_block(cs, seq, n_ctx, nb)
    in_specs, args = [], []
    for arr, width in ((u, w), (wv, w), (qd, w), (kd, w), (att, H_AB * c), (gates, LANES)):
        for d in range(2):
            in_specs.append(pl.BlockSpec((1, cs, width), lambda b, i, d=d: (d, blk(b, d, i), 0)))
            args.append(arr)
    o_f, o_b = pl.pallas_call(
        _gdn_scan_kernel,
        grid=(nb, n_steps),
        in_specs=in_specs,
        out_specs=[pl.BlockSpec((1, cs, w), lambda b, i, d=d: (0, blk(b, d, i), 0)) for d in range(2)],
        out_shape=[jax.ShapeDtypeStruct((1, t, w), F32)] * 2,
        scratch_shapes=[pltpu.VMEM((2, H_AB // 2, 2 * HEAD, 2 * HEAD), F32)],
        compiler_params=_cparams(("parallel", "arbitrary")),
        name="gdn_scan",
    )(*args)
    return o_f, o_b


def _gla_tables(c):
    idx = np.arange(c)
    sel, sgn, msk = [], [], []
    for d in range(2):
        sel_d, sgn_d, msk_d = [], [], []
        bs = c // 2
        while bs >= 1:
            pair = idx // (2 * bs)
            second = (idx // bs) % 2 == 1
            ref_row = pair * 2 * bs + (bs - 1 if d == 0 else bs)
            sel_d.append((idx[None, :] == ref_row[:, None]).astype(np.float32))
            qside = second if d == 0 else np.logical_not(second)
            sgn_d.append(np.where(qside, 1.0, -1.0)[:, None] * np.ones((1, HEAD)))
            same_pair = pair[:, None] == pair[None, :]
            msk_d.append(np.logical_and(same_pair, np.logical_and(qside[:, None], np.logical_not(qside)[None, :])))
            bs //= 2
        msk_d.append(np.eye(c, dtype=bool))
        sel.append(np.concatenate(sel_d, axis=0))
        sgn.append(np.stack(sgn_d))
        msk.append(np.stack(msk_d).astype(np.float32))
    return (jnp.asarray(np.stack(sel), dtype=BF16), jnp.asarray(np.stack(sgn), dtype=F32),
            jnp.asarray(np.stack(msk), dtype=F32))


def _gla_kernel(qf_ref, zf_ref, vf_ref, qb_ref, zb_ref, vb_ref, fb_ref, lb_ref, tri_ref, sel_ref, sgn_ref, msk_ref,
                of_ref, ob_ref, s_scr):
    @pl.when(pl.program_id(1) == 0)
    def _():
        s_scr[...] = jnp.zeros_like(s_scr)

    c = qf_ref.shape[0]
    n_lvl = sgn_ref.shape[1]
    dirs = range(2)
    qzv = ((qf_ref, zf_ref, vf_ref), (qb_ref, zb_ref, vb_ref))
    o_refs = (of_ref, ob_ref)
    chains = [(d, h) for d in dirs for h in range(H_C)]
    sl = [slice(h * HEAD, (h + 1) * HEAD) for d, h in chains]
    g_all, key_all, q_all = [], [], []
    for d in dirs:
        lb = lb_ref[d]
        sig_pos, sig_neg = _sigmoid_pair(qzv[d][1][...] + fb_ref[d])
        g_all.append(jnp.log2(lb + (1.0 - lb) * sig_pos))
        key_all.append((1.0 - lb) * sig_neg)
        q_all.append(qzv[d][0][...])
    gc_all = []
    for d in dirs:
        g_hi, g_lo = _split2(g_all[d])
        tri = tri_ref[d].astype(BF16)
        gc_all.append(_dot(tri, g_hi) + _dot(tri, g_lo))
    total_all = [jnp.sum(g, axis=0, keepdims=True) for g in g_all]
    refs = [_dot(sel_ref[d], gc_all[d].astype(BF16)) for d in dirs]
    diag = msk_ref[0, n_lvl] > 0.5
    att = [jnp.where(diag, _dot_nt(q_all[d][:, sl[i]].astype(BF16), key_all[d][:, sl[i]].astype(BF16)), 0.0)
           for i, (d, h) in enumerate(chains)]
    for l in range(n_lvl):
        bs = c >> (l + 1)
        x_all = []
        for d in dirs:
            ref_l = refs[d][l * c:(l + 1) * c]
            if bs >= SUBLANES:
                parts = []
                for a in range(0, c, 2 * bs):
                    first, second = slice(a, a + bs), slice(a + bs, a + 2 * bs)
                    q_rows, k_rows = (second, first) if d == 0 else (first, second)
                    xq = q_all[d][q_rows] * jnp.exp2(gc_all[d][q_rows] - ref_l[q_rows])
                    xk = key_all[d][k_rows] * jnp.exp2(ref_l[k_rows] - gc_all[d][k_rows])
                    parts += [xk, xq] if d == 0 else [xq, xk]
                x_all.append(jnp.concatenate(parts, axis=0).astype(BF16))
            else:
                sgn = jnp.concatenate([sgn_ref[d, l]] * H_C, axis=1)
                x_all.append((jnp.where(sgn > 0, q_all[d], key_all[d])
                              * jnp.exp2(sgn * (gc_all[d] - ref_l))).astype(BF16))
        x = [x_all[d][:, sl[i]] for i, (d, h) in enumerate(chains)]
        prod = [_dot_nt(xh, xh) for xh in x]
        in_level = [msk_ref[d, l] > 0.5 for d in dirs]
        att = [jnp.where(in_level[d], prod[i], att[i]) for i, (d, h) in enumerate(chains)]
    st = [s_scr[d, h] for d, h in chains]
    inter = [_dot_nt((q_all[d][:, sl[i]] * jnp.exp2(gc_all[d][:, sl[i]])).astype(BF16), st[i].astype(BF16))
             for i, (d, h) in enumerate(chains)]
    intra = [_dot(att[i].astype(BF16), qzv[d][2][:, sl[i]].astype(BF16)) for i, (d, h) in enumerate(chains)]
    upd = [_dot_tn(qzv[d][2][:, sl[i]].astype(BF16),
                   (key_all[d][:, sl[i]] * jnp.exp2(total_all[d][:, sl[i]] - gc_all[d][:, sl[i]])).astype(BF16))
           for i, (d, h) in enumerate(chains)]
    for i, (d, h) in enumerate(chains):
        o_refs[d][0, :, sl[i]] = inter[i] + intra[i]
        s_scr[d, h] = jnp.exp2(total_all[d][:, sl[i]]) * st[i] + upd[i]


def _gla_scan(p, f_bias, lb, seq, n_ctx, nb):
    t, pw = p.shape
    c = GLA_CHUNK
    w = H_C * HEAD
    tri = _tri_tables(c)
    sel, sgn, msk = _gla_tables(c)
    n_steps, blk = _step_block(c, seq, n_ctx, nb)
    dspec = lambda d, col: pl.BlockSpec((c, w), lambda b, i: (blk(b, d, i), col))
    whole = lambda a: pl.BlockSpec(a.shape, lambda b, i: (0,) * a.ndim)
    fb3, lb3 = f_bias.reshape(2, 1, w), lb.reshape(2, 1, w)
    consts = (fb3, lb3, tri, sel, sgn, msk)
    return pl.pallas_call(
        _gla_kernel,
        grid=(nb, n_steps),
        in_specs=[dspec(0, 0), dspec(0, 1), dspec(0, 3), dspec(1, 0), dspec(1, 2), dspec(1, 3)]
        + [whole(a) for a in consts],
        out_specs=[pl.BlockSpec((1, c, w), lambda b, i, d=d: (0, blk(b, d, i), 0)) for d in range(2)],
        out_shape=[jax.ShapeDtypeStruct((1, t, w), F32)] * 2,
        scratch_shapes=[pltpu.VMEM((2, H_C, HEAD, HEAD), F32)],
        compiler_params=_cparams(("parallel", "arbitrary")),
        name="gla_scan",
    )(p, p, p, p, p, p, *consts)


def _best_group(scores_t, bias):
    sel = [scores_t[e:e + 1, :] + bias[e:e + 1, :] for e in range(N_EXPERTS)]
    gscore = []
    for grp in range(N_GROUPS):
        v = sel[grp * EXPERTS_PER_GROUP:(grp + 1) * EXPERTS_PER_GROUP]
        best = None
        for a in range(EXPERTS_PER_GROUP):
            for b in range(a + 1, EXPERTS_PER_GROUP):
                pair = v[a] + v[b]
                best = pair if best is None else jnp.maximum(best, pair)
        gscore.append(best)
    best_g = jnp.zeros(gscore[0].shape, jnp.int32)
    best_v = gscore[0]
    for grp in range(1, N_GROUPS):
        better = gscore[grp] > best_v
        best_g = jnp.where(better, grp, best_g)
        best_v = jnp.where(better, gscore[grp], best_v)
    return best_g


def _outproj_kernel(x_ref, of0, ob0, gt0, of1, ob1, gt1, hn_ref, w_ref, g1_ref, sc_ref, sh_ref, nf_ref,
                    wr_ref, rb_ref, triu_ref, xo_ref, pos_ref, cnt_ref, hs_ref,
                    base_scr, h_scr, pos_v, pos_s, cnt_v, cnt_s, row_sem, meta_sem, *, first_half_sigmoid, cap):
    i = pl.program_id(0)
    n_steps = pl.num_programs(0)
    tm = x_ref.shape[0]
    slot = i % 2
    prev = 1 - slot
    dump = N_GROUPS * cap

    @pl.when(i == 0)
    def _():
        base_scr[...] = jnp.zeros_like(base_scr)
        h_scr[...] = jnp.zeros_like(h_scr)

        def init(r, carry):
            pos_s[1, r] = dump + r
            return carry
        lax.fori_loop(0, tm, init, 0)

    @pl.when(i > 0)
    def _():
        pltpu.make_async_copy(pos_v, pos_s.at[pl.ds(prev, 1), :], meta_sem).wait()

    for r in range(tm):
        _rows_copy(h_scr.at[prev], hs_ref, row_sem, r, pos_s[prev, r]).start()

    feats = []
    for half, (of, ob, gt) in enumerate(((of0, ob0, gt0), (of1, ob1, gt1))):
        o = of[0] + ob[0]
        gate = gt[...]
        for h in range(o.shape[1] // HEAD):
            sl = slice(h * HEAD, (h + 1) * HEAD)
            t = o[:, sl]
            y = t * lax.rsqrt(jnp.mean(t * t, axis=-1, keepdims=True) + EPS)
            y = y * hn_ref[:, half * o.shape[1] + h * HEAD: half * o.shape[1] + (h + 1) * HEAD]
            gz = gate[:, sl]
            act = _sigmoid(gz) if (half == 0 and first_half_sigmoid) else _silu(gz)
            feats.append((y * act).astype(BF16))
    feats = jnp.concatenate(feats, axis=1)
    xn = x_ref[...] + g1_ref[0] * _dot(feats, w_ref[...])
    xo_ref[...] = xn
    y = xn * lax.rsqrt(jnp.mean(xn * xn, axis=-1, keepdims=True) + EPS) * nf_ref[...]
    hl = y * (1.0 + sc_ref[0]) + sh_ref[0]
    logits = _dot3(hl, wr_ref[...])
    scores_t = _sigmoid(logits).T
    best_g = _best_group(scores_t, rb_ref[...])
    onehot = jnp.concatenate([(best_g == g).astype(F32) for g in range(N_GROUPS)]
                             + [jnp.zeros((SUBLANES - N_GROUPS, tm), F32)], axis=0)
    before = _dot(onehot.astype(BF16), triu_ref[...])
    base = base_scr[...]
    rank = jnp.sum(onehot * (before + base[:, 0:1]), axis=0, keepdims=True)
    pos = best_g * cap + rank.astype(jnp.int32)
    pos_ref[0] = pos
    base = base + jnp.sum(onehot, axis=1, keepdims=True)
    base_scr[...] = base
    cnt_ref[...] = base

    pltpu.make_async_copy(h_scr.at[prev], hs_ref.at[pl.ds(0, tm), :], row_sem).wait()
    h_scr[slot] = hl
    pos_v[...] = pos
    pltpu.make_async_copy(pos_v, pos_s.at[pl.ds(slot, 1), :], meta_sem).start()

    @pl.when(i == n_steps - 1)
    def _():
        pltpu.make_async_copy(pos_v, pos_s.at[pl.ds(slot, 1), :], meta_sem).wait()

        def issue(r, carry):
            _rows_copy(h_scr.at[slot], hs_ref, row_sem, r, pos_s[slot, r]).start()
            return carry
        lax.fori_loop(0, tm, issue, 0, unroll=ROW_DMA_UNROLL)
        pltpu.make_async_copy(h_scr.at[slot], hs_ref.at[pl.ds(0, tm), :], row_sem).wait()
        cnt_v[...] = base.astype(jnp.int32)
        counts = pltpu.make_async_copy(cnt_v, cnt_s, meta_sem)
        counts.start()
        counts.wait()
        h_scr[prev] = jnp.zeros((tm, h_scr.shape[2]), F32)
        zero = h_scr.at[prev]

        def pad_copies(action):
            for g in range(N_GROUPS):
                cnt = cnt_s[g, 0]
                up = (cnt + SUBLANES - 1) // SUBLANES * SUBLANES
                for r in range(SUBLANES - 1):
                    @pl.when(cnt + r < up)
                    def _():
                        action(_rows_copy(zero, hs_ref, row_sem, 0, g * cap + cnt + r))
                action(pltpu.make_async_copy(zero, hs_ref.at[pl.ds(pl.multiple_of(g * cap + up, SUBLANES), tm), :],
                                             row_sem))

        pad_copies(lambda cp: cp.start())
        pad_copies(lambda cp: cp.wait())


def _outproj(xs, o_halves, gate_src, gate_cols, hn_gain, w_out, w_idx, mod, layer, nf_gain, wr_pad, rb_col,
             seq, nb, n_rows, first_half_sigmoid):
    t, d = xs.shape
    tm = TOK_TILE
    half = d // 2
    base = layer * SUBLANES * 6

    def mod_spec(which):
        return pl.BlockSpec((1, 1, d), lambda i: (base + _mod_row(i, tm, seq, nb) * 6 + which, 0, 0))

    in_specs = [pl.BlockSpec((tm, d), lambda i: (i, 0))]
    args = [xs]
    for ((arr_f, dir_f), (arr_b, dir_b), col), gcol in zip(o_halves, gate_cols):
        in_specs += [pl.BlockSpec((1, tm, half), lambda i, col=col, dd=dir_f: (dd, i, col)),
                     pl.BlockSpec((1, tm, half), lambda i, col=col, dd=dir_b: (dd, i, col)),
                     pl.BlockSpec((tm, half), lambda i, gcol=gcol: (i, gcol))]
        args += [arr_f, arr_b, gate_src]
    in_specs += [pl.BlockSpec((1, d), lambda i: (0, 0)),
                 pl.BlockSpec((None, d, d), lambda i: (w_idx, 0, 0)),
                 mod_spec(2), mod_spec(4), mod_spec(3),
                 pl.BlockSpec((1, d), lambda i: (0, 0)),
                 pl.BlockSpec((d, LANES), lambda i: (0, 0)),
                 pl.BlockSpec((N_EXPERTS, 1), lambda i: (0, 0)),
                 pl.BlockSpec((tm, tm), lambda i: (0, 0))]
    idx = np.arange(tm)
    triu = jnp.asarray(idx[:, None] < idx[None, :], dtype=BF16)
    args += [hn_gain.reshape(1, d), w_out, mod, mod, mod, nf_gain.reshape(1, d), wr_pad, rb_col, triu]
    n_tiles = n_rows // tm
    assert tm == MOE_TILE
    cap = n_rows + MOE_TILE
    return pl.pallas_call(
        functools.partial(_outproj_kernel, first_half_sigmoid=first_half_sigmoid, cap=cap),
        grid=(n_tiles,),
        in_specs=in_specs,
        out_specs=[pl.BlockSpec((tm, d), lambda i: (i, 0)),
                   pl.BlockSpec((1, 1, tm), lambda i: (i, 0, 0)),
                   pl.BlockSpec((SUBLANES, LANES), lambda i: (0, 0)),
                   pl.BlockSpec(memory_space=pl.ANY)],
        out_shape=[jax.ShapeDtypeStruct((t, d), F32), jax.ShapeDtypeStruct((n_tiles, 1, tm), jnp.int32),
                   jax.ShapeDtypeStruct((SUBLANES, LANES), F32),
                   jax.ShapeDtypeStruct((N_GROUPS * cap + tm, d), F32)],
        scratch_shapes=[pltpu.VMEM((SUBLANES, LANES), F32), pltpu.VMEM((2, tm, d), F32),
                        pltpu.VMEM((1, tm), jnp.int32), pltpu.SMEM((2, tm), jnp.int32),
                        pltpu.VMEM((SUBLANES, LANES), jnp.int32), pltpu.SMEM((SUBLANES, LANES), jnp.int32),
                        pltpu.SemaphoreType.DMA(()), pltpu.SemaphoreType.DMA(())],
        compiler_params=_cparams(("arbitrary",)),
        name="outproj",
    )(*args)


def _group_gates(x, wr, bias_row, grp):
    scores = _sigmoid(_dot3(x, wr))
    lane_i = lax.broadcasted_iota(jnp.int32, scores.shape, 1)
    lane = lane_i.astype(F32)
    m = jnp.where(lane_i // EXPERTS_PER_GROUP == grp, scores + bias_row, -jnp.inf)
    picks = []
    for _ in range(2):
        top = jnp.max(m, axis=-1, keepdims=True)
        idx = jnp.min(jnp.where(m == top, lane, float(LANES)), axis=-1, keepdims=True)
        picks.append(idx)
        m = jnp.where(lane == idx, -jnp.inf, m)
    w = [jnp.sum(jnp.where(lane == idx, scores, 0.0), axis=-1, keepdims=True) for idx in picks]
    tot = w[0] + w[1]
    return jnp.where(lane == picks[0], w[0] / tot, jnp.where(lane == picks[1], w[1] / tot, 0.0))


def _group_ffn_kernel(tg_ref, tb_ref, nv_ref, x_ref, wr_ref, rb_ref, w1_ref, w3_ref, w2_ref, o_ref):
    i = pl.program_id(0)

    @pl.when(i < nv_ref[0])
    def _():
        grp = tg_ref[i]
        x = x_ref[...]
        xb = x.astype(BF16)
        gate = _group_gates(x, wr_ref[...], rb_ref[...], grp)
        lane = lax.broadcasted_iota(jnp.int32, gate.shape, 1)
        acc = None
        for j in range(EXPERTS_PER_GROUP):
            gcol = jnp.sum(jnp.where(lane == grp * EXPERTS_PER_GROUP + j, gate, 0.0), axis=-1, keepdims=True)
            a = _silu(_dot(xb, w1_ref[j])) * _dot(xb, w3_ref[j]) * gcol
            y = _dot(a.astype(BF16), w2_ref[j])
            acc = y if acc is None else acc + y
        o_ref[...] = acc


def _group_ffn(h_sorted, tile_grp, tile_blk, n_valid, wr_pad, rb_row, w1, w3, w2, cap, layer):
    p_rows, d = h_sorted.shape
    tm = MOE_TILE
    f = w1.shape[2]
    n_tiles = tile_grp.shape[0]
    blocks_per_group = cap // tm
    row_blk = lambda i, tg, tb, nv: (tg[i] * blocks_per_group + tb[i], 0)
    experts = lambda i, tg, tb, nv: (layer * N_GROUPS + tg[i], 0, 0)
    w_in_spec = pl.BlockSpec((EXPERTS_PER_GROUP, d, f), experts)
    return pl.pallas_call(
        _group_ffn_kernel,
        grid_spec=pltpu.PrefetchScalarGridSpec(
            num_scalar_prefetch=3, grid=(n_tiles,),
            in_specs=[pl.BlockSpec((tm, d), row_blk),
                      pl.BlockSpec((d, LANES), lambda i, tg, tb, nv: (0, 0)),
                      pl.BlockSpec((1, LANES), lambda i, tg, tb, nv: (0, 0)),
                      w_in_spec, w_in_spec, pl.BlockSpec((EXPERTS_PER_GROUP, f, d), experts)],
            out_specs=pl.BlockSpec((tm, d), row_blk)),
        out_shape=jax.ShapeDtypeStruct((p_rows, d), F32),
        compiler_params=_cparams(("arbitrary",)),
        name="moe_group_ffn",
    )(tile_grp, tile_blk, n_valid, h_sorted, wr_pad, rb_row, w1, w3, w2)


def _combine_kernel(pos_ref, x_ref, y_ref, g2_ref, nfin_ref, o_ref, buf, sem, *, final_norm):
    i = pl.program_id(0)
    tm = x_ref.shape[0]
    slot = i % 2

    def gather(tile, slot_):
        def issue(r, carry):
            _rows_copy(y_ref, buf.at[slot_], sem.at[slot_], pos_ref[tile * tm + r], r).start()
            return carry
        lax.fori_loop(0, tm, issue, 0, unroll=ROW_DMA_UNROLL)

    @pl.when(i == 0)
    def _():
        gather(0, 0)

    @pl.when(i + 1 < pl.num_programs(0))
    def _():
        gather(i + 1, 1 - slot)

    pltpu.make_async_copy(y_ref.at[pl.ds(0, tm), :], buf.at[slot], sem.at[slot]).wait()
    y = x_ref[...] + g2_ref[0] * buf[slot]
    if final_norm:
        y = y * lax.rsqrt(jnp.mean(y * y, axis=-1, keepdims=True) + EPS) * nfin_ref[...]
    o_ref[...] = y


def _combine(xs, y_sorted, pos, mod, layer, nfin, seq, nb, n_rows, final_norm):
    t, d = xs.shape
    tm = TOK_TILE
    base = layer * SUBLANES * 6
    out_rows = n_rows if final_norm else t
    return pl.pallas_call(
        functools.partial(_combine_kernel, final_norm=final_norm),
        grid_spec=pltpu.PrefetchScalarGridSpec(
            num_scalar_prefetch=1, grid=(n_rows // tm,),
            in_specs=[pl.BlockSpec((tm, d), lambda i, pos: (i, 0)),
                      pl.BlockSpec(memory_space=pl.ANY),
                      pl.BlockSpec((1, 1, d), lambda i, pos: (base + _mod_row(i, tm, seq, nb) * 6 + 5, 0, 0)),
                      pl.BlockSpec((1, d), lambda i, pos: (0, 0))],
            out_specs=pl.BlockSpec((tm, d), lambda i, pos: (i, 0)),
            scratch_shapes=[pltpu.VMEM((2, tm, d), F32), pltpu.SemaphoreType.DMA((2,))]),
        out_shape=jax.ShapeDtypeStruct((out_rows, d), F32),
        compiler_params=_cparams(("arbitrary",)),
        name="moe_combine",
    )(pos, xs, y_sorted, mod, nfin.reshape(1, d))


def _moe_ffn(h_sorted, pos_rows, cnt, wr_pad, rb_row, w1, w3, w2, layer, n_rows):
    tm = MOE_TILE
    cap = n_rows + tm
    pos = pos_rows.reshape(-1)
    counts = cnt[:N_GROUPS, 0].astype(jnp.int32)
    tiles = (counts + tm - 1) // tm
    ends = jnp.cumsum(tiles)
    n_tiles = n_rows // tm + N_GROUPS
    step = jnp.minimum(jnp.arange(n_tiles, dtype=jnp.int32), ends[-1] - 1)
    tile_grp = jnp.sum(step[:, None] >= ends[None, :], axis=1).astype(jnp.int32)
    tile_blk = step - (ends - tiles)[tile_grp]
    y_sorted = _group_ffn(h_sorted, tile_grp, tile_blk, ends[-1:].astype(jnp.int32), wr_pad, rb_row, w1, w3, w2, cap,
                          layer)
    return y_sorted, pos


def kernel(x, c, ctx, c_ctx, w_mod, b_mod, norm_mix, norm_ffn, norm_final, ab_w_in, ab_i_bias, ab_f_bias,
           ab_conv, ab_a_log, ab_dt_bias, ab_norm_a, ab_norm_b, ab_w_out, c_w_in, c_f_bias, c_lb_raw, c_norm,
           c_w_out, w_router, router_bias, w1, w3, w2):
    nb, seq, d = x.shape
    n_ctx = ctx.shape[1]
    depth = w_mod.shape[0]
    n_lat = nb * seq
    assert nb + 1 <= SUBLANES and seq % 1024 == 0 and (nb * n_ctx) % 1024 == 0 and n_ctx % 256 == 0

    xs = jnp.concatenate([x.reshape(n_lat, d), ctx.reshape(nb * n_ctx, d)], axis=0)
    cpad = jnp.zeros((SUBLANES, d), F32).at[:nb].set(c).at[nb].set(c_ctx)
    mod = _modulation(cpad, w_mod, b_mod).reshape(depth * SUBLANES * 6, 1, d)

    lb_p = jax.nn.softmax(c_lb_raw.astype(F32), axis=0)
    lb_all = jnp.cumsum(lb_p, axis=0) - lb_p[0:1]

    wr_pad = jnp.zeros((d, LANES), F32).at[:, :N_EXPERTS].set(w_router)
    rb_col = router_bias.astype(F32).reshape(N_EXPERTS, 1)
    rb_row = jnp.zeros((1, LANES), F32).at[0, :N_EXPERTS].set(router_bias.astype(F32))
    w_main = H_AB * HEAD * 8
    ab_w_in_b, c_w_in_b = ab_w_in.astype(BF16), c_w_in.astype(BF16)
    ab_w_out_b, c_w_out_b = ab_w_out.astype(BF16), c_w_out.astype(BF16)
    n_exp, _, d_ff = w1.shape[1:]
    w1_b = w1.astype(BF16).reshape(depth * n_exp, d, d_ff)
    w3_b = w3.astype(BF16).reshape(depth * n_exp, d, d_ff)
    w2_b = w2.astype(BF16).reshape(depth * n_exp, d_ff, d)

    out = None
    column_major = False
    pending = None
    for layer in range(depth):
        last = layer == depth - 1
        j = layer // 2
        want_cm = layer % 2 == 1 and j % 2 == 1
        if want_cm != column_major:
            if pending is not None:
                xs = _combine(xs, *pending[:2], mod, pending[2], norm_final, seq, nb, xs.shape[0], False)
                pending = None
            if want_cm:
                xs = _to_column_major(xs, n_lat, seq)
            else:
                xs = jnp.concatenate([_from_column_major(xs[:n_lat], seq), xs[n_lat:]], axis=0)
            column_major = want_cm
        if layer % 2 == 0:
            w_in = ab_w_in[j]
            gw = w_in[:, w_main:].reshape(d, 4, 2, H_AB)
            wg = jnp.zeros((2, d, LANES), F32).at[:, :, :4 * H_AB].set(
                jnp.transpose(gw, (2, 0, 1, 3)).reshape(2, d, 4 * H_AB)).astype(BF16)
            zeros = jnp.zeros((2, H_AB), F32)
            bias = jnp.concatenate([ab_i_bias[j], ab_f_bias[j], zeros, ab_dt_bias[j]], axis=1)
            alog = jnp.concatenate([zeros, zeros, zeros, ab_a_log[j]], axis=1)
            gpar = jnp.zeros((2, SUBLANES, LANES), F32).at[:, 0, :4 * H_AB].set(bias).at[:, 1, :4 * H_AB].set(alog)
            p, gates, *xs_new = _inproj(xs, norm_mix[layer], mod, layer, ab_w_in_b, j, w_main, seq, nb, wg, gpar,
                                        pending=pending)
            ha_f, ha_b = _mlstm_scan(p, gates, seq, n_ctx, nb)
            qkv = _gdn_prep(p, ab_conv[j], seq, n_ctx, nb)
            ob_f, ob_b = _gdn_scan(qkv, gates, seq, n_ctx, nb)
            o_halves = (((ha_f, 0), (ha_b, 0), 0), ((ob_f, 0), (ob_b, 0), 0))
            gate_cols = (3, 7)
            hn_gain = jnp.concatenate([ab_norm_a[j], ab_norm_b[j]])
            w_out = ab_w_out_b
        else:
            p, *xs_new = _inproj(xs, norm_mix[layer], mod, layer, c_w_in_b, j, c_w_in.shape[2], seq, nb,
                                 silu_cols=H_C * HEAD, pending=pending)
            oc_f, oc_b = _gla_scan(p, c_f_bias[j], lb_all[j], seq, n_ctx, nb)
            o_halves = (((oc_f, 0), (oc_b, 0), 0), ((oc_f, 0), (oc_b, 0), 1))
            gate_cols = (8, 9)
            hn_gain = c_norm[j]
            w_out = c_w_out_b
        if pending is not None:
            (xs,) = xs_new
            pending = None
        n_rows = n_lat if last else n_lat + nb * n_ctx
        xs, pos_rows, cnt, h_sorted = _outproj(xs, o_halves, p, gate_cols, hn_gain, w_out, j, mod, layer,
                                               norm_ffn[layer], wr_pad, rb_col, seq, nb, n_rows, layer % 2 == 0)
        y_sorted, pos = _moe_ffn(h_sorted, pos_rows, cnt, wr_pad, rb_row, w1_b, w3_b, w2_b, layer, n_rows)
        if last:
            res = _combine(xs, y_sorted, pos, mod, layer, norm_final, seq, nb, n_rows, True)
            out = _from_column_major(res, seq) if column_major else res
        else:
            pending = (y_sorted, pos, layer)
    return out.reshape(nb, seq, d)
```

```python
import functools

import numpy as np
import jax
import jax.numpy as jnp
from jax import lax
from jax.experimental import pallas as pl
from jax.experimental.pallas import tpu as pltpu

F32 = jnp.float32
BF16 = jnp.bfloat16

EPS = 1e-6
GRID_W = 64
HEAD = 128
H_AB = 4
H_C = 8
CONV_K = 5
N_EXPERTS = 16
N_GROUPS = 4
EXPERTS_PER_GROUP = N_EXPERTS // N_GROUPS
LANES = 128
SUBLANES = 8
VMEM_LIMIT = 56 * 1024 * 1024

TOK_TILE = 512
MOE_TILE = 512
ROW_DMA_UNROLL = 8
MLSTM_CHUNK = 128
GDN_CHUNK = 64
GDN_SCAN_CHUNKS = 4
GLA_CHUNK = 128
GDN_INV_BLOCK = 16


def _cparams(sem):
    return pltpu.CompilerParams(dimension_semantics=sem, vmem_limit_bytes=VMEM_LIMIT)


def _dot(a, b):
    return jnp.dot(a, b, preferred_element_type=F32)


def _dot_nt(a, b):
    return lax.dot_general(a, b, (((1,), (1,)), ((), ())), preferred_element_type=F32)


def _dot_tn(a, b):
    return lax.dot_general(a, b, (((0,), (0,)), ((), ())), preferred_element_type=F32)


def _split2(x):
    hi = x.astype(BF16)
    lo = (x - hi.astype(F32)).astype(BF16)
    return hi, lo


def _split3(x):
    hi = x.astype(BF16)
    r = x - hi.astype(F32)
    mid = r.astype(BF16)
    lo = (r - mid.astype(F32)).astype(BF16)
    return hi, mid, lo


def _dot_exact_lhs(m_bf16, x):
    hi, mid, lo = _split3(x)
    return _dot(m_bf16, hi) + _dot(m_bf16, mid) + _dot(m_bf16, lo)


def _dot3(a, b):
    ah, al = _split2(a)
    bh, bl = _split2(b)
    return _dot(ah, bh) + _dot(ah, bl) + _dot(al, bh)


def _dot_nt3(a, b):
    ah, al = _split2(a)
    bh, bl = _split2(b)
    return _dot_nt(ah, bh) + _dot_nt(ah, bl) + _dot_nt(al, bh)


def _sigmoid(z):
    return 1.0 / (1.0 + jnp.exp(-z))


def _sigmoid_pair(z):
    return _sigmoid(z), _sigmoid(-z)


def _silu(z):
    return z * _sigmoid(z)


def _mod_kernel(c_ref, w_ref, b_ref, o_ref):
    s = _silu(c_ref[...]).astype(BF16)
    o_ref[0] = _dot(s, w_ref[0].astype(BF16)) + b_ref[0]


def _modulation(cpad, w_mod, b_mod):
    depth, d, n = w_mod.shape
    tn = 1536
    return pl.pallas_call(
        _mod_kernel,
        grid=(depth, n // tn),
        in_specs=[pl.BlockSpec((SUBLANES, d), lambda l, j: (0, 0)),
                  pl.BlockSpec((1, d, tn), lambda l, j: (l, 0, j)),
                  pl.BlockSpec((1, 1, tn), lambda l, j: (l, 0, j))],
        out_specs=pl.BlockSpec((1, SUBLANES, tn), lambda l, j: (l, 0, j)),
        out_shape=jax.ShapeDtypeStruct((depth, SUBLANES, n), F32),
        compiler_params=_cparams(("parallel", "parallel")),
        name="modulation",
    )(cpad, w_mod, b_mod.reshape(depth, 1, n))


def _gate_activations(raw, par):
    y = raw + par[0:1, :]
    e = jnp.exp(-jnp.abs(y))
    sp = jnp.log1p(e)
    lsig = jnp.minimum(y, 0.0) - sp
    r = 1.0 / (1.0 + e)
    sig = jnp.where(y >= 0, r, e * r)
    gdec = -jnp.exp(par[1:2, :]) * (jnp.maximum(y, 0.0) + sp)
    lane = lax.broadcasted_iota(jnp.int32, raw.shape, 1)
    return jnp.where(lane < 4, y, jnp.where(lane < 8, lsig, jnp.where(lane < 12, sig, gdec)))


def _rows_copy(src, dst, sem, src_row, dst_row):
    return pltpu.make_async_copy(src.at[pl.ds(src_row, 1), :], dst.at[pl.ds(dst_row, 1), :], sem)


def _inproj_kernel(*refs, with_gates, p_cols, silu_cols, fused_combine):
    refs = list(refs)
    pos_ref = refs.pop(0) if fused_combine else None
    x_ref, gain_ref, sc_ref, sh_ref, w_ref = refs[:5]
    rest = refs[5:]
    if with_gates:
        wg_ref, gpar_ref = rest[:2]
        rest = rest[2:]
    if fused_combine:
        y_ref, g2_ref = rest[:2]
        rest = rest[2:]
    p_ref = rest.pop(0)
    if with_gates:
        g_ref = rest.pop(0)
    x = x_ref[...]
    if fused_combine:
        xs_ref, buf, sem = rest
        i = pl.program_id(0)
        n_steps = pl.num_programs(0)
        tm = x_ref.shape[0]
        slot = i % 2

        @pl.when(i == 0)
        def _():
            def issue(r, carry):
                _rows_copy(y_ref, buf.at[0], sem.at[0], pos_ref[r], r).start()
                return carry
            lax.fori_loop(0, tm, issue, 0, unroll=ROW_DMA_UNROLL)

        nxt = jnp.minimum(i + 1, n_steps - 1)
        for r in range(tm):
            _rows_copy(y_ref, buf.at[1 - slot], sem.at[1 - slot], pos_ref[nxt * tm + r], r).start()
        pltpu.make_async_copy(y_ref.at[pl.ds(0, tm), :], buf.at[slot], sem.at[slot]).wait()
        x = x + g2_ref[0] * buf[slot]
        xs_ref[...] = x

        @pl.when(i == n_steps - 1)
        def _():
            pltpu.make_async_copy(y_ref.at[pl.ds(0, tm), :], buf.at[1 - slot], sem.at[1 - slot]).wait()
    y = x * lax.rsqrt(jnp.mean(x * x, axis=-1, keepdims=True) + EPS) * gain_ref[...]
    u = (y * (1.0 + sc_ref[0]) + sh_ref[0]).astype(BF16)
    if with_gates:
        for d in range(2):
            g_ref[d] = _gate_activations(_dot(u, wg_ref[d]), gpar_ref[d])
    tn = 1024
    for j in range(p_cols // tn):
        res = _dot(u, w_ref[:, j * tn:(j + 1) * tn])
        if (j + 1) * tn <= silu_cols:
            res = _silu(res)
        p_ref[:, j * tn:(j + 1) * tn] = res


def _mod_row(i, tm, seq, nb):
    return jnp.minimum((i * tm) // seq, nb)


def _inproj(xs, gain, mod, layer, w_all, w_idx, p, seq, nb, wg=None, gpar=None, silu_cols=0, pending=None):
    t, d = xs.shape
    tm = TOK_TILE
    with_gates = wg is not None
    fused = pending is not None
    base = layer * SUBLANES * 6

    def mod_spec(which, base_=base):
        return pl.BlockSpec((1, 1, d), lambda i, *_: (base_ + _mod_row(i, tm, seq, nb) * 6 + which, 0, 0))

    in_specs = [pl.BlockSpec((tm, d), lambda i, *_: (i, 0)),
                pl.BlockSpec((1, d), lambda i, *_: (0, 0)),
                mod_spec(1), mod_spec(0),
                pl.BlockSpec((None, d, p), lambda i, *_: (w_idx, 0, 0), pipeline_mode=pl.Buffered(1))]
    args = [xs, gain.reshape(1, d), mod, mod, w_all]
    out_specs = [pl.BlockSpec((tm, p), lambda i, *_: (i, 0))]
    out_shape = [jax.ShapeDtypeStruct((t, p), F32)]
    scratch, prefetch = [], []
    if with_gates:
        in_specs += [pl.BlockSpec((2, d, LANES), lambda i, *_: (0, 0, 0)),
                     pl.BlockSpec((2, SUBLANES, LANES), lambda i, *_: (0, 0, 0))]
        args += [wg, gpar]
        out_specs.append(pl.BlockSpec((2, tm, LANES), lambda i, *_: (0, i, 0)))
        out_shape.append(jax.ShapeDtypeStruct((2, t, LANES), F32))
    if fused:
        y_sorted, pos, prev_layer = pending
        in_specs += [pl.BlockSpec(memory_space=pl.ANY), mod_spec(5, prev_layer * SUBLANES * 6)]
        args += [y_sorted, mod]
        out_specs.append(pl.BlockSpec((tm, d), lambda i, *_: (i, 0)))
        out_shape.append(jax.ShapeDtypeStruct((t, d), F32))
        scratch = [pltpu.VMEM((2, tm, d), F32), pltpu.SemaphoreType.DMA((2,))]
        prefetch = [pos]
    return pl.pallas_call(
        functools.partial(_inproj_kernel, with_gates=with_gates, p_cols=p, silu_cols=silu_cols, fused_combine=fused),
        grid_spec=pltpu.PrefetchScalarGridSpec(
            num_scalar_prefetch=len(prefetch), grid=(t // tm,),
            in_specs=in_specs, out_specs=out_specs, scratch_shapes=scratch),
        out_shape=out_shape,
        compiler_params=_cparams(("arbitrary",) if fused else ("parallel",)),
        name="inproj",
    )(*prefetch, *args)


def _gdn_prep_kernel(*refs, blocks_per_seq, n_lat_blocks, scale):
    w_ref, o_ref = refs[9], refs[10]
    i = pl.program_id(0)
    tm = o_ref.shape[0]
    w = refs[1].shape[1]
    in_lat = i < n_lat_blocks
    first = jnp.logical_or(jnp.logical_not(in_lat), i % blocks_per_seq == 0)
    last = jnp.logical_or(jnp.logical_not(in_lat), i % blocks_per_seq == blocks_per_seq - 1)
    pad = (CONV_K - 1) // 2
    for part in range(3):
        prev_ref, x_ref, next_ref = refs[3 * part:3 * part + 3]
        prev = jnp.where(first, 0.0, prev_ref[...])
        nxt = jnp.where(last, 0.0, next_ref[...])
        xe = jnp.concatenate([prev, x_ref[...], nxt], axis=0)
        n = xe.shape[0]
        acc = None
        for j in range(CONV_K):
            shift = (pad - j) % n
            xr = xe if shift == 0 else pltpu.roll(xe, shift, 0)
            term = xr[SUBLANES:SUBLANES + tm] * w_ref[j:j + 1, part * w:(part + 1) * w]
            acc = term if acc is None else acc + term
        y = _silu(acc)
        for h in range(H_AB):
            t = y[:, h * HEAD:(h + 1) * HEAD]
            if part < 2:
                t = t * lax.rsqrt(jnp.sum(t * t, axis=-1, keepdims=True) + EPS)
            if part == 0:
                t = t * scale
            o_ref[:, part * w + h * HEAD:part * w + (h + 1) * HEAD] = t


def _gdn_prep(p, conv_w, seq, n_ctx, nb):
    t = p.shape[0]
    tm = 256
    w = H_AB * HEAD
    first_col = 4
    wpad = jnp.zeros((SUBLANES, 3 * w), F32).at[:CONV_K].set(conv_w)
    hb = tm // SUBLANES
    nblk8 = t // SUBLANES
    in_specs = []
    for c in range(3):
        in_specs += [pl.BlockSpec((SUBLANES, w), lambda i, c=c: (jnp.maximum(i * hb - 1, 0), first_col + c)),
                     pl.BlockSpec((tm, w), lambda i, c=c: (i, first_col + c)),
                     pl.BlockSpec((SUBLANES, w), lambda i, c=c: (jnp.minimum((i + 1) * hb, nblk8 - 1), first_col + c))]
    in_specs.append(pl.BlockSpec((SUBLANES, 3 * w), lambda i: (0, 0)))
    return pl.pallas_call(
        functools.partial(_gdn_prep_kernel, blocks_per_seq=seq // tm, n_lat_blocks=nb * seq // tm,
                          scale=HEAD ** -0.5),
        grid=(t // tm,),
        in_specs=in_specs,
        out_specs=pl.BlockSpec((tm, 3 * w), lambda i: (i, 0)),
        out_shape=jax.ShapeDtypeStruct((t, 3 * w), F32),
        compiler_params=_cparams(("parallel",)),
        name="gdn_prep",
    )(*([p] * 9), wpad)


def _tri_tables(c):
    idx = np.arange(c)
    fwd = (idx[None, :] <= idx[:, None]).astype(np.float32)
    return jnp.asarray(np.stack([fwd, fwd.T]))


def _to_cm_kernel(x3_ref, x2_ref, o_ref, *, n_lat_tiles):
    i = pl.program_id(0)
    rows = x3_ref.shape[0]

    @pl.when(i < n_lat_tiles)
    def _():
        for wl in range(SUBLANES):
            o_ref[wl * rows:(wl + 1) * rows, :] = x3_ref[:, wl, :]

    @pl.when(i >= n_lat_tiles)
    def _():
        o_ref[...] = x2_ref[...]


def _from_cm_kernel(x_ref, o_ref):
    rows = o_ref.shape[0]
    for wl in range(SUBLANES):
        o_ref[:, wl, :] = x_ref[wl * rows:(wl + 1) * rows, :]


def _to_column_major(xs, n_lat, seq):
    t, d = xs.shape
    rows = seq // GRID_W
    tile = rows * SUBLANES
    assert rows % SUBLANES == 0 and n_lat % tile == 0 and (t - n_lat) % tile == 0
    n_lat_tiles = n_lat // tile
    per_b = GRID_W // SUBLANES
    lat = lambda i: jnp.minimum(i, n_lat_tiles - 1)
    return pl.pallas_call(
        functools.partial(_to_cm_kernel, n_lat_tiles=n_lat_tiles),
        grid=(t // tile,),
        in_specs=[pl.BlockSpec((rows, SUBLANES, d), lambda i: (lat(i) // per_b, lat(i) % per_b, 0)),
                  pl.BlockSpec((tile, d), lambda i: (jnp.maximum(i, n_lat_tiles - 1), 0))],
        out_specs=pl.BlockSpec((tile, d), lambda i: (i, 0)),
        out_shape=jax.ShapeDtypeStruct((t, d), xs.dtype),
        compiler_params=_cparams(("parallel",)),
        name="to_column_major",
    )(xs.reshape(t // GRID_W, GRID_W, d), xs)


def _from_column_major(y, seq):
    n_lat, d = y.shape
    rows = seq // GRID_W
    tile = rows * SUBLANES
    per_b = GRID_W // SUBLANES
    out = pl.pallas_call(
        _from_cm_kernel,
        grid=(n_lat // tile,),
        in_specs=[pl.BlockSpec((tile, d), lambda i: (i, 0))],
        out_specs=pl.BlockSpec((rows, SUBLANES, d), lambda i: (i // per_b, i % per_b, 0)),
        out_shape=jax.ShapeDtypeStruct((n_lat // GRID_W, GRID_W, d), y.dtype),
        compiler_params=_cparams(("parallel",)),
        name="from_column_major",
    )(y)
    return out.reshape(n_lat, d)


def _mlstm_kernel(qf, kf, vf, qb, kb, vb, gf, gb, tri_ref, hf_ref, hb_ref, s_scr, m_scr, *, scale):
    @pl.when(pl.program_id(1) == 0)
    def _():
        s_scr[...] = jnp.zeros_like(s_scr)
        m_scr[...] = jnp.zeros_like(m_scr)

    c = gf.shape[1]
    ones_col = (lax.broadcasted_iota(jnp.int32, (c, HEAD), 1) == 0).astype(BF16)
    qkv = ((qf, kf, vf), (qb, kb, vb))
    h_out = (hf_ref, hb_ref)
    g = [gf[0], gb[0]]
    mask = [tri_ref[d] > 0.5 for d in range(2)]
    b = [_dot_exact_lhs(tri_ref[d].astype(BF16), g[d]) for d in range(2)]
    total = [jnp.sum(x, axis=0, keepdims=True) for x in g]
    g_t = [x.T for x in g]
    b_t = [x.T for x in b]
    chains = [(d, h) for d in range(2) for h in range(H_AB)]
    sl = [slice(h * HEAD, (h + 1) * HEAD) for d, h in chains]
    q = [qkv[d][0][:, sl[i]].astype(BF16) for i, (d, h) in enumerate(chains)]
    k = [qkv[d][1][:, sl[i]] * scale for i, (d, h) in enumerate(chains)]
    v_aug = [jnp.concatenate([qkv[d][2][:, sl[i]].astype(BF16), ones_col], axis=1) for i, (d, h) in enumerate(chains)]
    s = [s_scr[d, h] for d, h in chains]
    qk = [_dot_nt(q[i], k[i].astype(BF16)) for i in range(len(chains))]
    qs = [_dot(q[i], s[i].astype(BF16)) for i in range(len(chains))]
    w_intra, w_inter, m_ts, colvs, m_sts, tots = [], [], [], [], [], []
    for i, (d, h) in enumerate(chains):
        li_c, li_r = g[d][:, h:h + 1], g_t[d][h:h + 1, :]
        b_c, b_r = b[d][:, 4 + h:5 + h], b_t[d][4 + h:5 + h, :]
        tot = total[d][:, 4 + h:5 + h]
        m_st = m_scr[d, h][0:1, 0:1]
        d_log = jnp.where(mask[d], b_c - b_r + li_r, -jnp.inf)
        m_inter = b_c + m_st
        m_t = jnp.maximum(m_inter, jnp.max(d_log, axis=-1, keepdims=True))
        w_inter.append(jnp.exp(m_inter - m_t))
        w_intra.append((jnp.exp(d_log - m_t) * qk[i]).astype(BF16))
        m_ts.append(m_t); colvs.append(tot - b_c + li_c); m_sts.append(m_st); tots.append(tot)
    intra = [_dot(w_intra[i], v_aug[i]) for i in range(len(chains))]
    m_new = [jnp.maximum(tots[i] + m_sts[i], jnp.max(colvs[i], axis=0, keepdims=True)) for i in range(len(chains))]
    upd = [_dot_tn((k[i] * jnp.exp(colvs[i] - m_new[i])).astype(BF16), v_aug[i]) for i in range(len(chains))]
    for i, (d, h) in enumerate(chains):
        num = w_inter[i] * qs[i] + intra[i]
        den = num[:, HEAD:HEAD + 1]
        h_out[d][0, :, sl[i]] = num[:, :HEAD] / jnp.maximum(jnp.abs(den), jnp.exp(-m_ts[i]))
        s_scr[d, h] = jnp.exp(tots[i] + m_sts[i] - m_new[i]) * s[i] + upd[i]
        m_scr[d, h] = jnp.broadcast_to(m_new[i], (SUBLANES, LANES))


def _step_block(chunk, seq, n_ctx, nb):
    n_c, n_l = n_ctx // chunk, seq // chunk
    off = nb * n_l

    def fn(b, d, i):
        j = i - n_c
        ctx_blk = off + b * n_c + (i if d == 0 else n_c - 1 - i)
        lat_blk = b * n_l + (j if d == 0 else n_l - 1 - j)
        return jnp.where(i < n_c, ctx_blk, lat_blk)

    return n_c + n_l, fn


def _mlstm_scan(p, gates, seq, n_ctx, nb):
    t = p.shape[0]
    c = MLSTM_CHUNK
    w = H_AB * HEAD
    n_steps, blk = _step_block(c, seq, n_ctx, nb)
    in_specs = [pl.BlockSpec((c, w), lambda b, i, d=d, col=col: (blk(b, d, i), col)) for d in range(2) for col in range(3)]
    in_specs += [pl.BlockSpec((1, c, LANES), lambda b, i, d=d: (d, blk(b, d, i), 0)) for d in range(2)]
    in_specs.append(pl.BlockSpec((2, c, c), lambda b, i: (0, 0, 0)))
    return pl.pallas_call(
        functools.partial(_mlstm_kernel, scale=HEAD ** -0.5),
        grid=(nb, n_steps),
        in_specs=in_specs,
        out_specs=[pl.BlockSpec((1, c, w), lambda b, i, d=d: (0, blk(b, d, i), 0)) for d in range(2)],
        out_shape=[jax.ShapeDtypeStruct((1, t, w), F32)] * 2,
        scratch_shapes=[pltpu.VMEM((2, H_AB, HEAD, 2 * HEAD), F32), pltpu.VMEM((2, H_AB, SUBLANES, LANES), F32)],
        compiler_params=_cparams(("parallel", "arbitrary")),
        name="mlstm_scan",
    )(p, p, p, p, p, p, gates, gates, _tri_tables(c))


def _gdn_inv_masks(c):
    idx = np.arange(c)
    blk = lambda n: (idx[:, None] // n) == (idx[None, :] // n)
    levels = [blk(GDN_INV_BLOCK)]
    n = GDN_INV_BLOCK
    while n < c:
        levels.append(np.logical_and(blk(2 * n), np.logical_not(blk(n))))
        n *= 2
    levels = np.stack(levels).astype(np.float32)
    return jnp.asarray(np.tile(levels, (1, 1, H_AB)))


def _block_diag(x, n_blocks):
    blk = lax.broadcasted_iota(jnp.int32, x.shape, 1) // (x.shape[1] // n_blocks)
    return jnp.concatenate([jnp.where(blk == j, x, jnp.zeros_like(x)) for j in range(n_blocks)], axis=0)


def _heads_to_lanes(x, first, width):
    return jnp.concatenate([jnp.broadcast_to(x[:, first + h:first + h + 1], (x.shape[0], width))
                            for h in range(H_AB)], axis=1)


def _unit_tri_inverse(a_list, eye, inv_masks):
    prod = lambda xs, ys: [_dot(x.astype(BF16), _block_diag(y.astype(BF16), H_AB)) for x, y in zip(xs, ys)]
    pw = [a * inv_masks[0] for a in a_list]
    inv = [eye - p for p in pw]
    n = 2
    while n < GDN_INV_BLOCK:
        pw = prod(pw, pw)
        inv = [i + t for i, t in zip(inv, prod(inv, pw))]
        n *= 2
    for lvl in range(1, inv_masks.shape[0]):
        t = prod(prod(inv, [a * inv_masks[lvl] for a in a_list]), inv)
        inv = [i - x for i, x in zip(inv, t)]
    return inv


def _gdn_chunk_kernel(q_ref, k_ref, v_ref, g_ref, tri_ref, im_ref, u_ref, w_ref, qd_ref, kd_ref, att_ref):
    c = GDN_CHUNK
    n_chunks = q_ref.shape[0] // c
    inv_masks = im_ref[...]
    row = lax.broadcasted_iota(jnp.int32, (c, H_AB * c), 0)
    col = lax.broadcasted_iota(jnp.int32, (c, H_AB * c), 1) % c
    eye_b = row == col
    eye = eye_b.astype(F32)
    masks = [col <= row, col >= row]
    stricts = [col < row, col > row]
    tris = [tri_ref[d].astype(BF16) for d in range(2)]
    ones = jnp.ones((c, c), BF16)
    chunks = range(n_chunks)
    rows = [slice(ci * c, (ci + 1) * c) for ci in chunks]
    groups = [(ci, d) for ci in chunks for d in range(2)]
    k_hl = [_split2(k_ref[r, :]) for r in rows]
    kbd = [(_block_diag(hi, H_AB), _block_diag(lo, H_AB)) for hi, lo in k_hl]
    r_hi = [_dot_nt(jnp.concatenate([k_hl[ci][0], k_hl[ci][1], q_ref[rows[ci], :].astype(BF16)], axis=0), kbd[ci][0])
            for ci in chunks]
    r_lo = [_dot_nt(k_hl[ci][0], kbd[ci][1]) for ci in chunks]
    kk = [r_hi[ci][:c] + r_hi[ci][c:2 * c] + r_lo[ci] for ci in chunks]
    g = [g_ref[d, rows[ci], :] for ci, d in groups]
    gc = [_dot_exact_lhs(tris[d], g[i]) for i, (ci, d) in enumerate(groups)]
    gc_c = [_heads_to_lanes(x, 12, c) for x in gc]
    gc_r = [_dot_exact_lhs(ones, eye * x) for x in gc_c]
    decay = [jnp.exp(jnp.where(masks[d], gc_c[i] - gc_r[i], -jnp.inf)) for i, (ci, d) in enumerate(groups)]
    a = [jnp.where(stricts[d], _heads_to_lanes(g[i], 8, c) * kk[ci] * decay[i], 0.0)
         for i, (ci, d) in enumerate(groups)]
    inv = _unit_tri_inverse(a, eye, inv_masks)
    inv_hl = [jnp.concatenate(_split2(x), axis=0) for x in inv]
    gc_w = [_heads_to_lanes(x, 12, HEAD) for x in gc]
    eg = [jnp.exp(x) for x in gc_w]
    beta = [_heads_to_lanes(x, 8, HEAD) for x in g]
    res_u = [_dot(inv_hl[i], _block_diag((v_ref[rows[ci], :] * beta[i]).astype(BF16), H_AB))
             for i, (ci, d) in enumerate(groups)]
    res_w = [_dot(inv_hl[i], _block_diag((k_ref[rows[ci], :] * (beta[i] * eg[i])).astype(BF16), H_AB))
             for i, (ci, d) in enumerate(groups)]
    for i, (ci, d) in enumerate(groups):
        r = rows[ci]
        total = _heads_to_lanes(jnp.sum(g[i], axis=0, keepdims=True), 12, HEAD)
        u_ref[d, r, :] = (res_u[i][:c] + res_u[i][c:]).astype(BF16)
        w_ref[d, r, :] = (res_w[i][:c] + res_w[i][c:]).astype(BF16)
        qd_ref[d, r, :] = (q_ref[r, :] * eg[i]).astype(BF16)
        kd_ref[d, r, :] = (k_ref[r, :] * jnp.exp(total - gc_w[i])).astype(BF16)
        att_ref[d, r, :] = (r_hi[ci][2 * c:] * decay[i]).astype(BF16)


def _gdn_scan_kernel(*refs):
    ins, o_refs, s_scr = refs[:12], refs[12:14], refs[14]
    c = GDN_CHUNK

    @pl.when(pl.program_id(1) == 0)
    def _():
        s_scr[...] = jnp.zeros_like(s_scr)

    pw = 2 * HEAD
    rblk = lax.broadcasted_iota(jnp.int32, (pw, pw), 0) // HEAD
    cblk = lax.broadcasted_iota(jnp.int32, (pw, pw), 1) // HEAD
    on_diag = rblk == cblk
    chains = [(d, pr) for d in range(2) for pr in range(H_AB // 2)]
    refs_of = lambda d: ins[d::2]
    sl = [slice(pr * pw, (pr + 1) * pw) for d, pr in chains]
    s = [s_scr[d, pr] for d, pr in chains]
    n_sub = o_refs[0].shape[1] // c
    for k in range(n_sub):
        rows = [slice(kk * c, (kk + 1) * c) for kk in (k, n_sub - 1 - k)]
        res = [_dot(jnp.concatenate([refs_of(d)[1][0, rows[d], sl[i]], refs_of(d)[2][0, rows[d], sl[i]]], axis=0),
                    s[i].astype(BF16)) for i, (d, pr) in enumerate(chains)]
        v_new = [(refs_of(d)[0][0, rows[d], sl[i]].astype(F32) - res[i][:c]).astype(BF16)
                 for i, (d, pr) in enumerate(chains)]
        intra = [_dot(refs_of(d)[4][0, rows[d], pr * 2 * c:(pr + 1) * 2 * c], _block_diag(v_new[i], 2))
                 for i, (d, pr) in enumerate(chains)]
        upd = [_dot_tn(refs_of(d)[3][0, rows[d], sl[i]], v_new[i]) for i, (d, pr) in enumerate(chains)]
        totals = [jnp.sum(refs_of(d)[5][0, rows[d], :], axis=0, keepdims=True) for d in range(2)]
        for i, (d, pr) in enumerate(chains):
            o_refs[d][0, rows[d], sl[i]] = res[i][c:] + intra[i]
            gl = jnp.concatenate([jnp.broadcast_to(jnp.exp(totals[d][:, 12 + 2 * pr + j:13 + 2 * pr + j]), (1, HEAD))
                                  for j in range(2)], axis=1)
            s[i] = gl * s[i] + jnp.where(on_diag, upd[i], 0.0)
    for i, (d, pr) in enumerate(chains):
        s_scr[d, pr] = s[i]


def _gdn_scan(qkv, gates, seq, n_ctx, nb):
    t = qkv.shape[0]
    c = GDN_CHUNK
    w = H_AB * HEAD
    tri = _tri_tables(c)
    im = _gdn_inv_masks(c)
    tm = TOK_TILE
    sds = lambda width: jax.ShapeDtypeStruct((2, t, width), BF16)
    u, wv, qd, kd, att = pl.pallas_call(
        _gdn_chunk_kernel,
        grid=(t // tm,),
        in_specs=[pl.BlockSpec((tm, w), lambda i, col=col: (i, col)) for col in range(3)] + [
            pl.BlockSpec((2, tm, LANES), lambda i: (0, i, 0)),
            pl.BlockSpec((2, c, c), lambda i: (0, 0, 0)),
            pl.BlockSpec(im.shape, lambda i: (0, 0, 0))],
        out_specs=[pl.BlockSpec((2, tm, w), lambda i: (0, i, 0))] * 4
        + [pl.BlockSpec((2, tm, H_AB * c), lambda i: (0, i, 0))],
        out_shape=[sds(w)] * 4 + [sds(H_AB * c)],
        compiler_params=_cparams(("parallel",)),
        name="gdn_chunk",
    )(qkv, qkv, qkv, gates, tri, im)

    cs = c * GDN_SCAN_CHUNKS
    n_steps, blk = _step_block(cs, seq, n_ctx, nb)
    in_specs, args = [], []
    for arr, width in ((u, w), (wv, w), (qd, w), (kd, w), (att, H_AB * c), (gates, LANES)):
        for d in range(2):
            in_specs.append(pl.BlockSpec((1, cs, width), lambda b, i, d=d: (d, blk(b, d, i), 0)))
            args.append(arr)
    o_f, o_b = pl.pallas_call(
        _gdn_scan_kernel,
        grid=(nb, n_steps),
        in_specs=in_specs,
        out_specs=[pl.BlockSpec((1, cs, w), lambda b, i, d=d: (0, blk(b, d, i), 0)) for d in range(2)],
        out_shape=[jax.ShapeDtypeStruct((1, t, w), F32)] * 2,
        scratch_shapes=[pltpu.VMEM((2, H_AB // 2, 2 * HEAD, 2 * HEAD), F32)],
        compiler_params=_cparams(("parallel", "arbitrary")),
        name="gdn_scan",
    )(*args)
    return o_f, o_b


def _gla_tables(c):
    idx = np.arange(c)
    sel, sgn, msk = [], [], []
    for d in range(2):
        sel_d, sgn_d, msk_d = [], [], []
        bs = c // 2
        while bs >= 1:
            pair = idx // (2 * bs)
            second = (idx // bs) % 2 == 1
            ref_row = pair * 2 * bs + (bs - 1 if d == 0 else bs)
            sel_d.append((idx[None, :] == ref_row[:, None]).astype(np.float32))
            qside = second if d == 0 else np.logical_not(second)
            sgn_d.append(np.where(qside, 1.0, -1.0)[:, None] * np.ones((1, HEAD)))
            same_pair = pair[:, None] == pair[None, :]
            msk_d.append(np.logical_and(same_pair, np.logical_and(qside[:, None], np.logical_not(qside)[None, :])))
            bs //= 2
        msk_d.append(np.eye(c, dtype=bool))
        sel.append(np.concatenate(sel_d, axis=0))
        sgn.append(np.stack(sgn_d))
        msk.append(np.stack(msk_d).astype(np.float32))
    return (jnp.asarray(np.stack(sel), dtype=BF16), jnp.asarray(np.stack(sgn), dtype=F32),
            jnp.asarray(np.stack(msk), dtype=F32))


def _gla_kernel(qf_ref, zf_ref, vf_ref, qb_ref, zb_ref, vb_ref, fb_ref, lb_ref, tri_ref, sel_ref, sgn_ref, msk_ref,
                of_ref, ob_ref, s_scr):
    @pl.when(pl.program_id(1) == 0)
    def _():
        s_scr[...] = jnp.zeros_like(s_scr)

    c = qf_ref.shape[0]
    n_lvl = sgn_ref.shape[1]
    dirs = range(2)
    qzv = ((qf_ref, zf_ref, vf_ref), (qb_ref, zb_ref, vb_ref))
    o_refs = (of_ref, ob_ref)
    chains = [(d, h) for d in dirs for h in range(H_C)]
    sl = [slice(h * HEAD, (h + 1) * HEAD) for d, h in chains]
    g_all, key_all, q_all = [], [], []
    for d in dirs:
        lb = lb_ref[d]
        sig_pos, sig_neg = _sigmoid_pair(qzv[d][1][...] + fb_ref[d])
        g_all.append(jnp.log2(lb + (1.0 - lb) * sig_pos))
        key_all.append((1.0 - lb) * sig_neg)
        q_all.append(qzv[d][0][...])
    gc_all = []
    for d in dirs:
        g_hi, g_lo = _split2(g_all[d])
        tri = tri_ref[d].astype(BF16)
        gc_all.append(_dot(tri, g_hi) + _dot(tri, g_lo))
    total_all = [jnp.sum(g, axis=0, keepdims=True) for g in g_all]
    refs = [_dot(sel_ref[d], gc_all[d].astype(BF16)) for d in dirs]
    diag = msk_ref[0, n_lvl] > 0.5
    att = [jnp.where(diag, _dot_nt(q_all[d][:, sl[i]].astype(BF16), key_all[d][:, sl[i]].astype(BF16)), 0.0)
           for i, (d, h) in enumerate(chains)]
    for l in range(n_lvl):
        bs = c >> (l + 1)
        x_all = []
        for d in dirs:
            ref_l = refs[d][l * c:(l + 1) * c]
            if bs >= SUBLANES:
                parts = []
                for a in range(0, c, 2 * bs):
                    first, second = slice(a, a + bs), slice(a + bs, a + 2 * bs)
                    q_rows, k_rows = (second, first) if d == 0 else (first, second)
                    xq = q_all[d][q_rows] * jnp.exp2(gc_all[d][q_rows] - ref_l[q_rows])
                    xk = key_all[d][k_rows] * jnp.exp2(ref_l[k_rows] - gc_all[d][k_rows])
                    parts += [xk, xq] if d == 0 else [xq, xk]
                x_all.append(jnp.concatenate(parts, axis=0).astype(BF16))
            else:
                sgn = jnp.concatenate([sgn_ref[d, l]] * H_C, axis=1)
                x_all.append((jnp.where(sgn > 0, q_all[d], key_all[d])
                              * jnp.exp2(sgn * (gc_all[d] - ref_l))).astype(BF16))
        x = [x_all[d][:, sl[i]] for i, (d, h) in enumerate(chains)]
        prod = [_dot_nt(xh, xh) for xh in x]
        in_level = [msk_ref[d, l] > 0.5 for d in dirs]
        att = [jnp.where(in_level[d], prod[i], att[i]) for i, (d, h) in enumerate(chains)]
    st = [s_scr[d, h] for d, h in chains]
    inter = [_dot_nt((q_all[d][:, sl[i]] * jnp.exp2(gc_all[d][:, sl[i]])).astype(BF16), st[i].astype(BF16))
             for i, (d, h) in enumerate(chains)]
    intra = [_dot(att[i].astype(BF16), qzv[d][2][:, sl[i]].astype(BF16)) for i, (d, h) in enumerate(chains)]
    upd = [_dot_tn(qzv[d][2][:, sl[i]].astype(BF16),
                   (key_all[d][:, sl[i]] * jnp.exp2(total_all[d][:, sl[i]] - gc_all[d][:, sl[i]])).astype(BF16))
           for i, (d, h) in enumerate(chains)]
    for i, (d, h) in enumerate(chains):
        o_refs[d][0, :, sl[i]] = inter[i] + intra[i]
        s_scr[d, h] = jnp.exp2(total_all[d][:, sl[i]]) * st[i] + upd[i]


def _gla_scan(p, f_bias, lb, seq, n_ctx, nb):
    t, pw = p.shape
    c = GLA_CHUNK
    w = H_C * HEAD
    tri = _tri_tables(c)
    sel, sgn, msk = _gla_tables(c)
    n_steps, blk = _step_block(c, seq, n_ctx, nb)
    dspec = lambda d, col: pl.BlockSpec((c, w), lambda b, i: (blk(b, d, i), col))
    whole = lambda a: pl.BlockSpec(a.shape, lambda b, i: (0,) * a.ndim)
    fb3, lb3 = f_bias.reshape(2, 1, w), lb.reshape(2, 1, w)
    consts = (fb3, lb3, tri, sel, sgn, msk)
    return pl.pallas_call(
        _gla_kernel,
        grid=(nb, n_steps),
        in_specs=[dspec(0, 0), dspec(0, 1), dspec(0, 3), dspec(1, 0), dspec(1, 2), dspec(1, 3)]
        + [whole(a) for a in consts],
        out_specs=[pl.BlockSpec((1, c, w), lambda b, i, d=d: (0, blk(b, d, i), 0)) for d in range(2)],
        out_shape=[jax.ShapeDtypeStruct((1, t, w), F32)] * 2,
        scratch_shapes=[pltpu.VMEM((2, H_C, HEAD, HEAD), F32)],
        compiler_params=_cparams(("parallel", "arbitrary")),
        name="gla_scan",
    )(p, p, p, p, p, p, *consts)


def _best_group(scores_t, bias):
    sel = [scores_t[e:e + 1, :] + bias[e:e + 1, :] for e in range(N_EXPERTS)]
    gscore = []
    for grp in range(N_GROUPS):
        v = sel[grp * EXPERTS_PER_GROUP:(grp + 1) * EXPERTS_PER_GROUP]
        best = None
        for a in range(EXPERTS_PER_GROUP):
            for b in range(a + 1, EXPERTS_PER_GROUP):
                pair = v[a] + v[b]
                best = pair if best is None else jnp.maximum(best, pair)
        gscore.append(best)
    best_g = jnp.zeros(gscore[0].shape, jnp.int32)
    best_v = gscore[0]
    for grp in range(1, N_GROUPS):
        better = gscore[grp] > best_v
        best_g = jnp.where(better, grp, best_g)
        best_v = jnp.where(better, gscore[grp], best_v)
    return best_g


def _outproj_kernel(x_ref, of0, ob0, gt0, of1, ob1, gt1, hn_ref, w_ref, g1_ref, sc_ref, sh_ref, nf_ref,
                    wr_ref, rb_ref, triu_ref, xo_ref, pos_ref, cnt_ref, hs_ref,
                    base_scr, h_scr, pos_v, pos_s, cnt_v, cnt_s, row_sem, meta_sem, *, first_half_sigmoid, cap):
    i = pl.program_id(0)
    n_steps = pl.num_programs(0)
    tm = x_ref.shape[0]
    slot = i % 2
    prev = 1 - slot
    dump = N_GROUPS * cap

    @pl.when(i == 0)
    def _():
        base_scr[...] = jnp.zeros_like(base_scr)
        h_scr[...] = jnp.zeros_like(h_scr)

        def init(r, carry):
            pos_s[1, r] = dump + r
            return carry
        lax.fori_loop(0, tm, init, 0)

    @pl.when(i > 0)
    def _():
        pltpu.make_async_copy(pos_v, pos_s.at[pl.ds(prev, 1), :], meta_sem).wait()

    for r in range(tm):
        _rows_copy(h_scr.at[prev], hs_ref, row_sem, r, pos_s[prev, r]).start()

    feats = []
    for half, (of, ob, gt) in enumerate(((of0, ob0, gt0), (of1, ob1, gt1))):
        o = of[0] + ob[0]
        gate = gt[...]
        for h in range(o.shape[1] // HEAD):
            sl = slice(h * HEAD, (h + 1) * HEAD)
            t = o[:, sl]
            y = t * lax.rsqrt(jnp.mean(t * t, axis=-1, keepdims=True) + EPS)
            y = y * hn_ref[:, half * o.shape[1] + h * HEAD: half * o.shape[1] + (h + 1) * HEAD]
            gz = gate[:, sl]
            act = _sigmoid(gz) if (half == 0 and first_half_sigmoid) else _silu(gz)
            feats.append((y * act).astype(BF16))
    feats = jnp.concatenate(feats, axis=1)
    xn = x_ref[...] + g1_ref[0] * _dot(feats, w_ref[...])
    xo_ref[...] = xn
    y = xn * lax.rsqrt(jnp.mean(xn * xn, axis=-1, keepdims=True) + EPS) * nf_ref[...]
    hl = y * (1.0 + sc_ref[0]) + sh_ref[0]
    logits = _dot3(hl, wr_ref[...])
    scores_t = _sigmoid(logits).T
    best_g = _best_group(scores_t, rb_ref[...])
    onehot = jnp.concatenate([(best_g == g).astype(F32) for g in range(N_GROUPS)]
                             + [jnp.zeros((SUBLANES - N_GROUPS, tm), F32)], axis=0)
    before = _dot(onehot.astype(BF16), triu_ref[...])
    base = base_scr[...]
    rank = jnp.sum(onehot * (before + base[:, 0:1]), axis=0, keepdims=True)
    pos = best_g * cap + rank.astype(jnp.int32)
    pos_ref[0] = pos
    base = base + jnp.sum(onehot, axis=1, keepdims=True)
    base_scr[...] = base
    cnt_ref[...] = base

    pltpu.make_async_copy(h_scr.at[prev], hs_ref.at[pl.ds(0, tm), :], row_sem).wait()
    h_scr[slot] = hl
    pos_v[...] = pos
    pltpu.make_async_copy(pos_v, pos_s.at[pl.ds(slot, 1), :], meta_sem).start()

    @pl.when(i == n_steps - 1)
    def _():
        pltpu.make_async_copy(pos_v, pos_s.at[pl.ds(slot, 1), :], meta_sem).wait()

        def issue(r, carry):
            _rows_copy(h_scr.at[slot], hs_ref, row_sem, r, pos_s[slot, r]).start()
            return carry
        lax.fori_loop(0, tm, issue, 0, unroll=ROW_DMA_UNROLL)
        pltpu.make_async_copy(h_scr.at[slot], hs_ref.at[pl.ds(0, tm), :], row_sem).wait()
        cnt_v[...] = base.astype(jnp.int32)
        counts = pltpu.make_async_copy(cnt_v, cnt_s, meta_sem)
        counts.start()
        counts.wait()
        h_scr[prev] = jnp.zeros((tm, h_scr.shape[2]), F32)
        zero = h_scr.at[prev]

        def pad_copies(action):
            for g in range(N_GROUPS):
                cnt = cnt_s[g, 0]
                up = (cnt + SUBLANES - 1) // SUBLANES * SUBLANES
                for r in range(SUBLANES - 1):
                    @pl.when(cnt + r < up)
                    def _():
                        action(_rows_copy(zero, hs_ref, row_sem, 0, g * cap + cnt + r))
                action(pltpu.make_async_copy(zero, hs_ref.at[pl.ds(pl.multiple_of(g * cap + up, SUBLANES), tm), :],
                                             row_sem))

        pad_copies(lambda cp: cp.start())
        pad_copies(lambda cp: cp.wait())


def _outproj(xs, o_halves, gate_src, gate_cols, hn_gain, w_out, w_idx, mod, layer, nf_gain, wr_pad, rb_col,
             seq, nb, n_rows, first_half_sigmoid):
    t, d = xs.shape
    tm = TOK_TILE
    half = d // 2
    base = layer * SUBLANES * 6

    def mod_spec(which):
        return pl.BlockSpec((1, 1, d), lambda i: (base + _mod_row(i, tm, seq, nb) * 6 + which, 0, 0))

    in_specs = [pl.BlockSpec((tm, d), lambda i: (i, 0))]
    args = [xs]
    for ((arr_f, dir_f), (arr_b, dir_b), col), gcol in zip(o_halves, gate_cols):
        in_specs += [pl.BlockSpec((1, tm, half), lambda i, col=col, dd=dir_f: (dd, i, col)),
                     pl.BlockSpec((1, tm, half), lambda i, col=col, dd=dir_b: (dd, i, col)),
                     pl.BlockSpec((tm, half), lambda i, gcol=gcol: (i, gcol))]
        args += [arr_f, arr_b, gate_src]
    in_specs += [pl.BlockSpec((1, d), lambda i: (0, 0)),
                 pl.BlockSpec((None, d, d), lambda i: (w_idx, 0, 0)),
                 mod_spec(2), mod_spec(4), mod_spec(3),
                 pl.BlockSpec((1, d), lambda i: (0, 0)),
                 pl.BlockSpec((d, LANES), lambda i: (0, 0)),
                 pl.BlockSpec((N_EXPERTS, 1), lambda i: (0, 0)),
                 pl.BlockSpec((tm, tm), lambda i: (0, 0))]
    idx = np.arange(tm)
    triu = jnp.asarray(idx[:, None] < idx[None, :], dtype=BF16)
    args += [hn_gain.reshape(1, d), w_out, mod, mod, mod, nf_gain.reshape(1, d), wr_pad, rb_col, triu]
    n_tiles = n_rows // tm
    assert tm == MOE_TILE
    cap = n_rows + MOE_TILE
    return pl.pallas_call(
        functools.partial(_outproj_kernel, first_half_sigmoid=first_half_sigmoid, cap=cap),
        grid=(n_tiles,),
        in_specs=in_specs,
        out_specs=[pl.BlockSpec((tm, d), lambda i: (i, 0)),
                   pl.BlockSpec((1, 1, tm), lambda i: (i, 0, 0)),
                   pl.BlockSpec((SUBLANES, LANES), lambda i: (0, 0)),
                   pl.BlockSpec(memory_space=pl.ANY)],
        out_shape=[jax.ShapeDtypeStruct((t, d), F32), jax.ShapeDtypeStruct((n_tiles, 1, tm), jnp.int32),
                   jax.ShapeDtypeStruct((SUBLANES, LANES), F32),
                   jax.ShapeDtypeStruct((N_GROUPS * cap + tm, d), F32)],
        scratch_shapes=[pltpu.VMEM((SUBLANES, LANES), F32), pltpu.VMEM((2, tm, d), F32),
                        pltpu.VMEM((1, tm), jnp.int32), pltpu.SMEM((2, tm), jnp.int32),
                        pltpu.VMEM((SUBLANES, LANES), jnp.int32), pltpu.SMEM((SUBLANES, LANES), jnp.int32),
                        pltpu.SemaphoreType.DMA(()), pltpu.SemaphoreType.DMA(())],
        compiler_params=_cparams(("arbitrary",)),
        name="outproj",
    )(*args)


def _group_gates(x, wr, bias_row, grp):
    scores = _sigmoid(_dot3(x, wr))
    lane_i = lax.broadcasted_iota(jnp.int32, scores.shape, 1)
    lane = lane_i.astype(F32)
    m = jnp.where(lane_i // EXPERTS_PER_GROUP == grp, scores + bias_row, -jnp.inf)
    picks = []
    for _ in range(2):
        top = jnp.max(m, axis=-1, keepdims=True)
        idx = jnp.min(jnp.where(m == top, lane, float(LANES)), axis=-1, keepdims=True)
        picks.append(idx)
        m = jnp.where(lane == idx, -jnp.inf, m)
    w = [jnp.sum(jnp.where(lane == idx, scores, 0.0), axis=-1, keepdims=True) for idx in picks]
    tot = w[0] + w[1]
    return jnp.where(lane == picks[0], w[0] / tot, jnp.where(lane == picks[1], w[1] / tot, 0.0))


def _group_ffn_kernel(tg_ref, tb_ref, nv_ref, x_ref, wr_ref, rb_ref, w1_ref, w3_ref, w2_ref, o_ref):
    i = pl.program_id(0)

    @pl.when(i < nv_ref[0])
    def _():
        grp = tg_ref[i]
        x = x_ref[...]
        xb = x.astype(BF16)
        gate = _group_gates(x, wr_ref[...], rb_ref[...], grp)
        lane = lax.broadcasted_iota(jnp.int32, gate.shape, 1)
        f = w1_ref.shape[2] // EXPERTS_PER_GROUP
        gexp = jnp.concatenate(
            [jnp.broadcast_to(jnp.sum(jnp.where(lane == grp * EXPERTS_PER_GROUP + j, gate, 0.0), axis=-1, keepdims=True),
                              (gate.shape[0], f)) for j in range(EXPERTS_PER_GROUP)], axis=1)
        a = _silu(_dot(xb, w1_ref[0])) * _dot(xb, w3_ref[0]) * gexp
        o_ref[...] = _dot(a.astype(BF16), w2_ref[0])


def _group_ffn(h_sorted, tile_grp, tile_blk, n_valid, wr_pad, rb_row, w1, w3, w2, cap, layer):
    p_rows, d = h_sorted.shape
    tm = MOE_TILE
    gf = w1.shape[2]
    n_tiles = tile_grp.shape[0]
    blocks_per_group = cap // tm
    row_blk = lambda i, tg, tb, nv: (tg[i] * blocks_per_group + tb[i], 0)
    experts = lambda i, tg, tb, nv: (layer * N_GROUPS + tg[i], 0, 0)
    w_in_spec = pl.BlockSpec((1, d, gf), experts)
    return pl.pallas_call(
        _group_ffn_kernel,
        grid_spec=pltpu.PrefetchScalarGridSpec(
            num_scalar_prefetch=3, grid=(n_tiles,),
            in_specs=[pl.BlockSpec((tm, d), row_blk),
                      pl.BlockSpec((d, LANES), lambda i, tg, tb, nv: (0, 0)),
                      pl.BlockSpec((1, LANES), lambda i, tg, tb, nv: (0, 0)),
                      w_in_spec, w_in_spec, pl.BlockSpec((1, gf, d), experts)],
            out_specs=pl.BlockSpec((tm, d), row_blk)),
        out_shape=jax.ShapeDtypeStruct((p_rows, d), F32),
        compiler_params=_cparams(("arbitrary",)),
        name="moe_group_ffn",
    )(tile_grp, tile_blk, n_valid, h_sorted, wr_pad, rb_row, w1, w3, w2)


def _combine_kernel(pos_ref, x_ref, y_ref, g2_ref, nfin_ref, o_ref, buf, sem, *, final_norm):
    i = pl.program_id(0)
    tm = x_ref.shape[0]
    slot = i % 2

    def gather(tile, slot_):
        def issue(r, carry):
            _rows_copy(y_ref, buf.at[slot_], sem.at[slot_], pos_ref[tile * tm + r], r).start()
            return carry
        lax.fori_loop(0, tm, issue, 0, unroll=ROW_DMA_UNROLL)

    @pl.when(i == 0)
    def _():
        gather(0, 0)

    @pl.when(i + 1 < pl.num_programs(0))
    def _():
        gather(i + 1, 1 - slot)

    pltpu.make_async_copy(y_ref.at[pl.ds(0, tm), :], buf.at[slot], sem.at[slot]).wait()
    y = x_ref[...] + g2_ref[0] * buf[slot]
    if final_norm:
        y = y * lax.rsqrt(jnp.mean(y * y, axis=-1, keepdims=True) + EPS) * nfin_ref[...]
    o_ref[...] = y


def _combine(xs, y_sorted, pos, mod, layer, nfin, seq, nb, n_rows, final_norm):
    t, d = xs.shape
    tm = TOK_TILE
    base = layer * SUBLANES * 6
    out_rows = n_rows if final_norm else t
    return pl.pallas_call(
        functools.partial(_combine_kernel, final_norm=final_norm),
        grid_spec=pltpu.PrefetchScalarGridSpec(
            num_scalar_prefetch=1, grid=(n_rows // tm,),
            in_specs=[pl.BlockSpec((tm, d), lambda i, pos: (i, 0)),
                      pl.BlockSpec(memory_space=pl.ANY),
                      pl.BlockSpec((1, 1, d), lambda i, pos: (base + _mod_row(i, tm, seq, nb) * 6 + 5, 0, 0)),
                      pl.BlockSpec((1, d), lambda i, pos: (0, 0))],
            out_specs=pl.BlockSpec((tm, d), lambda i, pos: (i, 0)),
            scratch_shapes=[pltpu.VMEM((2, tm, d), F32), pltpu.SemaphoreType.DMA((2,))]),
        out_shape=jax.ShapeDtypeStruct((out_rows, d), F32),
        compiler_params=_cparams(("arbitrary",)),
        name="moe_combine",
    )(pos, xs, y_sorted, mod, nfin.reshape(1, d))


def _moe_ffn(h_sorted, pos_rows, cnt, wr_pad, rb_row, w1, w3, w2, layer, n_rows):
    tm = MOE_TILE
    cap = n_rows + tm
    pos = pos_rows.reshape(-1)
    counts = cnt[:N_GROUPS, 0].astype(jnp.int32)
    tiles = (counts + tm - 1) // tm
    ends = jnp.cumsum(tiles)
    n_tiles = n_rows // tm + N_GROUPS
    step = jnp.minimum(jnp.arange(n_tiles, dtype=jnp.int32), ends[-1] - 1)
    tile_grp = jnp.sum(step[:, None] >= ends[None, :], axis=1).astype(jnp.int32)
    tile_blk = step - (ends - tiles)[tile_grp]
    y_sorted = _group_ffn(h_sorted, tile_grp, tile_blk, ends[-1:].astype(jnp.int32), wr_pad, rb_row, w1, w3, w2, cap,
                          layer)
    return y_sorted, pos


def kernel(x, c, ctx, c_ctx, w_mod, b_mod, norm_mix, norm_ffn, norm_final, ab_w_in, ab_i_bias, ab_f_bias,
           ab_conv, ab_a_log, ab_dt_bias, ab_norm_a, ab_norm_b, ab_w_out, c_w_in, c_f_bias, c_lb_raw, c_norm,
           c_w_out, w_router, router_bias, w1, w3, w2):
    nb, seq, d = x.shape
    n_ctx = ctx.shape[1]
    depth = w_mod.shape[0]
    n_lat = nb * seq
    assert nb + 1 <= SUBLANES and seq % 1024 == 0 and (nb * n_ctx) % 1024 == 0 and n_ctx % 256 == 0

    xs = jnp.concatenate([x.reshape(n_lat, d), ctx.reshape(nb * n_ctx, d)], axis=0)
    cpad = jnp.zeros((SUBLANES, d), F32).at[:nb].set(c).at[nb].set(c_ctx)
    mod = _modulation(cpad, w_mod, b_mod).reshape(depth * SUBLANES * 6, 1, d)

    lb_p = jax.nn.softmax(c_lb_raw.astype(F32), axis=0)
    lb_all = jnp.cumsum(lb_p, axis=0) - lb_p[0:1]

    wr_pad = jnp.zeros((d, LANES), F32).at[:, :N_EXPERTS].set(w_router)
    rb_col = router_bias.astype(F32).reshape(N_EXPERTS, 1)
    rb_row = jnp.zeros((1, LANES), F32).at[0, :N_EXPERTS].set(router_bias.astype(F32))
    w_main = H_AB * HEAD * 8
    ab_w_in_b, c_w_in_b = ab_w_in.astype(BF16), c_w_in.astype(BF16)
    ab_w_out_b, c_w_out_b = ab_w_out.astype(BF16), c_w_out.astype(BF16)
    n_exp, _, d_ff = w1.shape[1:]
    n_grp = depth * n_exp // EXPERTS_PER_GROUP
    side_by_side = lambda w: jnp.transpose(w.astype(BF16).reshape(n_grp, EXPERTS_PER_GROUP, d, d_ff),
                                           (0, 2, 1, 3)).reshape(n_grp, d, EXPERTS_PER_GROUP * d_ff)
    w1_b, w3_b = side_by_side(w1), side_by_side(w3)
    w2_b = w2.astype(BF16).reshape(n_grp, EXPERTS_PER_GROUP * d_ff, d)

    out = None
    column_major = False
    pending = None
    for layer in range(depth):
        last = layer == depth - 1
        j = layer // 2
        want_cm = layer % 2 == 1 and j % 2 == 1
        if want_cm != column_major:
            if pending is not None:
                xs = _combine(xs, *pending[:2], mod, pending[2], norm_final, seq, nb, xs.shape[0], False)
                pending = None
            if want_cm:
                xs = _to_column_major(xs, n_lat, seq)
            else:
                xs = jnp.concatenate([_from_column_major(xs[:n_lat], seq), xs[n_lat:]], axis=0)
            column_major = want_cm
        if layer % 2 == 0:
            w_in = ab_w_in[j]
            gw = w_in[:, w_main:].reshape(d, 4, 2, H_AB)
            wg = jnp.zeros((2, d, LANES), F32).at[:, :, :4 * H_AB].set(
                jnp.transpose(gw, (2, 0, 1, 3)).reshape(2, d, 4 * H_AB)).astype(BF16)
            zeros = jnp.zeros((2, H_AB), F32)
            bias = jnp.concatenate([ab_i_bias[j], ab_f_bias[j], zeros, ab_dt_bias[j]], axis=1)
            alog = jnp.concatenate([zeros, zeros, zeros, ab_a_log[j]], axis=1)
            gpar = jnp.zeros((2, SUBLANES, LANES), F32).at[:, 0, :4 * H_AB].set(bias).at[:, 1, :4 * H_AB].set(alog)
            p, gates, *xs_new = _inproj(xs, norm_mix[layer], mod, layer, ab_w_in_b, j, w_main, seq, nb, wg, gpar,
                                        pending=pending)
            ha_f, ha_b = _mlstm_scan(p, gates, seq, n_ctx, nb)
            qkv = _gdn_prep(p, ab_conv[j], seq, n_ctx, nb)
            ob_f, ob_b = _gdn_scan(qkv, gates, seq, n_ctx, nb)
            o_halves = (((ha_f, 0), (ha_b, 0), 0), ((ob_f, 0), (ob_b, 0), 0))
            gate_cols = (3, 7)
            hn_gain = jnp.concatenate([ab_norm_a[j], ab_norm_b[j]])
            w_out = ab_w_out_b
        else:
            p, *xs_new = _inproj(xs, norm_mix[layer], mod, layer, c_w_in_b, j, c_w_in.shape[2], seq, nb,
                                 silu_cols=H_C * HEAD, pending=pending)
            oc_f, oc_b = _gla_scan(p, c_f_bias[j], lb_all[j], seq, n_ctx, nb)
            o_halves = (((oc_f, 0), (oc_b, 0), 0), ((oc_f, 0), (oc_b, 0), 1))
            gate_cols = (8, 9)
            hn_gain = c_norm[j]
            w_out = c_w_out_b
        if pending is not None:
            (xs,) = xs_new
            pending = None
        n_rows = n_lat if last else n_lat + nb * n_ctx
        xs, pos_rows, cnt, h_sorted = _outproj(xs, o_halves, p, gate_cols, hn_gain, w_out, j, mod, layer,
                                               norm_ffn[layer], wr_pad, rb_col, seq, nb, n_rows, layer % 2 == 0)
        y_sorted, pos = _moe_ffn(h_sorted, pos_rows, cnt, wr_pad, rb_row, w1_b, w3_b, w2_b, layer, n_rows)
        if last:
            res = _combine(xs, y_sorted, pos, mod, layer, norm_final, seq, nb, n_rows, True)
            out = _from_column_major(res, seq) if column_major else res
        else:
            pending = (y_sorted, pos, layer)
    return out.reshape(nb, seq, d)
```

```python
import functools

import numpy as np
import jax
import jax.numpy as jnp
from jax import lax
from jax.experimental import pallas as pl
from jax.experimental.pallas import tpu as pltpu

F32 = jnp.float32
BF16 = jnp.bfloat16

EPS = 1e-6
GRID_W = 64
HEAD = 128
H_AB = 4
H_C = 8
CONV_K = 5
N_EXPERTS = 16
N_GROUPS = 4
EXPERTS_PER_GROUP = N_EXPERTS // N_GROUPS
LANES = 128
SUBLANES = 8
VMEM_LIMIT = 56 * 1024 * 1024

TOK_TILE = 512
MOE_TILE = 512
ROW_DMA_UNROLL = 8
MLSTM_CHUNK = 128
GDN_CHUNK = 64
GDN_SCAN_CHUNKS = 4
GLA_CHUNK = 128
GDN_INV_BLOCK = 16


def _cparams(sem):
    return pltpu.CompilerParams(dimension_semantics=sem, vmem_limit_bytes=VMEM_LIMIT)


def _dot(a, b):
    return jnp.dot(a, b, preferred_element_type=F32)


def _dot_nt(a, b):
    return lax.dot_general(a, b, (((1,), (1,)), ((), ())), preferred_element_type=F32)


def _dot_tn(a, b):
    return lax.dot_general(a, b, (((0,), (0,)), ((), ())), preferred_element_type=F32)


def _split2(x):
    hi = x.astype(BF16)
    lo = (x - hi.astype(F32)).astype(BF16)
    return hi, lo


def _split3(x):
    hi = x.astype(BF16)
    r = x - hi.astype(F32)
    mid = r.astype(BF16)
    lo = (r - mid.astype(F32)).astype(BF16)
    return hi, mid, lo


def _dot_exact_lhs(m_bf16, x):
    hi, mid, lo = _split3(x)
    return _dot(m_bf16, hi) + _dot(m_bf16, mid) + _dot(m_bf16, lo)


def _dot3(a, b):
    ah, al = _split2(a)
    bh, bl = _split2(b)
    return _dot(ah, bh) + _dot(ah, bl) + _dot(al, bh)


def _dot_nt3(a, b):
    ah, al = _split2(a)
    bh, bl = _split2(b)
    return _dot_nt(ah, bh) + _dot_nt(ah, bl) + _dot_nt(al, bh)


def _sigmoid(z):
    return 1.0 / (1.0 + jnp.exp(-z))


def _sigmoid_pair(z):
    return _sigmoid(z), _sigmoid(-z)


def _silu(z):
    return z * _sigmoid(z)


def _mod_kernel(c_ref, w_ref, b_ref, o_ref):
    s = _silu(c_ref[...]).astype(BF16)
    o_ref[0] = _dot(s, w_ref[0].astype(BF16)) + b_ref[0]


def _modulation(cpad, w_mod, b_mod):
    depth, d, n = w_mod.shape
    tn = 1536
    return pl.pallas_call(
        _mod_kernel,
        grid=(depth, n // tn),
        in_specs=[pl.BlockSpec((SUBLANES, d), lambda l, j: (0, 0)),
                  pl.BlockSpec((1, d, tn), lambda l, j: (l, 0, j)),
                  pl.BlockSpec((1, 1, tn), lambda l, j: (l, 0, j))],
        out_specs=pl.BlockSpec((1, SUBLANES, tn), lambda l, j: (l, 0, j)),
        out_shape=jax.ShapeDtypeStruct((depth, SUBLANES, n), F32),
        compiler_params=_cparams(("parallel", "parallel")),
        name="modulation",
    )(cpad, w_mod, b_mod.reshape(depth, 1, n))


def _gate_activations(raw, par):
    y = raw + par[0:1, :]
    e = jnp.exp(-jnp.abs(y))
    sp = jnp.log1p(e)
    lsig = jnp.minimum(y, 0.0) - sp
    r = 1.0 / (1.0 + e)
    sig = jnp.where(y >= 0, r, e * r)
    gdec = -jnp.exp(par[1:2, :]) * (jnp.maximum(y, 0.0) + sp)
    lane = lax.broadcasted_iota(jnp.int32, raw.shape, 1)
    return jnp.where(lane < 4, y, jnp.where(lane < 8, lsig, jnp.where(lane < 12, sig, gdec)))


def _rows_copy(src, dst, sem, src_row, dst_row):
    return pltpu.make_async_copy(src.at[pl.ds(src_row, 1), :], dst.at[pl.ds(dst_row, 1), :], sem)


def _inproj_kernel(*refs, with_gates, p_cols, silu_cols, fused_combine):
    refs = list(refs)
    pos_ref = refs.pop(0) if fused_combine else None
    x_ref, gain_ref, sc_ref, sh_ref, w_ref = refs[:5]
    rest = refs[5:]
    if with_gates:
        wg_ref, gpar_ref = rest[:2]
        rest = rest[2:]
    if fused_combine:
        y_ref, g2_ref = rest[:2]
        rest = rest[2:]
    p_ref = rest.pop(0)
    if with_gates:
        g_ref = rest.pop(0)
    x = x_ref[...]
    if fused_combine:
        xs_ref, buf, sem = rest
        i = pl.program_id(0)
        n_steps = pl.num_programs(0)
        tm = x_ref.shape[0]
        slot = i % 2

        @pl.when(i == 0)
        def _():
            def issue(r, carry):
                _rows_copy(y_ref, buf.at[0], sem.at[0], pos_ref[r], r).start()
                return carry
            lax.fori_loop(0, tm, issue, 0, unroll=ROW_DMA_UNROLL)

        nxt = jnp.minimum(i + 1, n_steps - 1)
        for r in range(tm):
            _rows_copy(y_ref, buf.at[1 - slot], sem.at[1 - slot], pos_ref[nxt * tm + r], r).start()
        pltpu.make_async_copy(y_ref.at[pl.ds(0, tm), :], buf.at[slot], sem.at[slot]).wait()
        x = x + g2_ref[0] * buf[slot]
        xs_ref[...] = x

        @pl.when(i == n_steps - 1)
        def _():
            pltpu.make_async_copy(y_ref.at[pl.ds(0, tm), :], buf.at[1 - slot], sem.at[1 - slot]).wait()
    y = x * lax.rsqrt(jnp.mean(x * x, axis=-1, keepdims=True) + EPS) * gain_ref[...]
    u = (y * (1.0 + sc_ref[0]) + sh_ref[0]).astype(BF16)
    if with_gates:
        for d in range(2):
            g_ref[d] = _gate_activations(_dot(u, wg_ref[d]), gpar_ref[d])
    tn = 1024
    for j in range(p_cols // tn):
        res = _dot(u, w_ref[:, j * tn:(j + 1) * tn])
        if (j + 1) * tn <= silu_cols:
            res = _silu(res)
        p_ref[:, j * tn:(j + 1) * tn] = res


def _mod_row(i, tm, seq, nb):
    return jnp.minimum((i * tm) // seq, nb)


def _inproj(xs, gain, mod, layer, w_all, w_idx, p, seq, nb, wg=None, gpar=None, silu_cols=0, pending=None):
    t, d = xs.shape
    tm = TOK_TILE
    with_gates = wg is not None
    fused = pending is not None
    base = layer * SUBLANES * 6

    def mod_spec(which, base_=base):
        return pl.BlockSpec((1, 1, d), lambda i, *_: (base_ + _mod_row(i, tm, seq, nb) * 6 + which, 0, 0))

    in_specs = [pl.BlockSpec((tm, d), lambda i, *_: (i, 0)),
                pl.BlockSpec((1, d), lambda i, *_: (0, 0)),
                mod_spec(1), mod_spec(0),
                pl.BlockSpec((None, d, p), lambda i, *_: (w_idx, 0, 0), pipeline_mode=pl.Buffered(1))]
    args = [xs, gain.reshape(1, d), mod, mod, w_all]
    out_specs = [pl.BlockSpec((tm, p), lambda i, *_: (i, 0))]
    out_shape = [jax.ShapeDtypeStruct((t, p), F32)]
    scratch, prefetch = [], []
    if with_gates:
        in_specs += [pl.BlockSpec((2, d, LANES), lambda i, *_: (0, 0, 0)),
                     pl.BlockSpec((2, SUBLANES, LANES), lambda i, *_: (0, 0, 0))]
        args += [wg, gpar]
        out_specs.append(pl.BlockSpec((2, tm, LANES), lambda i, *_: (0, i, 0)))
        out_shape.append(jax.ShapeDtypeStruct((2, t, LANES), F32))
    if fused:
        y_sorted, pos, prev_layer = pending
        in_specs += [pl.BlockSpec(memory_space=pl.ANY), mod_spec(5, prev_layer * SUBLANES * 6)]
        args += [y_sorted, mod]
        out_specs.append(pl.BlockSpec((tm, d), lambda i, *_: (i, 0)))
        out_shape.append(jax.ShapeDtypeStruct((t, d), F32))
        scratch = [pltpu.VMEM((2, tm, d), F32), pltpu.SemaphoreType.DMA((2,))]
        prefetch = [pos]
    return pl.pallas_call(
        functools.partial(_inproj_kernel, with_gates=with_gates, p_cols=p, silu_cols=silu_cols, fused_combine=fused),
        grid_spec=pltpu.PrefetchScalarGridSpec(
            num_scalar_prefetch=len(prefetch), grid=(t // tm,),
            in_specs=in_specs, out_specs=out_specs, scratch_shapes=scratch),
        out_shape=out_shape,
        compiler_params=_cparams(("arbitrary",) if fused else ("parallel",)),
        name="inproj",
    )(*prefetch, *args)


def _gdn_prep_kernel(*refs, blocks_per_seq, n_lat_blocks, scale):
    w_ref, o_ref = refs[9], refs[10]
    i = pl.program_id(0)
    tm = o_ref.shape[0]
    w = refs[1].shape[1]
    in_lat = i < n_lat_blocks
    first = jnp.logical_or(jnp.logical_not(in_lat), i % blocks_per_seq == 0)
    last = jnp.logical_or(jnp.logical_not(in_lat), i % blocks_per_seq == blocks_per_seq - 1)
    pad = (CONV_K - 1) // 2
    for part in range(3):
        prev_ref, x_ref, next_ref = refs[3 * part:3 * part + 3]
        prev = jnp.where(first, 0.0, prev_ref[...])
        nxt = jnp.where(last, 0.0, next_ref[...])
        xe = jnp.concatenate([prev, x_ref[...], nxt], axis=0)
        n = xe.shape[0]
        acc = None
        for j in range(CONV_K):
            shift = (pad - j) % n
            xr = xe if shift == 0 else pltpu.roll(xe, shift, 0)
            term = xr[SUBLANES:SUBLANES + tm] * w_ref[j:j + 1, part * w:(part + 1) * w]
            acc = term if acc is None else acc + term
        y = _silu(acc)
        for h in range(H_AB):
            t = y[:, h * HEAD:(h + 1) * HEAD]
            if part < 2:
                t = t * lax.rsqrt(jnp.sum(t * t, axis=-1, keepdims=True) + EPS)
            if part == 0:
                t = t * scale
            o_ref[:, part * w + h * HEAD:part * w + (h + 1) * HEAD] = t


def _gdn_prep(p, conv_w, seq, n_ctx, nb):
    t = p.shape[0]
    tm = 256
    w = H_AB * HEAD
    first_col = 4
    wpad = jnp.zeros((SUBLANES, 3 * w), F32).at[:CONV_K].set(conv_w)
    hb = tm // SUBLANES
    nblk8 = t // SUBLANES
    in_specs = []
    for c in range(3):
        in_specs += [pl.BlockSpec((SUBLANES, w), lambda i, c=c: (jnp.maximum(i * hb - 1, 0), first_col + c)),
                     pl.BlockSpec((tm, w), lambda i, c=c: (i, first_col + c)),
                     pl.BlockSpec((SUBLANES, w), lambda i, c=c: (jnp.minimum((i + 1) * hb, nblk8 - 1), first_col + c))]
    in_specs.append(pl.BlockSpec((SUBLANES, 3 * w), lambda i: (0, 0)))
    return pl.pallas_call(
        functools.partial(_gdn_prep_kernel, blocks_per_seq=seq // tm, n_lat_blocks=nb * seq // tm,
                          scale=HEAD ** -0.5),
        grid=(t // tm,),
        in_specs=in_specs,
        out_specs=pl.BlockSpec((tm, 3 * w), lambda i: (i, 0)),
        out_shape=jax.ShapeDtypeStruct((t, 3 * w), F32),
        compiler_params=_cparams(("parallel",)),
        name="gdn_prep",
    )(*([p] * 9), wpad)


def _tri_tables(c):
    idx = np.arange(c)
    fwd = (idx[None, :] <= idx[:, None]).astype(np.float32)
    return jnp.asarray(np.stack([fwd, fwd.T]))


def _to_cm_kernel(x3_ref, x2_ref, o_ref, *, n_lat_tiles):
    i = pl.program_id(0)
    rows = x3_ref.shape[0]

    @pl.when(i < n_lat_tiles)
    def _():
        for wl in range(SUBLANES):
            o_ref[wl * rows:(wl + 1) * rows, :] = x3_ref[:, wl, :]

    @pl.when(i >= n_lat_tiles)
    def _():
        o_ref[...] = x2_ref[...]


def _from_cm_kernel(x_ref, o_ref):
    rows = o_ref.shape[0]
    for wl in range(SUBLANES):
        o_ref[:, wl, :] = x_ref[wl * rows:(wl + 1) * rows, :]


def _to_column_major(xs, n_lat, seq):
    t, d = xs.shape
    rows = seq // GRID_W
    tile = rows * SUBLANES
    assert rows % SUBLANES == 0 and n_lat % tile == 0 and (t - n_lat) % tile == 0
    n_lat_tiles = n_lat // tile
    per_b = GRID_W // SUBLANES
    lat = lambda i: jnp.minimum(i, n_lat_tiles - 1)
    return pl.pallas_call(
        functools.partial(_to_cm_kernel, n_lat_tiles=n_lat_tiles),
        grid=(t // tile,),
        in_specs=[pl.BlockSpec((rows, SUBLANES, d), lambda i: (lat(i) // per_b, lat(i) % per_b, 0)),
                  pl.BlockSpec((tile, d), lambda i: (jnp.maximum(i, n_lat_tiles - 1), 0))],
        out_specs=pl.BlockSpec((tile, d), lambda i: (i, 0)),
        out_shape=jax.ShapeDtypeStruct((t, d), xs.dtype),
        compiler_params=_cparams(("parallel",)),
        name="to_column_major",
    )(xs.reshape(t // GRID_W, GRID_W, d), xs)


def _from_column_major(y, seq):
    n_lat, d = y.shape
    rows = seq // GRID_W
    tile = rows * SUBLANES
    per_b = GRID_W // SUBLANES
    out = pl.pallas_call(
        _from_cm_kernel,
        grid=(n_lat // tile,),
        in_specs=[pl.BlockSpec((tile, d), lambda i: (i, 0))],
        out_specs=pl.BlockSpec((rows, SUBLANES, d), lambda i: (i // per_b, i % per_b, 0)),
        out_shape=jax.ShapeDtypeStruct((n_lat // GRID_W, GRID_W, d), y.dtype),
        compiler_params=_cparams(("parallel",)),
        name="from_column_major",
    )(y)
    return out.reshape(n_lat, d)


def _mlstm_kernel(qf, kf, vf, qb, kb, vb, gf, gb, tri_ref, hf_ref, hb_ref, s_scr, m_scr, *, scale):
    @pl.when(pl.program_id(1) == 0)
    def _():
        s_scr[...] = jnp.zeros_like(s_scr)
        m_scr[...] = jnp.zeros_like(m_scr)

    c = gf.shape[1]
    ones_col = (lax.broadcasted_iota(jnp.int32, (c, HEAD), 1) == 0).astype(BF16)
    qkv = ((qf, kf, vf), (qb, kb, vb))
    h_out = (hf_ref, hb_ref)
    g = [gf[0], gb[0]]
    mask = [tri_ref[d] > 0.5 for d in range(2)]
    b = [_dot_exact_lhs(tri_ref[d].astype(BF16), g[d]) for d in range(2)]
    total = [jnp.sum(x, axis=0, keepdims=True) for x in g]
    g_t = [x.T for x in g]
    b_t = [x.T for x in b]
    chains = [(d, h) for d in range(2) for h in range(H_AB)]
    sl = [slice(h * HEAD, (h + 1) * HEAD) for d, h in chains]
    q = [qkv[d][0][:, sl[i]].astype(BF16) for i, (d, h) in enumerate(chains)]
    k = [qkv[d][1][:, sl[i]] * scale for i, (d, h) in enumerate(chains)]
    v_aug = [jnp.concatenate([qkv[d][2][:, sl[i]].astype(BF16), ones_col], axis=1) for i, (d, h) in enumerate(chains)]
    s = [s_scr[d, h] for d, h in chains]
    qk = [_dot_nt(q[i], k[i].astype(BF16)) for i in range(len(chains))]
    qs = [_dot(q[i], s[i].astype(BF16)) for i in range(len(chains))]
    w_intra, w_inter, m_ts, colvs, m_sts, tots = [], [], [], [], [], []
    for i, (d, h) in enumerate(chains):
        li_c, li_r = g[d][:, h:h + 1], g_t[d][h:h + 1, :]
        b_c, b_r = b[d][:, 4 + h:5 + h], b_t[d][4 + h:5 + h, :]
        tot = total[d][:, 4 + h:5 + h]
        m_st = m_scr[d, h][0:1, 0:1]
        d_log = jnp.where(mask[d], b_c - b_r + li_r, -jnp.inf)
        m_inter = b_c + m_st
        m_t = jnp.maximum(m_inter, jnp.max(d_log, axis=-1, keepdims=True))
        w_inter.append(jnp.exp(m_inter - m_t))
        w_intra.append((jnp.exp(d_log - m_t) * qk[i]).astype(BF16))
        m_ts.append(m_t); colvs.append(tot - b_c + li_c); m_sts.append(m_st); tots.append(tot)
    intra = [_dot(w_intra[i], v_aug[i]) for i in range(len(chains))]
    m_new = [jnp.maximum(tots[i] + m_sts[i], jnp.max(colvs[i], axis=0, keepdims=True)) for i in range(len(chains))]
    upd = [_dot_tn((k[i] * jnp.exp(colvs[i] - m_new[i])).astype(BF16), v_aug[i]) for i in range(len(chains))]
    for i, (d, h) in enumerate(chains):
        num = w_inter[i] * qs[i] + intra[i]
        den = num[:, HEAD:HEAD + 1]
        h_out[d][0, :, sl[i]] = num[:, :HEAD] / jnp.maximum(jnp.abs(den), jnp.exp(-m_ts[i]))
        s_scr[d, h] = jnp.exp(tots[i] + m_sts[i] - m_new[i]) * s[i] + upd[i]
        m_scr[d, h] = jnp.broadcast_to(m_new[i], (SUBLANES, LANES))


def _step_block(chunk, seq, n_ctx, nb):
    n_c, n_l = n_ctx // chunk, seq // chunk
    off = nb * n_l

    def fn(b, d, i):
        j = i - n_c
        ctx_blk = off + b * n_c + (i if d == 0 else n_c - 1 - i)
        lat_blk = b * n_l + (j if d == 0 else n_l - 1 - j)
        return jnp.where(i < n_c, ctx_blk, lat_blk)

    return n_c + n_l, fn


def _mlstm_scan(p, gates, seq, n_ctx, nb):
    t = p.shape[0]
    c = MLSTM_CHUNK
    w = H_AB * HEAD
    n_steps, blk = _step_block(c, seq, n_ctx, nb)
    in_specs = [pl.BlockSpec((c, w), lambda b, i, d=d, col=col: (blk(b, d, i), col)) for d in range(2) for col in range(3)]
    in_specs += [pl.BlockSpec((1, c, LANES), lambda b, i, d=d: (d, blk(b, d, i), 0)) for d in range(2)]
    in_specs.append(pl.BlockSpec((2, c, c), lambda b, i: (0, 0, 0)))
    return pl.pallas_call(
        functools.partial(_mlstm_kernel, scale=HEAD ** -0.5),
        grid=(nb, n_steps),
        in_specs=in_specs,
        out_specs=[pl.BlockSpec((1, c, w), lambda b, i, d=d: (0, blk(b, d, i), 0)) for d in range(2)],
        out_shape=[jax.ShapeDtypeStruct((1, t, w), F32)] * 2,
        scratch_shapes=[pltpu.VMEM((2, H_AB, HEAD, 2 * HEAD), F32), pltpu.VMEM((2, H_AB, SUBLANES, LANES), F32)],
        compiler_params=_cparams(("parallel", "arbitrary")),
        name="mlstm_scan",
    )(p, p, p, p, p, p, gates, gates, _tri_tables(c))


def _gdn_inv_masks(c):
    idx = np.arange(c)
    blk = lambda n: (idx[:, None] // n) == (idx[None, :] // n)
    levels = [blk(GDN_INV_BLOCK)]
    n = GDN_INV_BLOCK
    while n < c:
        levels.append(np.logical_and(blk(2 * n), np.logical_not(blk(n))))
        n *= 2
    levels = np.stack(levels).astype(np.float32)
    return jnp.asarray(np.tile(levels, (1, 1, H_AB)))


def _block_diag(x, n_blocks):
    blk = lax.broadcasted_iota(jnp.int32, x.shape, 1) // (x.shape[1] // n_blocks)
    return jnp.concatenate([jnp.where(blk == j, x, jnp.zeros_like(x)) for j in range(n_blocks)], axis=0)


def _heads_to_lanes(x, first, width):
    return jnp.concatenate([jnp.broadcast_to(x[:, first + h:first + h + 1], (x.shape[0], width))
                            for h in range(H_AB)], axis=1)


def _unit_tri_inverse(a_list, eye, inv_masks):
    prod = lambda xs, ys: [_dot(x.astype(BF16), _block_diag(y.astype(BF16), H_AB)) for x, y in zip(xs, ys)]
    pw = [a * inv_masks[0] for a in a_list]
    inv = [eye - p for p in pw]
    n = 2
    while n < GDN_INV_BLOCK:
        pw = prod(pw, pw)
        inv = [i + t for i, t in zip(inv, prod(inv, pw))]
        n *= 2
    for lvl in range(1, inv_masks.shape[0]):
        t = prod(prod(inv, [a * inv_masks[lvl] for a in a_list]), inv)
        inv = [i - x for i, x in zip(inv, t)]
    return inv


def _gdn_chunk_kernel(q_ref, k_ref, v_ref, g_ref, tri_ref, im_ref, u_ref, w_ref, qd_ref, kd_ref, att_ref):
    c = GDN_CHUNK
    n_chunks = q_ref.shape[0] // c
    inv_masks = im_ref[...]
    row = lax.broadcasted_iota(jnp.int32, (c, H_AB * c), 0)
    col = lax.broadcasted_iota(jnp.int32, (c, H_AB * c), 1) % c
    eye_b = row == col
    eye = eye_b.astype(F32)
    masks = [col <= row, col >= row]
    stricts = [col < row, col > row]
    tris = [tri_ref[d].astype(BF16) for d in range(2)]
    ones = jnp.ones((c, c), BF16)
    chunks = range(n_chunks)
    rows = [slice(ci * c, (ci + 1) * c) for ci in chunks]
    groups = [(ci, d) for ci in chunks for d in range(2)]
    k_hl = [_split2(k_ref[r, :]) for r in rows]
    kbd = [(_block_diag(hi, H_AB), _block_diag(lo, H_AB)) for hi, lo in k_hl]
    r_hi = [_dot_nt(jnp.concatenate([k_hl[ci][0], k_hl[ci][1], q_ref[rows[ci], :].astype(BF16)], axis=0), kbd[ci][0])
            for ci in chunks]
    r_lo = [_dot_nt(k_hl[ci][0], kbd[ci][1]) for ci in chunks]
    kk = [r_hi[ci][:c] + r_hi[ci][c:2 * c] + r_lo[ci] for ci in chunks]
    g = [g_ref[d, rows[ci], :] for ci, d in groups]
    gc = [_dot_exact_lhs(tris[d], g[i]) for i, (ci, d) in enumerate(groups)]
    gc_c = [_heads_to_lanes(x, 12, c) for x in gc]
    gc_r = [_dot_exact_lhs(ones, eye * x) for x in gc_c]
    decay = [jnp.exp(jnp.where(masks[d], gc_c[i] - gc_r[i], -jnp.inf)) for i, (ci, d) in enumerate(groups)]
    a = [jnp.where(stricts[d], _heads_to_lanes(g[i], 8, c) * kk[ci] * decay[i], 0.0)
         for i, (ci, d) in enumerate(groups)]
    inv = _unit_tri_inverse(a, eye, inv_masks)
    inv_hl = [jnp.concatenate(_split2(x), axis=0) for x in inv]
    gc_w = [_heads_to_lanes(x, 12, HEAD) for x in gc]
    eg = [jnp.exp(x) for x in gc_w]
    beta = [_heads_to_lanes(x, 8, HEAD) for x in g]
    res_u = [_dot(inv_hl[i], _block_diag((v_ref[rows[ci], :] * beta[i]).astype(BF16), H_AB))
             for i, (ci, d) in enumerate(groups)]
    res_w = [_dot(inv_hl[i], _block_diag((k_ref[rows[ci], :] * (beta[i] * eg[i])).astype(BF16), H_AB))
             for i, (ci, d) in enumerate(groups)]
    for i, (ci, d) in enumerate(groups):
        r = rows[ci]
        total = _heads_to_lanes(jnp.sum(g[i], axis=0, keepdims=True), 12, HEAD)
        u_ref[d, r, :] = (res_u[i][:c] + res_u[i][c:]).astype(BF16)
        w_ref[d, r, :] = (res_w[i][:c] + res_w[i][c:]).astype(BF16)
        qd_ref[d, r, :] = (q_ref[r, :] * eg[i]).astype(BF16)
        kd_ref[d, r, :] = (k_ref[r, :] * jnp.exp(total - gc_w[i])).astype(BF16)
        att_ref[d, r, :] = (r_hi[ci][2 * c:] * decay[i]).astype(BF16)


def _gdn_scan_kernel(*refs):
    ins, o_refs, s_scr = refs[:12], refs[12:14], refs[14]
    c = GDN_CHUNK

    @pl.when(pl.program_id(1) == 0)
    def _():
        s_scr[...] = jnp.zeros_like(s_scr)

    pw = 2 * HEAD
    rblk = lax.broadcasted_iota(jnp.int32, (pw, pw), 0) // HEAD
    cblk = lax.broadcasted_iota(jnp.int32, (pw, pw), 1) // HEAD
    on_diag = rblk == cblk
    chains = [(d, pr) for d in range(2) for pr in range(H_AB // 2)]
    refs_of = lambda d: ins[d::2]
    sl = [slice(pr * pw, (pr + 1) * pw) for d, pr in chains]
    s = [s_scr[d, pr] for d, pr in chains]
    n_sub = o_refs[0].shape[1] // c
    for k in range(n_sub):
        rows = [slice(kk * c, (kk + 1) * c) for kk in (k, n_sub - 1 - k)]
        res = [_dot(jnp.concatenate([refs_of(d)[1][0, rows[d], sl[i]], refs_of(d)[2][0, rows[d], sl[i]]], axis=0),
                    s[i].astype(BF16)) for i, (d, pr) in enumerate(chains)]
        v_new = [(refs_of(d)[0][0, rows[d], sl[i]].astype(F32) - res[i][:c]).astype(BF16)
                 for i, (d, pr) in enumerate(chains)]
        intra = [_dot(refs_of(d)[4][0, rows[d], pr * 2 * c:(pr + 1) * 2 * c], _block_diag(v_new[i], 2))
                 for i, (d, pr) in enumerate(chains)]
        upd = [_dot_tn(refs_of(d)[3][0, rows[d], sl[i]], v_new[i]) for i, (d, pr) in enumerate(chains)]
        totals = [jnp.sum(refs_of(d)[5][0, rows[d], :], axis=0, keepdims=True) for d in range(2)]
        for i, (d, pr) in enumerate(chains):
            o_refs[d][0, rows[d], sl[i]] = res[i][c:] + intra[i]
            gl = jnp.concatenate([jnp.broadcast_to(jnp.exp(totals[d][:, 12 + 2 * pr + j:13 + 2 * pr + j]), (1, HEAD))
                                  for j in range(2)], axis=1)
            s[i] = gl * s[i] + jnp.where(on_diag, upd[i], 0.0)
    for i, (d, pr) in enumerate(chains):
        s_scr[d, pr] = s[i]


def _gdn_scan(qkv, gates, seq, n_ctx, nb):
    t = qkv.shape[0]
    c = GDN_CHUNK
    w = H_AB * HEAD
    tri = _tri_tables(c)
    im = _gdn_inv_masks(c)
    tm = TOK_TILE
    sds = lambda width: jax.ShapeDtypeStruct((2, t, width), BF16)
    u, wv, qd, kd, att = pl.pallas_call(
        _gdn_chunk_kernel,
        grid=(t // tm,),
        in_specs=[pl.BlockSpec((tm, w), lambda i, col=col: (i, col)) for col in range(3)] + [
            pl.BlockSpec((2, tm, LANES), lambda i: (0, i, 0)),
            pl.BlockSpec((2, c, c), lambda i: (0, 0, 0)),
            pl.BlockSpec(im.shape, lambda i: (0, 0, 0))],
        out_specs=[pl.BlockSpec((2, tm, w), lambda i: (0, i, 0))] * 4
        + [pl.BlockSpec((2, tm, H_AB * c), lambda i: (0, i, 0))],
        out_shape=[sds(w)] * 4 + [sds(H_AB * c)],
        compiler_params=_cparams(("parallel",)),
        name="gdn_chunk",
    )(qkv, qkv, qkv, gates, tri, im)

    cs = c * GDN_SCAN_CHUNKS
    n_steps, blk = _step_block(cs, seq, n_ctx, nb)
    in_specs, args = [], []
    for arr, width in ((u, w), (wv, w), (qd, w), (kd, w), (att, H_AB * c), (gates, LANES)):
        for d in range(2):
            in_specs.append(pl.BlockSpec((1, cs, width), lambda b, i, d=d: (d, blk(b, d, i), 0)))
            args.append(arr)
    o_f, o_b = pl.pallas_call(
        _gdn_scan_kernel,
        grid=(nb, n_steps),
        in_specs=in_specs,
        out_specs=[pl.BlockSpec((1, cs, w), lambda b, i, d=d: (0, blk(b, d, i), 0)) for d in range(2)],
        out_shape=[jax.ShapeDtypeStruct((1, t, w), F32)] * 2,
        scratch_shapes=[pltpu.VMEM((2, H_AB // 2, 2 * HEAD, 2 * HEAD), F32)],
        compiler_params=_cparams(("parallel", "arbitrary")),
        name="gdn_scan",
    )(*args)
    return o_f, o_b


def _gla_tables(c):
    idx = np.arange(c)
    sel, sgn, msk = [], [], []
    for d in range(2):
        sel_d, sgn_d, msk_d = [], [], []
        bs = c // 2
        while bs >= 1:
            pair = idx // (2 * bs)
            second = (idx // bs) % 2 == 1
            ref_row = pair * 2 * bs + (bs - 1 if d == 0 else bs)
            sel_d.append((idx[None, :] == ref_row[:, None]).astype(np.float32))
            qside = second if d == 0 else np.logical_not(second)
            sgn_d.append(np.where(qside, 1.0, -1.0)[:, None] * np.ones((1, HEAD)))
            same_pair = pair[:, None] == pair[None, :]
            msk_d.append(np.logical_and(same_pair, np.logical_and(qside[:, None], np.logical_not(qside)[None, :])))
            bs //= 2
        msk_d.append(np.eye(c, dtype=bool))
        sel.append(np.concatenate(sel_d, axis=0))
        sgn.append(np.stack(sgn_d))
        msk.append(np.stack(msk_d).astype(np.float32))
    return (jnp.asarray(np.stack(sel), dtype=BF16), jnp.asarray(np.stack(sgn), dtype=F32),
            jnp.asarray(np.stack(msk), dtype=F32))


def _gla_kernel(qf_ref, zf_ref, vf_ref, qb_ref, zb_ref, vb_ref, fb_ref, lb_ref, tri_ref, sel_ref, sgn_ref, msk_ref,
                of_ref, ob_ref, s_scr):
    @pl.when(pl.program_id(1) == 0)
    def _():
        s_scr[...] = jnp.zeros_like(s_scr)

    c = qf_ref.shape[0]
    n_lvl = sgn_ref.shape[1]
    dirs = range(2)
    qzv = ((qf_ref, zf_ref, vf_ref), (qb_ref, zb_ref, vb_ref))
    o_refs = (of_ref, ob_ref)
    chains = [(d, h) for d in dirs for h in range(H_C)]
    sl = [slice(h * HEAD, (h + 1) * HEAD) for d, h in chains]
    g_all, key_all, q_all = [], [], []
    for d in dirs:
        lb = lb_ref[d]
        sig_pos, sig_neg = _sigmoid_pair(qzv[d][1][...] + fb_ref[d])
        g_all.append(jnp.log2(lb + (1.0 - lb) * sig_pos))
        key_all.append((1.0 - lb) * sig_neg)
        q_all.append(qzv[d][0][...])
    gc_all = []
    for d in dirs:
        g_hi, g_lo = _split2(g_all[d])
        tri = tri_ref[d].astype(BF16)
        gc_all.append(_dot(tri, g_hi) + _dot(tri, g_lo))
    total_all = [jnp.sum(g, axis=0, keepdims=True) for g in g_all]
    refs = [_dot(sel_ref[d], gc_all[d].astype(BF16)) for d in dirs]
    diag = msk_ref[0, n_lvl] > 0.5
    att = [jnp.where(diag, _dot_nt(q_all[d][:, sl[i]].astype(BF16), key_all[d][:, sl[i]].astype(BF16)), 0.0)
           for i, (d, h) in enumerate(chains)]
    for l in range(n_lvl):
        bs = c >> (l + 1)
        x_all = []
        for d in dirs:
            ref_l = refs[d][l * c:(l + 1) * c]
            if bs >= SUBLANES:
                parts = []
                for a in range(0, c, 2 * bs):
                    first, second = slice(a, a + bs), slice(a + bs, a + 2 * bs)
                    q_rows, k_rows = (second, first) if d == 0 else (first, second)
                    xq = q_all[d][q_rows] * jnp.exp2(gc_all[d][q_rows] - ref_l[q_rows])
                    xk = key_all[d][k_rows] * jnp.exp2(ref_l[k_rows] - gc_all[d][k_rows])
                    parts += [xk, xq] if d == 0 else [xq, xk]
                x_all.append(jnp.concatenate(parts, axis=0).astype(BF16))
            else:
                sgn = jnp.concatenate([sgn_ref[d, l]] * H_C, axis=1)
                x_all.append((jnp.where(sgn > 0, q_all[d], key_all[d])
                              * jnp.exp2(sgn * (gc_all[d] - ref_l))).astype(BF16))
        x = [x_all[d][:, sl[i]] for i, (d, h) in enumerate(chains)]
        prod = [_dot_nt(xh, xh) for xh in x]
        in_level = [msk_ref[d, l] > 0.5 for d in dirs]
        att = [jnp.where(in_level[d], prod[i], att[i]) for i, (d, h) in enumerate(chains)]
    st = [s_scr[d, h] for d, h in chains]
    inter = [_dot_nt((q_all[d][:, sl[i]] * jnp.exp2(gc_all[d][:, sl[i]])).astype(BF16), st[i].astype(BF16))
             for i, (d, h) in enumerate(chains)]
    intra = [_dot(att[i].astype(BF16), qzv[d][2][:, sl[i]].astype(BF16)) for i, (d, h) in enumerate(chains)]
    upd = [_dot_tn(qzv[d][2][:, sl[i]].astype(BF16),
                   (key_all[d][:, sl[i]] * jnp.exp2(total_all[d][:, sl[i]] - gc_all[d][:, sl[i]])).astype(BF16))
           for i, (d, h) in enumerate(chains)]
    for i, (d, h) in enumerate(chains):
        o_refs[d][0, :, sl[i]] = inter[i] + intra[i]
        s_scr[d, h] = jnp.exp2(total_all[d][:, sl[i]]) * st[i] + upd[i]


def _gla_scan(p, f_bias, lb, seq, n_ctx, nb):
    t, pw = p.shape
    c = GLA_CHUNK
    w = H_C * HEAD
    tri = _tri_tables(c)
    sel, sgn, msk = _gla_tables(c)
    n_steps, blk = _step_block(c, seq, n_ctx, nb)
    dspec = lambda d, col: pl.BlockSpec((c, w), lambda b, i: (blk(b, d, i), col))
    whole = lambda a: pl.BlockSpec(a.shape, lambda b, i: (0,) * a.ndim)
    fb3, lb3 = f_bias.reshape(2, 1, w), lb.reshape(2, 1, w)
    consts = (fb3, lb3, tri, sel, sgn, msk)
    return pl.pallas_call(
        _gla_kernel,
        grid=(nb, n_steps),
        in_specs=[dspec(0, 0), dspec(0, 1), dspec(0, 3), dspec(1, 0), dspec(1, 2), dspec(1, 3)]
        + [whole(a) for a in consts],
        out_specs=[pl.BlockSpec((1, c, w), lambda b, i, d=d: (0, blk(b, d, i), 0)) for d in range(2)],
        out_shape=[jax.ShapeDtypeStruct((1, t, w), F32)] * 2,
        scratch_shapes=[pltpu.VMEM((2, H_C, HEAD, HEAD), F32)],
        compiler_params=_cparams(("parallel", "arbitrary")),
        name="gla_scan",
    )(p, p, p, p, p, p, *consts)


def _best_group(scores_t, bias):
    sel = [scores_t[e:e + 1, :] + bias[e:e + 1, :] for e in range(N_EXPERTS)]
    gscore = []
    for grp in range(N_GROUPS):
        v = sel[grp * EXPERTS_PER_GROUP:(grp + 1) * EXPERTS_PER_GROUP]
        best = None
        for a in range(EXPERTS_PER_GROUP):
            for b in range(a + 1, EXPERTS_PER_GROUP):
                pair = v[a] + v[b]
                best = pair if best is None else jnp.maximum(best, pair)
        gscore.append(best)
    best_g = jnp.zeros(gscore[0].shape, jnp.int32)
    best_v = gscore[0]
    for grp in range(1, N_GROUPS):
        better = gscore[grp] > best_v
        best_g = jnp.where(better, grp, best_g)
        best_v = jnp.where(better, gscore[grp], best_v)
    return best_g


def _outproj_kernel(x_ref, of0, ob0, gt0, of1, ob1, gt1, hn_ref, w_ref, g1_ref, sc_ref, sh_ref, nf_ref,
                    wr_ref, rb_ref, triu_ref, xo_ref, pos_ref, cnt_ref, hs_ref,
                    base_scr, h_scr, pos_v, pos_s, cnt_v, cnt_s, row_sem, meta_sem, *, first_half_sigmoid, cap):
    i = pl.program_id(0)
    n_steps = pl.num_programs(0)
    tm = x_ref.shape[0]
    slot = i % 2
    prev = 1 - slot
    dump = N_GROUPS * cap

    @pl.when(i == 0)
    def _():
        base_scr[...] = jnp.zeros_like(base_scr)
        h_scr[...] = jnp.zeros_like(h_scr)

        def init(r, carry):
            pos_s[1, r] = dump + r
            return carry
        lax.fori_loop(0, tm, init, 0)

    @pl.when(i > 0)
    def _():
        pltpu.make_async_copy(pos_v, pos_s.at[pl.ds(prev, 1), :], meta_sem).wait()

    for r in range(tm):
        _rows_copy(h_scr.at[prev], hs_ref, row_sem, r, pos_s[prev, r]).start()

    feats = []
    for half, (of, ob, gt) in enumerate(((of0, ob0, gt0), (of1, ob1, gt1))):
        o = of[0] + ob[0]
        gate = gt[...]
        for h in range(o.shape[1] // HEAD):
            sl = slice(h * HEAD, (h + 1) * HEAD)
            t = o[:, sl]
            y = t * lax.rsqrt(jnp.mean(t * t, axis=-1, keepdims=True) + EPS)
            y = y * hn_ref[:, half * o.shape[1] + h * HEAD: half * o.shape[1] + (h + 1) * HEAD]
            gz = gate[:, sl]
            act = _sigmoid(gz) if (half == 0 and first_half_sigmoid) else _silu(gz)
            feats.append((y * act).astype(BF16))
    feats = jnp.concatenate(feats, axis=1)
    xn = x_ref[...] + g1_ref[0] * _dot(feats, w_ref[...])
    xo_ref[...] = xn
    y = xn * lax.rsqrt(jnp.mean(xn * xn, axis=-1, keepdims=True) + EPS) * nf_ref[...]
    hl = y * (1.0 + sc_ref[0]) + sh_ref[0]
    logits = _dot3(hl, wr_ref[...])
    scores_t = _sigmoid(logits).T
    best_g = _best_group(scores_t, rb_ref[...])
    onehot = jnp.concatenate([(best_g == g).astype(F32) for g in range(N_GROUPS)]
                             + [jnp.zeros((SUBLANES - N_GROUPS, tm), F32)], axis=0)
    before = _dot(onehot.astype(BF16), triu_ref[...])
    base = base_scr[...]
    rank = jnp.sum(onehot * (before + base[:, 0:1]), axis=0, keepdims=True)
    pos = best_g * cap + rank.astype(jnp.int32)
    pos_ref[0] = pos
    base = base + jnp.sum(onehot, axis=1, keepdims=True)
    base_scr[...] = base
    cnt_ref[...] = base

    pltpu.make_async_copy(h_scr.at[prev], hs_ref.at[pl.ds(0, tm), :], row_sem).wait()
    h_scr[slot] = hl
    pos_v[...] = pos
    pltpu.make_async_copy(pos_v, pos_s.at[pl.ds(slot, 1), :], meta_sem).start()

    @pl.when(i == n_steps - 1)
    def _():
        pltpu.make_async_copy(pos_v, pos_s.at[pl.ds(slot, 1), :], meta_sem).wait()

        def issue(r, carry):
            _rows_copy(h_scr.at[slot], hs_ref, row_sem, r, pos_s[slot, r]).start()
            return carry
        lax.fori_loop(0, tm, issue, 0, unroll=ROW_DMA_UNROLL)
        pltpu.make_async_copy(h_scr.at[slot], hs_ref.at[pl.ds(0, tm), :], row_sem).wait()
        cnt_v[...] = base.astype(jnp.int32)
        counts = pltpu.make_async_copy(cnt_v, cnt_s, meta_sem)
        counts.start()
        counts.wait()
        h_scr[prev] = jnp.zeros((tm, h_scr.shape[2]), F32)
        zero = h_scr.at[prev]

        def pad_copies(action):
            for g in range(N_GROUPS):
                cnt = cnt_s[g, 0]
                up = (cnt + SUBLANES - 1) // SUBLANES * SUBLANES
                for r in range(SUBLANES - 1):
                    @pl.when(cnt + r < up)
                    def _():
                        action(_rows_copy(zero, hs_ref, row_sem, 0, g * cap + cnt + r))
                action(pltpu.make_async_copy(zero, hs_ref.at[pl.ds(pl.multiple_of(g * cap + up, SUBLANES), tm), :],
                                             row_sem))

        pad_copies(lambda cp: cp.start())
        pad_copies(lambda cp: cp.wait())


def _outproj(xs, o_halves, gate_src, gate_cols, hn_gain, w_out, w_idx, mod, layer, nf_gain, wr_pad, rb_col,
             seq, nb, n_rows, first_half_sigmoid):
    t, d = xs.shape
    tm = TOK_TILE
    half = d // 2
    base = layer * SUBLANES * 6

    def mod_spec(which):
        return pl.BlockSpec((1, 1, d), lambda i: (base + _mod_row(i, tm, seq, nb) * 6 + which, 0, 0))

    in_specs = [pl.BlockSpec((tm, d), lambda i: (i, 0))]
    args = [xs]
    for ((arr_f, dir_f), (arr_b, dir_b), col), gcol in zip(o_halves, gate_cols):
        in_specs += [pl.BlockSpec((1, tm, half), lambda i, col=col, dd=dir_f: (dd, i, col)),
                     pl.BlockSpec((1, tm, half), lambda i, col=col, dd=dir_b: (dd, i, col)),
                     pl.BlockSpec((tm, half), lambda i, gcol=gcol: (i, gcol))]
        args += [arr_f, arr_b, gate_src]
    in_specs += [pl.BlockSpec((1, d), lambda i: (0, 0)),
                 pl.BlockSpec((None, d, d), lambda i: (w_idx, 0, 0)),
                 mod_spec(2), mod_spec(4), mod_spec(3),
                 pl.BlockSpec((1, d), lambda i: (0, 0)),
                 pl.BlockSpec((d, LANES), lambda i: (0, 0)),
                 pl.BlockSpec((N_EXPERTS, 1), lambda i: (0, 0)),
                 pl.BlockSpec((tm, tm), lambda i: (0, 0))]
    idx = np.arange(tm)
    triu = jnp.asarray(idx[:, None] < idx[None, :], dtype=BF16)
    args += [hn_gain.reshape(1, d), w_out, mod, mod, mod, nf_gain.reshape(1, d), wr_pad, rb_col, triu]
    n_tiles = n_rows // tm
    assert tm == MOE_TILE
    cap = n_rows + MOE_TILE
    return pl.pallas_call(
        functools.partial(_outproj_kernel, first_half_sigmoid=first_half_sigmoid, cap=cap),
        grid=(n_tiles,),
        in_specs=in_specs,
        out_specs=[pl.BlockSpec((tm, d), lambda i: (i, 0)),
                   pl.BlockSpec((1, 1, tm), lambda i: (i, 0, 0)),
                   pl.BlockSpec((SUBLANES, LANES), lambda i: (0, 0)),
                   pl.BlockSpec(memory_space=pl.ANY)],
        out_shape=[jax.ShapeDtypeStruct((t, d), F32), jax.ShapeDtypeStruct((n_tiles, 1, tm), jnp.int32),
                   jax.ShapeDtypeStruct((SUBLANES, LANES), F32),
                   jax.ShapeDtypeStruct((N_GROUPS * cap + tm, d), F32)],
        scratch_shapes=[pltpu.VMEM((SUBLANES, LANES), F32), pltpu.VMEM((2, tm, d), F32),
                        pltpu.VMEM((1, tm), jnp.int32), pltpu.SMEM((2, tm), jnp.int32),
                        pltpu.VMEM((SUBLANES, LANES), jnp.int32), pltpu.SMEM((SUBLANES, LANES), jnp.int32),
                        pltpu.SemaphoreType.DMA(()), pltpu.SemaphoreType.DMA(())],
        compiler_params=_cparams(("arbitrary",)),
        name="outproj",
    )(*args)


def _group_gates(x, wr, bias_row, grp):
    scores = _sigmoid(_dot3(x, wr))
    lane_i = lax.broadcasted_iota(jnp.int32, scores.shape, 1)
    lane = lane_i.astype(F32)
    m = jnp.where(lane_i // EXPERTS_PER_GROUP == grp, scores + bias_row, -jnp.inf)
    picks = []
    for _ in range(2):
        top = jnp.max(m, axis=-1, keepdims=True)
        idx = jnp.min(jnp.where(m == top, lane, float(LANES)), axis=-1, keepdims=True)
        picks.append(idx)
        m = jnp.where(lane == idx, -jnp.inf, m)
    w = [jnp.sum(jnp.where(lane == idx, scores, 0.0), axis=-1, keepdims=True) for idx in picks]
    tot = w[0] + w[1]
    return jnp.where(lane == picks[0], w[0] / tot, jnp.where(lane == picks[1], w[1] / tot, 0.0))


def _group_ffn_kernel(tg_ref, tb_ref, nv_ref, x_ref, wr_ref, rb_ref, w1_ref, w3_ref, w2_ref, o_ref):
    i = pl.program_id(0)

    @pl.when(i < nv_ref[0])
    def _():
        grp = tg_ref[i]
        x = x_ref[...]
        xb = x.astype(BF16)
        gate = _group_gates(x, wr_ref[...], rb_ref[...], grp)
        lane = lax.broadcasted_iota(jnp.int32, gate.shape, 1)
        acts = []
        for j in range(EXPERTS_PER_GROUP):
            gcol = jnp.sum(jnp.where(lane == grp * EXPERTS_PER_GROUP + j, gate, 0.0), axis=-1, keepdims=True)
            acts.append((_silu(_dot(xb, w1_ref[j])) * _dot(xb, w3_ref[j]) * gcol).astype(BF16))
        o_ref[...] = _dot(jnp.concatenate(acts, axis=1), w2_ref[0])


def _group_ffn(h_sorted, tile_grp, tile_blk, n_valid, wr_pad, rb_row, w1, w3, w2, cap, layer):
    p_rows, d = h_sorted.shape
    tm = MOE_TILE
    f = w1.shape[2]
    n_tiles = tile_grp.shape[0]
    blocks_per_group = cap // tm
    row_blk = lambda i, tg, tb, nv: (tg[i] * blocks_per_group + tb[i], 0)
    experts = lambda i, tg, tb, nv: (layer * N_GROUPS + tg[i], 0, 0)
    w_in_spec = pl.BlockSpec((EXPERTS_PER_GROUP, d, f), experts)
    return pl.pallas_call(
        _group_ffn_kernel,
        grid_spec=pltpu.PrefetchScalarGridSpec(
            num_scalar_prefetch=3, grid=(n_tiles,),
            in_specs=[pl.BlockSpec((tm, d), row_blk),
                      pl.BlockSpec((d, LANES), lambda i, tg, tb, nv: (0, 0)),
                      pl.BlockSpec((1, LANES), lambda i, tg, tb, nv: (0, 0)),
                      w_in_spec, w_in_spec, pl.BlockSpec((1, EXPERTS_PER_GROUP * f, d), experts)],
            out_specs=pl.BlockSpec((tm, d), row_blk)),
        out_shape=jax.ShapeDtypeStruct((p_rows, d), F32),
        compiler_params=_cparams(("arbitrary",)),
        name="moe_group_ffn",
    )(tile_grp, tile_blk, n_valid, h_sorted, wr_pad, rb_row, w1, w3, w2)


def _combine_kernel(pos_ref, x_ref, y_ref, g2_ref, nfin_ref, o_ref, buf, sem, *, final_norm):
    i = pl.program_id(0)
    tm = x_ref.shape[0]
    slot = i % 2

    def gather(tile, slot_):
        def issue(r, carry):
            _rows_copy(y_ref, buf.at[slot_], sem.at[slot_], pos_ref[tile * tm + r], r).start()
            return carry
        lax.fori_loop(0, tm, issue, 0, unroll=ROW_DMA_UNROLL)

    @pl.when(i == 0)
    def _():
        gather(0, 0)

    @pl.when(i + 1 < pl.num_programs(0))
    def _():
        gather(i + 1, 1 - slot)

    pltpu.make_async_copy(y_ref.at[pl.ds(0, tm), :], buf.at[slot], sem.at[slot]).wait()
    y = x_ref[...] + g2_ref[0] * buf[slot]
    if final_norm:
        y = y * lax.rsqrt(jnp.mean(y * y, axis=-1, keepdims=True) + EPS) * nfin_ref[...]
    o_ref[...] = y


def _combine(xs, y_sorted, pos, mod, layer, nfin, seq, nb, n_rows, final_norm):
    t, d = xs.shape
    tm = TOK_TILE
    base = layer * SUBLANES * 6
    out_rows = n_rows if final_norm else t
    return pl.pallas_call(
        functools.partial(_combine_kernel, final_norm=final_norm),
        grid_spec=pltpu.PrefetchScalarGridSpec(
            num_scalar_prefetch=1, grid=(n_rows // tm,),
            in_specs=[pl.BlockSpec((tm, d), lambda i, pos: (i, 0)),
                      pl.BlockSpec(memory_space=pl.ANY),
                      pl.BlockSpec((1, 1, d), lambda i, pos: (base + _mod_row(i, tm, seq, nb) * 6 + 5, 0, 0)),
                      pl.BlockSpec((1, d), lambda i, pos: (0, 0))],
            out_specs=pl.BlockSpec((tm, d), lambda i, pos: (i, 0)),
            scratch_shapes=[pltpu.VMEM((2, tm, d), F32), pltpu.SemaphoreType.DMA((2,))]),
        out_shape=jax.ShapeDtypeStruct((out_rows, d), F32),
        compiler_params=_cparams(("arbitrary",)),
        name="moe_combine",
    )(pos, xs, y_sorted, mod, nfin.reshape(1, d))


def _moe_ffn(h_sorted, pos_rows, cnt, wr_pad, rb_row, w1, w3, w2, layer, n_rows):
    tm = MOE_TILE
    cap = n_rows + tm
    pos = pos_rows.reshape(-1)
    counts = cnt[:N_GROUPS, 0].astype(jnp.int32)
    tiles = (counts + tm - 1) // tm
    ends = jnp.cumsum(tiles)
    n_tiles = n_rows // tm + N_GROUPS
    step = jnp.minimum(jnp.arange(n_tiles, dtype=jnp.int32), ends[-1] - 1)
    tile_grp = jnp.sum(step[:, None] >= ends[None, :], axis=1).astype(jnp.int32)
    tile_blk = step - (ends - tiles)[tile_grp]
    y_sorted = _group_ffn(h_sorted, tile_grp, tile_blk, ends[-1:].astype(jnp.int32), wr_pad, rb_row, w1, w3, w2, cap,
                          layer)
    return y_sorted, pos


def kernel(x, c, ctx, c_ctx, w_mod, b_mod, norm_mix, norm_ffn, norm_final, ab_w_in, ab_i_bias, ab_f_bias,
           ab_conv, ab_a_log, ab_dt_bias, ab_norm_a, ab_norm_b, ab_w_out, c_w_in, c_f_bias, c_lb_raw, c_norm,
           c_w_out, w_router, router_bias, w1, w3, w2):
    nb, seq, d = x.shape
    n_ctx = ctx.shape[1]
    depth = w_mod.shape[0]
    n_lat = nb * seq
    assert nb + 1 <= SUBLANES and seq % 1024 == 0 and (nb * n_ctx) % 1024 == 0 and n_ctx % 256 == 0

    xs = jnp.concatenate([x.reshape(n_lat, d), ctx.reshape(nb * n_ctx, d)], axis=0)
    cpad = jnp.zeros((SUBLANES, d), F32).at[:nb].set(c).at[nb].set(c_ctx)
    mod = _modulation(cpad, w_mod, b_mod).reshape(depth * SUBLANES * 6, 1, d)

    lb_p = jax.nn.softmax(c_lb_raw.astype(F32), axis=0)
    lb_all = jnp.cumsum(lb_p, axis=0) - lb_p[0:1]

    wr_pad = jnp.zeros((d, LANES), F32).at[:, :N_EXPERTS].set(w_router)
    rb_col = router_bias.astype(F32).reshape(N_EXPERTS, 1)
    rb_row = jnp.zeros((1, LANES), F32).at[0, :N_EXPERTS].set(router_bias.astype(F32))
    w_main = H_AB * HEAD * 8
    ab_w_in_b, c_w_in_b = ab_w_in.astype(BF16), c_w_in.astype(BF16)
    ab_w_out_b, c_w_out_b = ab_w_out.astype(BF16), c_w_out.astype(BF16)
    n_exp, _, d_ff = w1.shape[1:]
    w1_b = w1.astype(BF16).reshape(depth * n_exp, d, d_ff)
    w3_b = w3.astype(BF16).reshape(depth * n_exp, d, d_ff)
    w2_b = w2.astype(BF16).reshape(depth * n_exp // EXPERTS_PER_GROUP, EXPERTS_PER_GROUP * d_ff, d)

    out = None
    column_major = False
    pending = None
    for layer in range(depth):
        last = layer == depth - 1
        j = layer // 2
        want_cm = layer % 2 == 1 and j % 2 == 1
        if want_cm != column_major:
            if pending is not None:
                xs = _combine(xs, *pending[:2], mod, pending[2], norm_final, seq, nb, xs.shape[0], False)
                pending = None
            if want_cm:
                xs = _to_column_major(xs, n_lat, seq)
            else:
                xs = jnp.concatenate([_from_column_major(xs[:n_lat], seq), xs[n_lat:]], axis=0)
            column_major = want_cm
        if layer % 2 == 0:
            w_in = ab_w_in[j]
            gw = w_in[:, w_main:].reshape(d, 4, 2, H_AB)
            wg = jnp.zeros((2, d, LANES), F32).at[:, :, :4 * H_AB].set(
                jnp.transpose(gw, (2, 0, 1, 3)).reshape(2, d, 4 * H_AB)).astype(BF16)
            zeros = jnp.zeros((2, H_AB), F32)
            bias = jnp.concatenate([ab_i_bias[j], ab_f_bias[j], zeros, ab_dt_bias[j]], axis=1)
            alog = jnp.concatenate([zeros, zeros, zeros, ab_a_log[j]], axis=1)
            gpar = jnp.zeros((2, SUBLANES, LANES), F32).at[:, 0, :4 * H_AB].set(bias).at[:, 1, :4 * H_AB].set(alog)
            p, gates, *xs_new = _inproj(xs, norm_mix[layer], mod, layer, ab_w_in_b, j, w_main, seq, nb, wg, gpar,
                                        pending=pending)
            ha_f, ha_b = _mlstm_scan(p, gates, seq, n_ctx, nb)
            qkv = _gdn_prep(p, ab_conv[j], seq, n_ctx, nb)
            ob_f, ob_b = _gdn_scan(qkv, gates, seq, n_ctx, nb)
            o_halves = (((ha_f, 0), (ha_b, 0), 0), ((ob_f, 0), (ob_b, 0), 0))
            gate_cols = (3, 7)
            hn_gain = jnp.concatenate([ab_norm_a[j], ab_norm_b[j]])
            w_out = ab_w_out_b
        else:
            p, *xs_new = _inproj(xs, norm_mix[layer], mod, layer, c_w_in_b, j, c_w_in.shape[2], seq, nb,
                                 silu_cols=H_C * HEAD, pending=pending)
            oc_f, oc_b = _gla_scan(p, c_f_bias[j], lb_all[j], seq, n_ctx, nb)
            o_halves = (((oc_f, 0), (oc_b, 0), 0), ((oc_f, 0), (oc_b, 0), 1))
            gate_cols = (8, 9)
            hn_gain = c_norm[j]
            w_out = c_w_out_b
        if pending is not None:
            (xs,) = xs_new
            pending = None
        n_rows = n_lat if last else n_lat + nb * n_ctx
        xs, pos_rows, cnt, h_sorted = _outproj(xs, o_halves, p, gate_cols, hn_gain, w_out, j, mod, layer,
                                               norm_ffn[layer], wr_pad, rb_col, seq, nb, n_rows, layer % 2 == 0)
        y_sorted, pos = _moe_ffn(h_sorted, pos_rows, cnt, wr_pad, rb_row, w1_b, w3_b, w2_b, layer, n_rows)
        if last:
            res = _combine(xs, y_sorted, pos, mod, layer, norm_final, seq, nb, n_rows, True)
            out = _from_column_major(res, seq) if column_major else res
        else:
            pending = (y_sorted, pos, layer)
    return out.reshape(nb, seq, d)
```

```python
import functools

import numpy as np
import jax
import jax.numpy as jnp
from jax import lax
from jax.experimental import pallas as pl
from jax.experimental.pallas import tpu as pltpu

F32 = jnp.float32
BF16 = jnp.bfloat16

EPS = 1e-6
GRID_W = 64
HEAD = 128
H_AB = 4
H_C = 8
CONV_K = 5
N_EXPERTS = 16
N_GROUPS = 4
EXPERTS_PER_GROUP = N_EXPERTS // N_GROUPS
LANES = 128
SUBLANES = 8
VMEM_LIMIT = 56 * 1024 * 1024

TOK_TILE = 512
MOE_TILE = 512
ROW_DMA_UNROLL = 8
MLSTM_CHUNK = 128
GDN_CHUNK = 64
GDN_SCAN_CHUNKS = 4
GLA_CHUNK = 128
GDN_INV_BLOCK = 16


def _cparams(sem):
    return pltpu.CompilerParams(dimension_semantics=sem, vmem_limit_bytes=VMEM_LIMIT)


def _dot(a, b):
    return jnp.dot(a, b, preferred_element_type=F32)


def _dot_nt(a, b):
    return lax.dot_general(a, b, (((1,), (1,)), ((), ())), preferred_element_type=F32)


def _dot_tn(a, b):
    return lax.dot_general(a, b, (((0,), (0,)), ((), ())), preferred_element_type=F32)


def _split2(x):
    hi = x.astype(BF16)
    lo = (x - hi.astype(F32)).astype(BF16)
    return hi, lo


def _split3(x):
    hi = x.astype(BF16)
    r = x - hi.astype(F32)
    mid = r.astype(BF16)
    lo = (r - mid.astype(F32)).astype(BF16)
    return hi, mid, lo


def _dot_exact_lhs(m_bf16, x):
    hi, mid, lo = _split3(x)
    return _dot(m_bf16, hi) + _dot(m_bf16, mid) + _dot(m_bf16, lo)


def _dot3(a, b):
    ah, al = _split2(a)
    bh, bl = _split2(b)
    return _dot(ah, bh) + _dot(ah, bl) + _dot(al, bh)


def _dot_nt3(a, b):
    ah, al = _split2(a)
    bh, bl = _split2(b)
    return _dot_nt(ah, bh) + _dot_nt(ah, bl) + _dot_nt(al, bh)


def _sigmoid(z):
    return 1.0 / (1.0 + jnp.exp(-z))


def _sigmoid_pair(z):
    return _sigmoid(z), _sigmoid(-z)


def _silu(z):
    return z * _sigmoid(z)


def _mod_kernel(c_ref, w_ref, b_ref, o_ref):
    s = _silu(c_ref[...]).astype(BF16)
    o_ref[0] = _dot(s, w_ref[0].astype(BF16)) + b_ref[0]


def _modulation(cpad, w_mod, b_mod):
    depth, d, n = w_mod.shape
    tn = 1536
    return pl.pallas_call(
        _mod_kernel,
        grid=(depth, n // tn),
        in_specs=[pl.BlockSpec((SUBLANES, d), lambda l, j: (0, 0)),
                  pl.BlockSpec((1, d, tn), lambda l, j: (l, 0, j)),
                  pl.BlockSpec((1, 1, tn), lambda l, j: (l, 0, j))],
        out_specs=pl.BlockSpec((1, SUBLANES, tn), lambda l, j: (l, 0, j)),
        out_shape=jax.ShapeDtypeStruct((depth, SUBLANES, n), F32),
        compiler_params=_cparams(("parallel", "parallel")),
        name="modulation",
    )(cpad, w_mod, b_mod.reshape(depth, 1, n))


def _gate_activations(raw, par):
    y = raw + par[0:1, :]
    e = jnp.exp(-jnp.abs(y))
    sp = jnp.log1p(e)
    lsig = jnp.minimum(y, 0.0) - sp
    r = 1.0 / (1.0 + e)
    sig = jnp.where(y >= 0, r, e * r)
    gdec = -jnp.exp(par[1:2, :]) * (jnp.maximum(y, 0.0) + sp)
    lane = lax.broadcasted_iota(jnp.int32, raw.shape, 1)
    return jnp.where(lane < 4, y, jnp.where(lane < 8, lsig, jnp.where(lane < 12, sig, gdec)))


def _rows_copy(src, dst, sem, src_row, dst_row):
    return pltpu.make_async_copy(src.at[pl.ds(src_row, 1), :], dst.at[pl.ds(dst_row, 1), :], sem)


def _inproj_kernel(*refs, with_gates, p_cols, silu_cols, fused_combine):
    refs = list(refs)
    pos_ref = refs.pop(0) if fused_combine else None
    x_ref, gain_ref, sc_ref, sh_ref, w_ref = refs[:5]
    rest = refs[5:]
    if with_gates:
        wg_ref, gpar_ref = rest[:2]
        rest = rest[2:]
    if fused_combine:
        y_ref, g2_ref = rest[:2]
        rest = rest[2:]
    p_ref = rest.pop(0)
    if with_gates:
        g_ref = rest.pop(0)
    x = x_ref[...]
    if fused_combine:
        xs_ref, buf, sem = rest
        i = pl.program_id(0)
        n_steps = pl.num_programs(0)
        tm = x_ref.shape[0]
        slot = i % 2

        @pl.when(i == 0)
        def _():
            def issue(r, carry):
                _rows_copy(y_ref, buf.at[0], sem.at[0], pos_ref[r], r).start()
                return carry
            lax.fori_loop(0, tm, issue, 0, unroll=ROW_DMA_UNROLL)

        nxt = jnp.minimum(i + 1, n_steps - 1)
        for r in range(tm):
            _rows_copy(y_ref, buf.at[1 - slot], sem.at[1 - slot], pos_ref[nxt * tm + r], r).start(priority=r % 2)
        pltpu.make_async_copy(y_ref.at[pl.ds(0, tm), :], buf.at[slot], sem.at[slot]).wait()
        x = x + g2_ref[0] * buf[slot]
        xs_ref[...] = x

        @pl.when(i == n_steps - 1)
        def _():
            pltpu.make_async_copy(y_ref.at[pl.ds(0, tm), :], buf.at[1 - slot], sem.at[1 - slot]).wait()
    y = x * lax.rsqrt(jnp.mean(x * x, axis=-1, keepdims=True) + EPS) * gain_ref[...]
    u = (y * (1.0 + sc_ref[0]) + sh_ref[0]).astype(BF16)
    if with_gates:
        for d in range(2):
            g_ref[d] = _gate_activations(_dot(u, wg_ref[d]), gpar_ref[d])
    tn = 1024
    for j in range(p_cols // tn):
        res = _dot(u, w_ref[:, j * tn:(j + 1) * tn])
        if (j + 1) * tn <= silu_cols:
            res = _silu(res)
        p_ref[:, j * tn:(j + 1) * tn] = res


def _mod_row(i, tm, seq, nb):
    return jnp.minimum((i * tm) // seq, nb)


def _inproj(xs, gain, mod, layer, w_all, w_idx, p, seq, nb, wg=None, gpar=None, silu_cols=0, pending=None):
    t, d = xs.shape
    tm = TOK_TILE
    with_gates = wg is not None
    fused = pending is not None
    base = layer * SUBLANES * 6

    def mod_spec(which, base_=base):
        return pl.BlockSpec((1, 1, d), lambda i, *_: (base_ + _mod_row(i, tm, seq, nb) * 6 + which, 0, 0))

    in_specs = [pl.BlockSpec((tm, d), lambda i, *_: (i, 0)),
                pl.BlockSpec((1, d), lambda i, *_: (0, 0)),
                mod_spec(1), mod_spec(0),
                pl.BlockSpec((None, d, p), lambda i, *_: (w_idx, 0, 0), pipeline_mode=pl.Buffered(1))]
    args = [xs, gain.reshape(1, d), mod, mod, w_all]
    out_specs = [pl.BlockSpec((tm, p), lambda i, *_: (i, 0))]
    out_shape = [jax.ShapeDtypeStruct((t, p), F32)]
    scratch, prefetch = [], []
    if with_gates:
        in_specs += [pl.BlockSpec((2, d, LANES), lambda i, *_: (0, 0, 0)),
                     pl.BlockSpec((2, SUBLANES, LANES), lambda i, *_: (0, 0, 0))]
        args += [wg, gpar]
        out_specs.append(pl.BlockSpec((2, tm, LANES), lambda i, *_: (0, i, 0)))
        out_shape.append(jax.ShapeDtypeStruct((2, t, LANES), F32))
    if fused:
        y_sorted, pos, prev_layer = pending
        in_specs += [pl.BlockSpec(memory_space=pl.ANY), mod_spec(5, prev_layer * SUBLANES * 6)]
        args += [y_sorted, mod]
        out_specs.append(pl.BlockSpec((tm, d), lambda i, *_: (i, 0)))
        out_shape.append(jax.ShapeDtypeStruct((t, d), F32))
        scratch = [pltpu.VMEM((2, tm, d), F32), pltpu.SemaphoreType.DMA((2,))]
        prefetch = [pos]
    return pl.pallas_call(
        functools.partial(_inproj_kernel, with_gates=with_gates, p_cols=p, silu_cols=silu_cols, fused_combine=fused),
        grid_spec=pltpu.PrefetchScalarGridSpec(
            num_scalar_prefetch=len(prefetch), grid=(t // tm,),
            in_specs=in_specs, out_specs=out_specs, scratch_shapes=scratch),
        out_shape=out_shape,
        compiler_params=_cparams(("arbitrary",) if fused else ("parallel",)),
        name="inproj",
    )(*prefetch, *args)


def _gdn_prep_kernel(*refs, blocks_per_seq, n_lat_blocks, scale):
    w_ref, o_ref = refs[9], refs[10]
    i = pl.program_id(0)
    tm = o_ref.shape[0]
    w = refs[1].shape[1]
    in_lat = i < n_lat_blocks
    first = jnp.logical_or(jnp.logical_not(in_lat), i % blocks_per_seq == 0)
    last = jnp.logical_or(jnp.logical_not(in_lat), i % blocks_per_seq == blocks_per_seq - 1)
    pad = (CONV_K - 1) // 2
    for part in range(3):
        prev_ref, x_ref, next_ref = refs[3 * part:3 * part + 3]
        prev = jnp.where(first, 0.0, prev_ref[...])
        nxt = jnp.where(last, 0.0, next_ref[...])
        xe = jnp.concatenate([prev, x_ref[...], nxt], axis=0)
        n = xe.shape[0]
        acc = None
        for j in range(CONV_K):
            shift = (pad - j) % n
            xr = xe if shift == 0 else pltpu.roll(xe, shift, 0)
            term = xr[SUBLANES:SUBLANES + tm] * w_ref[j:j + 1, part * w:(part + 1) * w]
            acc = term if acc is None else acc + term
        y = _silu(acc)
        for h in range(H_AB):
            t = y[:, h * HEAD:(h + 1) * HEAD]
            if part < 2:
                t = t * lax.rsqrt(jnp.sum(t * t, axis=-1, keepdims=True) + EPS)
            if part == 0:
                t = t * scale
            o_ref[:, part * w + h * HEAD:part * w + (h + 1) * HEAD] = t


def _gdn_prep(p, conv_w, seq, n_ctx, nb):
    t = p.shape[0]
    tm = 256
    w = H_AB * HEAD
    first_col = 4
    wpad = jnp.zeros((SUBLANES, 3 * w), F32).at[:CONV_K].set(conv_w)
    hb = tm // SUBLANES
    nblk8 = t // SUBLANES
    in_specs = []
    for c in range(3):
        in_specs += [pl.BlockSpec((SUBLANES, w), lambda i, c=c: (jnp.maximum(i * hb - 1, 0), first_col + c)),
                     pl.BlockSpec((tm, w), lambda i, c=c: (i, first_col + c)),
                     pl.BlockSpec((SUBLANES, w), lambda i, c=c: (jnp.minimum((i + 1) * hb, nblk8 - 1), first_col + c))]
    in_specs.append(pl.BlockSpec((SUBLANES, 3 * w), lambda i: (0, 0)))
    return pl.pallas_call(
        functools.partial(_gdn_prep_kernel, blocks_per_seq=seq // tm, n_lat_blocks=nb * seq // tm,
                          scale=HEAD ** -0.5),
        grid=(t // tm,),
        in_specs=in_specs,
        out_specs=pl.BlockSpec((tm, 3 * w), lambda i: (i, 0)),
        out_shape=jax.ShapeDtypeStruct((t, 3 * w), F32),
        compiler_params=_cparams(("parallel",)),
        name="gdn_prep",
    )(*([p] * 9), wpad)


def _tri_tables(c):
    idx = np.arange(c)
    fwd = (idx[None, :] <= idx[:, None]).astype(np.float32)
    return jnp.asarray(np.stack([fwd, fwd.T]))


def _to_cm_kernel(x3_ref, x2_ref, o_ref, *, n_lat_tiles):
    i = pl.program_id(0)
    rows = x3_ref.shape[0]

    @pl.when(i < n_lat_tiles)
    def _():
        for wl in range(SUBLANES):
            o_ref[wl * rows:(wl + 1) * rows, :] = x3_ref[:, wl, :]

    @pl.when(i >= n_lat_tiles)
    def _():
        o_ref[...] = x2_ref[...]


def _from_cm_kernel(x_ref, o_ref):
    rows = o_ref.shape[0]
    for wl in range(SUBLANES):
        o_ref[:, wl, :] = x_ref[wl * rows:(wl + 1) * rows, :]


def _to_column_major(xs, n_lat, seq):
    t, d = xs.shape
    rows = seq // GRID_W
    tile = rows * SUBLANES
    assert rows % SUBLANES == 0 and n_lat % tile == 0 and (t - n_lat) % tile == 0
    n_lat_tiles = n_lat // tile
    per_b = GRID_W // SUBLANES
    lat = lambda i: jnp.minimum(i, n_lat_tiles - 1)
    return pl.pallas_call(
        functools.partial(_to_cm_kernel, n_lat_tiles=n_lat_tiles),
        grid=(t // tile,),
        in_specs=[pl.BlockSpec((rows, SUBLANES, d), lambda i: (lat(i) // per_b, lat(i) % per_b, 0)),
                  pl.BlockSpec((tile, d), lambda i: (jnp.maximum(i, n_lat_tiles - 1), 0))],
        out_specs=pl.BlockSpec((tile, d), lambda i: (i, 0)),
        out_shape=jax.ShapeDtypeStruct((t, d), xs.dtype),
        compiler_params=_cparams(("parallel",)),
        name="to_column_major",
    )(xs.reshape(t // GRID_W, GRID_W, d), xs)


def _from_column_major(y, seq):
    n_lat, d = y.shape
    rows = seq // GRID_W
    tile = rows * SUBLANES
    per_b = GRID_W // SUBLANES
    out = pl.pallas_call(
        _from_cm_kernel,
        grid=(n_lat // tile,),
        in_specs=[pl.BlockSpec((tile, d), lambda i: (i, 0))],
        out_specs=pl.BlockSpec((rows, SUBLANES, d), lambda i: (i // per_b, i % per_b, 0)),
        out_shape=jax.ShapeDtypeStruct((n_lat // GRID_W, GRID_W, d), y.dtype),
        compiler_params=_cparams(("parallel",)),
        name="from_column_major",
    )(y)
    return out.reshape(n_lat, d)


def _mlstm_kernel(qf, kf, vf, qb, kb, vb, gf, gb, tri_ref, hf_ref, hb_ref, s_scr, m_scr, *, scale):
    @pl.when(pl.program_id(1) == 0)
    def _():
        s_scr[...] = jnp.zeros_like(s_scr)
        m_scr[...] = jnp.zeros_like(m_scr)

    c = gf.shape[1]
    ones_col = (lax.broadcasted_iota(jnp.int32, (c, HEAD), 1) == 0).astype(BF16)
    qkv = ((qf, kf, vf), (qb, kb, vb))
    h_out = (hf_ref, hb_ref)
    g = [gf[0], gb[0]]
    mask = [tri_ref[d] > 0.5 for d in range(2)]
    b = [_dot_exact_lhs(tri_ref[d].astype(BF16), g[d]) for d in range(2)]
    total = [jnp.sum(x, axis=0, keepdims=True) for x in g]
    g_t = [x.T for x in g]
    b_t = [x.T for x in b]
    chains = [(d, h) for d in range(2) for h in range(H_AB)]
    sl = [slice(h * HEAD, (h + 1) * HEAD) for d, h in chains]
    q = [qkv[d][0][:, sl[i]].astype(BF16) for i, (d, h) in enumerate(chains)]
    k = [qkv[d][1][:, sl[i]] * scale for i, (d, h) in enumerate(chains)]
    v_aug = [jnp.concatenate([qkv[d][2][:, sl[i]].astype(BF16), ones_col], axis=1) for i, (d, h) in enumerate(chains)]
    s = [s_scr[d, h] for d, h in chains]
    qk = [_dot_nt(q[i], k[i].astype(BF16)) for i in range(len(chains))]
    qs = [_dot(q[i], s[i].astype(BF16)) for i in range(len(chains))]
    w_intra, w_inter, m_ts, colvs, m_sts, tots = [], [], [], [], [], []
    for i, (d, h) in enumerate(chains):
        li_c, li_r = g[d][:, h:h + 1], g_t[d][h:h + 1, :]
        b_c, b_r = b[d][:, 4 + h:5 + h], b_t[d][4 + h:5 + h, :]
        tot = total[d][:, 4 + h:5 + h]
        m_st = m_scr[d, h][0:1, 0:1]
        d_log = jnp.where(mask[d], b_c - b_r + li_r, -jnp.inf)
        m_inter = b_c + m_st
        m_t = jnp.maximum(m_inter, jnp.max(d_log, axis=-1, keepdims=True))
        w_inter.append(jnp.exp(m_inter - m_t))
        w_intra.append((jnp.exp(d_log - m_t) * qk[i]).astype(BF16))
        m_ts.append(m_t); colvs.append(tot - b_c + li_c); m_sts.append(m_st); tots.append(tot)
    intra = [_dot(w_intra[i], v_aug[i]) for i in range(len(chains))]
    m_new = [jnp.maximum(tots[i] + m_sts[i], jnp.max(colvs[i], axis=0, keepdims=True)) for i in range(len(chains))]
    upd = [_dot_tn((k[i] * jnp.exp(colvs[i] - m_new[i])).astype(BF16), v_aug[i]) for i in range(len(chains))]
    for i, (d, h) in enumerate(chains):
        num = w_inter[i] * qs[i] + intra[i]
        den = num[:, HEAD:HEAD + 1]
        h_out[d][0, :, sl[i]] = num[:, :HEAD] / jnp.maximum(jnp.abs(den), jnp.exp(-m_ts[i]))
        s_scr[d, h] = jnp.exp(tots[i] + m_sts[i] - m_new[i]) * s[i] + upd[i]
        m_scr[d, h] = jnp.broadcast_to(m_new[i], (SUBLANES, LANES))


def _step_block(chunk, seq, n_ctx, nb):
    n_c, n_l = n_ctx // chunk, seq // chunk
    off = nb * n_l

    def fn(b, d, i):
        j = i - n_c
        ctx_blk = off + b * n_c + (i if d == 0 else n_c - 1 - i)
        lat_blk = b * n_l + (j if d == 0 else n_l - 1 - j)
        return jnp.where(i < n_c, ctx_blk, lat_blk)

    return n_c + n_l, fn


def _mlstm_scan(p, gates, seq, n_ctx, nb):
    t = p.shape[0]
    c = MLSTM_CHUNK
    w = H_AB * HEAD
    n_steps, blk = _step_block(c, seq, n_ctx, nb)
    in_specs = [pl.BlockSpec((c, w), lambda b, i, d=d, col=col: (blk(b, d, i), col)) for d in range(2) for col in range(3)]
    in_specs += [pl.BlockSpec((1, c, LANES), lambda b, i, d=d: (d, blk(b, d, i), 0)) for d in range(2)]
    in_specs.append(pl.BlockSpec((2, c, c), lambda b, i: (0, 0, 0)))
    return pl.pallas_call(
        functools.partial(_mlstm_kernel, scale=HEAD ** -0.5),
        grid=(nb, n_steps),
        in_specs=in_specs,
        out_specs=[pl.BlockSpec((1, c, w), lambda b, i, d=d: (0, blk(b, d, i), 0)) for d in range(2)],
        out_shape=[jax.ShapeDtypeStruct((1, t, w), F32)] * 2,
        scratch_shapes=[pltpu.VMEM((2, H_AB, HEAD, 2 * HEAD), F32), pltpu.VMEM((2, H_AB, SUBLANES, LANES), F32)],
        compiler_params=_cparams(("parallel", "arbitrary")),
        name="mlstm_scan",
    )(p, p, p, p, p, p, gates, gates, _tri_tables(c))


def _gdn_inv_masks(c):
    idx = np.arange(c)
    blk = lambda n: (idx[:, None] // n) == (idx[None, :] // n)
    levels = [blk(GDN_INV_BLOCK)]
    n = GDN_INV_BLOCK
    while n < c:
        levels.append(np.logical_and(blk(2 * n), np.logical_not(blk(n))))
        n *= 2
    levels = np.stack(levels).astype(np.float32)
    return jnp.asarray(np.tile(levels, (1, 1, H_AB)))


def _block_diag(x, n_blocks):
    blk = lax.broadcasted_iota(jnp.int32, x.shape, 1) // (x.shape[1] // n_blocks)
    return jnp.concatenate([jnp.where(blk == j, x, jnp.zeros_like(x)) for j in range(n_blocks)], axis=0)


def _heads_to_lanes(x, first, width):
    return jnp.concatenate([jnp.broadcast_to(x[:, first + h:first + h + 1], (x.shape[0], width))
                            for h in range(H_AB)], axis=1)


def _unit_tri_inverse(a_list, eye, inv_masks):
    prod = lambda xs, ys: [_dot(x.astype(BF16), _block_diag(y.astype(BF16), H_AB)) for x, y in zip(xs, ys)]
    pw = [a * inv_masks[0] for a in a_list]
    inv = [eye - p for p in pw]
    n = 2
    while n < GDN_INV_BLOCK:
        pw = prod(pw, pw)
        inv = [i + t for i, t in zip(inv, prod(inv, pw))]
        n *= 2
    for lvl in range(1, inv_masks.shape[0]):
        t = prod(prod(inv, [a * inv_masks[lvl] for a in a_list]), inv)
        inv = [i - x for i, x in zip(inv, t)]
    return inv


def _gdn_chunk_kernel(q_ref, k_ref, v_ref, g_ref, tri_ref, im_ref, u_ref, w_ref, qd_ref, kd_ref, att_ref):
    c = GDN_CHUNK
    n_chunks = q_ref.shape[0] // c
    inv_masks = im_ref[...]
    row = lax.broadcasted_iota(jnp.int32, (c, H_AB * c), 0)
    col = lax.broadcasted_iota(jnp.int32, (c, H_AB * c), 1) % c
    eye_b = row == col
    eye = eye_b.astype(F32)
    masks = [col <= row, col >= row]
    stricts = [col < row, col > row]
    tris = [tri_ref[d].astype(BF16) for d in range(2)]
    ones = jnp.ones((c, c), BF16)
    chunks = range(n_chunks)
    rows = [slice(ci * c, (ci + 1) * c) for ci in chunks]
    groups = [(ci, d) for ci in chunks for d in range(2)]
    k_hl = [_split2(k_ref[r, :]) for r in rows]
    kbd = [(_block_diag(hi, H_AB), _block_diag(lo, H_AB)) for hi, lo in k_hl]
    r_hi = [_dot_nt(jnp.concatenate([k_hl[ci][0], k_hl[ci][1], q_ref[rows[ci], :].astype(BF16)], axis=0), kbd[ci][0])
            for ci in chunks]
    r_lo = [_dot_nt(k_hl[ci][0], kbd[ci][1]) for ci in chunks]
    kk = [r_hi[ci][:c] + r_hi[ci][c:2 * c] + r_lo[ci] for ci in chunks]
    g = [g_ref[d, rows[ci], :] for ci, d in groups]
    gc = [_dot_exact_lhs(tris[d], g[i]) for i, (ci, d) in enumerate(groups)]
    gc_c = [_heads_to_lanes(x, 12, c) for x in gc]
    gc_r = [_dot_exact_lhs(ones, eye * x) for x in gc_c]
    decay = [jnp.exp(jnp.where(masks[d], gc_c[i] - gc_r[i], -jnp.inf)) for i, (ci, d) in enumerate(groups)]
    a = [jnp.where(stricts[d], _heads_to_lanes(g[i], 8, c) * kk[ci] * decay[i], 0.0)
         for i, (ci, d) in enumerate(groups)]
    inv = _unit_tri_inverse(a, eye, inv_masks)
    inv_hl = [jnp.concatenate(_split2(x), axis=0) for x in inv]
    gc_w = [_heads_to_lanes(x, 12, HEAD) for x in gc]
    eg = [jnp.exp(x) for x in gc_w]
    beta = [_heads_to_lanes(x, 8, HEAD) for x in g]
    res_u = [_dot(inv_hl[i], _block_diag((v_ref[rows[ci], :] * beta[i]).astype(BF16), H_AB))
             for i, (ci, d) in enumerate(groups)]
    res_w = [_dot(inv_hl[i], _block_diag((k_ref[rows[ci], :] * (beta[i] * eg[i])).astype(BF16), H_AB))
             for i, (ci, d) in enumerate(groups)]
    for i, (ci, d) in enumerate(groups):
        r = rows[ci]
        total = _heads_to_lanes(jnp.sum(g[i], axis=0, keepdims=True), 12, HEAD)
        u_ref[d, r, :] = (res_u[i][:c] + res_u[i][c:]).astype(BF16)
        w_ref[d, r, :] = (res_w[i][:c] + res_w[i][c:]).astype(BF16)
        qd_ref[d, r, :] = (q_ref[r, :] * eg[i]).astype(BF16)
        kd_ref[d, r, :] = (k_ref[r, :] * jnp.exp(total - gc_w[i])).astype(BF16)
        att_ref[d, r, :] = (r_hi[ci][2 * c:] * decay[i]).astype(BF16)


def _gdn_scan_kernel(*refs):
    ins, o_refs, s_scr = refs[:12], refs[12:14], refs[14]
    c = GDN_CHUNK

    @pl.when(pl.program_id(1) == 0)
    def _():
        s_scr[...] = jnp.zeros_like(s_scr)

    pw = 2 * HEAD
    rblk = lax.broadcasted_iota(jnp.int32, (pw, pw), 0) // HEAD
    cblk = lax.broadcasted_iota(jnp.int32, (pw, pw), 1) // HEAD
    on_diag = rblk == cblk
    chains = [(d, pr) for d in range(2) for pr in range(H_AB // 2)]
    refs_of = lambda d: ins[d::2]
    sl = [slice(pr * pw, (pr + 1) * pw) for d, pr in chains]
    s = [s_scr[d, pr] for d, pr in chains]
    n_sub = o_refs[0].shape[1] // c
    for k in range(n_sub):
        rows = [slice(kk * c, (kk + 1) * c) for kk in (k, n_sub - 1 - k)]
        res = [_dot(jnp.concatenate([refs_of(d)[1][0, rows[d], sl[i]], refs_of(d)[2][0, rows[d], sl[i]]], axis=0),
                    s[i].astype(BF16)) for i, (d, pr) in enumerate(chains)]
        v_new = [(refs_of(d)[0][0, rows[d], sl[i]].astype(F32) - res[i][:c]).astype(BF16)
                 for i, (d, pr) in enumerate(chains)]
        intra = [_dot(refs_of(d)[4][0, rows[d], pr * 2 * c:(pr + 1) * 2 * c], _block_diag(v_new[i], 2))
                 for i, (d, pr) in enumerate(chains)]
        upd = [_dot_tn(refs_of(d)[3][0, rows[d], sl[i]], v_new[i]) for i, (d, pr) in enumerate(chains)]
        totals = [jnp.sum(refs_of(d)[5][0, rows[d], :], axis=0, keepdims=True) for d in range(2)]
        for i, (d, pr) in enumerate(chains):
            o_refs[d][0, rows[d], sl[i]] = res[i][c:] + intra[i]
            gl = jnp.concatenate([jnp.broadcast_to(jnp.exp(totals[d][:, 12 + 2 * pr + j:13 + 2 * pr + j]), (1, HEAD))
                                  for j in range(2)], axis=1)
            s[i] = gl * s[i] + jnp.where(on_diag, upd[i], 0.0)
    for i, (d, pr) in enumerate(chains):
        s_scr[d, pr] = s[i]


def _gdn_scan(qkv, gates, seq, n_ctx, nb):
    t = qkv.shape[0]
    c = GDN_CHUNK
    w = H_AB * HEAD
    tri = _tri_tables(c)
    im = _gdn_inv_masks(c)
    tm = TOK_TILE
    sds = lambda width: jax.ShapeDtypeStruct((2, t, width), BF16)
    u, wv, qd, kd, att = pl.pallas_call(
        _gdn_chunk_kernel,
        grid=(t // tm,),
        in_specs=[pl.BlockSpec((tm, w), lambda i, col=col: (i, col)) for col in range(3)] + [
            pl.BlockSpec((2, tm, LANES), lambda i: (0, i, 0)),
            pl.BlockSpec((2, c, c), lambda i: (0, 0, 0)),
            pl.BlockSpec(im.shape, lambda i: (0, 0, 0))],
        out_specs=[pl.BlockSpec((2, tm, w), lambda i: (0, i, 0))] * 4
        + [pl.BlockSpec((2, tm, H_AB * c), lambda i: (0, i, 0))],
        out_shape=[sds(w)] * 4 + [sds(H_AB * c)],
        compiler_params=_cparams(("parallel",)),
        name="gdn_chunk",
    )(qkv, qkv, qkv, gates, tri, im)

    cs = c * GDN_SCAN_CHUNKS
    n_steps, blk = _step_block(cs, seq, n_ctx, nb)
    in_specs, args = [], []
    for arr, width in ((u, w), (wv, w), (qd, w), (kd, w), (att, H_AB * c), (gates, LANES)):
        for d in range(2):
            in_specs.append(pl.BlockSpec((1, cs, width), lambda b, i, d=d: (d, blk(b, d, i), 0)))
            args.append(arr)
    o_f, o_b = pl.pallas_call(
        _gdn_scan_kernel,
        grid=(nb, n_steps),
        in_specs=in_specs,
        out_specs=[pl.BlockSpec((1, cs, w), lambda b, i, d=d: (0, blk(b, d, i), 0)) for d in range(2)],
        out_shape=[jax.ShapeDtypeStruct((1, t, w), F32)] * 2,
        scratch_shapes=[pltpu.VMEM((2, H_AB // 2, 2 * HEAD, 2 * HEAD), F32)],
        compiler_params=_cparams(("parallel", "arbitrary")),
        name="gdn_scan",
    )(*args)
    return o_f, o_b


def _gla_tables(c):
    idx = np.arange(c)
    sel, sgn, msk = [], [], []
    for d in range(2):
        sel_d, sgn_d, msk_d = [], [], []
        bs = c // 2
        while bs >= 1:
            pair = idx // (2 * bs)
            second = (idx // bs) % 2 == 1
            ref_row = pair * 2 * bs + (bs - 1 if d == 0 else bs)
            sel_d.append((idx[None, :] == ref_row[:, None]).astype(np.float32))
            qside = second if d == 0 else np.logical_not(second)
            sgn_d.append(np.where(qside, 1.0, -1.0)[:, None] * np.ones((1, HEAD)))
            same_pair = pair[:, None] == pair[None, :]
            msk_d.append(np.logical_and(same_pair, np.logical_and(qside[:, None], np.logical_not(qside)[None, :])))
            bs //= 2
        msk_d.append(np.eye(c, dtype=bool))
        sel.append(np.concatenate(sel_d, axis=0))
        sgn.append(np.stack(sgn_d))
        msk.append(np.stack(msk_d).astype(np.float32))
    return (jnp.asarray(np.stack(sel), dtype=BF16), jnp.asarray(np.stack(sgn), dtype=F32),
            jnp.asarray(np.stack(msk), dtype=F32))


def _gla_kernel(qf_ref, zf_ref, vf_ref, qb_ref, zb_ref, vb_ref, fb_ref, lb_ref, tri_ref, sel_ref, sgn_ref, msk_ref,
                of_ref, ob_ref, s_scr):
    @pl.when(pl.program_id(1) == 0)
    def _():
        s_scr[...] = jnp.zeros_like(s_scr)

    c = qf_ref.shape[0]
    n_lvl = sgn_ref.shape[1]
    dirs = range(2)
    qzv = ((qf_ref, zf_ref, vf_ref), (qb_ref, zb_ref, vb_ref))
    o_refs = (of_ref, ob_ref)
    chains = [(d, h) for d in dirs for h in range(H_C)]
    sl = [slice(h * HEAD, (h + 1) * HEAD) for d, h in chains]
    g_all, key_all, q_all = [], [], []
    for d in dirs:
        lb = lb_ref[d]
        sig_pos, sig_neg = _sigmoid_pair(qzv[d][1][...] + fb_ref[d])
        g_all.append(jnp.log2(lb + (1.0 - lb) * sig_pos))
        key_all.append((1.0 - lb) * sig_neg)
        q_all.append(qzv[d][0][...])
    gc_all = []
    for d in dirs:
        g_hi, g_lo = _split2(g_all[d])
        tri = tri_ref[d].astype(BF16)
        gc_all.append(_dot(tri, g_hi) + _dot(tri, g_lo))
    total_all = [jnp.sum(g, axis=0, keepdims=True) for g in g_all]
    refs = [_dot(sel_ref[d], gc_all[d].astype(BF16)) for d in dirs]
    diag = msk_ref[0, n_lvl] > 0.5
    att = [jnp.where(diag, _dot_nt(q_all[d][:, sl[i]].astype(BF16), key_all[d][:, sl[i]].astype(BF16)), 0.0)
           for i, (d, h) in enumerate(chains)]
    for l in range(n_lvl):
        bs = c >> (l + 1)
        x_all = []
        for d in dirs:
            ref_l = refs[d][l * c:(l + 1) * c]
            if bs >= SUBLANES:
                parts = []
                for a in range(0, c, 2 * bs):
                    first, second = slice(a, a + bs), slice(a + bs, a + 2 * bs)
                    q_rows, k_rows = (second, first) if d == 0 else (first, second)
                    xq = q_all[d][q_rows] * jnp.exp2(gc_all[d][q_rows] - ref_l[q_rows])
                    xk = key_all[d][k_rows] * jnp.exp2(ref_l[k_rows] - gc_all[d][k_rows])
                    parts += [xk, xq] if d == 0 else [xq, xk]
                x_all.append(jnp.concatenate(parts, axis=0).astype(BF16))
            else:
                sgn = jnp.concatenate([sgn_ref[d, l]] * H_C, axis=1)
                x_all.append((jnp.where(sgn > 0, q_all[d], key_all[d])
                              * jnp.exp2(sgn * (gc_all[d] - ref_l))).astype(BF16))
        x = [x_all[d][:, sl[i]] for i, (d, h) in enumerate(chains)]
        prod = [_dot_nt(xh, xh) for xh in x]
        in_level = [msk_ref[d, l] > 0.5 for d in dirs]
        att = [jnp.where(in_level[d], prod[i], att[i]) for i, (d, h) in enumerate(chains)]
    st = [s_scr[d, h] for d, h in chains]
    inter = [_dot_nt((q_all[d][:, sl[i]] * jnp.exp2(gc_all[d][:, sl[i]])).astype(BF16), st[i].astype(BF16))
             for i, (d, h) in enumerate(chains)]
    intra = [_dot(att[i].astype(BF16), qzv[d][2][:, sl[i]].astype(BF16)) for i, (d, h) in enumerate(chains)]
    upd = [_dot_tn(qzv[d][2][:, sl[i]].astype(BF16),
                   (key_all[d][:, sl[i]] * jnp.exp2(total_all[d][:, sl[i]] - gc_all[d][:, sl[i]])).astype(BF16))
           for i, (d, h) in enumerate(chains)]
    for i, (d, h) in enumerate(chains):
        o_refs[d][0, :, sl[i]] = inter[i] + intra[i]
        s_scr[d, h] = jnp.exp2(total_all[d][:, sl[i]]) * st[i] + upd[i]


def _gla_scan(p, f_bias, lb, seq, n_ctx, nb):
    t, pw = p.shape
    c = GLA_CHUNK
    w = H_C * HEAD
    tri = _tri_tables(c)
    sel, sgn, msk = _gla_tables(c)
    n_steps, blk = _step_block(c, seq, n_ctx, nb)
    dspec = lambda d, col: pl.BlockSpec((c, w), lambda b, i: (blk(b, d, i), col))
    whole = lambda a: pl.BlockSpec(a.shape, lambda b, i: (0,) * a.ndim)
    fb3, lb3 = f_bias.reshape(2, 1, w), lb.reshape(2, 1, w)
    consts = (fb3, lb3, tri, sel, sgn, msk)
    return pl.pallas_call(
        _gla_kernel,
        grid=(nb, n_steps),
        in_specs=[dspec(0, 0), dspec(0, 1), dspec(0, 3), dspec(1, 0), dspec(1, 2), dspec(1, 3)]
        + [whole(a) for a in consts],
        out_specs=[pl.BlockSpec((1, c, w), lambda b, i, d=d: (0, blk(b, d, i), 0)) for d in range(2)],
        out_shape=[jax.ShapeDtypeStruct((1, t, w), F32)] * 2,
        scratch_shapes=[pltpu.VMEM((2, H_C, HEAD, HEAD), F32)],
        compiler_params=_cparams(("parallel", "arbitrary")),
        name="gla_scan",
    )(p, p, p, p, p, p, *consts)


def _best_group(scores_t, bias):
    sel = [scores_t[e:e + 1, :] + bias[e:e + 1, :] for e in range(N_EXPERTS)]
    gscore = []
    for grp in range(N_GROUPS):
        v = sel[grp * EXPERTS_PER_GROUP:(grp + 1) * EXPERTS_PER_GROUP]
        best = None
        for a in range(EXPERTS_PER_GROUP):
            for b in range(a + 1, EXPERTS_PER_GROUP):
                pair = v[a] + v[b]
                best = pair if best is None else jnp.maximum(best, pair)
        gscore.append(best)
    best_g = jnp.zeros(gscore[0].shape, jnp.int32)
    best_v = gscore[0]
    for grp in range(1, N_GROUPS):
        better = gscore[grp] > best_v
        best_g = jnp.where(better, grp, best_g)
        best_v = jnp.where(better, gscore[grp], best_v)
    return best_g


def _outproj_kernel(x_ref, of0, ob0, gt0, of1, ob1, gt1, hn_ref, w_ref, g1_ref, sc_ref, sh_ref, nf_ref,
                    wr_ref, rb_ref, triu_ref, xo_ref, pos_ref, cnt_ref, hs_ref,
                    base_scr, h_scr, pos_v, pos_s, cnt_v, cnt_s, row_sem, meta_sem, *, first_half_sigmoid, cap):
    i = pl.program_id(0)
    n_steps = pl.num_programs(0)
    tm = x_ref.shape[0]
    slot = i % 2
    prev = 1 - slot
    dump = N_GROUPS * cap

    @pl.when(i == 0)
    def _():
        base_scr[...] = jnp.zeros_like(base_scr)
        h_scr[...] = jnp.zeros_like(h_scr)

        def init(r, carry):
            pos_s[1, r] = dump + r
            return carry
        lax.fori_loop(0, tm, init, 0)

    @pl.when(i > 0)
    def _():
        pltpu.make_async_copy(pos_v, pos_s.at[pl.ds(prev, 1), :], meta_sem).wait()

    for r in range(tm):
        _rows_copy(h_scr.at[prev], hs_ref, row_sem, r, pos_s[prev, r]).start(priority=r % 2)

    feats = []
    for half, (of, ob, gt) in enumerate(((of0, ob0, gt0), (of1, ob1, gt1))):
        o = of[0] + ob[0]
        gate = gt[...]
        for h in range(o.shape[1] // HEAD):
            sl = slice(h * HEAD, (h + 1) * HEAD)
            t = o[:, sl]
            y = t * lax.rsqrt(jnp.mean(t * t, axis=-1, keepdims=True) + EPS)
            y = y * hn_ref[:, half * o.shape[1] + h * HEAD: half * o.shape[1] + (h + 1) * HEAD]
            gz = gate[:, sl]
            act = _sigmoid(gz) if (half == 0 and first_half_sigmoid) else _silu(gz)
            feats.append((y * act).astype(BF16))
    feats = jnp.concatenate(feats, axis=1)
    xn = x_ref[...] + g1_ref[0] * _dot(feats, w_ref[...])
    xo_ref[...] = xn
    y = xn * lax.rsqrt(jnp.mean(xn * xn, axis=-1, keepdims=True) + EPS) * nf_ref[...]
    hl = y * (1.0 + sc_ref[0]) + sh_ref[0]
    logits = _dot3(hl, wr_ref[...])
    scores_t = _sigmoid(logits).T
    best_g = _best_group(scores_t, rb_ref[...])
    onehot = jnp.concatenate([(best_g == g).astype(F32) for g in range(N_GROUPS)]
                             + [jnp.zeros((SUBLANES - N_GROUPS, tm), F32)], axis=0)
    before = _dot(onehot.astype(BF16), triu_ref[...])
    base = base_scr[...]
    rank = jnp.sum(onehot * (before + base[:, 0:1]), axis=0, keepdims=True)
    pos = best_g * cap + rank.astype(jnp.int32)
    pos_ref[0] = pos
    base = base + jnp.sum(onehot, axis=1, keepdims=True)
    base_scr[...] = base
    cnt_ref[...] = base

    pltpu.make_async_copy(h_scr.at[prev], hs_ref.at[pl.ds(0, tm), :], row_sem).wait()
    h_scr[slot] = hl
    pos_v[...] = pos
    pltpu.make_async_copy(pos_v, pos_s.at[pl.ds(slot, 1), :], meta_sem).start()

    @pl.when(i == n_steps - 1)
    def _():
        pltpu.make_async_copy(pos_v, pos_s.at[pl.ds(slot, 1), :], meta_sem).wait()

        def issue(r, carry):
            _rows_copy(h_scr.at[slot], hs_ref, row_sem, r, pos_s[slot, r]).start()
            return carry
        lax.fori_loop(0, tm, issue, 0, unroll=ROW_DMA_UNROLL)
        pltpu.make_async_copy(h_scr.at[slot], hs_ref.at[pl.ds(0, tm), :], row_sem).wait()
        cnt_v[...] = base.astype(jnp.int32)
        counts = pltpu.make_async_copy(cnt_v, cnt_s, meta_sem)
        counts.start()
        counts.wait()
        h_scr[prev] = jnp.zeros((tm, h_scr.shape[2]), F32)
        zero = h_scr.at[prev]

        def pad_copies(action):
            for g in range(N_GROUPS):
                cnt = cnt_s[g, 0]
                up = (cnt + SUBLANES - 1) // SUBLANES * SUBLANES
                for r in range(SUBLANES - 1):
                    @pl.when(cnt + r < up)
                    def _():
                        action(_rows_copy(zero, hs_ref, row_sem, 0, g * cap + cnt + r))
                action(pltpu.make_async_copy(zero, hs_ref.at[pl.ds(pl.multiple_of(g * cap + up, SUBLANES), tm), :],
                                             row_sem))

        pad_copies(lambda cp: cp.start())
        pad_copies(lambda cp: cp.wait())


def _outproj(xs, o_halves, gate_src, gate_cols, hn_gain, w_out, w_idx, mod, layer, nf_gain, wr_pad, rb_col,
             seq, nb, n_rows, first_half_sigmoid):
    t, d = xs.shape
    tm = TOK_TILE
    half = d // 2
    base = layer * SUBLANES * 6

    def mod_spec(which):
        return pl.BlockSpec((1, 1, d), lambda i: (base + _mod_row(i, tm, seq, nb) * 6 + which, 0, 0))

    in_specs = [pl.BlockSpec((tm, d), lambda i: (i, 0))]
    args = [xs]
    for ((arr_f, dir_f), (arr_b, dir_b), col), gcol in zip(o_halves, gate_cols):
        in_specs += [pl.BlockSpec((1, tm, half), lambda i, col=col, dd=dir_f: (dd, i, col)),
                     pl.BlockSpec((1, tm, half), lambda i, col=col, dd=dir_b: (dd, i, col)),
                     pl.BlockSpec((tm, half), lambda i, gcol=gcol: (i, gcol))]
        args += [arr_f, arr_b, gate_src]
    in_specs += [pl.BlockSpec((1, d), lambda i: (0, 0)),
                 pl.BlockSpec((None, d, d), lambda i: (w_idx, 0, 0)),
                 mod_spec(2), mod_spec(4), mod_spec(3),
                 pl.BlockSpec((1, d), lambda i: (0, 0)),
                 pl.BlockSpec((d, LANES), lambda i: (0, 0)),
                 pl.BlockSpec((N_EXPERTS, 1), lambda i: (0, 0)),
                 pl.BlockSpec((tm, tm), lambda i: (0, 0))]
    idx = np.arange(tm)
    triu = jnp.asarray(idx[:, None] < idx[None, :], dtype=BF16)
    args += [hn_gain.reshape(1, d), w_out, mod, mod, mod, nf_gain.reshape(1, d), wr_pad, rb_col, triu]
    n_tiles = n_rows // tm
    assert tm == MOE_TILE
    cap = n_rows + MOE_TILE
    return pl.pallas_call(
        functools.partial(_outproj_kernel, first_half_sigmoid=first_half_sigmoid, cap=cap),
        grid=(n_tiles,),
        in_specs=in_specs,
        out_specs=[pl.BlockSpec((tm, d), lambda i: (i, 0)),
                   pl.BlockSpec((1, 1, tm), lambda i: (i, 0, 0)),
                   pl.BlockSpec((SUBLANES, LANES), lambda i: (0, 0)),
                   pl.BlockSpec(memory_space=pl.ANY)],
        out_shape=[jax.ShapeDtypeStruct((t, d), F32), jax.ShapeDtypeStruct((n_tiles, 1, tm), jnp.int32),
                   jax.ShapeDtypeStruct((SUBLANES, LANES), F32),
                   jax.ShapeDtypeStruct((N_GROUPS * cap + tm, d), F32)],
        scratch_shapes=[pltpu.VMEM((SUBLANES, LANES), F32), pltpu.VMEM((2, tm, d), F32),
                        pltpu.VMEM((1, tm), jnp.int32), pltpu.SMEM((2, tm), jnp.int32),
                        pltpu.VMEM((SUBLANES, LANES), jnp.int32), pltpu.SMEM((SUBLANES, LANES), jnp.int32),
                        pltpu.SemaphoreType.DMA(()), pltpu.SemaphoreType.DMA(())],
        compiler_params=_cparams(("arbitrary",)),
        name="outproj",
    )(*args)


def _group_gates(x, wr, bias_row, grp):
    scores = _sigmoid(_dot3(x, wr))
    lane_i = lax.broadcasted_iota(jnp.int32, scores.shape, 1)
    lane = lane_i.astype(F32)
    m = jnp.where(lane_i // EXPERTS_PER_GROUP == grp, scores + bias_row, -jnp.inf)
    picks = []
    for _ in range(2):
        top = jnp.max(m, axis=-1, keepdims=True)
        idx = jnp.min(jnp.where(m == top, lane, float(LANES)), axis=-1, keepdims=True)
        picks.append(idx)
        m = jnp.where(lane == idx, -jnp.inf, m)
    w = [jnp.sum(jnp.where(lane == idx, scores, 0.0), axis=-1, keepdims=True) for idx in picks]
    tot = w[0] + w[1]
    return jnp.where(lane == picks[0], w[0] / tot, jnp.where(lane == picks[1], w[1] / tot, 0.0))


def _group_ffn_kernel(tg_ref, tb_ref, nv_ref, x_ref, wr_ref, rb_ref, w1_ref, w3_ref, w2_ref, o_ref):
    i = pl.program_id(0)

    @pl.when(i < nv_ref[0])
    def _():
        grp = tg_ref[i]
        x = x_ref[...]
        xb = x.astype(BF16)
        gate = _group_gates(x, wr_ref[...], rb_ref[...], grp)
        lane = lax.broadcasted_iota(jnp.int32, gate.shape, 1)
        acts = []
        for j in range(EXPERTS_PER_GROUP):
            gcol = jnp.sum(jnp.where(lane == grp * EXPERTS_PER_GROUP + j, gate, 0.0), axis=-1, keepdims=True)
            acts.append((_silu(_dot(xb, w1_ref[j])) * _dot(xb, w3_ref[j]) * gcol).astype(BF16))
        o_ref[...] = _dot(jnp.concatenate(acts, axis=1), w2_ref[0])


def _group_ffn(h_sorted, tile_grp, tile_blk, n_valid, wr_pad, rb_row, w1, w3, w2, cap, layer):
    p_rows, d = h_sorted.shape
    tm = MOE_TILE
    f = w1.shape[2]
    n_tiles = tile_grp.shape[0]
    blocks_per_group = cap // tm
    row_blk = lambda i, tg, tb, nv: (tg[i] * blocks_per_group + tb[i], 0)
    experts = lambda i, tg, tb, nv: (layer * N_GROUPS + tg[i], 0, 0)
    w_in_spec = pl.BlockSpec((EXPERTS_PER_GROUP, d, f), experts)
    return pl.pallas_call(
        _group_ffn_kernel,
        grid_spec=pltpu.PrefetchScalarGridSpec(
            num_scalar_prefetch=3, grid=(n_tiles,),
            in_specs=[pl.BlockSpec((tm, d), row_blk),
                      pl.BlockSpec((d, LANES), lambda i, tg, tb, nv: (0, 0)),
                      pl.BlockSpec((1, LANES), lambda i, tg, tb, nv: (0, 0)),
                      w_in_spec, w_in_spec, pl.BlockSpec((1, EXPERTS_PER_GROUP * f, d), experts)],
            out_specs=pl.BlockSpec((tm, d), row_blk)),
        out_shape=jax.ShapeDtypeStruct((p_rows, d), F32),
        compiler_params=_cparams(("arbitrary",)),
        name="moe_group_ffn",
    )(tile_grp, tile_blk, n_valid, h_sorted, wr_pad, rb_row, w1, w3, w2)


def _combine_kernel(pos_ref, x_ref, y_ref, g2_ref, nfin_ref, o_ref, buf, sem, *, final_norm):
    i = pl.program_id(0)
    tm = x_ref.shape[0]
    slot = i % 2

    def gather(tile, slot_):
        def issue(r2, carry):
            for k in range(2):
                r = 2 * r2 + k
                _rows_copy(y_ref, buf.at[slot_], sem.at[slot_], pos_ref[tile * tm + r], r).start(priority=k)
            return carry
        lax.fori_loop(0, tm // 2, issue, 0, unroll=ROW_DMA_UNROLL // 2)

    @pl.when(i == 0)
    def _():
        gather(0, 0)

    @pl.when(i + 1 < pl.num_programs(0))
    def _():
        gather(i + 1, 1 - slot)

    pltpu.make_async_copy(y_ref.at[pl.ds(0, tm), :], buf.at[slot], sem.at[slot]).wait()
    y = x_ref[...] + g2_ref[0] * buf[slot]
    if final_norm:
        y = y * lax.rsqrt(jnp.mean(y * y, axis=-1, keepdims=True) + EPS) * nfin_ref[...]
    o_ref[...] = y


def _combine(xs, y_sorted, pos, mod, layer, nfin, seq, nb, n_rows, final_norm):
    t, d = xs.shape
    tm = TOK_TILE
    base = layer * SUBLANES * 6
    out_rows = n_rows if final_norm else t
    return pl.pallas_call(
        functools.partial(_combine_kernel, final_norm=final_norm),
        grid_spec=pltpu.PrefetchScalarGridSpec(
            num_scalar_prefetch=1, grid=(n_rows // tm,),
            in_specs=[pl.BlockSpec((tm, d), lambda i, pos: (i, 0)),
                      pl.BlockSpec(memory_space=pl.ANY),
                      pl.BlockSpec((1, 1, d), lambda i, pos: (base + _mod_row(i, tm, seq, nb) * 6 + 5, 0, 0)),
                      pl.BlockSpec((1, d), lambda i, pos: (0, 0))],
            out_specs=pl.BlockSpec((tm, d), lambda i, pos: (i, 0)),
            scratch_shapes=[pltpu.VMEM((2, tm, d), F32), pltpu.SemaphoreType.DMA((2,))]),
        out_shape=jax.ShapeDtypeStruct((out_rows, d), F32),
        compiler_params=_cparams(("arbitrary",)),
        name="moe_combine",
    )(pos, xs, y_sorted, mod, nfin.reshape(1, d))


def _moe_ffn(h_sorted, pos_rows, cnt, wr_pad, rb_row, w1, w3, w2, layer, n_rows):
    tm = MOE_TILE
    cap = n_rows + tm
    pos = pos_rows.reshape(-1)
    counts = cnt[:N_GROUPS, 0].astype(jnp.int32)
    tiles = (counts + tm - 1) // tm
    ends = jnp.cumsum(tiles)
    n_tiles = n_rows // tm + N_GROUPS
    step = jnp.minimum(jnp.arange(n_tiles, dtype=jnp.int32), ends[-1] - 1)
    tile_grp = jnp.sum(step[:, None] >= ends[None, :], axis=1).astype(jnp.int32)
    tile_blk = step - (ends - tiles)[tile_grp]
    y_sorted = _group_ffn(h_sorted, tile_grp, tile_blk, ends[-1:].astype(jnp.int32), wr_pad, rb_row, w1, w3, w2, cap,
                          layer)
    return y_sorted, pos


def kernel(x, c, ctx, c_ctx, w_mod, b_mod, norm_mix, norm_ffn, norm_final, ab_w_in, ab_i_bias, ab_f_bias,
           ab_conv, ab_a_log, ab_dt_bias, ab_norm_a, ab_norm_b, ab_w_out, c_w_in, c_f_bias, c_lb_raw, c_norm,
           c_w_out, w_router, router_bias, w1, w3, w2):
    nb, seq, d = x.shape
    n_ctx = ctx.shape[1]
    depth = w_mod.shape[0]
    n_lat = nb * seq
    assert nb + 1 <= SUBLANES and seq % 1024 == 0 and (nb * n_ctx) % 1024 == 0 and n_ctx % 256 == 0

    xs = jnp.concatenate([x.reshape(n_lat, d), ctx.reshape(nb * n_ctx, d)], axis=0)
    cpad = jnp.zeros((SUBLANES, d), F32).at[:nb].set(c).at[nb].set(c_ctx)
    mod = _modulation(cpad, w_mod, b_mod).reshape(depth * SUBLANES * 6, 1, d)

    lb_p = jax.nn.softmax(c_lb_raw.astype(F32), axis=0)
    lb_all = jnp.cumsum(lb_p, axis=0) - lb_p[0:1]

    wr_pad = jnp.zeros((d, LANES), F32).at[:, :N_EXPERTS].set(w_router)
    rb_col = router_bias.astype(F32).reshape(N_EXPERTS, 1)
    rb_row = jnp.zeros((1, LANES), F32).at[0, :N_EXPERTS].set(router_bias.astype(F32))
    w_main = H_AB * HEAD * 8
    ab_w_in_b, c_w_in_b = ab_w_in.astype(BF16), c_w_in.astype(BF16)
    ab_w_out_b, c_w_out_b = ab_w_out.astype(BF16), c_w_out.astype(BF16)
    n_exp, _, d_ff = w1.shape[1:]
    w1_b = w1.astype(BF16).reshape(depth * n_exp, d, d_ff)
    w3_b = w3.astype(BF16).reshape(depth * n_exp, d, d_ff)
    w2_b = w2.astype(BF16).reshape(depth * n_exp // EXPERTS_PER_GROUP, EXPERTS_PER_GROUP * d_ff, d)

    out = None
    column_major = False
    pending = None
    for layer in range(depth):
        last = layer == depth - 1
        j = layer // 2
        want_cm = layer % 2 == 1 and j % 2 == 1
        if want_cm != column_major:
            if pending is not None:
                xs = _combine(xs, *pending[:2], mod, pending[2], norm_final, seq, nb, xs.shape[0], False)
                pending = None
            if want_cm:
                xs = _to_column_major(xs, n_lat, seq)
            else:
                xs = jnp.concatenate([_from_column_major(xs[:n_lat], seq), xs[n_lat:]], axis=0)
            column_major = want_cm
        if layer % 2 == 0:
            w_in = ab_w_in[j]
            gw = w_in[:, w_main:].reshape(d, 4, 2, H_AB)
            wg = jnp.zeros((2, d, LANES), F32).at[:, :, :4 * H_AB].set(
                jnp.transpose(gw, (2, 0, 1, 3)).reshape(2, d, 4 * H_AB)).astype(BF16)
            zeros = jnp.zeros((2, H_AB), F32)
            bias = jnp.concatenate([ab_i_bias[j], ab_f_bias[j], zeros, ab_dt_bias[j]], axis=1)
            alog = jnp.concatenate([zeros, zeros, zeros, ab_a_log[j]], axis=1)
            gpar = jnp.zeros((2, SUBLANES, LANES), F32).at[:, 0, :4 * H_AB].set(bias).at[:, 1, :4 * H_AB].set(alog)
            p, gates, *xs_new = _inproj(xs, norm_mix[layer], mod, layer, ab_w_in_b, j, w_main, seq, nb, wg, gpar,
                                        pending=pending)
            ha_f, ha_b = _mlstm_scan(p, gates, seq, n_ctx, nb)
            qkv = _gdn_prep(p, ab_conv[j], seq, n_ctx, nb)
            ob_f, ob_b = _gdn_scan(qkv, gates, seq, n_ctx, nb)
            o_halves = (((ha_f, 0), (ha_b, 0), 0), ((ob_f, 0), (ob_b, 0), 0))
            gate_cols = (3, 7)
            hn_gain = jnp.concatenate([ab_norm_a[j], ab_norm_b[j]])
            w_out = ab_w_out_b
        else:
            p, *xs_new = _inproj(xs, norm_mix[layer], mod, layer, c_w_in_b, j, c_w_in.shape[2], seq, nb,
                                 silu_cols=H_C * HEAD, pending=pending)
            oc_f, oc_b = _gla_scan(p, c_f_bias[j], lb_all[j], seq, n_ctx, nb)
            o_halves = (((oc_f, 0), (oc_b, 0), 0), ((oc_f, 0), (oc_b, 0), 1))
            gate_cols = (8, 9)
            hn_gain = c_norm[j]
            w_out = c_w_out_b
        if pending is not None:
            (xs,) = xs_new
            pending = None
        n_rows = n_lat if last else n_lat + nb * n_ctx
        xs, pos_rows, cnt, h_sorted = _outproj(xs, o_halves, p, gate_cols, hn_gain, w_out, j, mod, layer,
                                               norm_ffn[layer], wr_pad, rb_col, seq, nb, n_rows, layer % 2 == 0)
        y_sorted, pos = _moe_ffn(h_sorted, pos_rows, cnt, wr_pad, rb_row, w1_b, w3_b, w2_b, layer, n_rows)
        if last:
            res = _combine(xs, y_sorted, pos, mod, layer, norm_final, seq, nb, n_rows, True)
            out = _from_column_major(res, seq) if column_major else res
        else:
            pending = (y_sorted, pos, layer)
    return out.reshape(nb, seq, d)
```
